```python
import jax, jax.numpy as jnp
from jax import lax
import numpy as np

D_MODEL = 1024
BATCH = 32
SEQ = 2048
DEPTH = 2

N_MIXERS = 2
N_A = (DEPTH + 1) // 2
N_B = DEPTH // 2
D_RNN = D_MODEL
LRU_BLOCKS = 16
LRU_BW = D_RNN // LRU_BLOCKS
LRU_CONV = 4
LRU_C = 8.0
RWKV_N = 64
RWKV_H = D_MODEL // RWKV_N
R_DECAY = 64
R_AAA = 64
R_GATE = 160
GN_EPS = 64e-5
D_FF = 3 * D_MODEL
FFN_CONV = 3
RMS_EPS = 1e-6

kernel_name = 'hybrid_rglru_rwkv7_convffn'


def _rmsnorm(x, g):
    xf = x.astype(jnp.float32)
    y = xf * lax.rsqrt(jnp.mean(xf * xf, axis=-1, keepdims=True) + RMS_EPS)
    return (y * g.astype(jnp.float32)).astype(x.dtype)


def _causal_dwconv(x, w, b):
    k_width, seq = w.shape[0], x.shape[1]
    xp = jnp.pad(x, ((0, 0), (k_width - 1, 0), (0, 0)))
    out = b
    for j in range(k_width):
        out = out + xp[:, j:j + seq] * w[j]
    return out


def _lru_combine(c1, c2):
    a1, b1 = c1
    a2, b2 = c2
    return a1 * a2, a2 * b1 + b2


def _rglru_block(x, norm, w_in, b_in, conv_w, conv_b, gate_w, gate_b, lam, w_out, b_out):
    bsz, seq, _ = x.shape
    h = _rmsnorm(x, norm)
    u = jnp.einsum('btd,de->bte', h, w_in) + b_in
    y_branch = jax.nn.gelu(u[..., :D_RNN], approximate=True)
    xr = _causal_dwconv(u[..., D_RNN:], conv_w, conv_b)
    xb = xr.reshape(bsz, seq, LRU_BLOCKS, LRU_BW)
    gates = jax.nn.sigmoid(jnp.einsum('btnc,gncd->gbtnd', xb, gate_w) + gate_b[:, None, None])
    r_gate = gates[0].reshape(bsz, seq, D_RNN).astype(jnp.float32)
    i_gate = gates[1].reshape(bsz, seq, D_RNN).astype(jnp.float32)
    log_a = -LRU_C * r_gate * jax.nn.softplus(-lam.astype(jnp.float32))
    a = jnp.exp(log_a)
    mult = jnp.sqrt(-jnp.expm1(2.0 * log_a))
    bterm = mult * (i_gate * xr.astype(jnp.float32))
    _, hs = lax.associative_scan(_lru_combine, (a, bterm), axis=1)
    out = hs.astype(x.dtype) * y_branch
    return jnp.einsum('bte,ed->btd', out, w_out) + b_out


def _rwkv7_scan(r, w, k, v, aa, bb):
    bsz, _, nh, n = r.shape

    def step(S, inp):
        r_t, w_t, k_t, v_t, a_t, b_t = inp
        sa = jnp.einsum('bhij,bhj->bhi', S, a_t)
        S = S * w_t[:, :, None, :] + sa[..., None] * b_t[:, :, None, :] + v_t[..., :, None] * k_t[..., None, :]
        y = jnp.einsum('bhij,bhj->bhi', S, r_t)
        return S, y

    s0 = jnp.zeros((bsz, nh, n, n), jnp.float32)
    xs = tuple(t.transpose(1, 0, 2, 3) for t in (r, w, k, v, aa, bb))
    _, ys = lax.scan(step, s0, xs)
    return ys.transpose(1, 0, 2, 3)


def _rwkv7_block(x, norm, mix, w_rkv, w0, w1, w2, a0, a1, a2, g1, g2, k_k, k_a, r_k, ln_w, ln_b, w_out):
    bsz, seq, d = x.shape
    f32 = jnp.float32
    h = _rmsnorm(x, norm)
    xx = jnp.pad(h, ((0, 0), (1, 0), (0, 0)))[:, :-1] - h
    xs_rkv = jnp.stack([h + xx * mix[0], h + xx * mix[1], h + xx * mix[2]])
    rkv = jnp.einsum('sbtd,sde->sbte', xs_rkv, w_rkv)
    r, k, v = rkv[0], rkv[1], rkv[2]
    xw = h + xx * mix[3]
    xa = h + xx * mix[4]
    xg = h + xx * mix[5]
    w = -jax.nn.softplus(-(w0 + jnp.tanh(xw @ w1) @ w2).astype(f32)) - 0.5
    decay = jnp.exp(-jnp.exp(w))
    a = jax.nn.sigmoid((a0 + (xa @ a1) @ a2).astype(f32))
    g = jax.nn.sigmoid(xg @ g1) @ g2
    kf = k.astype(f32)
    kk = (kf * k_k).reshape(bsz, seq, RWKV_H, RWKV_N)
    kk = kk / jnp.maximum(jnp.linalg.norm(kk, axis=-1, keepdims=True), 1e-12)
    kf = kf * (1.0 + (a - 1.0) * k_a)
    hs = lambda t: t.reshape(bsz, seq, RWKV_H, RWKV_N)
    rh, kh, vh = hs(r.astype(f32)), hs(kf), hs(v.astype(f32))
    ah = hs(a)
    y = _rwkv7_scan(rh, hs(decay), kh, vh, -kk, kk * ah)
    mu = jnp.mean(y, axis=-1, keepdims=True)
    var = jnp.mean(jnp.square(y - mu), axis=-1, keepdims=True)
    y = ((y - mu) * lax.rsqrt(var + GN_EPS)).reshape(bsz, seq, d) * ln_w + ln_b
    bonus = jnp.sum(rh * kh * r_k, axis=-1, keepdims=True) * vh
    y = (y + bonus.reshape(bsz, seq, d)).astype(x.dtype)
    return jnp.einsum('btd,de->bte', y * g, w_out)


def _conv_ffn(x, norm, w_up, conv_w, conv_b, w_down):
    h = _rmsnorm(x, norm)
    u = jnp.einsum('btd,df->btf', h, w_up)
    gate = _causal_dwconv(u[..., :D_FF], conv_w, conv_b)
    hid = jax.nn.gelu(gate, approximate=True) * u[..., D_FF:]
    return jnp.einsum('btf,fd->btd', hid, w_down)


def _fwd_setup_inputs(seed: int = 0) -> dict:
    key = jax.random.key(seed)
    ks = iter(jax.random.split(key, 48))
    f32 = jnp.float32
    nrm = lambda shape, scale: jax.random.normal(next(ks), shape, f32) * scale
    uni = lambda shape, lo, hi: jax.random.uniform(next(ks), shape, f32, lo, hi)
    d = D_MODEL
    u_a = uni((N_A, D_RNN), 0.9, 0.999)
    a_init = u_a ** (1.0 / LRU_C)
    return {
        'x': nrm((BATCH, SEQ, d), 1.0),
        'lru_norm': 1.0 + nrm((N_A, d), 0.02),
        'lru_w_in': nrm((N_A, d, 2 * D_RNN), d ** -0.5),
        'lru_b_in': nrm((N_A, 2 * D_RNN), 0.01),
        'lru_conv_w': nrm((N_A, LRU_CONV, D_RNN), LRU_CONV ** -0.5),
        'lru_conv_b': nrm((N_A, D_RNN), 0.01),
        'lru_gate_w': nrm((N_A, 2, LRU_BLOCKS, LRU_BW, LRU_BW), LRU_BW ** -0.5),
        'lru_gate_b': nrm((N_A, 2, LRU_BLOCKS, LRU_BW), 0.01),
        'lru_lambda': jnp.log(a_init) - jnp.log1p(-a_init),
        'lru_w_out': nrm((N_A, D_RNN, d), D_RNN ** -0.5),
        'lru_b_out': nrm((N_A, d), 0.01),
        'rwkv_norm': 1.0 + nrm((N_B, d), 0.02),
        'rwkv_mix': uni((N_B, 6, d), 0.0, 1.0),
        'rwkv_w_rkv': nrm((N_B, 3, d, d), d ** -0.5),
        'rwkv_w0': uni((N_B, d), -5.0, -1.0),
        'rwkv_w1': nrm((N_B, d, R_DECAY), d ** -0.5),
        'rwkv_w2': nrm((N_B, R_DECAY, d), 0.1 * R_DECAY ** -0.5),
        'rwkv_a0': nrm((N_B, d), 0.1),
        'rwkv_a1': nrm((N_B, d, R_AAA), d ** -0.5),
        'rwkv_a2': nrm((N_B, R_AAA, d), 0.1 * R_AAA ** -0.5),
        'rwkv_g1': nrm((N_B, d, R_GATE), d ** -0.5),
        'rwkv_g2': nrm((N_B, R_GATE, d), R_GATE ** -0.5),
        'rwkv_k_k': 0.85 + nrm((N_B, d), 0.05),
        'rwkv_k_a': 1.0 + nrm((N_B, d), 0.05),
        'rwkv_r_k': nrm((N_B, RWKV_H, RWKV_N), 0.1),
        'rwkv_ln_w': 1.0 + nrm((N_B, d), 0.02),
        'rwkv_ln_b': nrm((N_B, d), 0.01),
        'rwkv_w_out': nrm((N_B, d, d), d ** -0.5),
        'ffn_norm': 1.0 + nrm((DEPTH, d), 0.02),
        'ffn_w_up': nrm((DEPTH, d, 2 * D_FF), d ** -0.5),
        'ffn_conv_w': nrm((DEPTH, FFN_CONV, D_FF), FFN_CONV ** -0.5),
        'ffn_conv_b': nrm((DEPTH, D_FF), 0.01),
        'ffn_w_down': nrm((DEPTH, D_FF, d), D_FF ** -0.5),
        'final_norm': 1.0 + nrm((d,), 0.02),
    }


def _fwd_reference(x, lru_norm, lru_w_in, lru_b_in, lru_conv_w, lru_conv_b, lru_gate_w, lru_gate_b, lru_lambda, lru_w_out, lru_b_out,
              rwkv_norm, rwkv_mix, rwkv_w_rkv, rwkv_w0, rwkv_w1, rwkv_w2, rwkv_a0, rwkv_a1, rwkv_a2, rwkv_g1, rwkv_g2,
              rwkv_k_k, rwkv_k_a, rwkv_r_k, rwkv_ln_w, rwkv_ln_b, rwkv_w_out,
              ffn_norm, ffn_w_up, ffn_conv_w, ffn_conv_b, ffn_w_down, final_norm):
    for layer in range(DEPTH):
        j = layer // N_MIXERS
        if layer % N_MIXERS == 0:
            x = x + _rglru_block(x, lru_norm[j], lru_w_in[j], lru_b_in[j], lru_conv_w[j], lru_conv_b[j],
                                 lru_gate_w[j], lru_gate_b[j], lru_lambda[j], lru_w_out[j], lru_b_out[j])
        else:
            x = x + _rwkv7_block(x, rwkv_norm[j], rwkv_mix[j], rwkv_w_rkv[j], rwkv_w0[j], rwkv_w1[j], rwkv_w2[j],
                                 rwkv_a0[j], rwkv_a1[j], rwkv_a2[j], rwkv_g1[j], rwkv_g2[j], rwkv_k_k[j], rwkv_k_a[j],
                                 rwkv_r_k[j], rwkv_ln_w[j], rwkv_ln_b[j], rwkv_w_out[j])
        x = x + _conv_ffn(x, ffn_norm[layer], ffn_w_up[layer], ffn_conv_w[layer], ffn_conv_b[layer], ffn_w_down[layer])
    return _rmsnorm(x, final_norm)


import jax as _jax
import jax.numpy as _jnp

TWIN_FORMAT = 'train_step'
FWD_PARAMS = ['x', 'lru_norm', 'lru_w_in', 'lru_b_in', 'lru_conv_w', 'lru_conv_b', 'lru_gate_w', 'lru_gate_b', 'lru_lambda', 'lru_w_out', 'lru_b_out', 'rwkv_norm', 'rwkv_mix', 'rwkv_w_rkv', 'rwkv_w0', 'rwkv_w1', 'rwkv_w2', 'rwkv_a0', 'rwkv_a1', 'rwkv_a2', 'rwkv_g1', 'rwkv_g2', 'rwkv_k_k', 'rwkv_k_a', 'rwkv_r_k', 'rwkv_ln_w', 'rwkv_ln_b', 'rwkv_w_out', 'ffn_norm', 'ffn_w_up', 'ffn_conv_w', 'ffn_conv_b', 'ffn_w_down', 'final_norm']
TWIN_WEIGHTS = ['lru_norm', 'lru_w_in', 'lru_b_in', 'lru_conv_w', 'lru_conv_b', 'lru_gate_w', 'lru_gate_b', 'lru_lambda', 'lru_w_out', 'lru_b_out', 'rwkv_norm', 'rwkv_mix', 'rwkv_w_rkv', 'rwkv_w0', 'rwkv_w1', 'rwkv_w2', 'rwkv_a0', 'rwkv_a1', 'rwkv_a2', 'rwkv_g1', 'rwkv_g2', 'rwkv_k_k', 'rwkv_k_a', 'rwkv_r_k', 'rwkv_ln_w', 'rwkv_ln_b', 'rwkv_w_out', 'ffn_norm', 'ffn_w_up', 'ffn_conv_w', 'ffn_conv_b', 'ffn_w_down', 'final_norm']
TWIN_DIFF_INPUT = 'x'
TWIN_INPUTS = ['x', 'lru_norm', 'lru_w_in', 'lru_b_in', 'lru_conv_w', 'lru_conv_b', 'lru_gate_w', 'lru_gate_b', 'lru_lambda', 'lru_w_out', 'lru_b_out', 'rwkv_norm', 'rwkv_mix', 'rwkv_w_rkv', 'rwkv_w0', 'rwkv_w1', 'rwkv_w2', 'rwkv_a0', 'rwkv_a1', 'rwkv_a2', 'rwkv_g1', 'rwkv_g2', 'rwkv_k_k', 'rwkv_k_a', 'rwkv_r_k', 'rwkv_ln_w', 'rwkv_ln_b', 'rwkv_w_out', 'ffn_norm', 'ffn_w_up', 'ffn_conv_w', 'ffn_conv_b', 'ffn_w_down', 'final_norm', 'loss_target', 'm_lru_norm', 'm_lru_w_in', 'm_lru_b_in', 'm_lru_conv_w', 'm_lru_conv_b', 'm_lru_gate_w', 'm_lru_gate_b', 'm_lru_lambda', 'm_lru_w_out', 'm_lru_b_out', 'm_rwkv_norm', 'm_rwkv_mix', 'm_rwkv_w_rkv', 'm_rwkv_w0', 'm_rwkv_w1', 'm_rwkv_w2', 'm_rwkv_a0', 'm_rwkv_a1', 'm_rwkv_a2', 'm_rwkv_g1', 'm_rwkv_g2', 'm_rwkv_k_k', 'm_rwkv_k_a', 'm_rwkv_r_k', 'm_rwkv_ln_w', 'm_rwkv_ln_b', 'm_rwkv_w_out', 'm_ffn_norm', 'm_ffn_w_up', 'm_ffn_conv_w', 'm_ffn_conv_b', 'm_ffn_w_down', 'm_final_norm', 'v_lru_norm', 'v_lru_w_in', 'v_lru_b_in', 'v_lru_conv_w', 'v_lru_conv_b', 'v_lru_gate_w', 'v_lru_gate_b', 'v_lru_lambda', 'v_lru_w_out', 'v_lru_b_out', 'v_rwkv_norm', 'v_rwkv_mix', 'v_rwkv_w_rkv', 'v_rwkv_w0', 'v_rwkv_w1', 'v_rwkv_w2', 'v_rwkv_a0', 'v_rwkv_a1', 'v_rwkv_a2', 'v_rwkv_g1', 'v_rwkv_g2', 'v_rwkv_k_k', 'v_rwkv_k_a', 'v_rwkv_r_k', 'v_rwkv_ln_w', 'v_rwkv_ln_b', 'v_rwkv_w_out', 'v_ffn_norm', 'v_ffn_w_up', 'v_ffn_conv_w', 'v_ffn_conv_b', 'v_ffn_w_down', 'v_final_norm']
TWIN_OUTPUTS = ['loss', 'grad_x', 'grad_lru_norm', 'grad_lru_w_in', 'grad_lru_b_in', 'grad_lru_conv_w', 'grad_lru_conv_b', 'grad_lru_gate_w', 'grad_lru_gate_b', 'grad_lru_lambda', 'grad_lru_w_out', 'grad_lru_b_out', 'grad_rwkv_norm', 'grad_rwkv_mix', 'grad_rwkv_w_rkv', 'grad_rwkv_w0', 'grad_rwkv_w1', 'grad_rwkv_w2', 'grad_rwkv_a0', 'grad_rwkv_a1', 'grad_rwkv_a2', 'grad_rwkv_g1', 'grad_rwkv_g2', 'grad_rwkv_k_k', 'grad_rwkv_k_a', 'grad_rwkv_r_k', 'grad_rwkv_ln_w', 'grad_rwkv_ln_b', 'grad_rwkv_w_out', 'grad_ffn_norm', 'grad_ffn_w_up', 'grad_ffn_conv_w', 'grad_ffn_conv_b', 'grad_ffn_w_down', 'grad_final_norm', 'delta_lru_norm', 'delta_lru_w_in', 'delta_lru_b_in', 'delta_lru_conv_w', 'delta_lru_conv_b', 'delta_lru_gate_w', 'delta_lru_gate_b', 'delta_lru_lambda', 'delta_lru_w_out', 'delta_lru_b_out', 'delta_rwkv_norm', 'delta_rwkv_mix', 'delta_rwkv_w_rkv', 'delta_rwkv_w0', 'delta_rwkv_w1', 'delta_rwkv_w2', 'delta_rwkv_a0', 'delta_rwkv_a1', 'delta_rwkv_a2', 'delta_rwkv_g1', 'delta_rwkv_g2', 'delta_rwkv_k_k', 'delta_rwkv_k_a', 'delta_rwkv_r_k', 'delta_rwkv_ln_w', 'delta_rwkv_ln_b', 'delta_rwkv_w_out', 'delta_ffn_norm', 'delta_ffn_w_up', 'delta_ffn_conv_w', 'delta_ffn_conv_b', 'delta_ffn_w_down', 'delta_final_norm', 'new_m_lru_norm', 'new_m_lru_w_in', 'new_m_lru_b_in', 'new_m_lru_conv_w', 'new_m_lru_conv_b', 'new_m_lru_gate_w', 'new_m_lru_gate_b', 'new_m_lru_lambda', 'new_m_lru_w_out', 'new_m_lru_b_out', 'new_m_rwkv_norm', 'new_m_rwkv_mix', 'new_m_rwkv_w_rkv', 'new_m_rwkv_w0', 'new_m_rwkv_w1', 'new_m_rwkv_w2', 'new_m_rwkv_a0', 'new_m_rwkv_a1', 'new_m_rwkv_a2', 'new_m_rwkv_g1', 'new_m_rwkv_g2', 'new_m_rwkv_k_k', 'new_m_rwkv_k_a', 'new_m_rwkv_r_k', 'new_m_rwkv_ln_w', 'new_m_rwkv_ln_b', 'new_m_rwkv_w_out', 'new_m_ffn_norm', 'new_m_ffn_w_up', 'new_m_ffn_conv_w', 'new_m_ffn_conv_b', 'new_m_ffn_w_down', 'new_m_final_norm', 'new_v_lru_norm', 'new_v_lru_w_in', 'new_v_lru_b_in', 'new_v_lru_conv_w', 'new_v_lru_conv_b', 'new_v_lru_gate_w', 'new_v_lru_gate_b', 'new_v_lru_lambda', 'new_v_lru_w_out', 'new_v_lru_b_out', 'new_v_rwkv_norm', 'new_v_rwkv_mix', 'new_v_rwkv_w_rkv', 'new_v_rwkv_w0', 'new_v_rwkv_w1', 'new_v_rwkv_w2', 'new_v_rwkv_a0', 'new_v_rwkv_a1', 'new_v_rwkv_a2', 'new_v_rwkv_g1', 'new_v_rwkv_g2', 'new_v_rwkv_k_k', 'new_v_rwkv_k_a', 'new_v_rwkv_r_k', 'new_v_rwkv_ln_w', 'new_v_rwkv_ln_b', 'new_v_rwkv_w_out', 'new_v_ffn_norm', 'new_v_ffn_w_up', 'new_v_ffn_conv_w', 'new_v_ffn_conv_b', 'new_v_ffn_w_down', 'new_v_final_norm']
TWIN_LEAF_KINDS = {'loss': 'loss', 'grad_x': 'grad_x', 'grad_lru_norm': 'grad_w', 'grad_lru_w_in': 'grad_w', 'grad_lru_b_in': 'grad_w', 'grad_lru_conv_w': 'grad_w', 'grad_lru_conv_b': 'grad_w', 'grad_lru_gate_w': 'grad_w', 'grad_lru_gate_b': 'grad_w', 'grad_lru_lambda': 'grad_w', 'grad_lru_w_out': 'grad_w', 'grad_lru_b_out': 'grad_w', 'grad_rwkv_norm': 'grad_w', 'grad_rwkv_mix': 'grad_w', 'grad_rwkv_w_rkv': 'grad_w', 'grad_rwkv_w0': 'grad_w', 'grad_rwkv_w1': 'grad_w', 'grad_rwkv_w2': 'grad_w', 'grad_rwkv_a0': 'grad_w', 'grad_rwkv_a1': 'grad_w', 'grad_rwkv_a2': 'grad_w', 'grad_rwkv_g1': 'grad_w', 'grad_rwkv_g2': 'grad_w', 'grad_rwkv_k_k': 'grad_w', 'grad_rwkv_k_a': 'grad_w', 'grad_rwkv_r_k': 'grad_w', 'grad_rwkv_ln_w': 'grad_w', 'grad_rwkv_ln_b': 'grad_w', 'grad_rwkv_w_out': 'grad_w', 'grad_ffn_norm': 'grad_w', 'grad_ffn_w_up': 'grad_w', 'grad_ffn_conv_w': 'grad_w', 'grad_ffn_conv_b': 'grad_w', 'grad_ffn_w_down': 'grad_w', 'grad_final_norm': 'grad_w', 'delta_lru_norm': 'delta_w', 'delta_lru_w_in': 'delta_w', 'delta_lru_b_in': 'delta_w', 'delta_lru_conv_w': 'delta_w', 'delta_lru_conv_b': 'delta_w', 'delta_lru_gate_w': 'delta_w', 'delta_lru_gate_b': 'delta_w', 'delta_lru_lambda': 'delta_w', 'delta_lru_w_out': 'delta_w', 'delta_lru_b_out': 'delta_w', 'delta_rwkv_norm': 'delta_w', 'delta_rwkv_mix': 'delta_w', 'delta_rwkv_w_rkv': 'delta_w', 'delta_rwkv_w0': 'delta_w', 'delta_rwkv_w1': 'delta_w', 'delta_rwkv_w2': 'delta_w', 'delta_rwkv_a0': 'delta_w', 'delta_rwkv_a1': 'delta_w', 'delta_rwkv_a2': 'delta_w', 'delta_rwkv_g1': 'delta_w', 'delta_rwkv_g2': 'delta_w', 'delta_rwkv_k_k': 'delta_w', 'delta_rwkv_k_a': 'delta_w', 'delta_rwkv_r_k': 'delta_w', 'delta_rwkv_ln_w': 'delta_w', 'delta_rwkv_ln_b': 'delta_w', 'delta_rwkv_w_out': 'delta_w', 'delta_ffn_norm': 'delta_w', 'delta_ffn_w_up': 'delta_w', 'delta_ffn_conv_w': 'delta_w', 'delta_ffn_conv_b': 'delta_w', 'delta_ffn_w_down': 'delta_w', 'delta_final_norm': 'delta_w', 'new_m_lru_norm': 'new_m', 'new_m_lru_w_in': 'new_m', 'new_m_lru_b_in': 'new_m', 'new_m_lru_conv_w': 'new_m', 'new_m_lru_conv_b': 'new_m', 'new_m_lru_gate_w': 'new_m', 'new_m_lru_gate_b': 'new_m', 'new_m_lru_lambda': 'new_m', 'new_m_lru_w_out': 'new_m', 'new_m_lru_b_out': 'new_m', 'new_m_rwkv_norm': 'new_m', 'new_m_rwkv_mix': 'new_m', 'new_m_rwkv_w_rkv': 'new_m', 'new_m_rwkv_w0': 'new_m', 'new_m_rwkv_w1': 'new_m', 'new_m_rwkv_w2': 'new_m', 'new_m_rwkv_a0': 'new_m', 'new_m_rwkv_a1': 'new_m', 'new_m_rwkv_a2': 'new_m', 'new_m_rwkv_g1': 'new_m', 'new_m_rwkv_g2': 'new_m', 'new_m_rwkv_k_k': 'new_m', 'new_m_rwkv_k_a': 'new_m', 'new_m_rwkv_r_k': 'new_m', 'new_m_rwkv_ln_w': 'new_m', 'new_m_rwkv_ln_b': 'new_m', 'new_m_rwkv_w_out': 'new_m', 'new_m_ffn_norm': 'new_m', 'new_m_ffn_w_up': 'new_m', 'new_m_ffn_conv_w': 'new_m', 'new_m_ffn_conv_b': 'new_m', 'new_m_ffn_w_down': 'new_m', 'new_m_final_norm': 'new_m', 'new_v_lru_norm': 'new_v', 'new_v_lru_w_in': 'new_v', 'new_v_lru_b_in': 'new_v', 'new_v_lru_conv_w': 'new_v', 'new_v_lru_conv_b': 'new_v', 'new_v_lru_gate_w': 'new_v', 'new_v_lru_gate_b': 'new_v', 'new_v_lru_lambda': 'new_v', 'new_v_lru_w_out': 'new_v', 'new_v_lru_b_out': 'new_v', 'new_v_rwkv_norm': 'new_v', 'new_v_rwkv_mix': 'new_v', 'new_v_rwkv_w_rkv': 'new_v', 'new_v_rwkv_w0': 'new_v', 'new_v_rwkv_w1': 'new_v', 'new_v_rwkv_w2': 'new_v', 'new_v_rwkv_a0': 'new_v', 'new_v_rwkv_a1': 'new_v', 'new_v_rwkv_a2': 'new_v', 'new_v_rwkv_g1': 'new_v', 'new_v_rwkv_g2': 'new_v', 'new_v_rwkv_k_k': 'new_v', 'new_v_rwkv_k_a': 'new_v', 'new_v_rwkv_r_k': 'new_v', 'new_v_rwkv_ln_w': 'new_v', 'new_v_rwkv_ln_b': 'new_v', 'new_v_rwkv_w_out': 'new_v', 'new_v_ffn_norm': 'new_v', 'new_v_ffn_w_up': 'new_v', 'new_v_ffn_conv_w': 'new_v', 'new_v_ffn_conv_b': 'new_v', 'new_v_ffn_w_down': 'new_v', 'new_v_final_norm': 'new_v'}


def _forward(args):
    return _fwd_reference(*[args[k] for k in FWD_PARAMS])


def _output_shape():
    out = _jax.eval_shape(lambda: _forward(_fwd_setup_inputs(0)))
    return out.shape, out.dtype

N_MICROBATCH = 1
ADAM_LR = 0.001
ADAM_B1 = 0.9
ADAM_B2 = 0.999
ADAM_EPS = 1e-08
ADAM_WD = 0.01
ADAM_STEP = 10
PER_EXAMPLE_BATCH_AXIS = {'x': 0, 'loss_target': 0}
SHARED_INPUTS = []
_WEIGHT_DTYPES = {'lru_norm': _jnp.float32, 'lru_w_in': _jnp.float32, 'lru_b_in': _jnp.float32, 'lru_conv_w': _jnp.float32, 'lru_conv_b': _jnp.float32, 'lru_gate_w': _jnp.float32, 'lru_gate_b': _jnp.float32, 'lru_lambda': _jnp.float32, 'lru_w_out': _jnp.float32, 'lru_b_out': _jnp.float32, 'rwkv_norm': _jnp.float32, 'rwkv_mix': _jnp.float32, 'rwkv_w_rkv': _jnp.float32, 'rwkv_w0': _jnp.float32, 'rwkv_w1': _jnp.float32, 'rwkv_w2': _jnp.float32, 'rwkv_a0': _jnp.float32, 'rwkv_a1': _jnp.float32, 'rwkv_a2': _jnp.float32, 'rwkv_g1': _jnp.float32, 'rwkv_g2': _jnp.float32, 'rwkv_k_k': _jnp.float32, 'rwkv_k_a': _jnp.float32, 'rwkv_r_k': _jnp.float32, 'rwkv_ln_w': _jnp.float32, 'rwkv_ln_b': _jnp.float32, 'rwkv_w_out': _jnp.float32, 'ffn_norm': _jnp.float32, 'ffn_w_up': _jnp.float32, 'ffn_conv_w': _jnp.float32, 'ffn_conv_b': _jnp.float32, 'ffn_w_down': _jnp.float32, 'final_norm': _jnp.float32}
MOMENT_SCALE = {'lru_norm': 2.845906e-01, 'lru_w_in': 1.670488e-01, 'lru_b_in': 1.742981e+00, 'lru_conv_w': 1.937654e-01, 'lru_conv_b': 2.023167e+00, 'lru_gate_w': 1.290440e-01, 'lru_gate_b': 5.572428e-02, 'lru_lambda': 9.997371e-02, 'lru_w_out': 1.904317e-01, 'lru_b_out': 3.843621e-01, 'rwkv_norm': 2.105610e-01, 'rwkv_mix': 1.366190e-01, 'rwkv_w_rkv': 1.170603e-01, 'rwkv_w0': 6.363984e-02, 'rwkv_w1': 3.529819e-03, 'rwkv_w2': 7.903053e-03, 'rwkv_a0': 5.082659e-02, 'rwkv_a1': 1.853915e-02, 'rwkv_a2': 4.586994e-02, 'rwkv_g1': 9.378030e-02, 'rwkv_g2': 1.094575e-01, 'rwkv_k_k': 1.555560e-01, 'rwkv_k_a': 1.226093e-01, 'rwkv_r_k': 2.436730e-01, 'rwkv_ln_w': 1.057286e-01, 'rwkv_ln_b': 1.282916e-01, 'rwkv_w_out': 1.100031e-01, 'ffn_norm': 2.030291e-01, 'ffn_w_up': 8.348310e-02, 'ffn_conv_w': 8.362823e-02, 'ffn_conv_b': 8.059861e-02, 'ffn_w_down': 1.422207e-01, 'final_norm': 6.404050e+01}


def _to_microbatches(a, axis):
    t = _jnp.moveaxis(a, axis, 0)
    t = t.reshape((N_MICROBATCH, t.shape[0] // N_MICROBATCH) + t.shape[1:])
    return _jnp.moveaxis(t, 1, axis + 1)


def setup_inputs(seed: int = 0) -> dict:
    inp = _fwd_setup_inputs(seed)
    key = _jax.random.fold_in(_jax.random.key(seed), 7919)
    shape, _ = _output_shape()
    out = dict(inp)
    out["loss_target"] = _jax.random.normal(_jax.random.fold_in(key, 0), shape, _jnp.float32)
    for i, name in enumerate(TWIN_WEIGHTS):
        w = inp[name].astype(_jnp.float32)
        if MOMENT_SCALE is None:
            s = _jnp.sqrt(_jnp.mean(_jnp.square(w)) + 1e-30)
        else:
            s = MOMENT_SCALE[name]
        km, kv = _jax.random.split(_jax.random.fold_in(key, i + 1))
        out[name] = w
        out["m_" + name] = s * _jax.random.normal(km, w.shape, _jnp.float32)
        out["v_" + name] = (s * s) * _jax.random.uniform(kv, w.shape, _jnp.float32, 0.5, 1.5)
    if N_MICROBATCH > 1:
        for name, axis in PER_EXAMPLE_BATCH_AXIS.items():
            out[name] = _to_microbatches(out[name], axis)
    return {'x': out['x'], 'lru_norm': out['lru_norm'], 'lru_w_in': out['lru_w_in'], 'lru_b_in': out['lru_b_in'], 'lru_conv_w': out['lru_conv_w'], 'lru_conv_b': out['lru_conv_b'], 'lru_gate_w': out['lru_gate_w'], 'lru_gate_b': out['lru_gate_b'], 'lru_lambda': out['lru_lambda'], 'lru_w_out': out['lru_w_out'], 'lru_b_out': out['lru_b_out'], 'rwkv_norm': out['rwkv_norm'], 'rwkv_mix': out['rwkv_mix'], 'rwkv_w_rkv': out['rwkv_w_rkv'], 'rwkv_w0': out['rwkv_w0'], 'rwkv_w1': out['rwkv_w1'], 'rwkv_w2': out['rwkv_w2'], 'rwkv_a0': out['rwkv_a0'], 'rwkv_a1': out['rwkv_a1'], 'rwkv_a2': out['rwkv_a2'], 'rwkv_g1': out['rwkv_g1'], 'rwkv_g2': out['rwkv_g2'], 'rwkv_k_k': out['rwkv_k_k'], 'rwkv_k_a': out['rwkv_k_a'], 'rwkv_r_k': out['rwkv_r_k'], 'rwkv_ln_w': out['rwkv_ln_w'], 'rwkv_ln_b': out['rwkv_ln_b'], 'rwkv_w_out': out['rwkv_w_out'], 'ffn_norm': out['ffn_norm'], 'ffn_w_up': out['ffn_w_up'], 'ffn_conv_w': out['ffn_conv_w'], 'ffn_conv_b': out['ffn_conv_b'], 'ffn_w_down': out['ffn_w_down'], 'final_norm': out['final_norm'], 'loss_target': out['loss_target'], 'm_lru_norm': out['m_lru_norm'], 'm_lru_w_in': out['m_lru_w_in'], 'm_lru_b_in': out['m_lru_b_in'], 'm_lru_conv_w': out['m_lru_conv_w'], 'm_lru_conv_b': out['m_lru_conv_b'], 'm_lru_gate_w': out['m_lru_gate_w'], 'm_lru_gate_b': out['m_lru_gate_b'], 'm_lru_lambda': out['m_lru_lambda'], 'm_lru_w_out': out['m_lru_w_out'], 'm_lru_b_out': out['m_lru_b_out'], 'm_rwkv_norm': out['m_rwkv_norm'], 'm_rwkv_mix': out['m_rwkv_mix'], 'm_rwkv_w_rkv': out['m_rwkv_w_rkv'], 'm_rwkv_w0': out['m_rwkv_w0'], 'm_rwkv_w1': out['m_rwkv_w1'], 'm_rwkv_w2': out['m_rwkv_w2'], 'm_rwkv_a0': out['m_rwkv_a0'], 'm_rwkv_a1': out['m_rwkv_a1'], 'm_rwkv_a2': out['m_rwkv_a2'], 'm_rwkv_g1': out['m_rwkv_g1'], 'm_rwkv_g2': out['m_rwkv_g2'], 'm_rwkv_k_k': out['m_rwkv_k_k'], 'm_rwkv_k_a': out['m_rwkv_k_a'], 'm_rwkv_r_k': out['m_rwkv_r_k'], 'm_rwkv_ln_w': out['m_rwkv_ln_w'], 'm_rwkv_ln_b': out['m_rwkv_ln_b'], 'm_rwkv_w_out': out['m_rwkv_w_out'], 'm_ffn_norm': out['m_ffn_norm'], 'm_ffn_w_up': out['m_ffn_w_up'], 'm_ffn_conv_w': out['m_ffn_conv_w'], 'm_ffn_conv_b': out['m_ffn_conv_b'], 'm_ffn_w_down': out['m_ffn_w_down'], 'm_final_norm': out['m_final_norm'], 'v_lru_norm': out['v_lru_norm'], 'v_lru_w_in': out['v_lru_w_in'], 'v_lru_b_in': out['v_lru_b_in'], 'v_lru_conv_w': out['v_lru_conv_w'], 'v_lru_conv_b': out['v_lru_conv_b'], 'v_lru_gate_w': out['v_lru_gate_w'], 'v_lru_gate_b': out['v_lru_gate_b'], 'v_lru_lambda': out['v_lru_lambda'], 'v_lru_w_out': out['v_lru_w_out'], 'v_lru_b_out': out['v_lru_b_out'], 'v_rwkv_norm': out['v_rwkv_norm'], 'v_rwkv_mix': out['v_rwkv_mix'], 'v_rwkv_w_rkv': out['v_rwkv_w_rkv'], 'v_rwkv_w0': out['v_rwkv_w0'], 'v_rwkv_w1': out['v_rwkv_w1'], 'v_rwkv_w2': out['v_rwkv_w2'], 'v_rwkv_a0': out['v_rwkv_a0'], 'v_rwkv_a1': out['v_rwkv_a1'], 'v_rwkv_a2': out['v_rwkv_a2'], 'v_rwkv_g1': out['v_rwkv_g1'], 'v_rwkv_g2': out['v_rwkv_g2'], 'v_rwkv_k_k': out['v_rwkv_k_k'], 'v_rwkv_k_a': out['v_rwkv_k_a'], 'v_rwkv_r_k': out['v_rwkv_r_k'], 'v_rwkv_ln_w': out['v_rwkv_ln_w'], 'v_rwkv_ln_b': out['v_rwkv_ln_b'], 'v_rwkv_w_out': out['v_rwkv_w_out'], 'v_ffn_norm': out['v_ffn_norm'], 'v_ffn_w_up': out['v_ffn_w_up'], 'v_ffn_conv_w': out['v_ffn_conv_w'], 'v_ffn_conv_b': out['v_ffn_conv_b'], 'v_ffn_w_down': out['v_ffn_w_down'], 'v_final_norm': out['v_final_norm']}


def _loss(weights, diff, rest, loss_target):
    with _jax.named_scope("forward"):
        args = {**rest, TWIN_DIFF_INPUT: diff, **{k: w.astype(_WEIGHT_DTYPES[k]) for k, w in weights.items()}}
        y = _forward(args)
    with _jax.named_scope("loss_head"):
        err = _jnp.square(y.astype(_jnp.float32) - loss_target)
        return 0.5 * _jnp.sum(_jnp.mean(err, axis=-1)) if err.ndim else 0.5 * err


def _adamw(w, g, m, v):
    m = ADAM_B1 * m + (1.0 - ADAM_B1) * g
    v = ADAM_B2 * v + (1.0 - ADAM_B2) * _jnp.square(g)
    m_hat = m / (1.0 - ADAM_B1 ** ADAM_STEP)
    v_hat = v / (1.0 - ADAM_B2 ** ADAM_STEP)
    delta = -ADAM_LR * (m_hat / (_jnp.sqrt(v_hat) + ADAM_EPS) + ADAM_WD * w)
    return delta, m, v


def reference(x, lru_norm, lru_w_in, lru_b_in, lru_conv_w, lru_conv_b, lru_gate_w, lru_gate_b, lru_lambda, lru_w_out, lru_b_out, rwkv_norm, rwkv_mix, rwkv_w_rkv, rwkv_w0, rwkv_w1, rwkv_w2, rwkv_a0, rwkv_a1, rwkv_a2, rwkv_g1, rwkv_g2, rwkv_k_k, rwkv_k_a, rwkv_r_k, rwkv_ln_w, rwkv_ln_b, rwkv_w_out, ffn_norm, ffn_w_up, ffn_conv_w, ffn_conv_b, ffn_w_down, final_norm, loss_target, m_lru_norm, m_lru_w_in, m_lru_b_in, m_lru_conv_w, m_lru_conv_b, m_lru_gate_w, m_lru_gate_b, m_lru_lambda, m_lru_w_out, m_lru_b_out, m_rwkv_norm, m_rwkv_mix, m_rwkv_w_rkv, m_rwkv_w0, m_rwkv_w1, m_rwkv_w2, m_rwkv_a0, m_rwkv_a1, m_rwkv_a2, m_rwkv_g1, m_rwkv_g2, m_rwkv_k_k, m_rwkv_k_a, m_rwkv_r_k, m_rwkv_ln_w, m_rwkv_ln_b, m_rwkv_w_out, m_ffn_norm, m_ffn_w_up, m_ffn_conv_w, m_ffn_conv_b, m_ffn_w_down, m_final_norm, v_lru_norm, v_lru_w_in, v_lru_b_in, v_lru_conv_w, v_lru_conv_b, v_lru_gate_w, v_lru_gate_b, v_lru_lambda, v_lru_w_out, v_lru_b_out, v_rwkv_norm, v_rwkv_mix, v_rwkv_w_rkv, v_rwkv_w0, v_rwkv_w1, v_rwkv_w2, v_rwkv_a0, v_rwkv_a1, v_rwkv_a2, v_rwkv_g1, v_rwkv_g2, v_rwkv_k_k, v_rwkv_k_a, v_rwkv_r_k, v_rwkv_ln_w, v_rwkv_ln_b, v_rwkv_w_out, v_ffn_norm, v_ffn_w_up, v_ffn_conv_w, v_ffn_conv_b, v_ffn_w_down, v_final_norm):
    given = dict(x=x, lru_norm=lru_norm, lru_w_in=lru_w_in, lru_b_in=lru_b_in, lru_conv_w=lru_conv_w, lru_conv_b=lru_conv_b, lru_gate_w=lru_gate_w, lru_gate_b=lru_gate_b, lru_lambda=lru_lambda, lru_w_out=lru_w_out, lru_b_out=lru_b_out, rwkv_norm=rwkv_norm, rwkv_mix=rwkv_mix, rwkv_w_rkv=rwkv_w_rkv, rwkv_w0=rwkv_w0, rwkv_w1=rwkv_w1, rwkv_w2=rwkv_w2, rwkv_a0=rwkv_a0, rwkv_a1=rwkv_a1, rwkv_a2=rwkv_a2, rwkv_g1=rwkv_g1, rwkv_g2=rwkv_g2, rwkv_k_k=rwkv_k_k, rwkv_k_a=rwkv_k_a, rwkv_r_k=rwkv_r_k, rwkv_ln_w=rwkv_ln_w, rwkv_ln_b=rwkv_ln_b, rwkv_w_out=rwkv_w_out, ffn_norm=ffn_norm, ffn_w_up=ffn_w_up, ffn_conv_w=ffn_conv_w, ffn_conv_b=ffn_conv_b, ffn_w_down=ffn_w_down, final_norm=final_norm, loss_target=loss_target, m_lru_norm=m_lru_norm, m_lru_w_in=m_lru_w_in, m_lru_b_in=m_lru_b_in, m_lru_conv_w=m_lru_conv_w, m_lru_conv_b=m_lru_conv_b, m_lru_gate_w=m_lru_gate_w, m_lru_gate_b=m_lru_gate_b, m_lru_lambda=m_lru_lambda, m_lru_w_out=m_lru_w_out, m_lru_b_out=m_lru_b_out, m_rwkv_norm=m_rwkv_norm, m_rwkv_mix=m_rwkv_mix, m_rwkv_w_rkv=m_rwkv_w_rkv, m_rwkv_w0=m_rwkv_w0, m_rwkv_w1=m_rwkv_w1, m_rwkv_w2=m_rwkv_w2, m_rwkv_a0=m_rwkv_a0, m_rwkv_a1=m_rwkv_a1, m_rwkv_a2=m_rwkv_a2, m_rwkv_g1=m_rwkv_g1, m_rwkv_g2=m_rwkv_g2, m_rwkv_k_k=m_rwkv_k_k, m_rwkv_k_a=m_rwkv_k_a, m_rwkv_r_k=m_rwkv_r_k, m_rwkv_ln_w=m_rwkv_ln_w, m_rwkv_ln_b=m_rwkv_ln_b, m_rwkv_w_out=m_rwkv_w_out, m_ffn_norm=m_ffn_norm, m_ffn_w_up=m_ffn_w_up, m_ffn_conv_w=m_ffn_conv_w, m_ffn_conv_b=m_ffn_conv_b, m_ffn_w_down=m_ffn_w_down, m_final_norm=m_final_norm, v_lru_norm=v_lru_norm, v_lru_w_in=v_lru_w_in, v_lru_b_in=v_lru_b_in, v_lru_conv_w=v_lru_conv_w, v_lru_conv_b=v_lru_conv_b, v_lru_gate_w=v_lru_gate_w, v_lru_gate_b=v_lru_gate_b, v_lru_lambda=v_lru_lambda, v_lru_w_out=v_lru_w_out, v_lru_b_out=v_lru_b_out, v_rwkv_norm=v_rwkv_norm, v_rwkv_mix=v_rwkv_mix, v_rwkv_w_rkv=v_rwkv_w_rkv, v_rwkv_w0=v_rwkv_w0, v_rwkv_w1=v_rwkv_w1, v_rwkv_w2=v_rwkv_w2, v_rwkv_a0=v_rwkv_a0, v_rwkv_a1=v_rwkv_a1, v_rwkv_a2=v_rwkv_a2, v_rwkv_g1=v_rwkv_g1, v_rwkv_g2=v_rwkv_g2, v_rwkv_k_k=v_rwkv_k_k, v_rwkv_k_a=v_rwkv_k_a, v_rwkv_r_k=v_rwkv_r_k, v_rwkv_ln_w=v_rwkv_ln_w, v_rwkv_ln_b=v_rwkv_ln_b, v_rwkv_w_out=v_rwkv_w_out, v_ffn_norm=v_ffn_norm, v_ffn_w_up=v_ffn_w_up, v_ffn_conv_w=v_ffn_conv_w, v_ffn_conv_b=v_ffn_conv_b, v_ffn_w_down=v_ffn_w_down, v_final_norm=v_final_norm)
    weights = {n: given[n] for n in TWIN_WEIGHTS}
    shared = {n: given[n] for n in SHARED_INPUTS}
    per_example = {n: given[n] for n in ['x']}
    grad_fn = _jax.value_and_grad(_loss, argnums=(0, 1))

    def one_microbatch(ex, loss_target):
        ex = dict(ex)
        diff = ex.pop(TWIN_DIFF_INPUT)
        return grad_fn(weights, diff, {**shared, **ex}, loss_target)

    if N_MICROBATCH == 1:
        loss, (grad_w, grad_x) = one_microbatch(per_example, given["loss_target"])
    else:
        def body(carry, xs):
            loss_sum, grad_sum = carry
            l_k, (gw_k, gx_k) = one_microbatch(xs[0], xs[1])
            with _jax.named_scope("update"):
                return (loss_sum + l_k, _jax.tree.map(_jnp.add, grad_sum, gw_k)), gx_k

        init = (_jnp.zeros((), _jnp.float32), _jax.tree.map(_jnp.zeros_like, weights))
        (loss, grad_w), grad_x = _jax.lax.scan(body, init, (per_example, given["loss_target"]))
    with _jax.named_scope("update"):
        delta_w, new_m, new_v = {}, {}, {}
        for n in TWIN_WEIGHTS:
            delta_w[n], new_m[n], new_v[n] = _adamw(weights[n], grad_w[n], given["m_" + n], given["v_" + n])
    return (loss, grad_x, *[grad_w[n] for n in TWIN_WEIGHTS], *[delta_w[n] for n in TWIN_WEIGHTS],
            *[new_m[n] for n in TWIN_WEIGHTS], *[new_v[n] for n in TWIN_WEIGHTS])
```

```python
import functools
import math

import jax
import jax.numpy as jnp
from jax import lax
from jax.experimental import pallas as pl
from jax.experimental.pallas import tpu as pltpu

F32 = jnp.float32
BF16 = jnp.bfloat16
HI = lax.Precision.HIGHEST

N_DEV = 8
D = 1024
HEAD = 64
N_HEAD = D // HEAD
D_FF = 3 * D
LANE = 128
V7X_VMEM_BYTES = 64 * 1024 * 1024
VMEM_LIMIT = V7X_VMEM_BYTES - 8 * 1024 * 1024
CT = LANE
CHUNK = 64
SCAN_HEADS = 4
PACK_W = 1024
PACK_ROWS = 256

ADAM_LR, ADAM_B1, ADAM_B2, ADAM_EPS, ADAM_WD, ADAM_STEP = 0.001, 0.9, 0.999, 1e-08, 0.01, 10
RMS_EPS = 1e-6
GN_EPS = 64e-5
LRU_C = 8.0

WEIGHTS = ['lru_norm', 'lru_w_in', 'lru_b_in', 'lru_conv_w', 'lru_conv_b', 'lru_gate_w', 'lru_gate_b',
           'lru_lambda', 'lru_w_out', 'lru_b_out', 'rwkv_norm', 'rwkv_mix', 'rwkv_w_rkv', 'rwkv_w0', 'rwkv_w1',
           'rwkv_w2', 'rwkv_a0', 'rwkv_a1', 'rwkv_a2', 'rwkv_g1', 'rwkv_g2', 'rwkv_k_k', 'rwkv_k_a', 'rwkv_r_k',
           'rwkv_ln_w', 'rwkv_ln_b', 'rwkv_w_out', 'ffn_norm', 'ffn_w_up', 'ffn_conv_w', 'ffn_conv_b',
           'ffn_w_down', 'final_norm']
SHARD_AXIS = {'lru_w_in': 2, 'lru_conv_w': 2, 'lru_w_out': 1, 'rwkv_norm': 1, 'rwkv_mix': 2, 'rwkv_w_rkv': 2,
              'rwkv_w0': 1, 'rwkv_w1': 1, 'rwkv_w2': 2, 'rwkv_a0': 1, 'rwkv_a1': 1, 'rwkv_a2': 2, 'rwkv_g1': 1,
              'rwkv_g2': 2, 'rwkv_k_k': 1, 'rwkv_k_a': 1, 'rwkv_ln_w': 1, 'rwkv_ln_b': 1, 'rwkv_w_out': 1,
              'ffn_w_up': 2, 'ffn_conv_w': 2, 'ffn_w_down': 1}
GATHER_BF16 = ['lru_w_in', 'lru_w_out', 'rwkv_w_rkv', 'rwkv_w1', 'rwkv_a1', 'rwkv_g1', 'rwkv_w_out', 'ffn_w_up',
               'ffn_w_down']
GATHER_F32 = [n for n in WEIGHTS if n in SHARD_AXIS and n not in GATHER_BF16]
SHARDED = GATHER_BF16 + GATHER_F32
REPLICATED = [n for n in WEIGHTS if n not in SHARD_AXIS]


def _pcall(body, **kw):
    return pl.pallas_call(body, **kw)


def _params(n_grid):
    return pltpu.CompilerParams(dimension_semantics=("arbitrary",) * n_grid, vmem_limit_bytes=VMEM_LIMIT)


def _shift_rows(x, d, up):
    n = x.shape[0]
    idx = lax.broadcasted_iota(jnp.int32, x.shape, 0)
    if up:
        return jnp.where(idx < n - d, pltpu.roll(x, n - d, 0), 0.0)
    return jnp.where(idx >= d, pltpu.roll(x, d, 0), 0.0)


@functools.partial(jax.custom_vjp, nondiff_argnums=(1,))
def _shift_down(x, d):
    return _shift_rows(x, d, False)


def _shift_down_fwd(x, d):
    return _shift_rows(x, d, False), None


def _shift_down_bwd(d, _, g):
    return (_shift_rows(g, d, True),)


_shift_down.defvjp(_shift_down_fwd, _shift_down_bwd)


def _scan_doubling(a, b, up):
    n = a.shape[0]
    d = 1
    while d < n:
        b = b + a * _shift_rows(b, d, up)
        a = a * _shift_rows(a, d, up)
        d *= 2
    return b


@jax.custom_vjp
def _linear_scan(a, b):
    return _scan_doubling(a, b, False)


def _linear_scan_fwd(a, b):
    h = _scan_doubling(a, b, False)
    return h, (a, h)


def _linear_scan_bwd(res, dh):
    a, h = res
    g = _scan_doubling(_shift_rows(a, 1, True), dh, True)
    return g * _shift_rows(h, 1, False), g


_linear_scan.defvjp(_linear_scan_fwd, _linear_scan_bwd)


def _causal_conv(x, w, b):
    k = w.shape[0]
    out = b + x * w[k - 1:k]
    for j in range(k - 1):
        out = out + _shift_down(x, k - 1 - j) * w[j:j + 1]
    return out


def _gelu(x):
    return jax.nn.gelu(x, approximate=True)


def _rmsnorm(x, g):
    return x * lax.rsqrt(jnp.mean(x * x, axis=-1, keepdims=True) + RMS_EPS) * g


def _neg_expm1(x):
    series = x * (1.0 + x * 0.5 * (1.0 + x * (1.0 / 3.0) * (1.0 + x * 0.25 * (1.0 + x * 0.2))))
    return -jnp.where(x > -0.1, series, jnp.exp(x) - 1.0)


def _dot(a, b, ca=1, cb=0, precision=None):
    return lax.dot_general(a, b, (((ca,), (cb,)), ((), ())), precision=precision, preferred_element_type=F32)


def _bdot(a, b):
    return _dot(a.astype(BF16), b.astype(BF16))


def _head_indicator():
    row = lax.broadcasted_iota(jnp.int32, (D, LANE), 0)
    col = lax.broadcasted_iota(jnp.int32, (D, LANE), 1)
    return (lax.shift_right_logical(row, 6) == col).astype(F32)


def _head_sum(x):
    e = _head_indicator()
    return _dot(_dot(x, e, precision=HI), e, 1, 1, precision=HI)


def _specs(items):
    return [pl.BlockSpec(bs, im) for (_, bs, im) in items]


def _stage_fwd(name, f, grid, ins, params, outs):
    n_in = len(ins) + len(params)

    def body(*refs):
        res = f(*[r[...] for r in refs[:n_in]])
        for o, v in zip(refs[n_in:], res):
            o[...] = v.astype(o.dtype)

    return _pcall(
        body, name=name, grid=grid, in_specs=_specs(ins + params),
        out_specs=[pl.BlockSpec(bs, im) for (_, _, bs, im) in outs],
        out_shape=[jax.ShapeDtypeStruct(s, dt) for (s, dt, _, _) in outs],
        compiler_params=_params(len(grid)),
    )(*[a for (a, _, _) in ins + params])


def _stage_bwd(name, f, grid, ins, params, douts, din_dtypes, adds=None):
    adds = adds or {}
    n_in, n_par = len(ins), len(params)
    dout_items = [(a, bs, im) for (arrs, bs, im) in douts for a in arrs]
    add_items = [(adds[i], ins[i][1], ins[i][2]) for i in sorted(adds)]
    n_do, n_add = len(dout_items), len(add_items)

    def body(*refs):
        vals = [r[...] for r in refs[:n_in + n_par]]
        do_refs = list(refs[n_in + n_par:n_in + n_par + n_do])
        add_refs = dict(zip(sorted(adds), refs[n_in + n_par + n_do:n_in + n_par + n_do + n_add]))
        din_refs = refs[n_in + n_par + n_do + n_add:n_in + n_par + n_do + n_add + n_in]
        dpar_refs = refs[n_in + n_par + n_do + n_add + n_in:]
        cts = []
        for (arrs, _, _) in douts:
            ct = do_refs.pop(0)[...].astype(F32)
            for _ in arrs[1:]:
                ct = ct + do_refs.pop(0)[...].astype(F32)
            cts.append(ct)
        _, vjp = jax.vjp(f, *vals)
        grads = vjp(tuple(cts))
        for i, r in enumerate(din_refs):
            g = grads[i]
            if i in add_refs:
                g = g + add_refs[i][...]
            r[...] = g.astype(r.dtype)

        @pl.when(pl.program_id(len(grid) - 1) == 0)
        def _():
            for r in dpar_refs:
                r[...] = jnp.zeros(r.shape, r.dtype)

        for j, r in enumerate(dpar_refs):
            r[...] += grads[n_in + j]

    res = _pcall(
        body, name=name, grid=grid, in_specs=_specs(ins + params + dout_items + add_items),
        out_specs=_specs(ins + params),
        out_shape=[jax.ShapeDtypeStruct(a.shape, dt) for (a, _, _), dt in zip(ins, din_dtypes)]
        + [jax.ShapeDtypeStruct(a.shape, F32) for (a, _, _) in params],
        compiler_params=_params(len(grid)),
    )(*[a for (a, _, _) in ins + params + dout_items + add_items])
    return list(res[:n_in]), list(res[n_in:])


def _tile(n, want):
    t = min(n, want)
    while n % t:
        t //= 2
    return t


def _matmul(name, a, b, *, mode="nn", resid=None, out_dtype=F32, tm=512, tn=512, tk=1024):
    if mode == "tn":
        (kdim, m), n = a.shape, b.shape[1]
    else:
        (m, kdim), n = a.shape, (b.shape[1] if mode == "nn" else b.shape[0])
    tm, tn, tk = _tile(m, tm), _tile(n, tn), _tile(kdim, tk)
    nk = kdim // tk
    a_spec = pl.BlockSpec((tk, tm), lambda i, j, k: (k, i)) if mode == "tn" else pl.BlockSpec((tm, tk), lambda i, j, k: (i, k))
    b_spec = pl.BlockSpec((tn, tk), lambda i, j, k: (j, k)) if mode == "nt" else pl.BlockSpec((tk, tn), lambda i, j, k: (k, j))
    ca = 0 if mode == "tn" else 1
    cb = 1 if mode == "nt" else 0
    operands = [a, b]
    in_specs = [a_spec, b_spec]
    if resid is not None:
        operands.append(resid)
        in_specs.append(pl.BlockSpec((tm, tn), lambda i, j, k: (i, j)))

    def body(*refs):
        a_ref, b_ref = refs[0], refs[1]
        o_ref, acc_ref = refs[-2], refs[-1]
        k = pl.program_id(2)

        @pl.when(k == 0)
        def _():
            acc_ref[...] = jnp.zeros(acc_ref.shape, F32)

        acc_ref[...] += _dot(a_ref[...].astype(BF16), b_ref[...].astype(BF16), ca, cb)

        @pl.when(k == nk - 1)
        def _():
            r = acc_ref[...]
            if resid is not None:
                r = r + refs[2][...]
            o_ref[...] = r.astype(o_ref.dtype)

    return _pcall(
        body, name=name, grid=(m // tm, n // tn, nk), in_specs=in_specs,
        out_specs=pl.BlockSpec((tm, tn), lambda i, j, k: (i, j)),
        out_shape=jax.ShapeDtypeStruct((m, n), out_dtype),
        scratch_shapes=[pltpu.VMEM((tm, tn), F32)],
        compiler_params=_params(3),
    )(*operands)


def _add(name, a, b):
    rows, cols = a.shape
    tr = _tile(rows, 512)

    def body(a_ref, b_ref, o_ref):
        o_ref[...] = a_ref[...] + b_ref[...]

    spec = pl.BlockSpec((tr, cols), lambda i: (i, 0))
    return _pcall(body, name=name, grid=(rows // tr,), in_specs=[spec, spec], out_specs=spec,
                  out_shape=jax.ShapeDtypeStruct(a.shape, a.dtype), compiler_params=_params(1))(a, b)


def _f_lru_pre(x, norm, b_out):
    return _rmsnorm(x, norm), x + b_out


def _f_lru_core(u, b_y, b_x, cw, cb, gw, gb, lam):
    yb = _gelu(u[:, :CT] + b_y)
    xr = _causal_conv(u[:, CT:] + b_x, cw, cb)
    gr = jax.nn.sigmoid(_bdot(xr, gw[0, 0]) + gb[0:1])
    gi = jax.nn.sigmoid(_bdot(xr, gw[1, 0]) + gb[1:2])
    log_a = -LRU_C * gr * jax.nn.softplus(-lam)
    a = jnp.exp(log_a)
    bterm = jnp.sqrt(_neg_expm1(2.0 * log_a)) * (gi * xr)
    return (_linear_scan(a, bterm) * yb,)


def _f_norm(x, norm):
    return (_rmsnorm(x, norm),)


def _f_ffn_core(u, cw, cb):
    return (_gelu(_causal_conv(u[:, :CT], cw, cb)) * u[:, CT:],)


def _f_rwkv_mix(h, mix):
    xx = _shift_down(h, 1) - h
    return tuple(h + xx * mix[i:i + 1] for i in range(6))


def _f_rwkv_pre(k, lw1, la1, lg1, w0, a0, k_k, k_a, w2, a2, g2):
    wpre = w0 + _bdot(jnp.tanh(lw1), w2)
    apre = a0 + _bdot(la1, a2)
    g = _bdot(jax.nn.sigmoid(lg1), g2)
    log_decay = -jnp.exp(-jax.nn.softplus(-wpre) - 0.5)
    a = jax.nn.sigmoid(apre)
    kk = k * k_k
    kk = kk / jnp.maximum(jnp.sqrt(_head_sum(kk * kk)), 1e-12)
    kf = k * (1.0 + (a - 1.0) * k_a)
    return log_decay, kf, -kk, kk * a, g


def _f_rwkv_post(y, r, kf, v, g, ln_w, ln_b, r_k):
    inv = 1.0 / HEAD
    yc = y - _head_sum(y) * inv
    var = _head_sum(yc * yc) * inv
    yn = yc * lax.rsqrt(var + GN_EPS) * ln_w + ln_b
    bonus = _head_sum(r * kf * r_k) * v
    return ((yn + bonus) * g,)


def _rwkv_chunk(z0, r, lw, k, v, a, b):
    n = r.shape[0]
    row = lax.broadcasted_iota(jnp.int32, (n, n), 0)
    col = lax.broadcasted_iota(jnp.int32, (n, n), 1)
    incl, strict = row >= col, row > col
    cs = _dot(incl.astype(F32), lw, precision=HI)
    c_last = cs[n - 1:n]
    rt = r * jnp.exp(cs)
    at = a * jnp.exp(cs - lw)
    inv = jnp.exp(-cs)
    bt, kt = b * inv, k * inv
    a_ab = jnp.where(strict, _dot(at, bt, 1, 1, HI), 0.0)
    a_ak = jnp.where(strict, _dot(at, kt, 1, 1, HI), 0.0)
    a_rb = jnp.where(incl, _dot(rt, bt, 1, 1, HI), 0.0)
    a_rk = jnp.where(incl, _dot(rt, kt, 1, 1, HI), 0.0)
    inv_m = (row == col).astype(F32) + a_ab
    p = a_ab
    for _ in range(int(math.log2(n)) - 1):
        p = _dot(p, p, precision=HI)
        inv_m = inv_m + _dot(inv_m, p, precision=HI)
    u = _dot(inv_m, _dot(at, z0, precision=HI) + _dot(a_ak, v, precision=HI), precision=HI)
    y = _dot(rt, z0, precision=HI) + _dot(a_rb, u, precision=HI) + _dot(a_rk, v, precision=HI)
    tail = jnp.exp(c_last - cs)
    er = lax.broadcasted_iota(jnp.int32, (HEAD, HEAD), 0)
    ec = lax.broadcasted_iota(jnp.int32, (HEAD, HEAD), 1)
    decay_all = jnp.where(er == ec, jnp.exp(c_last), 0.0)
    z_l = (_dot(decay_all, z0, precision=HI) + _dot(b * tail, u, 0, 0, HI) + _dot(k * tail, v, 0, 0, HI))
    return y, z_l


def _rwkv_scan_fwd(seqs):
    bh, t, _ = seqs[0].shape
    g, n = _tile(bh, SCAN_HEADS), _tile(t, CHUNK)

    def body(r_ref, lw_ref, k_ref, v_ref, a_ref, b_ref, y_ref, zs_ref, z_ref):
        @pl.when(pl.program_id(1) == 0)
        def _():
            z_ref[...] = jnp.zeros(z_ref.shape, F32)

        for i in range(g):
            z0 = z_ref[i]
            zs_ref[i, 0] = z0
            y, z_l = _rwkv_chunk(z0, r_ref[i], lw_ref[i], k_ref[i], v_ref[i], a_ref[i], b_ref[i])
            y_ref[i] = y
            z_ref[i] = z_l

    seq_spec = pl.BlockSpec((g, n, HEAD), lambda i, c: (i, c, 0))
    return _pcall(
        body, name="rwkv_scan_fwd", grid=(bh // g, t // n), in_specs=[seq_spec] * 6,
        out_specs=[seq_spec, pl.BlockSpec((g, 1, HEAD, HEAD), lambda i, c: (i, c, 0, 0))],
        out_shape=[jax.ShapeDtypeStruct((bh, t, HEAD), F32), jax.ShapeDtypeStruct((bh, t // n, HEAD, HEAD), F32)],
        scratch_shapes=[pltpu.VMEM((g, HEAD, HEAD), F32)],
        compiler_params=_params(2),
    )(*seqs)


def _rwkv_scan_bwd(seqs, zs, dy):
    bh, t, _ = seqs[0].shape
    g, n = _tile(bh, SCAN_HEADS), _tile(t, CHUNK)
    nc = t // n

    def body(r_ref, lw_ref, k_ref, v_ref, a_ref, b_ref, zs_ref, dy_ref, dr, dlw, dk, dv, da, db, dz_ref):
        @pl.when(pl.program_id(1) == 0)
        def _():
            dz_ref[...] = jnp.zeros(dz_ref.shape, F32)

        for i in range(g):
            _, vjp = jax.vjp(_rwkv_chunk, zs_ref[i, 0], r_ref[i], lw_ref[i], k_ref[i], v_ref[i], a_ref[i], b_ref[i])
            grads = vjp((dy_ref[i], dz_ref[i]))
            dz_ref[i] = grads[0]
            for o, gr in zip((dr, dlw, dk, dv, da, db), grads[1:]):
                o[i] = gr

    seq_spec = pl.BlockSpec((g, n, HEAD), lambda i, c: (i, nc - 1 - c, 0))
    return _pcall(
        body, name="rwkv_scan_bwd", grid=(bh // g, nc),
        in_specs=[seq_spec] * 6 + [pl.BlockSpec((g, 1, HEAD, HEAD), lambda i, c: (i, nc - 1 - c, 0, 0)), seq_spec],
        out_specs=[seq_spec] * 6,
        out_shape=[jax.ShapeDtypeStruct((bh, t, HEAD), F32)] * 6,
        scratch_shapes=[pltpu.VMEM((g, HEAD, HEAD), F32)],
        compiler_params=_params(2),
    )(*seqs, zs, dy)


def _loss_head(x, target, norm):
    n = x.shape[0]
    tr = _tile(n, 256)

    def f(xv, gv, tv):
        err = _rmsnorm(xv, gv) - tv
        return 0.5 * jnp.sum(jnp.mean(err * err, axis=-1, keepdims=True), axis=0, keepdims=True)

    def body(x_ref, t_ref, g_ref, dx_ref, dg_ref, loss_ref):
        val, vjp = jax.vjp(lambda xv, gv: f(xv, gv, t_ref[...]), x_ref[...], g_ref[...])
        dx, dg = vjp(jnp.ones((1, 1), F32))
        dx_ref[...] = dx

        @pl.when(pl.program_id(0) == 0)
        def _():
            dg_ref[...] = jnp.zeros(dg_ref.shape, F32)
            loss_ref[...] = jnp.zeros(loss_ref.shape, F32)

        dg_ref[...] += dg
        loss_ref[...] += jnp.broadcast_to(val, loss_ref.shape)

    row = pl.BlockSpec((tr, D), lambda i: (i, 0))
    vec = pl.BlockSpec((1, D), lambda i: (0, 0))
    dx, dg, loss = _pcall(
        body, name="loss_head", grid=(n // tr,), in_specs=[row, row, vec],
        out_specs=[row, vec, pl.BlockSpec((8, LANE), lambda i: (0, 0))],
        out_shape=[jax.ShapeDtypeStruct((n, D), F32), jax.ShapeDtypeStruct((1, D), F32),
                   jax.ShapeDtypeStruct((8, LANE), F32)],
        compiler_params=_params(1),
    )(x, target, norm)
    return loss[0, 0], dx, dg


def _exchange(name, src, scatter):
    shape = src.shape[1:] if scatter else src.shape

    def body(src_ref, dst_ref, send_sems, recv_sems, local_sem):
        x, y, c = lax.axis_index("x"), lax.axis_index("y"), lax.axis_index("c")
        me = 4 * x + 2 * y + c
        own = src_ref.at[me] if scatter else src_ref
        mine = pltpu.make_async_copy(own, dst_ref.at[me], local_sem)
        mine.start()
        copies = []
        for m in range(1, N_DEV):
            px = 1 - x if m & 4 else x
            py = 1 - y if m & 2 else y
            pc = 1 - c if m & 1 else c
            part = src_ref.at[4 * px + 2 * py + pc] if scatter else src_ref
            cp = pltpu.make_async_remote_copy(
                src_ref=part, dst_ref=dst_ref.at[me], send_sem=send_sems.at[m - 1], recv_sem=recv_sems.at[m - 1],
                device_id=(px, py, pc), device_id_type=pl.DeviceIdType.MESH)
            cp.start()
            copies.append(cp)
        for cp in copies:
            cp.wait_recv()
        for cp in copies:
            cp.wait_send()
        mine.wait()

    return _pcall(
        body, name=name, in_specs=[pl.BlockSpec(memory_space=pl.ANY)],
        out_specs=pl.BlockSpec(memory_space=pl.ANY),
        out_shape=jax.ShapeDtypeStruct((N_DEV,) + tuple(shape), src.dtype),
        scratch_shapes=[pltpu.SemaphoreType.DMA((N_DEV - 1,)), pltpu.SemaphoreType.DMA((N_DEV - 1,)),
                        pltpu.SemaphoreType.DMA],
    )(src)


def _adamw(name, parts, w, m, v):
    rows = w.shape[0]
    tr = _tile(rows, PACK_ROWS)

    def body(p_ref, w_ref, m_ref, v_ref, g_ref, d_ref, nm_ref, nv_ref):
        g = p_ref[0]
        for k in range(1, N_DEV):
            g = g + p_ref[k]
        nm = ADAM_B1 * m_ref[...] + (1.0 - ADAM_B1) * g
        nv = ADAM_B2 * v_ref[...] + (1.0 - ADAM_B2) * jnp.square(g)
        m_hat = nm / (1.0 - ADAM_B1 ** ADAM_STEP)
        v_hat = nv / (1.0 - ADAM_B2 ** ADAM_STEP)
        g_ref[...] = g
        d_ref[...] = -ADAM_LR * (m_hat / (jnp.sqrt(v_hat) + ADAM_EPS) + ADAM_WD * w_ref[...])
        nm_ref[...] = nm
        nv_ref[...] = nv

    row = pl.BlockSpec((tr, PACK_W), lambda i: (i, 0))
    return _pcall(
        body, name=name, grid=(rows // tr,),
        in_specs=[pl.BlockSpec((N_DEV, tr, PACK_W), lambda i: (0, i, 0)), row, row, row],
        out_specs=[row] * 4, out_shape=[jax.ShapeDtypeStruct((rows, PACK_W), F32)] * 4,
        compiler_params=_params(1),
    )(parts, w, m, v)


def _pack(arrs, dtype, lead=()):
    flat = jnp.concatenate([a.astype(dtype).reshape(lead + (-1,)) for a in arrs], axis=-1)
    n = flat.shape[-1]
    quantum = PACK_W * PACK_ROWS
    total = -(-n // quantum) * quantum
    flat = jnp.pad(flat, [(0, 0)] * len(lead) + [(0, total - n)])
    return flat.reshape(lead + (total // PACK_W, PACK_W))


def _unpack(buf, shapes, lead=()):
    flat = buf.reshape(lead + (-1,))
    out, off = [], 0
    for s in shapes:
        n = math.prod(s)
        out.append(flat[..., off:off + n].reshape(lead + tuple(s)))
        off += n
    return out


def _unshard(g, axis):
    local = g.shape[1:]
    full = jnp.moveaxis(g, 0, axis)
    return full.reshape(local[:axis] + (N_DEV * local[axis],) + local[axis + 1:])


def _reshard(full, axis):
    s = full.shape
    blocked = full.reshape(s[:axis] + (N_DEV, s[axis] // N_DEV) + s[axis + 1:])
    return jnp.moveaxis(blocked, axis, 0)


def _interleave(w, half):
    rows = w.shape[0]
    pair = jnp.stack([w[:, :half].reshape(rows, half // CT, CT), w[:, half:].reshape(rows, half // CT, CT)], axis=2)
    return pair.reshape(rows, 2 * half)


def _deinterleave(w, half):
    rows = w.shape[0]
    pair = w.reshape(rows, half // CT, 2, CT)
    return jnp.concatenate([pair[:, :, 0].reshape(rows, half), pair[:, :, 1].reshape(rows, half)], axis=1)


def _to_heads(x, bsz):
    t = x.shape[0] // bsz
    return x.reshape(bsz, t, N_HEAD, HEAD).transpose(0, 2, 1, 3).reshape(bsz * N_HEAD, t, HEAD)


def _from_heads(x, bsz):
    t = x.shape[1]
    return x.reshape(bsz, N_HEAD, t, HEAD).transpose(0, 2, 1, 3).reshape(bsz * t, D)


def _row_item(a, tr):
    return (a, (tr, a.shape[1]), lambda i: (i, 0))


def _vec_item(a):
    return (a, a.shape, lambda i: (0,) * a.ndim)


def _seq_item(a, t, width):
    return (a, (t, width), lambda j, b: (b, j))


def _col_item(a, width):
    return (a, (a.shape[0], width), lambda j, b: (0, j))


def _ffn_fwd(tag, x, p, t):
    n = x.shape[0]
    tr = _tile(n, 512)
    norm_ins, norm_par = [_row_item(x, tr)], [_vec_item(p["norm"])]
    (h,) = _stage_fwd(f"{tag}_norm", _f_norm, (n // tr,), norm_ins, norm_par, [((n, D), BF16, (tr, D), lambda i: (i, 0))])
    u = _matmul(f"{tag}_up", h, p["w_up"], tm=1024, tn=768)
    grid = (D_FF // CT, n // t)
    core_par = [_col_item(p["conv_w"], CT), _col_item(p["conv_b"], CT)]
    (hid,) = _stage_fwd(f"{tag}_core", _f_ffn_core, grid, [_seq_item(u, t, 2 * CT)], core_par,
                        [((n, D_FF), BF16, (t, CT), lambda j, b: (b, j))])
    y = _matmul(f"{tag}_down", hid, p["w_down"], resid=x, tm=1024, tn=1024)
    return y, (x, h, u, hid)


def _ffn_bwd(tag, dy, saved, p, t):
    x, h, u, hid = saved
    n = x.shape[0]
    tr = _tile(n, 512)
    grads = {}
    dhid = _matmul(f"{tag}_down_da", dy, p["w_down"], mode="nt", tm=1024, tn=768)
    grads["w_down"] = _matmul(f"{tag}_down_dw", hid, dy, mode="tn", tm=768, tn=1024)
    grid = (D_FF // CT, n // t)
    core_par = [_col_item(p["conv_w"], CT), _col_item(p["conv_b"], CT)]
    (du,), (grads["conv_w"], grads["conv_b"]) = _stage_bwd(
        f"{tag}_core_bwd", _f_ffn_core, grid, [_seq_item(u, t, 2 * CT)], core_par,
        [([dhid], (t, CT), lambda j, b: (b, j))], [BF16])
    dh = _matmul(f"{tag}_up_da", du, p["w_up"], mode="nt", tm=1024, tn=1024)
    grads["w_up"] = _matmul(f"{tag}_up_dw", h, du, mode="tn", tm=1024, tn=768)
    (dx,), (grads["norm"],) = _stage_bwd(
        f"{tag}_norm_bwd", _f_norm, (n // tr,), [_row_item(x, tr)], [_vec_item(p["norm"])],
        [([dh], (tr, D), lambda i: (i, 0))], [F32], adds={0: dy})
    return dx, grads


def _lru_fwd(x, p, t):
    n = x.shape[0]
    tr = _tile(n, 512)
    row = lambda dt: ((n, D), dt, (tr, D), lambda i: (i, 0))
    h, xb = _stage_fwd("lru_pre", _f_lru_pre, (n // tr,), [_row_item(x, tr)],
                       [_vec_item(p["norm"]), _vec_item(p["b_out"])], [row(BF16), row(F32)])
    u = _matmul("lru_in", h, p["w_in"], tm=1024, tn=1024)
    out, = _stage_fwd("lru_core", _f_lru_core, (D // CT, n // t), [_seq_item(u, t, 2 * CT)], _lru_core_params(p),
                      [((n, D), BF16, (t, CT), lambda j, b: (b, j))])
    y = _matmul("lru_out", out, p["w_out"], resid=xb, tm=1024, tn=1024)
    return y, (x, h, u, out)


def _lru_core_params(p):
    return [_col_item(p["b_y"], CT), _col_item(p["b_x"], CT), _col_item(p["conv_w"], CT), _col_item(p["conv_b"], CT),
            (p["gate_w"], (2, 1, CT, CT), lambda j, b: (0, j, 0, 0)), _col_item(p["gate_b"], CT), _col_item(p["lam"], CT)]


def _lru_bwd(dy, saved, p, t):
    x, h, u, out = saved
    n = x.shape[0]
    tr = _tile(n, 512)
    grads = {}
    dout = _matmul("lru_out_da", dy, p["w_out"], mode="nt", tm=1024, tn=1024)
    grads["w_out"] = _matmul("lru_out_dw", out, dy, mode="tn", tm=1024, tn=1024)
    (du,), core_grads = _stage_bwd("lru_core_bwd", _f_lru_core, (D // CT, n // t), [_seq_item(u, t, 2 * CT)],
                                   _lru_core_params(p), [([dout], (t, CT), lambda j, b: (b, j))], [BF16])
    for k, g in zip(("b_y", "b_x", "conv_w", "conv_b", "gate_w", "gate_b", "lam"), core_grads):
        grads[k] = g
    dh = _matmul("lru_in_da", du, p["w_in"], mode="nt", tm=1024, tn=1024)
    grads["w_in"] = _matmul("lru_in_dw", h, du, mode="tn", tm=1024, tn=1024)
    row = (tr, D), lambda i: (i, 0)
    (dx,), (grads["norm"], grads["b_out"]) = _stage_bwd(
        "lru_pre_bwd", _f_lru_pre, (n // tr,), [_row_item(x, tr)], [_vec_item(p["norm"]), _vec_item(p["b_out"])],
        [([dh], *row), ([dy], *row)], [F32])
    return dx, grads


_RWKV_PRE_VECS = ("w0", "a0", "k_k", "k_a", "w2", "a2", "g2")
_RWKV_POST_VECS = ("ln_w", "ln_b", "r_k")


def _rwkv_fwd(x, p, t):
    n = x.shape[0]
    bsz = n // t
    tr = _tile(n, 512)
    ts = _tile(n, 128)
    (h,) = _stage_fwd("rwkv_norm", _f_norm, (n // tr,), [_row_item(x, tr)], [_vec_item(p["norm"])],
                      [((n, D), F32, (tr, D), lambda i: (i, 0))])
    mixed = _stage_fwd("rwkv_mix", _f_rwkv_mix, (D // CT, bsz), [_seq_item(h, t, CT)], [_col_item(p["mix"], CT)],
                       [((n, D), BF16, (t, CT), lambda j, b: (b, j))] * 6)
    xr, xk, xv, xw, xa, xg = mixed
    r = _matmul("rwkv_r", xr, p["w_r"], tm=1024, tn=1024)
    k = _matmul("rwkv_k", xk, p["w_k"], tm=1024, tn=1024)
    v = _matmul("rwkv_v", xv, p["w_v"], tm=1024, tn=1024)
    lw1 = _matmul("rwkv_w1", xw, p["w1"], tm=1024)
    la1 = _matmul("rwkv_a1", xa, p["a1"], tm=1024)
    lg1 = _matmul("rwkv_g1", xg, p["g1"], tm=1024)
    pre_ins = [_row_item(a, ts) for a in (k, lw1, la1, lg1)]
    pre_par = [_vec_item(p[q]) for q in _RWKV_PRE_VECS]
    row = ((n, D), F32, (ts, D), lambda i: (i, 0))
    lw, kf, aa, bb, g = _stage_fwd("rwkv_pre", _f_rwkv_pre, (n // ts,), pre_ins, pre_par, [row] * 5)
    seqs = [_to_heads(a, bsz) for a in (r, lw, kf, v, aa, bb)]
    y_h, zs = _rwkv_scan_fwd(seqs)
    y = _from_heads(y_h, bsz)
    post_ins = [_row_item(a, ts) for a in (y, r, kf, v, g)]
    post_par = [_vec_item(p[q]) for q in _RWKV_POST_VECS]
    (yg,) = _stage_fwd("rwkv_post", _f_rwkv_post, (n // ts,), post_ins, post_par,
                       [((n, D), BF16, (ts, D), lambda i: (i, 0))])
    out = _matmul("rwkv_out", yg, p["w_out"], resid=x, tm=1024, tn=1024)
    return out, (x, h, mixed, r, k, v, lw1, la1, lg1, lw, kf, aa, bb, g, zs, y, yg)


def _rwkv_bwd(dout, saved, p, t):
    x, h, mixed, r, k, v, lw1, la1, lg1, lw, kf, aa, bb, g, zs, y, yg = saved
    xr, xk, xv, xw, xa, xg = mixed
    n = x.shape[0]
    bsz = n // t
    tr = _tile(n, 512)
    ts = _tile(n, 128)
    grads = {}
    row_s = (ts, D), lambda i: (i, 0)
    dyg = _matmul("rwkv_out_da", dout, p["w_out"], mode="nt", tm=1024, tn=1024)
    grads["w_out"] = _matmul("rwkv_out_dw", yg, dout, mode="tn", tm=1024, tn=1024)
    post_ins = [_row_item(a, ts) for a in (y, r, kf, v, g)]
    post_par = [_vec_item(p[q]) for q in _RWKV_POST_VECS]
    (dy, dr_a, dkf_a, dv_a, dg), post_grads = _stage_bwd(
        "rwkv_post_bwd", _f_rwkv_post, (n // ts,), post_ins, post_par, [([dyg], *row_s)], [F32] * 5)
    grads.update(zip(_RWKV_POST_VECS, post_grads))
    seqs = [_to_heads(a, bsz) for a in (r, lw, kf, v, aa, bb)]
    d_heads = _rwkv_scan_bwd(seqs, zs, _to_heads(dy, bsz))
    dr_b, dlw, dkf_b, dv_b, daa, dbb = [_from_heads(a, bsz) for a in d_heads]
    dr = _add("rwkv_dr_sum", dr_a, dr_b)
    dv = _add("rwkv_dv_sum", dv_a, dv_b)
    pre_ins = [_row_item(a, ts) for a in (k, lw1, la1, lg1)]
    pre_par = [_vec_item(p[q]) for q in _RWKV_PRE_VECS]
    (dk, dlw1, dla1, dlg1), pre_grads = _stage_bwd(
        "rwkv_pre_bwd", _f_rwkv_pre, (n // ts,), pre_ins, pre_par,
        [([dlw], *row_s), ([dkf_a, dkf_b], *row_s), ([daa], *row_s), ([dbb], *row_s), ([dg], *row_s)], [F32] * 4)
    grads.update(zip(_RWKV_PRE_VECS, pre_grads))
    dmixed = []
    for tag, xin, dz, wname in (("r", xr, dr, "w_r"), ("k", xk, dk, "w_k"), ("v", xv, dv, "w_v"),
                                ("w1", xw, dlw1, "w1"), ("a1", xa, dla1, "a1"), ("g1", xg, dlg1, "g1")):
        dmixed.append(_matmul(f"rwkv_{tag}_da", dz, p[wname], mode="nt", tm=1024, tn=1024))
        grads[wname] = _matmul(f"rwkv_{tag}_dw", xin, dz, mode="tn", tm=1024, tn=1024)
    seq_blk = (t, CT), lambda j, b: (b, j)
    (dh,), (grads["mix"],) = _stage_bwd(
        "rwkv_mix_bwd", _f_rwkv_mix, (D // CT, bsz), [_seq_item(h, t, CT)], [_col_item(p["mix"], CT)],
        [([a], *seq_blk) for a in dmixed], [F32])
    (dx,), (grads["norm"],) = _stage_bwd(
        "rwkv_norm_bwd", _f_norm, (n // tr,), [_row_item(x, tr)], [_vec_item(p["norm"])],
        [([dh], (tr, D), lambda i: (i, 0))], [F32], adds={0: dout})
    return dx, grads


def _block_diag_gates(gate_w):
    z = jnp.zeros((2, D // CT, HEAD, HEAD), gate_w.dtype)
    even, odd = gate_w[:, 0::2], gate_w[:, 1::2]
    top = jnp.concatenate([even, z], axis=-1)
    bot = jnp.concatenate([z, odd], axis=-1)
    return jnp.concatenate([top, bot], axis=-2)


def _gate_blocks_grad(dg):
    even, odd = dg[:, :, :HEAD, :HEAD], dg[:, :, HEAD:, HEAD:]
    return jnp.stack([even, odd], axis=2).reshape(2, N_HEAD, HEAD, HEAD)


def _step(local, x, target):
    bsz, t, _ = x.shape
    n = bsz * t
    w = {k: local[k] for k in WEIGHTS}

    bf_parts = _exchange("gather_bf16", _pack([w[k] for k in GATHER_BF16], BF16), False)
    f32_parts = _exchange("gather_f32", _pack([w[k] for k in GATHER_F32], F32), False)
    full = {}
    for names, buf in ((GATHER_BF16, bf_parts), (GATHER_F32, f32_parts)):
        for k, g in zip(names, _unpack(buf, [w[k].shape for k in names], (N_DEV,))):
            full[k] = _unshard(g, SHARD_AXIS[k])
    for k in REPLICATED:
        full[k] = w[k]

    lru = dict(norm=full["lru_norm"], b_out=full["lru_b_out"], w_in=_interleave(full["lru_w_in"][0], D),
               b_y=full["lru_b_in"][:, :D], b_x=full["lru_b_in"][:, D:], conv_w=full["lru_conv_w"][0],
               conv_b=full["lru_conv_b"], gate_w=_block_diag_gates(full["lru_gate_w"][0]),
               gate_b=full["lru_gate_b"][0].reshape(2, D), lam=full["lru_lambda"], w_out=full["lru_w_out"][0])
    ffn = [dict(norm=full["ffn_norm"][l:l + 1], w_up=_interleave(full["ffn_w_up"][l], D_FF),
                conv_w=full["ffn_conv_w"][l], conv_b=full["ffn_conv_b"][l:l + 1], w_down=full["ffn_w_down"][l])
           for l in range(2)]
    rwkv = dict(norm=full["rwkv_norm"], mix=full["rwkv_mix"][0], w_r=full["rwkv_w_rkv"][0, 0],
                w_k=full["rwkv_w_rkv"][0, 1], w_v=full["rwkv_w_rkv"][0, 2], w0=full["rwkv_w0"], w1=full["rwkv_w1"][0],
                w2=full["rwkv_w2"][0], a0=full["rwkv_a0"], a1=full["rwkv_a1"][0], a2=full["rwkv_a2"][0],
                g1=full["rwkv_g1"][0], g2=full["rwkv_g2"][0], k_k=full["rwkv_k_k"], k_a=full["rwkv_k_a"],
                r_k=full["rwkv_r_k"].reshape(1, D), ln_w=full["rwkv_ln_w"], ln_b=full["rwkv_ln_b"],
                w_out=full["rwkv_w_out"][0])

    x0 = x.reshape(n, D)
    x1, s_lru = _lru_fwd(x0, lru, t)
    x2, s_ffn0 = _ffn_fwd("ffn0", x1, ffn[0], t)
    x3, s_rwkv = _rwkv_fwd(x2, rwkv, t)
    x4, s_ffn1 = _ffn_fwd("ffn1", x3, ffn[1], t)
    loss, dx4, d_final = _loss_head(x4, target.reshape(n, D), full["final_norm"].reshape(1, D))
    dx3, g_ffn1 = _ffn_bwd("ffn1", dx4, s_ffn1, ffn[1], t)
    dx2, g_rwkv = _rwkv_bwd(dx3, s_rwkv, rwkv, t)
    dx1, g_ffn0 = _ffn_bwd("ffn0", dx2, s_ffn0, ffn[0], t)
    dx0, g_lru = _lru_bwd(dx1, s_lru, lru, t)

    g_ffn = [g_ffn0, g_ffn1]
    gfull = {
        "lru_norm": g_lru["norm"], "lru_w_in": _deinterleave(g_lru["w_in"], D)[None],
        "lru_b_in": jnp.concatenate([g_lru["b_y"], g_lru["b_x"]], axis=1), "lru_conv_w": g_lru["conv_w"][None],
        "lru_conv_b": g_lru["conv_b"], "lru_gate_w": _gate_blocks_grad(g_lru["gate_w"])[None],
        "lru_gate_b": g_lru["gate_b"].reshape(1, 2, N_HEAD, HEAD), "lru_lambda": g_lru["lam"],
        "lru_w_out": g_lru["w_out"][None], "lru_b_out": g_lru["b_out"],
        "rwkv_norm": g_rwkv["norm"], "rwkv_mix": g_rwkv["mix"][None],
        "rwkv_w_rkv": jnp.stack([g_rwkv["w_r"], g_rwkv["w_k"], g_rwkv["w_v"]])[None],
        "rwkv_w0": g_rwkv["w0"], "rwkv_w1": g_rwkv["w1"][None], "rwkv_w2": g_rwkv["w2"][None],
        "rwkv_a0": g_rwkv["a0"], "rwkv_a1": g_rwkv["a1"][None], "rwkv_a2": g_rwkv["a2"][None],
        "rwkv_g1": g_rwkv["g1"][None], "rwkv_g2": g_rwkv["g2"][None], "rwkv_k_k": g_rwkv["k_k"],
        "rwkv_k_a": g_rwkv["k_a"], "rwkv_r_k": g_rwkv["r_k"].reshape(1, N_HEAD, HEAD), "rwkv_ln_w": g_rwkv["ln_w"],
        "rwkv_ln_b": g_rwkv["ln_b"], "rwkv_w_out": g_rwkv["w_out"][None],
        "ffn_norm": jnp.concatenate([g["norm"] for g in g_ffn]),
        "ffn_w_up": jnp.stack([_deinterleave(g["w_up"], D_FF) for g in g_ffn]),
        "ffn_conv_w": jnp.stack([g["conv_w"] for g in g_ffn]),
        "ffn_conv_b": jnp.concatenate([g["conv_b"] for g in g_ffn]),
        "ffn_w_down": jnp.stack([g["w_down"] for g in g_ffn]),
        "final_norm": d_final.reshape(D),
    }

    out = {}
    sh_parts = _exchange("grad_scatter", _pack([_reshard(gfull[k], SHARD_AXIS[k]) for k in SHARDED], F32, (N_DEV,)), True)
    rp_parts = _exchange("grad_gather", _pack([gfull[k] for k in REPLICATED], F32), False)
    for tag, names, parts in (("adamw_sharded", SHARDED, sh_parts), ("adamw_replicated", REPLICATED, rp_parts)):
        packs = [_pack([local[pre + k] for k in names], F32) for pre in ("", "m_", "v_")]
        res = _adamw(tag, parts, *packs)
        for kind, buf in zip(("grad", "delta", "new_m", "new_v"), res):
            for k, a in zip(names, _unpack(buf, [w[k].shape for k in names])):
                out[(kind, k)] = a
    return loss, dx0.reshape(x.shape), out


def kernel(x, lru_norm, lru_w_in, lru_b_in, lru_conv_w, lru_conv_b, lru_gate_w, lru_gate_b, lru_lambda, lru_w_out, lru_b_out, rwkv_norm, rwkv_mix, rwkv_w_rkv, rwkv_w0, rwkv_w1, rwkv_w2, rwkv_a0, rwkv_a1, rwkv_a2, rwkv_g1, rwkv_g2, rwkv_k_k, rwkv_k_a, rwkv_r_k, rwkv_ln_w, rwkv_ln_b, rwkv_w_out, ffn_norm, ffn_w_up, ffn_conv_w, ffn_conv_b, ffn_w_down, final_norm, loss_target, m_lru_norm, m_lru_w_in, m_lru_b_in, m_lru_conv_w, m_lru_conv_b, m_lru_gate_w, m_lru_gate_b, m_lru_lambda, m_lru_w_out, m_lru_b_out, m_rwkv_norm, m_rwkv_mix, m_rwkv_w_rkv, m_rwkv_w0, m_rwkv_w1, m_rwkv_w2, m_rwkv_a0, m_rwkv_a1, m_rwkv_a2, m_rwkv_g1, m_rwkv_g2, m_rwkv_k_k, m_rwkv_k_a, m_rwkv_r_k, m_rwkv_ln_w, m_rwkv_ln_b, m_rwkv_w_out, m_ffn_norm, m_ffn_w_up, m_ffn_conv_w, m_ffn_conv_b, m_ffn_w_down, m_final_norm, v_lru_norm, v_lru_w_in, v_lru_b_in, v_lru_conv_w, v_lru_conv_b, v_lru_gate_w, v_lru_gate_b, v_lru_lambda, v_lru_w_out, v_lru_b_out, v_rwkv_norm, v_rwkv_mix, v_rwkv_w_rkv, v_rwkv_w0, v_rwkv_w1, v_rwkv_w2, v_rwkv_a0, v_rwkv_a1, v_rwkv_a2, v_rwkv_g1, v_rwkv_g2, v_rwkv_k_k, v_rwkv_k_a, v_rwkv_r_k, v_rwkv_ln_w, v_rwkv_ln_b, v_rwkv_w_out, v_ffn_norm, v_ffn_w_up, v_ffn_conv_w, v_ffn_conv_b, v_ffn_w_down, v_final_norm):
    args = locals()
    local = {(pre + k): args[pre + k] for pre in ("", "m_", "v_") for k in WEIGHTS}
    loss_local, grad_x, out = _step(local, x, loss_target)
    loss = lax.psum(loss_local, ("x", "y", "c"))
    return (loss, grad_x, *[out[(kind, k)] for kind in ("grad", "delta", "new_m", "new_v") for k in WEIGHTS])
```

```python
import functools
import math

import jax
import jax.numpy as jnp
from jax import lax
from jax.experimental import pallas as pl
from jax.experimental.pallas import tpu as pltpu

F32 = jnp.float32
BF16 = jnp.bfloat16
HI = lax.Precision.HIGHEST

N_DEV = 8
D = 1024
HEAD = 64
N_HEAD = D // HEAD
D_FF = 3 * D
LANE = 128
V7X_VMEM_BYTES = 64 * 1024 * 1024
VMEM_LIMIT = V7X_VMEM_BYTES - 8 * 1024 * 1024
CT = LANE
CHUNK = 64
SCAN_HEADS = 8
PACK_W = 1024
PACK_ROWS = 256

ADAM_LR, ADAM_B1, ADAM_B2, ADAM_EPS, ADAM_WD, ADAM_STEP = 0.001, 0.9, 0.999, 1e-08, 0.01, 10
RMS_EPS = 1e-6
GN_EPS = 64e-5
LRU_C = 8.0

WEIGHTS = ['lru_norm', 'lru_w_in', 'lru_b_in', 'lru_conv_w', 'lru_conv_b', 'lru_gate_w', 'lru_gate_b',
           'lru_lambda', 'lru_w_out', 'lru_b_out', 'rwkv_norm', 'rwkv_mix', 'rwkv_w_rkv', 'rwkv_w0', 'rwkv_w1',
           'rwkv_w2', 'rwkv_a0', 'rwkv_a1', 'rwkv_a2', 'rwkv_g1', 'rwkv_g2', 'rwkv_k_k', 'rwkv_k_a', 'rwkv_r_k',
           'rwkv_ln_w', 'rwkv_ln_b', 'rwkv_w_out', 'ffn_norm', 'ffn_w_up', 'ffn_conv_w', 'ffn_conv_b',
           'ffn_w_down', 'final_norm']
SHARD_AXIS = {'lru_w_in': 2, 'lru_conv_w': 2, 'lru_w_out': 1, 'rwkv_norm': 1, 'rwkv_mix': 2, 'rwkv_w_rkv': 2,
              'rwkv_w0': 1, 'rwkv_w1': 1, 'rwkv_w2': 2, 'rwkv_a0': 1, 'rwkv_a1': 1, 'rwkv_a2': 2, 'rwkv_g1': 1,
              'rwkv_g2': 2, 'rwkv_k_k': 1, 'rwkv_k_a': 1, 'rwkv_ln_w': 1, 'rwkv_ln_b': 1, 'rwkv_w_out': 1,
              'ffn_w_up': 2, 'ffn_conv_w': 2, 'ffn_w_down': 1}
GATHER_BF16 = ['lru_w_in', 'lru_w_out', 'rwkv_w_rkv', 'rwkv_w1', 'rwkv_a1', 'rwkv_g1', 'rwkv_w_out', 'ffn_w_up',
               'ffn_w_down']
GATHER_F32 = [n for n in WEIGHTS if n in SHARD_AXIS and n not in GATHER_BF16]
SHARDED = GATHER_BF16 + GATHER_F32
REPLICATED = [n for n in WEIGHTS if n not in SHARD_AXIS]


def _pcall(body, **kw):
    return pl.pallas_call(body, **kw)


def _params(n_grid):
    return pltpu.CompilerParams(dimension_semantics=("arbitrary",) * n_grid, vmem_limit_bytes=VMEM_LIMIT)


def _shift_rows(x, d, up):
    n = x.shape[0]
    idx = lax.broadcasted_iota(jnp.int32, x.shape, 0)
    if up:
        return jnp.where(idx < n - d, pltpu.roll(x, n - d, 0), 0.0)
    return jnp.where(idx >= d, pltpu.roll(x, d, 0), 0.0)


@functools.partial(jax.custom_vjp, nondiff_argnums=(1,))
def _shift_down(x, d):
    return _shift_rows(x, d, False)


def _shift_down_fwd(x, d):
    return _shift_rows(x, d, False), None


def _shift_down_bwd(d, _, g):
    return (_shift_rows(g, d, True),)


_shift_down.defvjp(_shift_down_fwd, _shift_down_bwd)


def _scan_doubling(a, b, up):
    n = a.shape[0]
    d = 1
    while d < n:
        b = b + a * _shift_rows(b, d, up)
        a = a * _shift_rows(a, d, up)
        d *= 2
    return b


@jax.custom_vjp
def _linear_scan(a, b):
    return _scan_doubling(a, b, False)


def _linear_scan_fwd(a, b):
    h = _scan_doubling(a, b, False)
    return h, (a, h)


def _linear_scan_bwd(res, dh):
    a, h = res
    g = _scan_doubling(_shift_rows(a, 1, True), dh, True)
    return g * _shift_rows(h, 1, False), g


_linear_scan.defvjp(_linear_scan_fwd, _linear_scan_bwd)


def _causal_conv(x, w, b):
    k = w.shape[0]
    out = b + x * w[k - 1:k]
    for j in range(k - 1):
        out = out + _shift_down(x, k - 1 - j) * w[j:j + 1]
    return out


def _gelu(x):
    return jax.nn.gelu(x, approximate=True)


def _rmsnorm(x, g):
    return x * lax.rsqrt(jnp.mean(x * x, axis=-1, keepdims=True) + RMS_EPS) * g


def _neg_expm1(x):
    series = x * (1.0 + x * 0.5 * (1.0 + x * (1.0 / 3.0) * (1.0 + x * 0.25 * (1.0 + x * 0.2))))
    return -jnp.where(x > -0.1, series, jnp.exp(x) - 1.0)


def _dot(a, b, ca=1, cb=0, precision=None):
    return lax.dot_general(a, b, (((ca,), (cb,)), ((), ())), precision=precision, preferred_element_type=F32)


def _bdot(a, b):
    return _dot(a.astype(BF16), b.astype(BF16))


def _head_indicator():
    row = lax.broadcasted_iota(jnp.int32, (D, LANE), 0)
    col = lax.broadcasted_iota(jnp.int32, (D, LANE), 1)
    return (lax.shift_right_logical(row, 6) == col).astype(F32)


def _head_sum(x):
    e = _head_indicator()
    return _dot(_dot(x, e, precision=HI), e, 1, 1, precision=HI)


def _specs(items):
    return [pl.BlockSpec(bs, im) for (_, bs, im) in items]


def _stage_fwd(name, f, grid, ins, params, outs):
    n_in = len(ins) + len(params)

    def body(*refs):
        res = f(*[r[...] for r in refs[:n_in]])
        for o, v in zip(refs[n_in:], res):
            o[...] = v.astype(o.dtype)

    return _pcall(
        body, name=name, grid=grid, in_specs=_specs(ins + params),
        out_specs=[pl.BlockSpec(bs, im) for (_, _, bs, im) in outs],
        out_shape=[jax.ShapeDtypeStruct(s, dt) for (s, dt, _, _) in outs],
        compiler_params=_params(len(grid)),
    )(*[a for (a, _, _) in ins + params])


def _stage_bwd(name, f, grid, ins, params, douts, din_dtypes, adds=None):
    adds = adds or {}
    n_in, n_par = len(ins), len(params)
    dout_items = [(a, bs, im) for (arrs, bs, im) in douts for a in arrs]
    add_items = [(adds[i], ins[i][1], ins[i][2]) for i in sorted(adds)]
    n_do, n_add = len(dout_items), len(add_items)

    def body(*refs):
        vals = [r[...] for r in refs[:n_in + n_par]]
        do_refs = list(refs[n_in + n_par:n_in + n_par + n_do])
        add_refs = dict(zip(sorted(adds), refs[n_in + n_par + n_do:n_in + n_par + n_do + n_add]))
        din_refs = refs[n_in + n_par + n_do + n_add:n_in + n_par + n_do + n_add + n_in]
        dpar_refs = refs[n_in + n_par + n_do + n_add + n_in:]
        cts = []
        for (arrs, _, _) in douts:
            ct = do_refs.pop(0)[...].astype(F32)
            for _ in arrs[1:]:
                ct = ct + do_refs.pop(0)[...].astype(F32)
            cts.append(ct)
        _, vjp = jax.vjp(f, *vals)
        grads = vjp(tuple(cts))
        for i, r in enumerate(din_refs):
            g = grads[i]
            if i in add_refs:
                g = g + add_refs[i][...]
            r[...] = g.astype(r.dtype)

        @pl.when(pl.program_id(len(grid) - 1) == 0)
        def _():
            for r in dpar_refs:
                r[...] = jnp.zeros(r.shape, r.dtype)

        for j, r in enumerate(dpar_refs):
            r[...] += grads[n_in + j]

    res = _pcall(
        body, name=name, grid=grid, in_specs=_specs(ins + params + dout_items + add_items),
        out_specs=_specs(ins + params),
        out_shape=[jax.ShapeDtypeStruct(a.shape, dt) for (a, _, _), dt in zip(ins, din_dtypes)]
        + [jax.ShapeDtypeStruct(a.shape, F32) for (a, _, _) in params],
        compiler_params=_params(len(grid)),
    )(*[a for (a, _, _) in ins + params + dout_items + add_items])
    return list(res[:n_in]), list(res[n_in:])


def _tile(n, want):
    t = min(n, want)
    while n % t:
        t //= 2
    return t


def _matmul(name, a, b, *, mode="nn", resid=None, out_dtype=F32, tm=512, tn=512, tk=1024):
    if mode == "tn":
        (kdim, m), n = a.shape, b.shape[1]
    else:
        (m, kdim), n = a.shape, (b.shape[1] if mode == "nn" else b.shape[0])
    tm, tn, tk = _tile(m, tm), _tile(n, tn), _tile(kdim, tk)
    nk = kdim // tk
    a_spec = pl.BlockSpec((tk, tm), lambda i, j, k: (k, i)) if mode == "tn" else pl.BlockSpec((tm, tk), lambda i, j, k: (i, k))
    b_spec = pl.BlockSpec((tn, tk), lambda i, j, k: (j, k)) if mode == "nt" else pl.BlockSpec((tk, tn), lambda i, j, k: (k, j))
    ca = 0 if mode == "tn" else 1
    cb = 1 if mode == "nt" else 0
    operands = [a, b]
    in_specs = [a_spec, b_spec]
    if resid is not None:
        operands.append(resid)
        in_specs.append(pl.BlockSpec((tm, tn), lambda i, j, k: (i, j)))

    def body(*refs):
        a_ref, b_ref = refs[0], refs[1]
        o_ref, acc_ref = refs[-2], refs[-1]
        k = pl.program_id(2)

        @pl.when(k == 0)
        def _():
            acc_ref[...] = jnp.zeros(acc_ref.shape, F32)

        acc_ref[...] += _dot(a_ref[...].astype(BF16), b_ref[...].astype(BF16), ca, cb)

        @pl.when(k == nk - 1)
        def _():
            r = acc_ref[...]
            if resid is not None:
                r = r + refs[2][...]
            o_ref[...] = r.astype(o_ref.dtype)

    return _pcall(
        body, name=name, grid=(m // tm, n // tn, nk), in_specs=in_specs,
        out_specs=pl.BlockSpec((tm, tn), lambda i, j, k: (i, j)),
        out_shape=jax.ShapeDtypeStruct((m, n), out_dtype),
        scratch_shapes=[pltpu.VMEM((tm, tn), F32)],
        compiler_params=_params(3),
    )(*operands)


def _add(name, a, b):
    rows, cols = a.shape
    tr = _tile(rows, 512)

    def body(a_ref, b_ref, o_ref):
        o_ref[...] = a_ref[...] + b_ref[...]

    spec = pl.BlockSpec((tr, cols), lambda i: (i, 0))
    return _pcall(body, name=name, grid=(rows // tr,), in_specs=[spec, spec], out_specs=spec,
                  out_shape=jax.ShapeDtypeStruct(a.shape, a.dtype), compiler_params=_params(1))(a, b)


def _f_lru_pre(x, norm, b_out):
    return _rmsnorm(x, norm), x + b_out


def _f_lru_core(u, b_y, b_x, cw, cb, gw, gb, lam):
    yb = _gelu(u[:, :CT] + b_y)
    xr = _causal_conv(u[:, CT:] + b_x, cw, cb)
    gr = jax.nn.sigmoid(_bdot(xr, gw[0, 0]) + gb[0:1])
    gi = jax.nn.sigmoid(_bdot(xr, gw[1, 0]) + gb[1:2])
    log_a = -LRU_C * gr * jax.nn.softplus(-lam)
    a = jnp.exp(log_a)
    bterm = jnp.sqrt(_neg_expm1(2.0 * log_a)) * (gi * xr)
    return (_linear_scan(a, bterm) * yb,)


def _f_norm(x, norm):
    return (_rmsnorm(x, norm),)


def _f_ffn_core(u, cw, cb):
    return (_gelu(_causal_conv(u[:, :CT], cw, cb)) * u[:, CT:],)


def _f_rwkv_mix(h, mix):
    xx = _shift_down(h, 1) - h
    return tuple(h + xx * mix[i:i + 1] for i in range(6))


def _f_rwkv_pre(k, lw1, la1, lg1, w0, a0, k_k, k_a, w2, a2, g2):
    wpre = w0 + _bdot(jnp.tanh(lw1), w2)
    apre = a0 + _bdot(la1, a2)
    g = _bdot(jax.nn.sigmoid(lg1), g2)
    log_decay = -jnp.exp(-jax.nn.softplus(-wpre) - 0.5)
    a = jax.nn.sigmoid(apre)
    kk = k * k_k
    kk = kk / jnp.maximum(jnp.sqrt(_head_sum(kk * kk)), 1e-12)
    kf = k * (1.0 + (a - 1.0) * k_a)
    return log_decay, kf, -kk, kk * a, g


def _f_rwkv_post(y, r, kf, v, g, ln_w, ln_b, r_k):
    inv = 1.0 / HEAD
    yc = y - _head_sum(y) * inv
    var = _head_sum(yc * yc) * inv
    yn = yc * lax.rsqrt(var + GN_EPS) * ln_w + ln_b
    bonus = _head_sum(r * kf * r_k) * v
    return ((yn + bonus) * g,)


def _split_bf16(x):
    hi = x.astype(BF16)
    return hi, (x - hi.astype(F32)).astype(BF16)


def _hdot(a, b, ca, cb):
    return lax.dot_general(a, b, (((ca,), (cb,)), ((0,), (0,))), preferred_element_type=F32)


def _mm3_impl(a, b, ca, cb):
    a_hi, a_lo = _split_bf16(a)
    b_hi, b_lo = _split_bf16(b)
    lhs = jnp.concatenate([a_hi, a_hi, a_lo], axis=ca)
    rhs = jnp.concatenate([b_hi, b_lo, b_hi], axis=cb)
    return _hdot(lhs, rhs, ca, cb)


@functools.partial(jax.custom_vjp, nondiff_argnums=(2, 3))
def _mm3(a, b, ca, cb):
    return _mm3_impl(a, b, ca, cb)


def _mm3_fwd(a, b, ca, cb):
    return _mm3_impl(a, b, ca, cb), (a, b)


def _mm3_bwd(ca, cb, res, dc):
    a, b = res
    fa, fb = 3 - ca, 3 - cb
    da = _mm3(dc, b, 2, fb) if ca == 2 else _mm3(b, dc, fb, 2)
    db = _mm3(a, dc, fa, 1) if cb == 1 else _mm3(dc, a, 1, fa)
    return da, db


_mm3.defvjp(_mm3_fwd, _mm3_bwd)


def _tri_dot_impl(x, upper):
    g, n, _ = x.shape
    row = lax.broadcasted_iota(jnp.int32, (g, n, 3 * n), 1)
    col = lax.broadcasted_iota(jnp.int32, (g, n, 3 * n), 2)
    col = jnp.where(col >= 2 * n, col - 2 * n, jnp.where(col >= n, col - n, col))
    tri = (row <= col if upper else row >= col).astype(BF16)
    hi = x.astype(BF16)
    rem = x - hi.astype(F32)
    mid = rem.astype(BF16)
    lo = (rem - mid.astype(F32)).astype(BF16)
    return _hdot(tri, jnp.concatenate([hi, mid, lo], axis=1), 2, 1)


@functools.partial(jax.custom_vjp, nondiff_argnums=(1,))
def _tri_dot(x, upper):
    return _tri_dot_impl(x, upper)


def _tri_dot_fwd(x, upper):
    return _tri_dot_impl(x, upper), None


def _tri_dot_bwd(upper, _, g):
    return (_tri_dot(g, not upper),)


_tri_dot.defvjp(_tri_dot_fwd, _tri_dot_bwd)


def _rwkv_chunk(z0, r, lw, k, v, a, b):
    g, n, _ = r.shape
    row = lax.broadcasted_iota(jnp.int32, (g, n, n), 1)
    col = lax.broadcasted_iota(jnp.int32, (g, n, n), 2)
    incl, strict = row >= col, row > col
    cs = _tri_dot(lw, False)
    c_last = cs[:, n - 1:n]
    inv = jnp.exp(-cs)
    ar = jnp.concatenate([a * jnp.exp(cs - lw), r * jnp.exp(cs)], axis=1)
    bk = jnp.concatenate([b * inv, k * inv], axis=1)
    pair = _mm3(ar, bk, 2, 2)
    a_ab = jnp.where(strict, pair[:, :n, :n], 0.0)
    a_ak = jnp.where(strict, pair[:, :n, n:], 0.0)
    a_rbk = jnp.concatenate([jnp.where(incl, pair[:, n:, :n], 0.0), jnp.where(incl, pair[:, n:, n:], 0.0)], axis=2)
    arz = _mm3(ar, z0, 2, 1)
    inv_m = (row == col).astype(F32) + a_ab
    p = a_ab
    for _ in range(int(math.log2(n)) - 1):
        p = _mm3(p, p, 2, 1)
        inv_m = inv_m + _mm3(inv_m, p, 2, 1)
    u = _mm3(inv_m, arz[:, :n] + _mm3(a_ak, v, 2, 1), 2, 1)
    uv = jnp.concatenate([u, v], axis=1)
    y = arz[:, n:] + _mm3(a_rbk, uv, 2, 1)
    tail = jnp.exp(c_last - cs)
    er = lax.broadcasted_iota(jnp.int32, (g, HEAD, HEAD), 1)
    ec = lax.broadcasted_iota(jnp.int32, (g, HEAD, HEAD), 2)
    decay_all = jnp.where(er == ec, jnp.exp(c_last), 0.0)
    z_l = _mm3(jnp.concatenate([b * tail, k * tail, decay_all], axis=1), jnp.concatenate([uv, z0], axis=1), 1, 1)
    return y, z_l


def _rwkv_scan_fwd(seqs):
    bh, t, _ = seqs[0].shape
    g, n = _tile(bh, SCAN_HEADS), _tile(t, CHUNK)

    def body(r_ref, lw_ref, k_ref, v_ref, a_ref, b_ref, y_ref, zs_ref, z_ref):
        @pl.when(pl.program_id(1) == 0)
        def _():
            z_ref[...] = jnp.zeros(z_ref.shape, F32)

        z0 = z_ref[...]
        zs_ref[:, 0] = z0
        y, z_l = _rwkv_chunk(z0, r_ref[...], lw_ref[...], k_ref[...], v_ref[...], a_ref[...], b_ref[...])
        y_ref[...] = y
        z_ref[...] = z_l

    seq_spec = pl.BlockSpec((g, n, HEAD), lambda i, c: (i, c, 0))
    return _pcall(
        body, name="rwkv_scan_fwd", grid=(bh // g, t // n), in_specs=[seq_spec] * 6,
        out_specs=[seq_spec, pl.BlockSpec((g, 1, HEAD, HEAD), lambda i, c: (i, c, 0, 0))],
        out_shape=[jax.ShapeDtypeStruct((bh, t, HEAD), F32), jax.ShapeDtypeStruct((bh, t // n, HEAD, HEAD), F32)],
        scratch_shapes=[pltpu.VMEM((g, HEAD, HEAD), F32)],
        compiler_params=_params(2),
    )(*seqs)


def _rwkv_scan_bwd(seqs, zs, dy):
    bh, t, _ = seqs[0].shape
    g, n = _tile(bh, SCAN_HEADS), _tile(t, CHUNK)
    nc = t // n

    def body(r_ref, lw_ref, k_ref, v_ref, a_ref, b_ref, zs_ref, dy_ref, dr, dlw, dk, dv, da, db, dz_ref):
        @pl.when(pl.program_id(1) == 0)
        def _():
            dz_ref[...] = jnp.zeros(dz_ref.shape, F32)

        _, vjp = jax.vjp(_rwkv_chunk, zs_ref[:, 0], r_ref[...], lw_ref[...], k_ref[...], v_ref[...], a_ref[...],
                         b_ref[...])
        grads = vjp((dy_ref[...], dz_ref[...]))
        dz_ref[...] = grads[0]
        for o, gr in zip((dr, dlw, dk, dv, da, db), grads[1:]):
            o[...] = gr

    seq_spec = pl.BlockSpec((g, n, HEAD), lambda i, c: (i, nc - 1 - c, 0))
    return _pcall(
        body, name="rwkv_scan_bwd", grid=(bh // g, nc),
        in_specs=[seq_spec] * 6 + [pl.BlockSpec((g, 1, HEAD, HEAD), lambda i, c: (i, nc - 1 - c, 0, 0)), seq_spec],
        out_specs=[seq_spec] * 6,
        out_shape=[jax.ShapeDtypeStruct((bh, t, HEAD), F32)] * 6,
        scratch_shapes=[pltpu.VMEM((g, HEAD, HEAD), F32)],
        compiler_params=_params(2),
    )(*seqs, zs, dy)


def _loss_head(x, target, norm):
    n = x.shape[0]
    tr = _tile(n, 256)

    def f(xv, gv, tv):
        err = _rmsnorm(xv, gv) - tv
        return 0.5 * jnp.sum(jnp.mean(err * err, axis=-1, keepdims=True), axis=0, keepdims=True)

    def body(x_ref, t_ref, g_ref, dx_ref, dg_ref, loss_ref):
        val, vjp = jax.vjp(lambda xv, gv: f(xv, gv, t_ref[...]), x_ref[...], g_ref[...])
        dx, dg = vjp(jnp.ones((1, 1), F32))
        dx_ref[...] = dx

        @pl.when(pl.program_id(0) == 0)
        def _():
            dg_ref[...] = jnp.zeros(dg_ref.shape, F32)
            loss_ref[...] = jnp.zeros(loss_ref.shape, F32)

        dg_ref[...] += dg
        loss_ref[...] += jnp.broadcast_to(val, loss_ref.shape)

    row = pl.BlockSpec((tr, D), lambda i: (i, 0))
    vec = pl.BlockSpec((1, D), lambda i: (0, 0))
    dx, dg, loss = _pcall(
        body, name="loss_head", grid=(n // tr,), in_specs=[row, row, vec],
        out_specs=[row, vec, pl.BlockSpec((8, LANE), lambda i: (0, 0))],
        out_shape=[jax.ShapeDtypeStruct((n, D), F32), jax.ShapeDtypeStruct((1, D), F32),
                   jax.ShapeDtypeStruct((8, LANE), F32)],
        compiler_params=_params(1),
    )(x, target, norm)
    return loss[0, 0], dx, dg


def _exchange(name, src, scatter):
    shape = src.shape[1:] if scatter else src.shape

    def body(src_ref, dst_ref, send_sems, recv_sems, local_sem):
        x, y, c = lax.axis_index("x"), lax.axis_index("y"), lax.axis_index("c")
        me = 4 * x + 2 * y + c
        own = src_ref.at[me] if scatter else src_ref
        mine = pltpu.make_async_copy(own, dst_ref.at[me], local_sem)
        mine.start()
        copies = []
        for m in range(1, N_DEV):
            px = 1 - x if m & 4 else x
            py = 1 - y if m & 2 else y
            pc = 1 - c if m & 1 else c
            part = src_ref.at[4 * px + 2 * py + pc] if scatter else src_ref
            cp = pltpu.make_async_remote_copy(
                src_ref=part, dst_ref=dst_ref.at[me], send_sem=send_sems.at[m - 1], recv_sem=recv_sems.at[m - 1],
                device_id=(px, py, pc), device_id_type=pl.DeviceIdType.MESH)
            cp.start()
            copies.append(cp)
        for cp in copies:
            cp.wait_recv()
        for cp in copies:
            cp.wait_send()
        mine.wait()

    return _pcall(
        body, name=name, in_specs=[pl.BlockSpec(memory_space=pl.ANY)],
        out_specs=pl.BlockSpec(memory_space=pl.ANY),
        out_shape=jax.ShapeDtypeStruct((N_DEV,) + tuple(shape), src.dtype),
        scratch_shapes=[pltpu.SemaphoreType.DMA((N_DEV - 1,)), pltpu.SemaphoreType.DMA((N_DEV - 1,)),
                        pltpu.SemaphoreType.DMA],
    )(src)


def _adamw(name, parts, w, m, v):
    rows = w.shape[0]
    tr = _tile(rows, PACK_ROWS)

    def body(p_ref, w_ref, m_ref, v_ref, g_ref, d_ref, nm_ref, nv_ref):
        g = p_ref[0]
        for k in range(1, N_DEV):
            g = g + p_ref[k]
        nm = ADAM_B1 * m_ref[...] + (1.0 - ADAM_B1) * g
        nv = ADAM_B2 * v_ref[...] + (1.0 - ADAM_B2) * jnp.square(g)
        m_hat = nm / (1.0 - ADAM_B1 ** ADAM_STEP)
        v_hat = nv / (1.0 - ADAM_B2 ** ADAM_STEP)
        g_ref[...] = g
        d_ref[...] = -ADAM_LR * (m_hat / (jnp.sqrt(v_hat) + ADAM_EPS) + ADAM_WD * w_ref[...])
        nm_ref[...] = nm
        nv_ref[...] = nv

    row = pl.BlockSpec((tr, PACK_W), lambda i: (i, 0))
    return _pcall(
        body, name=name, grid=(rows // tr,),
        in_specs=[pl.BlockSpec((N_DEV, tr, PACK_W), lambda i: (0, i, 0)), row, row, row],
        out_specs=[row] * 4, out_shape=[jax.ShapeDtypeStruct((rows, PACK_W), F32)] * 4,
        compiler_params=_params(1),
    )(parts, w, m, v)


def _pack(arrs, dtype, lead=()):
    flat = jnp.concatenate([a.astype(dtype).reshape(lead + (-1,)) for a in arrs], axis=-1)
    n = flat.shape[-1]
    quantum = PACK_W * PACK_ROWS
    total = -(-n // quantum) * quantum
    flat = jnp.pad(flat, [(0, 0)] * len(lead) + [(0, total - n)])
    return flat.reshape(lead + (total // PACK_W, PACK_W))


def _unpack(buf, shapes, lead=()):
    flat = buf.reshape(lead + (-1,))
    out, off = [], 0
    for s in shapes:
        n = math.prod(s)
        out.append(flat[..., off:off + n].reshape(lead + tuple(s)))
        off += n
    return out


def _unshard(g, axis):
    local = g.shape[1:]
    full = jnp.moveaxis(g, 0, axis)
    return full.reshape(local[:axis] + (N_DEV * local[axis],) + local[axis + 1:])


def _reshard(full, axis):
    s = full.shape
    blocked = full.reshape(s[:axis] + (N_DEV, s[axis] // N_DEV) + s[axis + 1:])
    return jnp.moveaxis(blocked, axis, 0)


def _interleave(w, half):
    rows = w.shape[0]
    pair = jnp.stack([w[:, :half].reshape(rows, half // CT, CT), w[:, half:].reshape(rows, half // CT, CT)], axis=2)
    return pair.reshape(rows, 2 * half)


def _deinterleave(w, half):
    rows = w.shape[0]
    pair = w.reshape(rows, half // CT, 2, CT)
    return jnp.concatenate([pair[:, :, 0].reshape(rows, half), pair[:, :, 1].reshape(rows, half)], axis=1)


def _to_heads(x, bsz):
    t = x.shape[0] // bsz
    return x.reshape(bsz, t, N_HEAD, HEAD).transpose(0, 2, 1, 3).reshape(bsz * N_HEAD, t, HEAD)


def _from_heads(x, bsz):
    t = x.shape[1]
    return x.reshape(bsz, N_HEAD, t, HEAD).transpose(0, 2, 1, 3).reshape(bsz * t, D)


def _row_item(a, tr):
    return (a, (tr, a.shape[1]), lambda i: (i, 0))


def _vec_item(a):
    return (a, a.shape, lambda i: (0,) * a.ndim)


def _seq_item(a, t, width):
    return (a, (t, width), lambda j, b: (b, j))


def _col_item(a, width):
    return (a, (a.shape[0], width), lambda j, b: (0, j))


def _ffn_fwd(tag, x, p, t):
    n = x.shape[0]
    tr = _tile(n, 512)
    norm_ins, norm_par = [_row_item(x, tr)], [_vec_item(p["norm"])]
    (h,) = _stage_fwd(f"{tag}_norm", _f_norm, (n // tr,), norm_ins, norm_par, [((n, D), BF16, (tr, D), lambda i: (i, 0))])
    u = _matmul(f"{tag}_up", h, p["w_up"], tm=1024, tn=768)
    grid = (D_FF // CT, n // t)
    core_par = [_col_item(p["conv_w"], CT), _col_item(p["conv_b"], CT)]
    (hid,) = _stage_fwd(f"{tag}_core", _f_ffn_core, grid, [_seq_item(u, t, 2 * CT)], core_par,
                        [((n, D_FF), BF16, (t, CT), lambda j, b: (b, j))])
    y = _matmul(f"{tag}_down", hid, p["w_down"], resid=x, tm=1024, tn=1024)
    return y, (x, h, u, hid)


def _ffn_bwd(tag, dy, saved, p, t):
    x, h, u, hid = saved
    n = x.shape[0]
    tr = _tile(n, 512)
    grads = {}
    dhid = _matmul(f"{tag}_down_da", dy, p["w_down"], mode="nt", tm=1024, tn=768)
    grads["w_down"] = _matmul(f"{tag}_down_dw", hid, dy, mode="tn", tm=768, tn=1024)
    grid = (D_FF // CT, n // t)
    core_par = [_col_item(p["conv_w"], CT), _col_item(p["conv_b"], CT)]
    (du,), (grads["conv_w"], grads["conv_b"]) = _stage_bwd(
        f"{tag}_core_bwd", _f_ffn_core, grid, [_seq_item(u, t, 2 * CT)], core_par,
        [([dhid], (t, CT), lambda j, b: (b, j))], [BF16])
    dh = _matmul(f"{tag}_up_da", du, p["w_up"], mode="nt", tm=1024, tn=1024)
    grads["w_up"] = _matmul(f"{tag}_up_dw", h, du, mode="tn", tm=1024, tn=768)
    (dx,), (grads["norm"],) = _stage_bwd(
        f"{tag}_norm_bwd", _f_norm, (n // tr,), [_row_item(x, tr)], [_vec_item(p["norm"])],
        [([dh], (tr, D), lambda i: (i, 0))], [F32], adds={0: dy})
    return dx, grads


def _lru_fwd(x, p, t):
    n = x.shape[0]
    tr = _tile(n, 512)
    row = lambda dt: ((n, D), dt, (tr, D), lambda i: (i, 0))
    h, xb = _stage_fwd("lru_pre", _f_lru_pre, (n // tr,), [_row_item(x, tr)],
                       [_vec_item(p["norm"]), _vec_item(p["b_out"])], [row(BF16), row(F32)])
    u = _matmul("lru_in", h, p["w_in"], tm=1024, tn=1024)
    out, = _stage_fwd("lru_core", _f_lru_core, (D // CT, n // t), [_seq_item(u, t, 2 * CT)], _lru_core_params(p),
                      [((n, D), BF16, (t, CT), lambda j, b: (b, j))])
    y = _matmul("lru_out", out, p["w_out"], resid=xb, tm=1024, tn=1024)
    return y, (x, h, u, out)


def _lru_core_params(p):
    return [_col_item(p["b_y"], CT), _col_item(p["b_x"], CT), _col_item(p["conv_w"], CT), _col_item(p["conv_b"], CT),
            (p["gate_w"], (2, 1, CT, CT), lambda j, b: (0, j, 0, 0)), _col_item(p["gate_b"], CT), _col_item(p["lam"], CT)]


def _lru_bwd(dy, saved, p, t):
    x, h, u, out = saved
    n = x.shape[0]
    tr = _tile(n, 512)
    grads = {}
    dout = _matmul("lru_out_da", dy, p["w_out"], mode="nt", tm=1024, tn=1024)
    grads["w_out"] = _matmul("lru_out_dw", out, dy, mode="tn", tm=1024, tn=1024)
    (du,), core_grads = _stage_bwd("lru_core_bwd", _f_lru_core, (D // CT, n // t), [_seq_item(u, t, 2 * CT)],
                                   _lru_core_params(p), [([dout], (t, CT), lambda j, b: (b, j))], [BF16])
    for k, g in zip(("b_y", "b_x", "conv_w", "conv_b", "gate_w", "gate_b", "lam"), core_grads):
        grads[k] = g
    dh = _matmul("lru_in_da", du, p["w_in"], mode="nt", tm=1024, tn=1024)
    grads["w_in"] = _matmul("lru_in_dw", h, du, mode="tn", tm=1024, tn=1024)
    row = (tr, D), lambda i: (i, 0)
    (dx,), (grads["norm"], grads["b_out"]) = _stage_bwd(
        "lru_pre_bwd", _f_lru_pre, (n // tr,), [_row_item(x, tr)], [_vec_item(p["norm"]), _vec_item(p["b_out"])],
        [([dh], *row), ([dy], *row)], [F32])
    return dx, grads


_RWKV_PRE_VECS = ("w0", "a0", "k_k", "k_a", "w2", "a2", "g2")
_RWKV_POST_VECS = ("ln_w", "ln_b", "r_k")


def _rwkv_fwd(x, p, t):
    n = x.shape[0]
    bsz = n // t
    tr = _tile(n, 512)
    ts = _tile(n, 128)
    (h,) = _stage_fwd("rwkv_norm", _f_norm, (n // tr,), [_row_item(x, tr)], [_vec_item(p["norm"])],
                      [((n, D), F32, (tr, D), lambda i: (i, 0))])
    mixed = _stage_fwd("rwkv_mix", _f_rwkv_mix, (D // CT, bsz), [_seq_item(h, t, CT)], [_col_item(p["mix"], CT)],
                       [((n, D), BF16, (t, CT), lambda j, b: (b, j))] * 6)
    xr, xk, xv, xw, xa, xg = mixed
    r = _matmul("rwkv_r", xr, p["w_r"], tm=1024, tn=1024)
    k = _matmul("rwkv_k", xk, p["w_k"], tm=1024, tn=1024)
    v = _matmul("rwkv_v", xv, p["w_v"], tm=1024, tn=1024)
    lw1 = _matmul("rwkv_w1", xw, p["w1"], tm=1024)
    la1 = _matmul("rwkv_a1", xa, p["a1"], tm=1024)
    lg1 = _matmul("rwkv_g1", xg, p["g1"], tm=1024)
    pre_ins = [_row_item(a, ts) for a in (k, lw1, la1, lg1)]
    pre_par = [_vec_item(p[q]) for q in _RWKV_PRE_VECS]
    row = ((n, D), F32, (ts, D), lambda i: (i, 0))
    lw, kf, aa, bb, g = _stage_fwd("rwkv_pre", _f_rwkv_pre, (n // ts,), pre_ins, pre_par, [row] * 5)
    seqs = [_to_heads(a, bsz) for a in (r, lw, kf, v, aa, bb)]
    y_h, zs = _rwkv_scan_fwd(seqs)
    y = _from_heads(y_h, bsz)
    post_ins = [_row_item(a, ts) for a in (y, r, kf, v, g)]
    post_par = [_vec_item(p[q]) for q in _RWKV_POST_VECS]
    (yg,) = _stage_fwd("rwkv_post", _f_rwkv_post, (n // ts,), post_ins, post_par,
                       [((n, D), BF16, (ts, D), lambda i: (i, 0))])
    out = _matmul("rwkv_out", yg, p["w_out"], resid=x, tm=1024, tn=1024)
    return out, (x, h, mixed, r, k, v, lw1, la1, lg1, lw, kf, aa, bb, g, zs, y, yg)


def _rwkv_bwd(dout, saved, p, t):
    x, h, mixed, r, k, v, lw1, la1, lg1, lw, kf, aa, bb, g, zs, y, yg = saved
    xr, xk, xv, xw, xa, xg = mixed
    n = x.shape[0]
    bsz = n // t
    tr = _tile(n, 512)
    ts = _tile(n, 128)
    grads = {}
    row_s = (ts, D), lambda i: (i, 0)
    dyg = _matmul("rwkv_out_da", dout, p["w_out"], mode="nt", tm=1024, tn=1024)
    grads["w_out"] = _matmul("rwkv_out_dw", yg, dout, mode="tn", tm=1024, tn=1024)
    post_ins = [_row_item(a, ts) for a in (y, r, kf, v, g)]
    post_par = [_vec_item(p[q]) for q in _RWKV_POST_VECS]
    (dy, dr_a, dkf_a, dv_a, dg), post_grads = _stage_bwd(
        "rwkv_post_bwd", _f_rwkv_post, (n // ts,), post_ins, post_par, [([dyg], *row_s)], [F32] * 5)
    grads.update(zip(_RWKV_POST_VECS, post_grads))
    seqs = [_to_heads(a, bsz) for a in (r, lw, kf, v, aa, bb)]
    d_heads = _rwkv_scan_bwd(seqs, zs, _to_heads(dy, bsz))
    dr_b, dlw, dkf_b, dv_b, daa, dbb = [_from_heads(a, bsz) for a in d_heads]
    dr = _add("rwkv_dr_sum", dr_a, dr_b)
    dv = _add("rwkv_dv_sum", dv_a, dv_b)
    pre_ins = [_row_item(a, ts) for a in (k, lw1, la1, lg1)]
    pre_par = [_vec_item(p[q]) for q in _RWKV_PRE_VECS]
    (dk, dlw1, dla1, dlg1), pre_grads = _stage_bwd(
        "rwkv_pre_bwd", _f_rwkv_pre, (n // ts,), pre_ins, pre_par,
        [([dlw], *row_s), ([dkf_a, dkf_b], *row_s), ([daa], *row_s), ([dbb], *row_s), ([dg], *row_s)], [F32] * 4)
    grads.update(zip(_RWKV_PRE_VECS, pre_grads))
    dmixed = []
    for tag, xin, dz, wname in (("r", xr, dr, "w_r"), ("k", xk, dk, "w_k"), ("v", xv, dv, "w_v"),
                                ("w1", xw, dlw1, "w1"), ("a1", xa, dla1, "a1"), ("g1", xg, dlg1, "g1")):
        dmixed.append(_matmul(f"rwkv_{tag}_da", dz, p[wname], mode="nt", tm=1024, tn=1024))
        grads[wname] = _matmul(f"rwkv_{tag}_dw", xin, dz, mode="tn", tm=1024, tn=1024)
    seq_blk = (t, CT), lambda j, b: (b, j)
    (dh,), (grads["mix"],) = _stage_bwd(
        "rwkv_mix_bwd", _f_rwkv_mix, (D // CT, bsz), [_seq_item(h, t, CT)], [_col_item(p["mix"], CT)],
        [([a], *seq_blk) for a in dmixed], [F32])
    (dx,), (grads["norm"],) = _stage_bwd(
        "rwkv_norm_bwd", _f_norm, (n // tr,), [_row_item(x, tr)], [_vec_item(p["norm"])],
        [([dh], (tr, D), lambda i: (i, 0))], [F32], adds={0: dout})
    return dx, grads


def _block_diag_gates(gate_w):
    z = jnp.zeros((2, D // CT, HEAD, HEAD), gate_w.dtype)
    even, odd = gate_w[:, 0::2], gate_w[:, 1::2]
    top = jnp.concatenate([even, z], axis=-1)
    bot = jnp.concatenate([z, odd], axis=-1)
    return jnp.concatenate([top, bot], axis=-2)


def _gate_blocks_grad(dg):
    even, odd = dg[:, :, :HEAD, :HEAD], dg[:, :, HEAD:, HEAD:]
    return jnp.stack([even, odd], axis=2).reshape(2, N_HEAD, HEAD, HEAD)


def _step(local, x, target):
    bsz, t, _ = x.shape
    n = bsz * t
    w = {k: local[k] for k in WEIGHTS}

    bf_parts = _exchange("gather_bf16", _pack([w[k] for k in GATHER_BF16], BF16), False)
    f32_parts = _exchange("gather_f32", _pack([w[k] for k in GATHER_F32], F32), False)
    full = {}
    for names, buf in ((GATHER_BF16, bf_parts), (GATHER_F32, f32_parts)):
        for k, g in zip(names, _unpack(buf, [w[k].shape for k in names], (N_DEV,))):
            full[k] = _unshard(g, SHARD_AXIS[k])
    for k in REPLICATED:
        full[k] = w[k]

    lru = dict(norm=full["lru_norm"], b_out=full["lru_b_out"], w_in=_interleave(full["lru_w_in"][0], D),
               b_y=full["lru_b_in"][:, :D], b_x=full["lru_b_in"][:, D:], conv_w=full["lru_conv_w"][0],
               conv_b=full["lru_conv_b"], gate_w=_block_diag_gates(full["lru_gate_w"][0]),
               gate_b=full["lru_gate_b"][0].reshape(2, D), lam=full["lru_lambda"], w_out=full["lru_w_out"][0])
    ffn = [dict(norm=full["ffn_norm"][l:l + 1], w_up=_interleave(full["ffn_w_up"][l], D_FF),
                conv_w=full["ffn_conv_w"][l], conv_b=full["ffn_conv_b"][l:l + 1], w_down=full["ffn_w_down"][l])
           for l in range(2)]
    rwkv = dict(norm=full["rwkv_norm"], mix=full["rwkv_mix"][0], w_r=full["rwkv_w_rkv"][0, 0],
                w_k=full["rwkv_w_rkv"][0, 1], w_v=full["rwkv_w_rkv"][0, 2], w0=full["rwkv_w0"], w1=full["rwkv_w1"][0],
                w2=full["rwkv_w2"][0], a0=full["rwkv_a0"], a1=full["rwkv_a1"][0], a2=full["rwkv_a2"][0],
                g1=full["rwkv_g1"][0], g2=full["rwkv_g2"][0], k_k=full["rwkv_k_k"], k_a=full["rwkv_k_a"],
                r_k=full["rwkv_r_k"].reshape(1, D), ln_w=full["rwkv_ln_w"], ln_b=full["rwkv_ln_b"],
                w_out=full["rwkv_w_out"][0])

    x0 = x.reshape(n, D)
    x1, s_lru = _lru_fwd(x0, lru, t)
    x2, s_ffn0 = _ffn_fwd("ffn0", x1, ffn[0], t)
    x3, s_rwkv = _rwkv_fwd(x2, rwkv, t)
    x4, s_ffn1 = _ffn_fwd("ffn1", x3, ffn[1], t)
    loss, dx4, d_final = _loss_head(x4, target.reshape(n, D), full["final_norm"].reshape(1, D))
    dx3, g_ffn1 = _ffn_bwd("ffn1", dx4, s_ffn1, ffn[1], t)
    dx2, g_rwkv = _rwkv_bwd(dx3, s_rwkv, rwkv, t)
    dx1, g_ffn0 = _ffn_bwd("ffn0", dx2, s_ffn0, ffn[0], t)
    dx0, g_lru = _lru_bwd(dx1, s_lru, lru, t)

    g_ffn = [g_ffn0, g_ffn1]
    gfull = {
        "lru_norm": g_lru["norm"], "lru_w_in": _deinterleave(g_lru["w_in"], D)[None],
        "lru_b_in": jnp.concatenate([g_lru["b_y"], g_lru["b_x"]], axis=1), "lru_conv_w": g_lru["conv_w"][None],
        "lru_conv_b": g_lru["conv_b"], "lru_gate_w": _gate_blocks_grad(g_lru["gate_w"])[None],
        "lru_gate_b": g_lru["gate_b"].reshape(1, 2, N_HEAD, HEAD), "lru_lambda": g_lru["lam"],
        "lru_w_out": g_lru["w_out"][None], "lru_b_out": g_lru["b_out"],
        "rwkv_norm": g_rwkv["norm"], "rwkv_mix": g_rwkv["mix"][None],
        "rwkv_w_rkv": jnp.stack([g_rwkv["w_r"], g_rwkv["w_k"], g_rwkv["w_v"]])[None],
        "rwkv_w0": g_rwkv["w0"], "rwkv_w1": g_rwkv["w1"][None], "rwkv_w2": g_rwkv["w2"][None],
        "rwkv_a0": g_rwkv["a0"], "rwkv_a1": g_rwkv["a1"][None], "rwkv_a2": g_rwkv["a2"][None],
        "rwkv_g1": g_rwkv["g1"][None], "rwkv_g2": g_rwkv["g2"][None], "rwkv_k_k": g_rwkv["k_k"],
        "rwkv_k_a": g_rwkv["k_a"], "rwkv_r_k": g_rwkv["r_k"].reshape(1, N_HEAD, HEAD), "rwkv_ln_w": g_rwkv["ln_w"],
        "rwkv_ln_b": g_rwkv["ln_b"], "rwkv_w_out": g_rwkv["w_out"][None],
        "ffn_norm": jnp.concatenate([g["norm"] for g in g_ffn]),
        "ffn_w_up": jnp.stack([_deinterleave(g["w_up"], D_FF) for g in g_ffn]),
        "ffn_conv_w": jnp.stack([g["conv_w"] for g in g_ffn]),
        "ffn_conv_b": jnp.concatenate([g["conv_b"] for g in g_ffn]),
        "ffn_w_down": jnp.stack([g["w_down"] for g in g_ffn]),
        "final_norm": d_final.reshape(D),
    }

    out = {}
    sh_parts = _exchange("grad_scatter", _pack([_reshard(gfull[k], SHARD_AXIS[k]) for k in SHARDED], F32, (N_DEV,)), True)
    rp_parts = _exchange("grad_gather", _pack([gfull[k] for k in REPLICATED], F32), False)
    for tag, names, parts in (("adamw_sharded", SHARDED, sh_parts), ("adamw_replicated", REPLICATED, rp_parts)):
        packs = [_pack([local[pre + k] for k in names], F32) for pre in ("", "m_", "v_")]
        res = _adamw(tag, parts, *packs)
        for kind, buf in zip(("grad", "delta", "new_m", "new_v"), res):
            for k, a in zip(names, _unpack(buf, [w[k].shape for k in names])):
                out[(kind, k)] = a
    return loss, dx0.reshape(x.shape), out


def kernel(x, lru_norm, lru_w_in, lru_b_in, lru_conv_w, lru_conv_b, lru_gate_w, lru_gate_b, lru_lambda, lru_w_out, lru_b_out, rwkv_norm, rwkv_mix, rwkv_w_rkv, rwkv_w0, rwkv_w1, rwkv_w2, rwkv_a0, rwkv_a1, rwkv_a2, rwkv_g1, rwkv_g2, rwkv_k_k, rwkv_k_a, rwkv_r_k, rwkv_ln_w, rwkv_ln_b, rwkv_w_out, ffn_norm, ffn_w_up, ffn_conv_w, ffn_conv_b, ffn_w_down, final_norm, loss_target, m_lru_norm, m_lru_w_in, m_lru_b_in, m_lru_conv_w, m_lru_conv_b, m_lru_gate_w, m_lru_gate_b, m_lru_lambda, m_lru_w_out, m_lru_b_out, m_rwkv_norm, m_rwkv_mix, m_rwkv_w_rkv, m_rwkv_w0, m_rwkv_w1, m_rwkv_w2, m_rwkv_a0, m_rwkv_a1, m_rwkv_a2, m_rwkv_g1, m_rwkv_g2, m_rwkv_k_k, m_rwkv_k_a, m_rwkv_r_k, m_rwkv_ln_w, m_rwkv_ln_b, m_rwkv_w_out, m_ffn_norm, m_ffn_w_up, m_ffn_conv_w, m_ffn_conv_b, m_ffn_w_down, m_final_norm, v_lru_norm, v_lru_w_in, v_lru_b_in, v_lru_conv_w, v_lru_conv_b, v_lru_gate_w, v_lru_gate_b, v_lru_lambda, v_lru_w_out, v_lru_b_out, v_rwkv_norm, v_rwkv_mix, v_rwkv_w_rkv, v_rwkv_w0, v_rwkv_w1, v_rwkv_w2, v_rwkv_a0, v_rwkv_a1, v_rwkv_a2, v_rwkv_g1, v_rwkv_g2, v_rwkv_k_k, v_rwkv_k_a, v_rwkv_r_k, v_rwkv_ln_w, v_rwkv_ln_b, v_rwkv_w_out, v_ffn_norm, v_ffn_w_up, v_ffn_conv_w, v_ffn_conv_b, v_ffn_w_down, v_final_norm):
    args = locals()
    local = {(pre + k): args[pre + k] for pre in ("", "m_", "v_") for k in WEIGHTS}
    loss_local, grad_x, out = _step(local, x, loss_target)
    loss = lax.psum(loss_local, ("x", "y", "c"))
    return (loss, grad_x, *[out[(kind, k)] for kind in ("grad", "delta", "new_m", "new_v") for k in WEIGHTS])
```

```python
import functools
import math

import jax
import jax.numpy as jnp
from jax import lax
from jax.experimental import pallas as pl
from jax.experimental.pallas import tpu as pltpu

F32 = jnp.float32
BF16 = jnp.bfloat16
HI = lax.Precision.HIGHEST

N_DEV = 8
D = 1024
HEAD = 64
N_HEAD = D // HEAD
D_FF = 3 * D
LANE = 128
V7X_VMEM_BYTES = 64 * 1024 * 1024
VMEM_LIMIT = V7X_VMEM_BYTES - 8 * 1024 * 1024
CT = LANE
CHUNK = 64
SCAN_HEADS = 8
PACK_W = 1024
PACK_ROWS = 256

ADAM_LR, ADAM_B1, ADAM_B2, ADAM_EPS, ADAM_WD, ADAM_STEP = 0.001, 0.9, 0.999, 1e-08, 0.01, 10
RMS_EPS = 1e-6
GN_EPS = 64e-5
LRU_C = 8.0

WEIGHTS = ['lru_norm', 'lru_w_in', 'lru_b_in', 'lru_conv_w', 'lru_conv_b', 'lru_gate_w', 'lru_gate_b',
           'lru_lambda', 'lru_w_out', 'lru_b_out', 'rwkv_norm', 'rwkv_mix', 'rwkv_w_rkv', 'rwkv_w0', 'rwkv_w1',
           'rwkv_w2', 'rwkv_a0', 'rwkv_a1', 'rwkv_a2', 'rwkv_g1', 'rwkv_g2', 'rwkv_k_k', 'rwkv_k_a', 'rwkv_r_k',
           'rwkv_ln_w', 'rwkv_ln_b', 'rwkv_w_out', 'ffn_norm', 'ffn_w_up', 'ffn_conv_w', 'ffn_conv_b',
           'ffn_w_down', 'final_norm']
SHARD_AXIS = {'lru_w_in': 2, 'lru_conv_w': 2, 'lru_w_out': 1, 'rwkv_norm': 1, 'rwkv_mix': 2, 'rwkv_w_rkv': 2,
              'rwkv_w0': 1, 'rwkv_w1': 1, 'rwkv_w2': 2, 'rwkv_a0': 1, 'rwkv_a1': 1, 'rwkv_a2': 2, 'rwkv_g1': 1,
              'rwkv_g2': 2, 'rwkv_k_k': 1, 'rwkv_k_a': 1, 'rwkv_ln_w': 1, 'rwkv_ln_b': 1, 'rwkv_w_out': 1,
              'ffn_w_up': 2, 'ffn_conv_w': 2, 'ffn_w_down': 1}
GATHER_BF16 = ['lru_w_in', 'lru_w_out', 'rwkv_w_rkv', 'rwkv_w1', 'rwkv_a1', 'rwkv_g1', 'rwkv_w_out', 'ffn_w_up',
               'ffn_w_down']
GATHER_F32 = [n for n in WEIGHTS if n in SHARD_AXIS and n not in GATHER_BF16]
SHARDED = GATHER_BF16 + GATHER_F32
REPLICATED = [n for n in WEIGHTS if n not in SHARD_AXIS]


def _pcall(body, **kw):
    return pl.pallas_call(body, **kw)


def _params(n_grid):
    return pltpu.CompilerParams(dimension_semantics=("arbitrary",) * n_grid, vmem_limit_bytes=VMEM_LIMIT)


def _shift_rows(x, d, up):
    n = x.shape[0]
    idx = lax.broadcasted_iota(jnp.int32, x.shape, 0)
    if up:
        return jnp.where(idx < n - d, pltpu.roll(x, n - d, 0), 0.0)
    return jnp.where(idx >= d, pltpu.roll(x, d, 0), 0.0)


@functools.partial(jax.custom_vjp, nondiff_argnums=(1,))
def _shift_down(x, d):
    return _shift_rows(x, d, False)


def _shift_down_fwd(x, d):
    return _shift_rows(x, d, False), None


def _shift_down_bwd(d, _, g):
    return (_shift_rows(g, d, True),)


_shift_down.defvjp(_shift_down_fwd, _shift_down_bwd)


def _scan_doubling(a, b, up):
    n = a.shape[0]
    d = 1
    while d < n:
        b = b + a * _shift_rows(b, d, up)
        a = a * _shift_rows(a, d, up)
        d *= 2
    return b


@jax.custom_vjp
def _linear_scan(a, b):
    return _scan_doubling(a, b, False)


def _linear_scan_fwd(a, b):
    h = _scan_doubling(a, b, False)
    return h, (a, h)


def _linear_scan_bwd(res, dh):
    a, h = res
    g = _scan_doubling(_shift_rows(a, 1, True), dh, True)
    return g * _shift_rows(h, 1, False), g


_linear_scan.defvjp(_linear_scan_fwd, _linear_scan_bwd)


def _causal_conv(x, w, b):
    k = w.shape[0]
    out = b + x * w[k - 1:k]
    for j in range(k - 1):
        out = out + _shift_down(x, k - 1 - j) * w[j:j + 1]
    return out


def _gelu(x):
    return jax.nn.gelu(x, approximate=True)


def _rmsnorm(x, g):
    return x * lax.rsqrt(jnp.mean(x * x, axis=-1, keepdims=True) + RMS_EPS) * g


def _neg_expm1(x):
    series = x * (1.0 + x * 0.5 * (1.0 + x * (1.0 / 3.0) * (1.0 + x * 0.25 * (1.0 + x * 0.2))))
    return -jnp.where(x > -0.1, series, jnp.exp(x) - 1.0)


def _dot(a, b, ca=1, cb=0, precision=None):
    return lax.dot_general(a, b, (((ca,), (cb,)), ((), ())), precision=precision, preferred_element_type=F32)


def _bdot(a, b):
    return _dot(a.astype(BF16), b.astype(BF16))


def _split_bf16(x):
    hi = x.astype(BF16)
    return hi, (x - hi.astype(F32)).astype(BF16)


def _head_sum_impl(x):
    row = lax.broadcasted_iota(jnp.int32, (D, LANE), 0)
    col = lax.broadcasted_iota(jnp.int32, (D, LANE), 1)
    e = (lax.shift_right_logical(row, 6) == col).astype(BF16)
    hi, lo = _split_bf16(x)
    s_hi, s_lo = _split_bf16(_dot(hi, e) + _dot(lo, e))
    return _dot(s_hi, e, 1, 1) + _dot(s_lo, e, 1, 1)


@jax.custom_vjp
def _head_sum(x):
    return _head_sum_impl(x)


def _head_sum_fwd(x):
    return _head_sum_impl(x), None


def _head_sum_bwd(_, g):
    return (_head_sum(g),)


_head_sum.defvjp(_head_sum_fwd, _head_sum_bwd)


def _specs(items):
    return [pl.BlockSpec(it[1], it[2]) for it in items]


def _stage_fwd(name, f, grid, ins, params, outs):
    n_in = len(ins) + len(params)

    def body(*refs):
        res = f(*[r[...] for r in refs[:n_in]])
        for o, v in zip(refs[n_in:], res):
            o[...] = v.astype(o.dtype)

    return _pcall(
        body, name=name, grid=grid, in_specs=_specs(ins + params),
        out_specs=[pl.BlockSpec(bs, im) for (_, _, bs, im) in outs],
        out_shape=[jax.ShapeDtypeStruct(s, dt) for (s, dt, _, _) in outs],
        compiler_params=_params(len(grid)),
    )(*[it[0] for it in ins + params])


def _stage_bwd(name, f, grid, ins, params, douts, din_dtypes, adds=None):
    adds = adds or {}
    n_in, n_par = len(ins), len(params)
    dout_items = [(a, bs, im) for (arrs, bs, im) in douts for a in arrs]
    add_items = [(adds[i], ins[i][1], ins[i][2]) for i in sorted(adds)]
    n_do, n_add = len(dout_items), len(add_items)

    def body(*refs):
        vals = [r[...] for r in refs[:n_in + n_par]]
        do_refs = list(refs[n_in + n_par:n_in + n_par + n_do])
        add_refs = dict(zip(sorted(adds), refs[n_in + n_par + n_do:n_in + n_par + n_do + n_add]))
        din_refs = refs[n_in + n_par + n_do + n_add:n_in + n_par + n_do + n_add + n_in]
        dpar_refs = refs[n_in + n_par + n_do + n_add + n_in:]
        cts = []
        for (arrs, _, _) in douts:
            ct = do_refs.pop(0)[...].astype(F32)
            for _ in arrs[1:]:
                ct = ct + do_refs.pop(0)[...].astype(F32)
            cts.append(ct)
        _, vjp = jax.vjp(f, *vals)
        grads = vjp(tuple(cts))
        for i, r in enumerate(din_refs):
            g = grads[i]
            if i in add_refs:
                g = g + add_refs[i][...]
            r[...] = g.astype(r.dtype)

        @pl.when(pl.program_id(len(grid) - 1) == 0)
        def _():
            for r in dpar_refs:
                r[...] = jnp.zeros(r.shape, r.dtype)

        for j, r in enumerate(dpar_refs):
            r[...] += grads[n_in + j]

    din_shapes = [it[3][0] if len(it) > 3 else it[0].shape for it in ins]
    din_specs = [pl.BlockSpec(it[1], it[3][1] if len(it) > 3 else it[2]) for it in ins]
    res = _pcall(
        body, name=name, grid=grid, in_specs=_specs(ins + params + dout_items + add_items),
        out_specs=din_specs + _specs(params),
        out_shape=[jax.ShapeDtypeStruct(s, dt) for s, dt in zip(din_shapes, din_dtypes)]
        + [jax.ShapeDtypeStruct(it[0].shape, F32) for it in params],
        compiler_params=_params(len(grid)),
    )(*[it[0] for it in ins + params + dout_items + add_items])
    return list(res[:n_in]), list(res[n_in:])


def _tile(n, want):
    t = min(n, want)
    while n % t:
        t //= 2
    return t


def _matmul(name, a, b, *, mode="nn", resid=None, out_dtype=F32, tm=512, tn=512, tk=1024, b_koff=0):
    if mode == "tn":
        (kdim, m), n = a.shape, b.shape[1]
    else:
        (m, kdim), n = a.shape, (b.shape[1] if mode == "nn" else b.shape[0])
    tm, tn, tk = _tile(m, tm), _tile(n, tn), _tile(kdim, tk)
    nk = kdim // tk
    ko = b_koff // tk
    assert ko * tk == b_koff
    a_spec = pl.BlockSpec((tk, tm), lambda i, j, k: (k, i)) if mode == "tn" else pl.BlockSpec((tm, tk), lambda i, j, k: (i, k))
    b_spec = pl.BlockSpec((tn, tk), lambda i, j, k: (j, k + ko)) if mode == "nt" else pl.BlockSpec((tk, tn), lambda i, j, k: (k + ko, j))
    ca = 0 if mode == "tn" else 1
    cb = 1 if mode == "nt" else 0
    operands = [a, b]
    in_specs = [a_spec, b_spec]
    if resid is not None:
        operands.append(resid)
        in_specs.append(pl.BlockSpec((tm, tn), lambda i, j, k: (i, j)))

    def body(*refs):
        a_ref, b_ref = refs[0], refs[1]
        o_ref, acc_ref = refs[-2], refs[-1]
        k = pl.program_id(2)

        @pl.when(k == 0)
        def _():
            acc_ref[...] = jnp.zeros(acc_ref.shape, F32)

        acc_ref[...] += _dot(a_ref[...].astype(BF16), b_ref[...].astype(BF16), ca, cb)

        @pl.when(k == nk - 1)
        def _():
            r = acc_ref[...]
            if resid is not None:
                r = r + refs[2][...]
            o_ref[...] = r.astype(o_ref.dtype)

    return _pcall(
        body, name=name, grid=(m // tm, n // tn, nk), in_specs=in_specs,
        out_specs=pl.BlockSpec((tm, tn), lambda i, j, k: (i, j)),
        out_shape=jax.ShapeDtypeStruct((m, n), out_dtype),
        scratch_shapes=[pltpu.VMEM((tm, tn), F32)],
        compiler_params=_params(3),
    )(*operands)


def _add(name, a, b):
    rows, cols = a.shape
    tr = _tile(rows, 512)

    def body(a_ref, b_ref, o_ref):
        o_ref[...] = a_ref[...] + b_ref[...]

    spec = pl.BlockSpec((tr, cols), lambda i: (i, 0))
    return _pcall(body, name=name, grid=(rows // tr,), in_specs=[spec, spec], out_specs=spec,
                  out_shape=jax.ShapeDtypeStruct(a.shape, a.dtype), compiler_params=_params(1))(a, b)


def _f_lru_pre(x, norm, b_out):
    return _rmsnorm(x, norm), x + b_out


def _f_lru_core(uy, ux, b_y, b_x, cw, cb, gw, gb, lam):
    yb = _gelu(uy + b_y)
    xr = _causal_conv(ux + b_x, cw, cb)
    gr = jax.nn.sigmoid(_bdot(xr, gw[0, 0]) + gb[0:1])
    gi = jax.nn.sigmoid(_bdot(xr, gw[1, 0]) + gb[1:2])
    log_a = -LRU_C * gr * jax.nn.softplus(-lam)
    a = jnp.exp(log_a)
    bterm = jnp.sqrt(_neg_expm1(2.0 * log_a)) * (gi * xr)
    return (_linear_scan(a, bterm) * yb,)


def _f_norm(x, norm):
    return (_rmsnorm(x, norm),)


def _f_ffn_core(ug, uv, cw, cb):
    return (_gelu(_causal_conv(ug, cw, cb)) * uv,)


def _f_rwkv_mix(h, mix):
    xx = _shift_down(h, 1) - h
    return tuple(h + xx * mix[i:i + 1] for i in range(6))


def _f_rwkv_pre(k, lw1, la1, lg1, w0, a0, k_k, k_a, w2, a2, g2):
    wpre = w0 + _bdot(jnp.tanh(lw1), w2)
    apre = a0 + _bdot(la1, a2)
    g = _bdot(jax.nn.sigmoid(lg1), g2)
    log_decay = -jnp.exp(-jax.nn.softplus(-wpre) - 0.5)
    a = jax.nn.sigmoid(apre)
    kk = k * k_k
    kk = kk / jnp.maximum(jnp.sqrt(_head_sum(kk * kk)), 1e-12)
    kf = k * (1.0 + (a - 1.0) * k_a)
    return log_decay, kf, -kk, kk * a, g


def _f_rwkv_post(y, r, kf, v, g, ln_w, ln_b, r_k):
    inv = 1.0 / HEAD
    yc = y - _head_sum(y) * inv
    var = _head_sum(yc * yc) * inv
    yn = yc * lax.rsqrt(var + GN_EPS) * ln_w + ln_b
    bonus = _head_sum(r * kf * r_k) * v
    return ((yn + bonus) * g,)


def _hdot(a, b, ca, cb):
    return lax.dot_general(a, b, (((ca,), (cb,)), ((0,), (0,))), preferred_element_type=F32)


def _mm3_impl(a, b, ca, cb):
    a_hi, a_lo = _split_bf16(a)
    b_hi, b_lo = _split_bf16(b)
    lhs = jnp.concatenate([a_hi, a_hi, a_lo], axis=ca)
    rhs = jnp.concatenate([b_hi, b_lo, b_hi], axis=cb)
    return _hdot(lhs, rhs, ca, cb)


@functools.partial(jax.custom_vjp, nondiff_argnums=(2, 3))
def _mm3(a, b, ca, cb):
    return _mm3_impl(a, b, ca, cb)


def _mm3_fwd(a, b, ca, cb):
    return _mm3_impl(a, b, ca, cb), (a, b)


def _mm3_bwd(ca, cb, res, dc):
    a, b = res
    fa, fb = 3 - ca, 3 - cb
    da = _mm3(dc, b, 2, fb) if ca == 2 else _mm3(b, dc, fb, 2)
    db = _mm3(a, dc, fa, 1) if cb == 1 else _mm3(dc, a, 1, fa)
    return da, db


_mm3.defvjp(_mm3_fwd, _mm3_bwd)


def _tri_dot_impl(x, upper):
    g, n, _ = x.shape
    row = lax.broadcasted_iota(jnp.int32, (g, n, 3 * n), 1)
    col = lax.broadcasted_iota(jnp.int32, (g, n, 3 * n), 2)
    col = jnp.where(col >= 2 * n, col - 2 * n, jnp.where(col >= n, col - n, col))
    tri = (row <= col if upper else row >= col).astype(BF16)
    hi = x.astype(BF16)
    rem = x - hi.astype(F32)
    mid = rem.astype(BF16)
    lo = (rem - mid.astype(F32)).astype(BF16)
    return _hdot(tri, jnp.concatenate([hi, mid, lo], axis=1), 2, 1)


@functools.partial(jax.custom_vjp, nondiff_argnums=(1,))
def _tri_dot(x, upper):
    return _tri_dot_impl(x, upper)


def _tri_dot_fwd(x, upper):
    return _tri_dot_impl(x, upper), None


def _tri_dot_bwd(upper, _, g):
    return (_tri_dot(g, not upper),)


_tri_dot.defvjp(_tri_dot_fwd, _tri_dot_bwd)


def _rwkv_chunk(z0, r, lw, k, v, a, b):
    g, n, _ = r.shape
    row = lax.broadcasted_iota(jnp.int32, (g, n, n), 1)
    col = lax.broadcasted_iota(jnp.int32, (g, n, n), 2)
    incl, strict = row >= col, row > col
    cs = _tri_dot(lw, False)
    c_last = cs[:, n - 1:n]
    inv = jnp.exp(-cs)
    ar = jnp.concatenate([a * jnp.exp(cs - lw), r * jnp.exp(cs)], axis=1)
    bk = jnp.concatenate([b * inv, k * inv], axis=1)
    pair = _mm3(ar, bk, 2, 2)
    a_ab = jnp.where(strict, pair[:, :n, :n], 0.0)
    a_ak = jnp.where(strict, pair[:, :n, n:], 0.0)
    a_rbk = jnp.concatenate([jnp.where(incl, pair[:, n:, :n], 0.0), jnp.where(incl, pair[:, n:, n:], 0.0)], axis=2)
    arz = _mm3(ar, z0, 2, 1)
    inv_m = (row == col).astype(F32) + a_ab
    p = a_ab
    for _ in range(int(math.log2(n)) - 1):
        p = _mm3(p, p, 2, 1)
        inv_m = inv_m + _mm3(inv_m, p, 2, 1)
    u = _mm3(inv_m, arz[:, :n] + _mm3(a_ak, v, 2, 1), 2, 1)
    uv = jnp.concatenate([u, v], axis=1)
    y = arz[:, n:] + _mm3(a_rbk, uv, 2, 1)
    tail = jnp.exp(c_last - cs)
    er = lax.broadcasted_iota(jnp.int32, (g, HEAD, HEAD), 1)
    ec = lax.broadcasted_iota(jnp.int32, (g, HEAD, HEAD), 2)
    decay_all = jnp.where(er == ec, jnp.exp(c_last), 0.0)
    z_l = _mm3(jnp.concatenate([b * tail, k * tail, decay_all], axis=1), jnp.concatenate([uv, z0], axis=1), 1, 1)
    return y, z_l


def _heads_in(ref, g):
    x = ref[...]
    return jnp.stack([x[:, h * HEAD:(h + 1) * HEAD] for h in range(g)], axis=0)


def _heads_out(ref, x):
    ref[...] = jnp.concatenate([x[h] for h in range(x.shape[0])], axis=1)


def _scan_geometry(n_rows, t):
    g, n = _tile(N_HEAD, SCAN_HEADS), _tile(t, CHUNK)
    return g, n, t // n, N_HEAD // g, n_rows // t


def _rwkv_scan_fwd(seqs, t):
    rows = seqs[0].shape[0]
    g, n, nc, hpg, bsz = _scan_geometry(rows, t)

    def body(r_ref, lw_ref, k_ref, v_ref, a_ref, b_ref, y_ref, zs_ref, z_ref):
        @pl.when(pl.program_id(1) == 0)
        def _():
            z_ref[...] = jnp.zeros(z_ref.shape, F32)

        z0 = z_ref[...]
        zs_ref[:, 0] = z0
        y, z_l = _rwkv_chunk(z0, *[_heads_in(ref, g) for ref in (r_ref, lw_ref, k_ref, v_ref, a_ref, b_ref)])
        _heads_out(y_ref, y)
        z_ref[...] = z_l

    seq_spec = pl.BlockSpec((n, g * HEAD), lambda i, c: ((i // hpg) * nc + c, i % hpg))
    return _pcall(
        body, name="rwkv_scan_fwd", grid=(bsz * hpg, nc), in_specs=[seq_spec] * 6,
        out_specs=[seq_spec, pl.BlockSpec((g, 1, HEAD, HEAD), lambda i, c: (i, c, 0, 0))],
        out_shape=[jax.ShapeDtypeStruct((rows, D), F32), jax.ShapeDtypeStruct((bsz * N_HEAD, nc, HEAD, HEAD), F32)],
        scratch_shapes=[pltpu.VMEM((g, HEAD, HEAD), F32)],
        compiler_params=_params(2),
    )(*seqs)


def _rwkv_scan_bwd(seqs, zs, dy, t):
    rows = seqs[0].shape[0]
    g, n, nc, hpg, bsz = _scan_geometry(rows, t)

    def body(r_ref, lw_ref, k_ref, v_ref, a_ref, b_ref, zs_ref, dy_ref, dr, dlw, dk, dv, da, db, dz_ref):
        @pl.when(pl.program_id(1) == 0)
        def _():
            dz_ref[...] = jnp.zeros(dz_ref.shape, F32)

        _, vjp = jax.vjp(_rwkv_chunk, zs_ref[:, 0],
                         *[_heads_in(ref, g) for ref in (r_ref, lw_ref, k_ref, v_ref, a_ref, b_ref)])
        grads = vjp((_heads_in(dy_ref, g), dz_ref[...]))
        dz_ref[...] = grads[0]
        for o, gr in zip((dr, dlw, dk, dv, da, db), grads[1:]):
            _heads_out(o, gr)

    seq_spec = pl.BlockSpec((n, g * HEAD), lambda i, c: ((i // hpg) * nc + nc - 1 - c, i % hpg))
    return _pcall(
        body, name="rwkv_scan_bwd", grid=(bsz * hpg, nc),
        in_specs=[seq_spec] * 6 + [pl.BlockSpec((g, 1, HEAD, HEAD), lambda i, c: (i, nc - 1 - c, 0, 0)), seq_spec],
        out_specs=[seq_spec] * 6,
        out_shape=[jax.ShapeDtypeStruct((rows, D), F32)] * 6,
        scratch_shapes=[pltpu.VMEM((g, HEAD, HEAD), F32)],
        compiler_params=_params(2),
    )(*seqs, zs, dy)


def _loss_head(x, target, norm):
    n = x.shape[0]
    tr = _tile(n, 256)

    def f(xv, gv, tv):
        err = _rmsnorm(xv, gv) - tv
        return 0.5 * jnp.sum(jnp.mean(err * err, axis=-1, keepdims=True), axis=0, keepdims=True)

    def body(x_ref, t_ref, g_ref, dx_ref, dg_ref, loss_ref):
        val, vjp = jax.vjp(lambda xv, gv: f(xv, gv, t_ref[...]), x_ref[...], g_ref[...])
        dx, dg = vjp(jnp.ones((1, 1), F32))
        dx_ref[...] = dx

        @pl.when(pl.program_id(0) == 0)
        def _():
            dg_ref[...] = jnp.zeros(dg_ref.shape, F32)
            loss_ref[...] = jnp.zeros(loss_ref.shape, F32)

        dg_ref[...] += dg
        loss_ref[...] += jnp.broadcast_to(val, loss_ref.shape)

    row = pl.BlockSpec((tr, D), lambda i: (i, 0))
    vec = pl.BlockSpec((1, D), lambda i: (0, 0))
    dx, dg, loss = _pcall(
        body, name="loss_head", grid=(n // tr,), in_specs=[row, row, vec],
        out_specs=[row, vec, pl.BlockSpec((8, LANE), lambda i: (0, 0))],
        out_shape=[jax.ShapeDtypeStruct((n, D), F32), jax.ShapeDtypeStruct((1, D), F32),
                   jax.ShapeDtypeStruct((8, LANE), F32)],
        compiler_params=_params(1),
    )(x, target, norm)
    return loss[0, 0], dx, dg


def _exchange(name, src, scatter):
    shape = src.shape[1:] if scatter else src.shape

    def body(src_ref, dst_ref, send_sems, recv_sems, local_sem):
        x, y, c = lax.axis_index("x"), lax.axis_index("y"), lax.axis_index("c")
        me = 4 * x + 2 * y + c
        own = src_ref.at[me] if scatter else src_ref
        mine = pltpu.make_async_copy(own, dst_ref.at[me], local_sem)
        mine.start()
        copies = []
        for m in range(1, N_DEV):
            px = 1 - x if m & 4 else x
            py = 1 - y if m & 2 else y
            pc = 1 - c if m & 1 else c
            part = src_ref.at[4 * px + 2 * py + pc] if scatter else src_ref
            cp = pltpu.make_async_remote_copy(
                src_ref=part, dst_ref=dst_ref.at[me], send_sem=send_sems.at[m - 1], recv_sem=recv_sems.at[m - 1],
                device_id=(px, py, pc), device_id_type=pl.DeviceIdType.MESH)
            cp.start()
            copies.append(cp)
        for cp in copies:
            cp.wait_recv()
        for cp in copies:
            cp.wait_send()
        mine.wait()

    return _pcall(
        body, name=name, in_specs=[pl.BlockSpec(memory_space=pl.ANY)],
        out_specs=pl.BlockSpec(memory_space=pl.ANY),
        out_shape=jax.ShapeDtypeStruct((N_DEV,) + tuple(shape), src.dtype),
        scratch_shapes=[pltpu.SemaphoreType.DMA((N_DEV - 1,)), pltpu.SemaphoreType.DMA((N_DEV - 1,)),
                        pltpu.SemaphoreType.DMA],
    )(src)


def _adamw(name, parts, w, m, v):
    rows = w.shape[0]
    tr = _tile(rows, PACK_ROWS)

    def body(p_ref, w_ref, m_ref, v_ref, g_ref, d_ref, nm_ref, nv_ref):
        g = p_ref[0]
        for k in range(1, N_DEV):
            g = g + p_ref[k]
        nm = ADAM_B1 * m_ref[...] + (1.0 - ADAM_B1) * g
        nv = ADAM_B2 * v_ref[...] + (1.0 - ADAM_B2) * jnp.square(g)
        m_hat = nm / (1.0 - ADAM_B1 ** ADAM_STEP)
        v_hat = nv / (1.0 - ADAM_B2 ** ADAM_STEP)
        g_ref[...] = g
        d_ref[...] = -ADAM_LR * (m_hat / (jnp.sqrt(v_hat) + ADAM_EPS) + ADAM_WD * w_ref[...])
        nm_ref[...] = nm
        nv_ref[...] = nv

    row = pl.BlockSpec((tr, PACK_W), lambda i: (i, 0))
    return _pcall(
        body, name=name, grid=(rows // tr,),
        in_specs=[pl.BlockSpec((N_DEV, tr, PACK_W), lambda i: (0, i, 0)), row, row, row],
        out_specs=[row] * 4, out_shape=[jax.ShapeDtypeStruct((rows, PACK_W), F32)] * 4,
        compiler_params=_params(1),
    )(parts, w, m, v)


def _pack(arrs, dtype, lead=()):
    flat = jnp.concatenate([a.astype(dtype).reshape(lead + (-1,)) for a in arrs], axis=-1)
    n = flat.shape[-1]
    quantum = PACK_W * PACK_ROWS
    total = -(-n // quantum) * quantum
    flat = jnp.pad(flat, [(0, 0)] * len(lead) + [(0, total - n)])
    return flat.reshape(lead + (total // PACK_W, PACK_W))


def _unpack(buf, shapes, lead=()):
    flat = buf.reshape(lead + (-1,))
    out, off = [], 0
    for s in shapes:
        n = math.prod(s)
        out.append(flat[..., off:off + n].reshape(lead + tuple(s)))
        off += n
    return out


def _unshard(g, axis):
    local = g.shape[1:]
    full = jnp.moveaxis(g, 0, axis)
    return full.reshape(local[:axis] + (N_DEV * local[axis],) + local[axis + 1:])


def _reshard(full, axis):
    s = full.shape
    blocked = full.reshape(s[:axis] + (N_DEV, s[axis] // N_DEV) + s[axis + 1:])
    return jnp.moveaxis(blocked, axis, 0)


def _row_item(a, tr):
    return (a, (tr, a.shape[1]), lambda i: (i, 0))


def _vec_item(a):
    return (a, a.shape, lambda i: (0,) * a.ndim)


def _seq_item(a, t, width):
    return (a, (t, width), lambda j, b: (b, j))


def _seq_halves(a, t, width):
    half = a.shape[1] // 2
    off = half // width
    grad = ((a.shape[0], half), lambda j, b: (b, j))
    return [(a, (t, width), lambda j, b: (b, j), grad), (a, (t, width), lambda j, b: (b, j + off), grad)]


def _col_item(a, width):
    return (a, (a.shape[0], width), lambda j, b: (0, j))


def _ffn_fwd(tag, x, p, t):
    n = x.shape[0]
    tr = _tile(n, 512)
    norm_ins, norm_par = [_row_item(x, tr)], [_vec_item(p["norm"])]
    (h,) = _stage_fwd(f"{tag}_norm", _f_norm, (n // tr,), norm_ins, norm_par, [((n, D), BF16, (tr, D), lambda i: (i, 0))])
    u = _matmul(f"{tag}_up", h, p["w_up"], tm=1024, tn=768)
    grid = (D_FF // CT, n // t)
    core_par = [_col_item(p["conv_w"], CT), _col_item(p["conv_b"], CT)]
    (hid,) = _stage_fwd(f"{tag}_core", _f_ffn_core, grid, _seq_halves(u, t, CT), core_par,
                        [((n, D_FF), BF16, (t, CT), lambda j, b: (b, j))])
    y = _matmul(f"{tag}_down", hid, p["w_down"], resid=x, tm=1024, tn=1024)
    return y, (x, h, u, hid)


def _ffn_bwd(tag, dy, saved, p, t):
    x, h, u, hid = saved
    n = x.shape[0]
    tr = _tile(n, 512)
    grads = {}
    dhid = _matmul(f"{tag}_down_da", dy, p["w_down"], mode="nt", tm=1024, tn=768)
    grads["w_down"] = _matmul(f"{tag}_down_dw", hid, dy, mode="tn", tm=768, tn=1024)
    grid = (D_FF // CT, n // t)
    core_par = [_col_item(p["conv_w"], CT), _col_item(p["conv_b"], CT)]
    (dug, duv), (grads["conv_w"], grads["conv_b"]) = _stage_bwd(
        f"{tag}_core_bwd", _f_ffn_core, grid, _seq_halves(u, t, CT), core_par,
        [([dhid], (t, CT), lambda j, b: (b, j))], [BF16, BF16])
    dh = _matmul(f"{tag}_up_da_g", dug, p["w_up"], mode="nt", tm=1024, tn=1024)
    dh = _matmul(f"{tag}_up_da_v", duv, p["w_up"], mode="nt", tm=1024, tn=1024, b_koff=D_FF, resid=dh)
    grads["w_up"] = jnp.concatenate([_matmul(f"{tag}_up_dw_g", h, dug, mode="tn", tm=1024, tn=768),
                                     _matmul(f"{tag}_up_dw_v", h, duv, mode="tn", tm=1024, tn=768)], axis=1)
    (dx,), (grads["norm"],) = _stage_bwd(
        f"{tag}_norm_bwd", _f_norm, (n // tr,), [_row_item(x, tr)], [_vec_item(p["norm"])],
        [([dh], (tr, D), lambda i: (i, 0))], [F32], adds={0: dy})
    return dx, grads


def _lru_fwd(x, p, t):
    n = x.shape[0]
    tr = _tile(n, 512)
    row = lambda dt: ((n, D), dt, (tr, D), lambda i: (i, 0))
    h, xb = _stage_fwd("lru_pre", _f_lru_pre, (n // tr,), [_row_item(x, tr)],
                       [_vec_item(p["norm"]), _vec_item(p["b_out"])], [row(BF16), row(F32)])
    u = _matmul("lru_in", h, p["w_in"], tm=1024, tn=1024)
    out, = _stage_fwd("lru_core", _f_lru_core, (D // CT, n // t), _seq_halves(u, t, CT), _lru_core_params(p),
                      [((n, D), BF16, (t, CT), lambda j, b: (b, j))])
    y = _matmul("lru_out", out, p["w_out"], resid=xb, tm=1024, tn=1024)
    return y, (x, h, u, out)


def _lru_core_params(p):
    return [_col_item(p["b_y"], CT), _col_item(p["b_x"], CT), _col_item(p["conv_w"], CT), _col_item(p["conv_b"], CT),
            (p["gate_w"], (2, 1, CT, CT), lambda j, b: (0, j, 0, 0)), _col_item(p["gate_b"], CT), _col_item(p["lam"], CT)]


def _lru_bwd(dy, saved, p, t):
    x, h, u, out = saved
    n = x.shape[0]
    tr = _tile(n, 512)
    grads = {}
    dout = _matmul("lru_out_da", dy, p["w_out"], mode="nt", tm=1024, tn=1024)
    grads["w_out"] = _matmul("lru_out_dw", out, dy, mode="tn", tm=1024, tn=1024)
    (duy, dux), core_grads = _stage_bwd("lru_core_bwd", _f_lru_core, (D // CT, n // t), _seq_halves(u, t, CT),
                                        _lru_core_params(p), [([dout], (t, CT), lambda j, b: (b, j))], [BF16, BF16])
    for k, g in zip(("b_y", "b_x", "conv_w", "conv_b", "gate_w", "gate_b", "lam"), core_grads):
        grads[k] = g
    dh = _matmul("lru_in_da_y", duy, p["w_in"], mode="nt", tm=1024, tn=1024)
    dh = _matmul("lru_in_da_x", dux, p["w_in"], mode="nt", tm=1024, tn=1024, b_koff=D, resid=dh)
    grads["w_in"] = jnp.concatenate([_matmul("lru_in_dw_y", h, duy, mode="tn", tm=1024, tn=1024),
                                     _matmul("lru_in_dw_x", h, dux, mode="tn", tm=1024, tn=1024)], axis=1)
    row = (tr, D), lambda i: (i, 0)
    (dx,), (grads["norm"], grads["b_out"]) = _stage_bwd(
        "lru_pre_bwd", _f_lru_pre, (n // tr,), [_row_item(x, tr)], [_vec_item(p["norm"]), _vec_item(p["b_out"])],
        [([dh], *row), ([dy], *row)], [F32])
    return dx, grads


_RWKV_PRE_VECS = ("w0", "a0", "k_k", "k_a", "w2", "a2", "g2")
_RWKV_POST_VECS = ("ln_w", "ln_b", "r_k")


def _rwkv_fwd(x, p, t):
    n = x.shape[0]
    bsz = n // t
    tr = _tile(n, 512)
    ts = _tile(n, 128)
    (h,) = _stage_fwd("rwkv_norm", _f_norm, (n // tr,), [_row_item(x, tr)], [_vec_item(p["norm"])],
                      [((n, D), F32, (tr, D), lambda i: (i, 0))])
    mixed = _stage_fwd("rwkv_mix", _f_rwkv_mix, (D // CT, bsz), [_seq_item(h, t, CT)], [_col_item(p["mix"], CT)],
                       [((n, D), BF16, (t, CT), lambda j, b: (b, j))] * 6)
    xr, xk, xv, xw, xa, xg = mixed
    r = _matmul("rwkv_r", xr, p["w_r"], tm=1024, tn=1024)
    k = _matmul("rwkv_k", xk, p["w_k"], tm=1024, tn=1024)
    v = _matmul("rwkv_v", xv, p["w_v"], tm=1024, tn=1024)
    lw1 = _matmul("rwkv_w1", xw, p["w1"], tm=1024)
    la1 = _matmul("rwkv_a1", xa, p["a1"], tm=1024)
    lg1 = _matmul("rwkv_g1", xg, p["g1"], tm=1024)
    pre_ins = [_row_item(a, ts) for a in (k, lw1, la1, lg1)]
    pre_par = [_vec_item(p[q]) for q in _RWKV_PRE_VECS]
    row = ((n, D), F32, (ts, D), lambda i: (i, 0))
    lw, kf, aa, bb, g = _stage_fwd("rwkv_pre", _f_rwkv_pre, (n // ts,), pre_ins, pre_par, [row] * 5)
    y, zs = _rwkv_scan_fwd([r, lw, kf, v, aa, bb], t)
    post_ins = [_row_item(a, ts) for a in (y, r, kf, v, g)]
    post_par = [_vec_item(p[q]) for q in _RWKV_POST_VECS]
    (yg,) = _stage_fwd("rwkv_post", _f_rwkv_post, (n // ts,), post_ins, post_par,
                       [((n, D), BF16, (ts, D), lambda i: (i, 0))])
    out = _matmul("rwkv_out", yg, p["w_out"], resid=x, tm=1024, tn=1024)
    return out, (x, h, mixed, r, k, v, lw1, la1, lg1, lw, kf, aa, bb, g, zs, y, yg)


def _rwkv_bwd(dout, saved, p, t):
    x, h, mixed, r, k, v, lw1, la1, lg1, lw, kf, aa, bb, g, zs, y, yg = saved
    xr, xk, xv, xw, xa, xg = mixed
    n = x.shape[0]
    bsz = n // t
    tr = _tile(n, 512)
    ts = _tile(n, 128)
    grads = {}
    row_s = (ts, D), lambda i: (i, 0)
    dyg = _matmul("rwkv_out_da", dout, p["w_out"], mode="nt", tm=1024, tn=1024)
    grads["w_out"] = _matmul("rwkv_out_dw", yg, dout, mode="tn", tm=1024, tn=1024)
    post_ins = [_row_item(a, ts) for a in (y, r, kf, v, g)]
    post_par = [_vec_item(p[q]) for q in _RWKV_POST_VECS]
    (dy, dr_a, dkf_a, dv_a, dg), post_grads = _stage_bwd(
        "rwkv_post_bwd", _f_rwkv_post, (n // ts,), post_ins, post_par, [([dyg], *row_s)], [F32] * 5)
    grads.update(zip(_RWKV_POST_VECS, post_grads))
    dr_b, dlw, dkf_b, dv_b, daa, dbb = _rwkv_scan_bwd([r, lw, kf, v, aa, bb], zs, dy, t)
    dr = _add("rwkv_dr_sum", dr_a, dr_b)
    dv = _add("rwkv_dv_sum", dv_a, dv_b)
    pre_ins = [_row_item(a, ts) for a in (k, lw1, la1, lg1)]
    pre_par = [_vec_item(p[q]) for q in _RWKV_PRE_VECS]
    (dk, dlw1, dla1, dlg1), pre_grads = _stage_bwd(
        "rwkv_pre_bwd", _f_rwkv_pre, (n // ts,), pre_ins, pre_par,
        [([dlw], *row_s), ([dkf_a, dkf_b], *row_s), ([daa], *row_s), ([dbb], *row_s), ([dg], *row_s)], [F32] * 4)
    grads.update(zip(_RWKV_PRE_VECS, pre_grads))
    dmixed = []
    for tag, xin, dz, wname in (("r", xr, dr, "w_r"), ("k", xk, dk, "w_k"), ("v", xv, dv, "w_v"),
                                ("w1", xw, dlw1, "w1"), ("a1", xa, dla1, "a1"), ("g1", xg, dlg1, "g1")):
        dmixed.append(_matmul(f"rwkv_{tag}_da", dz, p[wname], mode="nt", tm=1024, tn=1024))
        grads[wname] = _matmul(f"rwkv_{tag}_dw", xin, dz, mode="tn", tm=1024, tn=1024)
    seq_blk = (t, CT), lambda j, b: (b, j)
    (dh,), (grads["mix"],) = _stage_bwd(
        "rwkv_mix_bwd", _f_rwkv_mix, (D // CT, bsz), [_seq_item(h, t, CT)], [_col_item(p["mix"], CT)],
        [([a], *seq_blk) for a in dmixed], [F32])
    (dx,), (grads["norm"],) = _stage_bwd(
        "rwkv_norm_bwd", _f_norm, (n // tr,), [_row_item(x, tr)], [_vec_item(p["norm"])],
        [([dh], (tr, D), lambda i: (i, 0))], [F32], adds={0: dout})
    return dx, grads


def _block_diag_gates(gate_w):
    z = jnp.zeros((2, D // CT, HEAD, HEAD), gate_w.dtype)
    even, odd = gate_w[:, 0::2], gate_w[:, 1::2]
    top = jnp.concatenate([even, z], axis=-1)
    bot = jnp.concatenate([z, odd], axis=-1)
    return jnp.concatenate([top, bot], axis=-2)


def _gate_blocks_grad(dg):
    even, odd = dg[:, :, :HEAD, :HEAD], dg[:, :, HEAD:, HEAD:]
    return jnp.stack([even, odd], axis=2).reshape(2, N_HEAD, HEAD, HEAD)


def _step(local, x, target):
    bsz, t, _ = x.shape
    n = bsz * t
    w = {k: local[k] for k in WEIGHTS}

    bf_parts = _exchange("gather_bf16", _pack([w[k] for k in GATHER_BF16], BF16), False)
    f32_parts = _exchange("gather_f32", _pack([w[k] for k in GATHER_F32], F32), False)
    full = {}
    for names, buf in ((GATHER_BF16, bf_parts), (GATHER_F32, f32_parts)):
        for k, g in zip(names, _unpack(buf, [w[k].shape for k in names], (N_DEV,))):
            full[k] = _unshard(g, SHARD_AXIS[k])
    for k in REPLICATED:
        full[k] = w[k]

    lru = dict(norm=full["lru_norm"], b_out=full["lru_b_out"], w_in=full["lru_w_in"][0],
               b_y=full["lru_b_in"][:, :D], b_x=full["lru_b_in"][:, D:], conv_w=full["lru_conv_w"][0],
               conv_b=full["lru_conv_b"], gate_w=_block_diag_gates(full["lru_gate_w"][0]),
               gate_b=full["lru_gate_b"][0].reshape(2, D), lam=full["lru_lambda"], w_out=full["lru_w_out"][0])
    ffn = [dict(norm=full["ffn_norm"][l:l + 1], w_up=full["ffn_w_up"][l],
                conv_w=full["ffn_conv_w"][l], conv_b=full["ffn_conv_b"][l:l + 1], w_down=full["ffn_w_down"][l])
           for l in range(2)]
    rwkv = dict(norm=full["rwkv_norm"], mix=full["rwkv_mix"][0], w_r=full["rwkv_w_rkv"][0, 0],
                w_k=full["rwkv_w_rkv"][0, 1], w_v=full["rwkv_w_rkv"][0, 2], w0=full["rwkv_w0"], w1=full["rwkv_w1"][0],
                w2=full["rwkv_w2"][0], a0=full["rwkv_a0"], a1=full["rwkv_a1"][0], a2=full["rwkv_a2"][0],
                g1=full["rwkv_g1"][0], g2=full["rwkv_g2"][0], k_k=full["rwkv_k_k"], k_a=full["rwkv_k_a"],
                r_k=full["rwkv_r_k"].reshape(1, D), ln_w=full["rwkv_ln_w"], ln_b=full["rwkv_ln_b"],
                w_out=full["rwkv_w_out"][0])

    x0 = x.reshape(n, D)
    x1, s_lru = _lru_fwd(x0, lru, t)
    x2, s_ffn0 = _ffn_fwd("ffn0", x1, ffn[0], t)
    x3, s_rwkv = _rwkv_fwd(x2, rwkv, t)
    x4, s_ffn1 = _ffn_fwd("ffn1", x3, ffn[1], t)
    loss, dx4, d_final = _loss_head(x4, target.reshape(n, D), full["final_norm"].reshape(1, D))
    dx3, g_ffn1 = _ffn_bwd("ffn1", dx4, s_ffn1, ffn[1], t)
    dx2, g_rwkv = _rwkv_bwd(dx3, s_rwkv, rwkv, t)
    dx1, g_ffn0 = _ffn_bwd("ffn0", dx2, s_ffn0, ffn[0], t)
    dx0, g_lru = _lru_bwd(dx1, s_lru, lru, t)

    g_ffn = [g_ffn0, g_ffn1]
    gfull = {
        "lru_norm": g_lru["norm"], "lru_w_in": g_lru["w_in"][None],
        "lru_b_in": jnp.concatenate([g_lru["b_y"], g_lru["b_x"]], axis=1), "lru_conv_w": g_lru["conv_w"][None],
        "lru_conv_b": g_lru["conv_b"], "lru_gate_w": _gate_blocks_grad(g_lru["gate_w"])[None],
        "lru_gate_b": g_lru["gate_b"].reshape(1, 2, N_HEAD, HEAD), "lru_lambda": g_lru["lam"],
        "lru_w_out": g_lru["w_out"][None], "lru_b_out": g_lru["b_out"],
        "rwkv_norm": g_rwkv["norm"], "rwkv_mix": g_rwkv["mix"][None],
        "rwkv_w_rkv": jnp.stack([g_rwkv["w_r"], g_rwkv["w_k"], g_rwkv["w_v"]])[None],
        "rwkv_w0": g_rwkv["w0"], "rwkv_w1": g_rwkv["w1"][None], "rwkv_w2": g_rwkv["w2"][None],
        "rwkv_a0": g_rwkv["a0"], "rwkv_a1": g_rwkv["a1"][None], "rwkv_a2": g_rwkv["a2"][None],
        "rwkv_g1": g_rwkv["g1"][None], "rwkv_g2": g_rwkv["g2"][None], "rwkv_k_k": g_rwkv["k_k"],
        "rwkv_k_a": g_rwkv["k_a"], "rwkv_r_k": g_rwkv["r_k"].reshape(1, N_HEAD, HEAD), "rwkv_ln_w": g_rwkv["ln_w"],
        "rwkv_ln_b": g_rwkv["ln_b"], "rwkv_w_out": g_rwkv["w_out"][None],
        "ffn_norm": jnp.concatenate([g["norm"] for g in g_ffn]),
        "ffn_w_up": jnp.stack([g["w_up"] for g in g_ffn]),
        "ffn_conv_w": jnp.stack([g["conv_w"] for g in g_ffn]),
        "ffn_conv_b": jnp.concatenate([g["conv_b"] for g in g_ffn]),
        "ffn_w_down": jnp.stack([g["w_down"] for g in g_ffn]),
        "final_norm": d_final.reshape(D),
    }

    out = {}
    sh_parts = _exchange("grad_scatter", _pack([_reshard(gfull[k], SHARD_AXIS[k]) for k in SHARDED], F32, (N_DEV,)), True)
    rp_parts = _exchange("grad_gather", _pack([gfull[k] for k in REPLICATED], F32), False)
    for tag, names, parts in (("adamw_sharded", SHARDED, sh_parts), ("adamw_replicated", REPLICATED, rp_parts)):
        packs = [_pack([local[pre + k] for k in names], F32) for pre in ("", "m_", "v_")]
        res = _adamw(tag, parts, *packs)
        for kind, buf in zip(("grad", "delta", "new_m", "new_v"), res):
            for k, a in zip(names, _unpack(buf, [w[k].shape for k in names])):
                out[(kind, k)] = a
    return loss, dx0.reshape(x.shape), out


def kernel(x, lru_norm, lru_w_in, lru_b_in, lru_conv_w, lru_conv_b, lru_gate_w, lru_gate_b, lru_lambda, lru_w_out, lru_b_out, rwkv_norm, rwkv_mix, rwkv_w_rkv, rwkv_w0, rwkv_w1, rwkv_w2, rwkv_a0, rwkv_a1, rwkv_a2, rwkv_g1, rwkv_g2, rwkv_k_k, rwkv_k_a, rwkv_r_k, rwkv_ln_w, rwkv_ln_b, rwkv_w_out, ffn_norm, ffn_w_up, ffn_conv_w, ffn_conv_b, ffn_w_down, final_norm, loss_target, m_lru_norm, m_lru_w_in, m_lru_b_in, m_lru_conv_w, m_lru_conv_b, m_lru_gate_w, m_lru_gate_b, m_lru_lambda, m_lru_w_out, m_lru_b_out, m_rwkv_norm, m_rwkv_mix, m_rwkv_w_rkv, m_rwkv_w0, m_rwkv_w1, m_rwkv_w2, m_rwkv_a0, m_rwkv_a1, m_rwkv_a2, m_rwkv_g1, m_rwkv_g2, m_rwkv_k_k, m_rwkv_k_a, m_rwkv_r_k, m_rwkv_ln_w, m_rwkv_ln_b, m_rwkv_w_out, m_ffn_norm, m_ffn_w_up, m_ffn_conv_w, m_ffn_conv_b, m_ffn_w_down, m_final_norm, v_lru_norm, v_lru_w_in, v_lru_b_in, v_lru_conv_w, v_lru_conv_b, v_lru_gate_w, v_lru_gate_b, v_lru_lambda, v_lru_w_out, v_lru_b_out, v_rwkv_norm, v_rwkv_mix, v_rwkv_w_rkv, v_rwkv_w0, v_rwkv_w1, v_rwkv_w2, v_rwkv_a0, v_rwkv_a1, v_rwkv_a2, v_rwkv_g1, v_rwkv_g2, v_rwkv_k_k, v_rwkv_k_a, v_rwkv_r_k, v_rwkv_ln_w, v_rwkv_ln_b, v_rwkv_w_out, v_ffn_norm, v_ffn_w_up, v_ffn_conv_w, v_ffn_conv_b, v_ffn_w_down, v_final_norm):
    args = locals()
    local = {(pre + k): args[pre + k] for pre in ("", "m_", "v_") for k in WEIGHTS}
    loss_local, grad_x, out = _step(local, x, loss_target)
    loss = lax.psum(loss_local, ("x", "y", "c"))
    return (loss, grad_x, *[out[(kind, k)] for kind in ("grad", "delta", "new_m", "new_v") for k in WEIGHTS])
```

```python
import functools
import math

import jax
import jax.numpy as jnp
from jax import lax
from jax.experimental import pallas as pl
from jax.experimental.pallas import tpu as pltpu

F32 = jnp.float32
BF16 = jnp.bfloat16
HI = lax.Precision.HIGHEST

N_DEV = 8
D = 1024
HEAD = 64
N_HEAD = D // HEAD
D_FF = 3 * D
LANE = 128
V7X_VMEM_BYTES = 64 * 1024 * 1024
VMEM_LIMIT = V7X_VMEM_BYTES - 8 * 1024 * 1024
CT = LANE
CHUNK = 64
SCAN_HEADS = 8
PACK_W = 1024
PACK_ROWS = 256

ADAM_LR, ADAM_B1, ADAM_B2, ADAM_EPS, ADAM_WD, ADAM_STEP = 0.001, 0.9, 0.999, 1e-08, 0.01, 10
RMS_EPS = 1e-6
GN_EPS = 64e-5
LRU_C = 8.0

WEIGHTS = ['lru_norm', 'lru_w_in', 'lru_b_in', 'lru_conv_w', 'lru_conv_b', 'lru_gate_w', 'lru_gate_b',
           'lru_lambda', 'lru_w_out', 'lru_b_out', 'rwkv_norm', 'rwkv_mix', 'rwkv_w_rkv', 'rwkv_w0', 'rwkv_w1',
           'rwkv_w2', 'rwkv_a0', 'rwkv_a1', 'rwkv_a2', 'rwkv_g1', 'rwkv_g2', 'rwkv_k_k', 'rwkv_k_a', 'rwkv_r_k',
           'rwkv_ln_w', 'rwkv_ln_b', 'rwkv_w_out', 'ffn_norm', 'ffn_w_up', 'ffn_conv_w', 'ffn_conv_b',
           'ffn_w_down', 'final_norm']
SHARD_AXIS = {'lru_w_in': 2, 'lru_conv_w': 2, 'lru_w_out': 1, 'rwkv_norm': 1, 'rwkv_mix': 2, 'rwkv_w_rkv': 2,
              'rwkv_w0': 1, 'rwkv_w1': 1, 'rwkv_w2': 2, 'rwkv_a0': 1, 'rwkv_a1': 1, 'rwkv_a2': 2, 'rwkv_g1': 1,
              'rwkv_g2': 2, 'rwkv_k_k': 1, 'rwkv_k_a': 1, 'rwkv_ln_w': 1, 'rwkv_ln_b': 1, 'rwkv_w_out': 1,
              'ffn_w_up': 2, 'ffn_conv_w': 2, 'ffn_w_down': 1}
GATHER_BF16 = ['lru_w_in', 'lru_w_out', 'rwkv_w_rkv', 'rwkv_w1', 'rwkv_a1', 'rwkv_g1', 'rwkv_w_out', 'ffn_w_up',
               'ffn_w_down']
REPLICATED = [n for n in WEIGHTS if n not in SHARD_AXIS]
STACKED = ['ffn_w_up', 'ffn_w_down', 'ffn_conv_w']
GROUPS = {
    "lru": ['lru_w_in', 'lru_w_out', 'lru_conv_w'],
    "ffn0": [f"{n}@0" for n in STACKED],
    "rwkv": [n for n in WEIGHTS if n.startswith("rwkv_") and n in SHARD_AXIS],
    "ffn1": [f"{n}@1" for n in STACKED],
}


def _base(name):
    return name.partition("@")[0]


def _pcall(body, **kw):
    return pl.pallas_call(body, **kw)


class _Carry:
    def __init__(self, items):
        self.items = list(items)
        self.landed = None


def _exchange_copies(src_ref, dst_ref, send_sems, recv_sems, local_sem, scatter):
    x, y, c = lax.axis_index("x"), lax.axis_index("y"), lax.axis_index("c")
    me = 4 * x + 2 * y + c
    mine = pltpu.make_async_copy(src_ref.at[me] if scatter else src_ref, dst_ref.at[me], local_sem)
    copies = []
    for m in range(1, N_DEV):
        px = 1 - x if m & 4 else x
        py = 1 - y if m & 2 else y
        pc = 1 - c if m & 1 else c
        part = src_ref.at[4 * px + 2 * py + pc] if scatter else src_ref
        copies.append(pltpu.make_async_remote_copy(
            src_ref=part, dst_ref=dst_ref.at[me], send_sem=send_sems.at[m - 1], recv_sem=recv_sems.at[m - 1],
            device_id=(px, py, pc), device_id_type=pl.DeviceIdType.MESH))
    return mine, copies


def _start_exchange(mine, copies):
    mine.start()
    for cp in copies:
        cp.start()


def _wait_exchange(mine, copies):
    for cp in copies:
        cp.wait_recv()
    for cp in copies:
        cp.wait_send()
    mine.wait()


_EXCHANGE_SEMS = [pltpu.SemaphoreType.DMA((N_DEV - 1,)), pltpu.SemaphoreType.DMA((N_DEV - 1,)), pltpu.SemaphoreType.DMA]


def _landing_shape(src, scatter):
    return jax.ShapeDtypeStruct((N_DEV,) + tuple(src.shape[1:] if scatter else src.shape), src.dtype)


def _call(body, operands, *, carry=None, name, grid, in_specs, out_specs, out_shape, scratch_shapes=(),
          compiler_params=None):
    if carry is None:
        return _pcall(body, name=name, grid=grid, in_specs=in_specs, out_specs=out_specs, out_shape=out_shape,
                      scratch_shapes=list(scratch_shapes), compiler_params=compiler_params)(*operands)
    single = not isinstance(out_specs, (list, tuple))
    out_specs_l = [out_specs] if single else list(out_specs)
    out_shape_l = [out_shape] if single else list(out_shape)
    n_in, n_out, n_scr, n_x = len(in_specs), len(out_specs_l), len(scratch_shapes), len(carry.items)
    flags = [sc for (_, sc) in carry.items]

    def wrapped(*refs):
        ins, refs = refs[:n_in], refs[n_in:]
        xsrc, refs = refs[:n_x], refs[n_x:]
        outs, refs = refs[:n_out], refs[n_out:]
        xdst, refs = refs[:n_x], refs[n_x:]
        scr, sems = refs[:n_scr], refs[n_scr:]
        first = functools.reduce(jnp.logical_and, [pl.program_id(i) == 0 for i in range(len(grid))])
        last = functools.reduce(jnp.logical_and, [pl.program_id(i) == grid[i] - 1 for i in range(len(grid))])

        def exchanges():
            return [_exchange_copies(xsrc[i], xdst[i], *sems[3 * i:3 * i + 3], flags[i]) for i in range(n_x)]

        @pl.when(first)
        def _():
            for mine, copies in exchanges():
                _start_exchange(mine, copies)

        body(*ins, *outs, *scr)

        @pl.when(last)
        def _():
            for mine, copies in exchanges():
                _wait_exchange(mine, copies)

    hbm = pl.BlockSpec(memory_space=pl.ANY)
    res = _pcall(
        wrapped, name=name, grid=grid, in_specs=list(in_specs) + [hbm] * n_x,
        out_specs=out_specs_l + [hbm] * n_x,
        out_shape=out_shape_l + [_landing_shape(a, sc) for (a, sc) in carry.items],
        scratch_shapes=list(scratch_shapes) + _EXCHANGE_SEMS * n_x, compiler_params=compiler_params,
    )(*operands, *[a for (a, _) in carry.items])
    carry.landed = list(res[n_out:])
    return res[0] if single else list(res[:n_out])


def _params(n_grid):
    return pltpu.CompilerParams(dimension_semantics=("arbitrary",) * n_grid, vmem_limit_bytes=VMEM_LIMIT)


def _shift_rows(x, d, up):
    n = x.shape[0]
    idx = lax.broadcasted_iota(jnp.int32, x.shape, 0)
    if up:
        return jnp.where(idx < n - d, pltpu.roll(x, n - d, 0), 0.0)
    return jnp.where(idx >= d, pltpu.roll(x, d, 0), 0.0)


@functools.partial(jax.custom_vjp, nondiff_argnums=(1,))
def _shift_down(x, d):
    return _shift_rows(x, d, False)


def _shift_down_fwd(x, d):
    return _shift_rows(x, d, False), None


def _shift_down_bwd(d, _, g):
    return (_shift_rows(g, d, True),)


_shift_down.defvjp(_shift_down_fwd, _shift_down_bwd)


def _scan_doubling(a, b, up):
    n = a.shape[0]
    d = 1
    while d < n:
        b = b + a * _shift_rows(b, d, up)
        a = a * _shift_rows(a, d, up)
        d *= 2
    return b


@jax.custom_vjp
def _linear_scan(a, b):
    return _scan_doubling(a, b, False)


def _linear_scan_fwd(a, b):
    h = _scan_doubling(a, b, False)
    return h, (a, h)


def _linear_scan_bwd(res, dh):
    a, h = res
    g = _scan_doubling(_shift_rows(a, 1, True), dh, True)
    return g * _shift_rows(h, 1, False), g


_linear_scan.defvjp(_linear_scan_fwd, _linear_scan_bwd)


def _causal_conv(x, w, b):
    k = w.shape[0]
    out = b + x * w[k - 1:k]
    for j in range(k - 1):
        out = out + _shift_down(x, k - 1 - j) * w[j:j + 1]
    return out


def _gelu(x):
    return jax.nn.gelu(x, approximate=True)


def _rmsnorm(x, g):
    return x * lax.rsqrt(jnp.mean(x * x, axis=-1, keepdims=True) + RMS_EPS) * g


def _neg_expm1(x):
    series = x * (1.0 + x * 0.5 * (1.0 + x * (1.0 / 3.0) * (1.0 + x * 0.25 * (1.0 + x * 0.2))))
    return -jnp.where(x > -0.1, series, jnp.exp(x) - 1.0)


def _dot(a, b, ca=1, cb=0, precision=None):
    return lax.dot_general(a, b, (((ca,), (cb,)), ((), ())), precision=precision, preferred_element_type=F32)


def _bdot(a, b):
    return _dot(a.astype(BF16), b.astype(BF16))


def _split_bf16(x):
    hi = x.astype(BF16)
    return hi, (x - hi.astype(F32)).astype(BF16)


def _head_sum_impl(x):
    row = lax.broadcasted_iota(jnp.int32, (D, LANE), 0)
    col = lax.broadcasted_iota(jnp.int32, (D, LANE), 1)
    e = (lax.shift_right_logical(row, 6) == col).astype(BF16)
    hi, lo = _split_bf16(x)
    s_hi, s_lo = _split_bf16(_dot(hi, e) + _dot(lo, e))
    return _dot(s_hi, e, 1, 1) + _dot(s_lo, e, 1, 1)


@jax.custom_vjp
def _head_sum(x):
    return _head_sum_impl(x)


def _head_sum_fwd(x):
    return _head_sum_impl(x), None


def _head_sum_bwd(_, g):
    return (_head_sum(g),)


_head_sum.defvjp(_head_sum_fwd, _head_sum_bwd)


def _specs(items):
    return [pl.BlockSpec(it[1], it[2]) for it in items]


def _stage_fwd(name, f, grid, ins, params, outs, carry=None):
    n_in = len(ins) + len(params)

    def body(*refs):
        res = f(*[r[...] for r in refs[:n_in]])
        for o, v in zip(refs[n_in:], res):
            o[...] = v.astype(o.dtype)

    return _call(
        body, [it[0] for it in ins + params], carry=carry, name=name, grid=grid, in_specs=_specs(ins + params),
        out_specs=[pl.BlockSpec(bs, im) for (_, _, bs, im) in outs],
        out_shape=[jax.ShapeDtypeStruct(s, dt) for (s, dt, _, _) in outs],
        compiler_params=_params(len(grid)),
    )


def _stage_bwd(name, f, grid, ins, params, douts, din_dtypes, adds=None, carry=None):
    adds = adds or {}
    n_in, n_par = len(ins), len(params)
    dout_items = [(a, bs, im) for (arrs, bs, im) in douts for a in arrs]
    add_items = [(adds[i], ins[i][1], ins[i][2]) for i in sorted(adds)]
    n_do, n_add = len(dout_items), len(add_items)

    def body(*refs):
        vals = [r[...] for r in refs[:n_in + n_par]]
        do_refs = list(refs[n_in + n_par:n_in + n_par + n_do])
        add_refs = dict(zip(sorted(adds), refs[n_in + n_par + n_do:n_in + n_par + n_do + n_add]))
        din_refs = refs[n_in + n_par + n_do + n_add:n_in + n_par + n_do + n_add + n_in]
        dpar_refs = refs[n_in + n_par + n_do + n_add + n_in:]
        cts = []
        for (arrs, _, _) in douts:
            ct = do_refs.pop(0)[...].astype(F32)
            for _ in arrs[1:]:
                ct = ct + do_refs.pop(0)[...].astype(F32)
            cts.append(ct)
        _, vjp = jax.vjp(f, *vals)
        grads = vjp(tuple(cts))
        for i, r in enumerate(din_refs):
            g = grads[i]
            if i in add_refs:
                g = g + add_refs[i][...]
            r[...] = g.astype(r.dtype)

        @pl.when(pl.program_id(len(grid) - 1) == 0)
        def _():
            for r in dpar_refs:
                r[...] = jnp.zeros(r.shape, r.dtype)

        for j, r in enumerate(dpar_refs):
            r[...] += grads[n_in + j]

    din_shapes = [it[3][0] if len(it) > 3 else it[0].shape for it in ins]
    din_specs = [pl.BlockSpec(it[1], it[3][1] if len(it) > 3 else it[2]) for it in ins]
    res = _call(
        body, [it[0] for it in ins + params + dout_items + add_items], carry=carry, name=name, grid=grid,
        in_specs=_specs(ins + params + dout_items + add_items), out_specs=din_specs + _specs(params),
        out_shape=[jax.ShapeDtypeStruct(s, dt) for s, dt in zip(din_shapes, din_dtypes)]
        + [jax.ShapeDtypeStruct(it[0].shape, F32) for it in params],
        compiler_params=_params(len(grid)),
    )
    return list(res[:n_in]), list(res[n_in:])


def _tile(n, want):
    t = min(n, want)
    while n % t:
        t //= 2
    return t


def _matmul(name, a, b, *, mode="nn", resid=None, out_dtype=F32, tm=512, tn=512, tk=1024, b_koff=0, carry=None):
    if mode == "tn":
        (kdim, m), n = a.shape, b.shape[1]
    else:
        (m, kdim), n = a.shape, (b.shape[1] if mode == "nn" else b.shape[0])
    tm, tn, tk = _tile(m, tm), _tile(n, tn), _tile(kdim, tk)
    nk = kdim // tk
    ko = b_koff // tk
    assert ko * tk == b_koff
    a_spec = pl.BlockSpec((tk, tm), lambda i, j, k: (k, i)) if mode == "tn" else pl.BlockSpec((tm, tk), lambda i, j, k: (i, k))
    b_spec = pl.BlockSpec((tn, tk), lambda i, j, k: (j, k + ko)) if mode == "nt" else pl.BlockSpec((tk, tn), lambda i, j, k: (k + ko, j))
    ca = 0 if mode == "tn" else 1
    cb = 1 if mode == "nt" else 0
    operands = [a, b]
    in_specs = [a_spec, b_spec]
    if resid is not None:
        operands.append(resid)
        in_specs.append(pl.BlockSpec((tm, tn), lambda i, j, k: (i, j)))

    def body(*refs):
        a_ref, b_ref = refs[0], refs[1]
        o_ref, acc_ref = refs[-2], refs[-1]
        k = pl.program_id(2)

        @pl.when(k == 0)
        def _():
            acc_ref[...] = jnp.zeros(acc_ref.shape, F32)

        acc_ref[...] += _dot(a_ref[...].astype(BF16), b_ref[...].astype(BF16), ca, cb)

        @pl.when(k == nk - 1)
        def _():
            r = acc_ref[...]
            if resid is not None:
                r = r + refs[2][...]
            o_ref[...] = r.astype(o_ref.dtype)

    return _call(
        body, operands, carry=carry, name=name, grid=(m // tm, n // tn, nk), in_specs=in_specs,
        out_specs=pl.BlockSpec((tm, tn), lambda i, j, k: (i, j)),
        out_shape=jax.ShapeDtypeStruct((m, n), out_dtype),
        scratch_shapes=[pltpu.VMEM((tm, tn), F32)],
        compiler_params=_params(3),
    )


def _add(name, a, b):
    rows, cols = a.shape
    tr = _tile(rows, 512)

    def body(a_ref, b_ref, o_ref):
        o_ref[...] = a_ref[...] + b_ref[...]

    spec = pl.BlockSpec((tr, cols), lambda i: (i, 0))
    return _pcall(body, name=name, grid=(rows // tr,), in_specs=[spec, spec], out_specs=spec,
                  out_shape=jax.ShapeDtypeStruct(a.shape, a.dtype), compiler_params=_params(1))(a, b)


def _f_lru_pre(x, norm, b_out):
    return _rmsnorm(x, norm), x + b_out


def _f_lru_core(uy, ux, b_y, b_x, cw, cb, gw, gb, lam):
    yb = _gelu(uy + b_y)
    xr = _causal_conv(ux + b_x, cw, cb)
    gr = jax.nn.sigmoid(_bdot(xr, gw[0, 0]) + gb[0:1])
    gi = jax.nn.sigmoid(_bdot(xr, gw[1, 0]) + gb[1:2])
    log_a = -LRU_C * gr * jax.nn.softplus(-lam)
    a = jnp.exp(log_a)
    bterm = jnp.sqrt(_neg_expm1(2.0 * log_a)) * (gi * xr)
    return (_linear_scan(a, bterm) * yb,)


def _f_norm(x, norm):
    return (_rmsnorm(x, norm),)


def _f_ffn_core(ug, uv, cw, cb):
    return (_gelu(_causal_conv(ug, cw, cb)) * uv,)


def _f_rwkv_mix(h, mix):
    xx = _shift_down(h, 1) - h
    return tuple(h + xx * mix[i:i + 1] for i in range(6))


def _f_rwkv_pre(k, lw1, la1, lg1, w0, a0, k_k, k_a, w2, a2, g2):
    wpre = w0 + _bdot(jnp.tanh(lw1), w2)
    apre = a0 + _bdot(la1, a2)
    g = _bdot(jax.nn.sigmoid(lg1), g2)
    log_decay = -jnp.exp(-jax.nn.softplus(-wpre) - 0.5)
    a = jax.nn.sigmoid(apre)
    kk = k * k_k
    kk = kk / jnp.maximum(jnp.sqrt(_head_sum(kk * kk)), 1e-12)
    kf = k * (1.0 + (a - 1.0) * k_a)
    return log_decay, kf, -kk, kk * a, g


def _f_rwkv_post(y, r, kf, v, g, ln_w, ln_b, r_k):
    inv = 1.0 / HEAD
    yc = y - _head_sum(y) * inv
    var = _head_sum(yc * yc) * inv
    yn = yc * lax.rsqrt(var + GN_EPS) * ln_w + ln_b
    bonus = _head_sum(r * kf * r_k) * v
    return ((yn + bonus) * g,)


def _hdot(a, b, ca, cb):
    return lax.dot_general(a, b, (((ca,), (cb,)), ((0,), (0,))), preferred_element_type=F32)


def _mm3_impl(a, b, ca, cb):
    a_hi, a_lo = _split_bf16(a)
    b_hi, b_lo = _split_bf16(b)
    lhs = jnp.concatenate([a_hi, a_hi, a_lo], axis=ca)
    rhs = jnp.concatenate([b_hi, b_lo, b_hi], axis=cb)
    return _hdot(lhs, rhs, ca, cb)


@functools.partial(jax.custom_vjp, nondiff_argnums=(2, 3))
def _mm3(a, b, ca, cb):
    return _mm3_impl(a, b, ca, cb)


def _mm3_fwd(a, b, ca, cb):
    return _mm3_impl(a, b, ca, cb), (a, b)


def _mm3_bwd(ca, cb, res, dc):
    a, b = res
    fa, fb = 3 - ca, 3 - cb
    da = _mm3(dc, b, 2, fb) if ca == 2 else _mm3(b, dc, fb, 2)
    db = _mm3(a, dc, fa, 1) if cb == 1 else _mm3(dc, a, 1, fa)
    return da, db


_mm3.defvjp(_mm3_fwd, _mm3_bwd)


def _tri_dot_impl(x, upper):
    g, n, _ = x.shape
    row = lax.broadcasted_iota(jnp.int32, (g, n, 3 * n), 1)
    col = lax.broadcasted_iota(jnp.int32, (g, n, 3 * n), 2)
    col = jnp.where(col >= 2 * n, col - 2 * n, jnp.where(col >= n, col - n, col))
    tri = (row <= col if upper else row >= col).astype(BF16)
    hi = x.astype(BF16)
    rem = x - hi.astype(F32)
    mid = rem.astype(BF16)
    lo = (rem - mid.astype(F32)).astype(BF16)
    return _hdot(tri, jnp.concatenate([hi, mid, lo], axis=1), 2, 1)


@functools.partial(jax.custom_vjp, nondiff_argnums=(1,))
def _tri_dot(x, upper):
    return _tri_dot_impl(x, upper)


def _tri_dot_fwd(x, upper):
    return _tri_dot_impl(x, upper), None


def _tri_dot_bwd(upper, _, g):
    return (_tri_dot(g, not upper),)


_tri_dot.defvjp(_tri_dot_fwd, _tri_dot_bwd)


def _rwkv_chunk(z0, r, lw, k, v, a, b):
    g, n, _ = r.shape
    row = lax.broadcasted_iota(jnp.int32, (g, n, n), 1)
    col = lax.broadcasted_iota(jnp.int32, (g, n, n), 2)
    incl, strict = row >= col, row > col
    cs = _tri_dot(lw, False)
    c_last = cs[:, n - 1:n]
    inv = jnp.exp(-cs)
    ar = jnp.concatenate([a * jnp.exp(cs - lw), r * jnp.exp(cs)], axis=1)
    bk = jnp.concatenate([b * inv, k * inv], axis=1)
    pair = _mm3(ar, bk, 2, 2)
    a_ab = jnp.where(strict, pair[:, :n, :n], 0.0)
    a_ak = jnp.where(strict, pair[:, :n, n:], 0.0)
    a_rbk = jnp.concatenate([jnp.where(incl, pair[:, n:, :n], 0.0), jnp.where(incl, pair[:, n:, n:], 0.0)], axis=2)
    arz = _mm3(ar, z0, 2, 1)
    inv_m = (row == col).astype(F32) + a_ab
    p = a_ab
    for _ in range(int(math.log2(n)) - 1):
        p = _mm3(p, p, 2, 1)
        inv_m = inv_m + _mm3(inv_m, p, 2, 1)
    u = _mm3(inv_m, arz[:, :n] + _mm3(a_ak, v, 2, 1), 2, 1)
    uv = jnp.concatenate([u, v], axis=1)
    y = arz[:, n:] + _mm3(a_rbk, uv, 2, 1)
    tail = jnp.exp(c_last - cs)
    er = lax.broadcasted_iota(jnp.int32, (g, HEAD, HEAD), 1)
    ec = lax.broadcasted_iota(jnp.int32, (g, HEAD, HEAD), 2)
    decay_all = jnp.where(er == ec, jnp.exp(c_last), 0.0)
    z_l = _mm3(jnp.concatenate([b * tail, k * tail, decay_all], axis=1), jnp.concatenate([uv, z0], axis=1), 1, 1)
    return y, z_l


def _heads_in(ref, g):
    x = ref[...]
    return jnp.stack([x[:, h * HEAD:(h + 1) * HEAD] for h in range(g)], axis=0)


def _heads_out(ref, x):
    ref[...] = jnp.concatenate([x[h] for h in range(x.shape[0])], axis=1)


def _scan_geometry(n_rows, t):
    g, n = _tile(N_HEAD, SCAN_HEADS), _tile(t, CHUNK)
    return g, n, t // n, N_HEAD // g, n_rows // t


def _rwkv_scan_fwd(seqs, t, carry=None):
    rows = seqs[0].shape[0]
    g, n, nc, hpg, bsz = _scan_geometry(rows, t)

    def body(r_ref, lw_ref, k_ref, v_ref, a_ref, b_ref, y_ref, zs_ref, z_ref):
        @pl.when(pl.program_id(1) == 0)
        def _():
            z_ref[...] = jnp.zeros(z_ref.shape, F32)

        z0 = z_ref[...]
        zs_ref[:, 0] = z0
        y, z_l = _rwkv_chunk(z0, *[_heads_in(ref, g) for ref in (r_ref, lw_ref, k_ref, v_ref, a_ref, b_ref)])
        _heads_out(y_ref, y)
        z_ref[...] = z_l

    seq_spec = pl.BlockSpec((n, g * HEAD), lambda i, c: ((i // hpg) * nc + c, i % hpg))
    return _call(
        body, list(seqs), carry=carry, name="rwkv_scan_fwd", grid=(bsz * hpg, nc), in_specs=[seq_spec] * 6,
        out_specs=[seq_spec, pl.BlockSpec((g, 1, HEAD, HEAD), lambda i, c: (i, c, 0, 0))],
        out_shape=[jax.ShapeDtypeStruct((rows, D), F32), jax.ShapeDtypeStruct((bsz * N_HEAD, nc, HEAD, HEAD), F32)],
        scratch_shapes=[pltpu.VMEM((g, HEAD, HEAD), F32)],
        compiler_params=_params(2),
    )


def _rwkv_scan_bwd(seqs, zs, dy, t, carry=None):
    rows = seqs[0].shape[0]
    g, n, nc, hpg, bsz = _scan_geometry(rows, t)

    def body(r_ref, lw_ref, k_ref, v_ref, a_ref, b_ref, zs_ref, dy_ref, dr, dlw, dk, dv, da, db, dz_ref):
        @pl.when(pl.program_id(1) == 0)
        def _():
            dz_ref[...] = jnp.zeros(dz_ref.shape, F32)

        _, vjp = jax.vjp(_rwkv_chunk, zs_ref[:, 0],
                         *[_heads_in(ref, g) for ref in (r_ref, lw_ref, k_ref, v_ref, a_ref, b_ref)])
        grads = vjp((_heads_in(dy_ref, g), dz_ref[...]))
        dz_ref[...] = grads[0]
        for o, gr in zip((dr, dlw, dk, dv, da, db), grads[1:]):
            _heads_out(o, gr)

    seq_spec = pl.BlockSpec((n, g * HEAD), lambda i, c: ((i // hpg) * nc + nc - 1 - c, i % hpg))
    return _call(
        body, [*seqs, zs, dy], carry=carry, name="rwkv_scan_bwd", grid=(bsz * hpg, nc),
        in_specs=[seq_spec] * 6 + [pl.BlockSpec((g, 1, HEAD, HEAD), lambda i, c: (i, nc - 1 - c, 0, 0)), seq_spec],
        out_specs=[seq_spec] * 6,
        out_shape=[jax.ShapeDtypeStruct((rows, D), F32)] * 6,
        scratch_shapes=[pltpu.VMEM((g, HEAD, HEAD), F32)],
        compiler_params=_params(2),
    )


def _loss_head(x, target, norm):
    n = x.shape[0]
    tr = _tile(n, 256)

    def f(xv, gv, tv):
        err = _rmsnorm(xv, gv) - tv
        return 0.5 * jnp.sum(jnp.mean(err * err, axis=-1, keepdims=True), axis=0, keepdims=True)

    def body(x_ref, t_ref, g_ref, dx_ref, dg_ref, loss_ref):
        val, vjp = jax.vjp(lambda xv, gv: f(xv, gv, t_ref[...]), x_ref[...], g_ref[...])
        dx, dg = vjp(jnp.ones((1, 1), F32))
        dx_ref[...] = dx

        @pl.when(pl.program_id(0) == 0)
        def _():
            dg_ref[...] = jnp.zeros(dg_ref.shape, F32)
            loss_ref[...] = jnp.zeros(loss_ref.shape, F32)

        dg_ref[...] += dg
        loss_ref[...] += jnp.broadcast_to(val, loss_ref.shape)

    row = pl.BlockSpec((tr, D), lambda i: (i, 0))
    vec = pl.BlockSpec((1, D), lambda i: (0, 0))
    dx, dg, loss = _pcall(
        body, name="loss_head", grid=(n // tr,), in_specs=[row, row, vec],
        out_specs=[row, vec, pl.BlockSpec((8, LANE), lambda i: (0, 0))],
        out_shape=[jax.ShapeDtypeStruct((n, D), F32), jax.ShapeDtypeStruct((1, D), F32),
                   jax.ShapeDtypeStruct((8, LANE), F32)],
        compiler_params=_params(1),
    )(x, target, norm)
    return loss[0, 0], dx, dg


def _exchange(name, items):
    n_x = len(items)

    def body(*refs):
        srcs, dsts, sems = refs[:n_x], refs[n_x:2 * n_x], refs[2 * n_x:]
        started = [_exchange_copies(srcs[i], dsts[i], *sems[3 * i:3 * i + 3], items[i][1]) for i in range(n_x)]
        for mine, copies in started:
            _start_exchange(mine, copies)
        for mine, copies in started:
            _wait_exchange(mine, copies)

    hbm = pl.BlockSpec(memory_space=pl.ANY)
    return _pcall(
        body, name=name, in_specs=[hbm] * n_x, out_specs=[hbm] * n_x,
        out_shape=[_landing_shape(a, sc) for (a, sc) in items], scratch_shapes=_EXCHANGE_SEMS * n_x,
    )(*[a for (a, _) in items])


def _adamw(name, parts, w, m, v):
    rows = w.shape[0]
    tr = _tile(rows, PACK_ROWS)

    def body(p_ref, w_ref, m_ref, v_ref, g_ref, d_ref, nm_ref, nv_ref):
        g = p_ref[0]
        for k in range(1, N_DEV):
            g = g + p_ref[k]
        nm = ADAM_B1 * m_ref[...] + (1.0 - ADAM_B1) * g
        nv = ADAM_B2 * v_ref[...] + (1.0 - ADAM_B2) * jnp.square(g)
        m_hat = nm / (1.0 - ADAM_B1 ** ADAM_STEP)
        v_hat = nv / (1.0 - ADAM_B2 ** ADAM_STEP)
        g_ref[...] = g
        d_ref[...] = -ADAM_LR * (m_hat / (jnp.sqrt(v_hat) + ADAM_EPS) + ADAM_WD * w_ref[...])
        nm_ref[...] = nm
        nv_ref[...] = nv

    row = pl.BlockSpec((tr, PACK_W), lambda i: (i, 0))
    return _pcall(
        body, name=name, grid=(rows // tr,),
        in_specs=[pl.BlockSpec((N_DEV, tr, PACK_W), lambda i: (0, i, 0)), row, row, row],
        out_specs=[row] * 4, out_shape=[jax.ShapeDtypeStruct((rows, PACK_W), F32)] * 4,
        compiler_params=_params(1),
    )(parts, w, m, v)


def _pack(arrs, dtype, lead=()):
    flat = jnp.concatenate([a.astype(dtype).reshape(lead + (-1,)) for a in arrs], axis=-1)
    n = flat.shape[-1]
    quantum = PACK_W * PACK_ROWS
    total = -(-n // quantum) * quantum
    flat = jnp.pad(flat, [(0, 0)] * len(lead) + [(0, total - n)])
    return flat.reshape(lead + (total // PACK_W, PACK_W))


def _unpack(buf, shapes, lead=()):
    flat = buf.reshape(lead + (-1,))
    out, off = [], 0
    for s in shapes:
        n = math.prod(s)
        out.append(flat[..., off:off + n].reshape(lead + tuple(s)))
        off += n
    return out


def _unshard(g, axis):
    local = g.shape[1:]
    full = jnp.moveaxis(g, 0, axis)
    return full.reshape(local[:axis] + (N_DEV * local[axis],) + local[axis + 1:])


def _reshard(full, axis):
    s = full.shape
    blocked = full.reshape(s[:axis] + (N_DEV, s[axis] // N_DEV) + s[axis + 1:])
    return jnp.moveaxis(blocked, axis, 0)


def _row_item(a, tr):
    return (a, (tr, a.shape[1]), lambda i: (i, 0))


def _vec_item(a):
    return (a, a.shape, lambda i: (0,) * a.ndim)


def _seq_item(a, t, width):
    return (a, (t, width), lambda j, b: (b, j))


def _seq_halves(a, t, width):
    half = a.shape[1] // 2
    off = half // width
    grad = ((a.shape[0], half), lambda j, b: (b, j))
    return [(a, (t, width), lambda j, b: (b, j), grad), (a, (t, width), lambda j, b: (b, j + off), grad)]


def _col_item(a, width):
    return (a, (a.shape[0], width), lambda j, b: (0, j))


def _ffn_fwd(tag, x, p, t, carry=None):
    n = x.shape[0]
    tr = _tile(n, 512)
    norm_ins, norm_par = [_row_item(x, tr)], [_vec_item(p["norm"])]
    (h,) = _stage_fwd(f"{tag}_norm", _f_norm, (n // tr,), norm_ins, norm_par, [((n, D), BF16, (tr, D), lambda i: (i, 0))])
    u = _matmul(f"{tag}_up", h, p["w_up"], tm=1024, tn=768, carry=carry)
    grid = (D_FF // CT, n // t)
    core_par = [_col_item(p["conv_w"], CT), _col_item(p["conv_b"], CT)]
    (hid,) = _stage_fwd(f"{tag}_core", _f_ffn_core, grid, _seq_halves(u, t, CT), core_par,
                        [((n, D_FF), BF16, (t, CT), lambda j, b: (b, j))])
    y = _matmul(f"{tag}_down", hid, p["w_down"], resid=x, tm=1024, tn=1024)
    return y, (x, h, u, hid)


def _ffn_bwd(tag, dy, saved, p, t, carry=None):
    x, h, u, hid = saved
    n = x.shape[0]
    tr = _tile(n, 512)
    grads = {}
    dhid = _matmul(f"{tag}_down_da", dy, p["w_down"], mode="nt", tm=1024, tn=768)
    grads["w_down"] = _matmul(f"{tag}_down_dw", hid, dy, mode="tn", tm=768, tn=1024)
    grid = (D_FF // CT, n // t)
    core_par = [_col_item(p["conv_w"], CT), _col_item(p["conv_b"], CT)]
    (dug, duv), (grads["conv_w"], grads["conv_b"]) = _stage_bwd(
        f"{tag}_core_bwd", _f_ffn_core, grid, _seq_halves(u, t, CT), core_par,
        [([dhid], (t, CT), lambda j, b: (b, j))], [BF16, BF16], carry=carry)
    dh = _matmul(f"{tag}_up_da_g", dug, p["w_up"], mode="nt", tm=1024, tn=1024)
    dh = _matmul(f"{tag}_up_da_v", duv, p["w_up"], mode="nt", tm=1024, tn=1024, b_koff=D_FF, resid=dh)
    grads["w_up"] = jnp.concatenate([_matmul(f"{tag}_up_dw_g", h, dug, mode="tn", tm=1024, tn=768),
                                     _matmul(f"{tag}_up_dw_v", h, duv, mode="tn", tm=1024, tn=768)], axis=1)
    (dx,), (grads["norm"],) = _stage_bwd(
        f"{tag}_norm_bwd", _f_norm, (n // tr,), [_row_item(x, tr)], [_vec_item(p["norm"])],
        [([dh], (tr, D), lambda i: (i, 0))], [F32], adds={0: dy})
    return dx, grads


def _lru_fwd(x, p, t, carry=None):
    n = x.shape[0]
    tr = _tile(n, 512)
    row = lambda dt: ((n, D), dt, (tr, D), lambda i: (i, 0))
    h, xb = _stage_fwd("lru_pre", _f_lru_pre, (n // tr,), [_row_item(x, tr)],
                       [_vec_item(p["norm"]), _vec_item(p["b_out"])], [row(BF16), row(F32)])
    u = _matmul("lru_in", h, p["w_in"], tm=1024, tn=1024)
    out, = _stage_fwd("lru_core", _f_lru_core, (D // CT, n // t), _seq_halves(u, t, CT), _lru_core_params(p),
                      [((n, D), BF16, (t, CT), lambda j, b: (b, j))], carry=carry)
    y = _matmul("lru_out", out, p["w_out"], resid=xb, tm=1024, tn=1024)
    return y, (x, h, u, out)


def _lru_core_params(p):
    return [_col_item(p["b_y"], CT), _col_item(p["b_x"], CT), _col_item(p["conv_w"], CT), _col_item(p["conv_b"], CT),
            (p["gate_w"], (2, 1, CT, CT), lambda j, b: (0, j, 0, 0)), _col_item(p["gate_b"], CT), _col_item(p["lam"], CT)]


def _lru_bwd(dy, saved, p, t, carry=None):
    x, h, u, out = saved
    n = x.shape[0]
    tr = _tile(n, 512)
    grads = {}
    dout = _matmul("lru_out_da", dy, p["w_out"], mode="nt", tm=1024, tn=1024)
    grads["w_out"] = _matmul("lru_out_dw", out, dy, mode="tn", tm=1024, tn=1024)
    (duy, dux), core_grads = _stage_bwd("lru_core_bwd", _f_lru_core, (D // CT, n // t), _seq_halves(u, t, CT),
                                        _lru_core_params(p), [([dout], (t, CT), lambda j, b: (b, j))], [BF16, BF16],
                                        carry=carry)
    for k, g in zip(("b_y", "b_x", "conv_w", "conv_b", "gate_w", "gate_b", "lam"), core_grads):
        grads[k] = g
    dh = _matmul("lru_in_da_y", duy, p["w_in"], mode="nt", tm=1024, tn=1024)
    dh = _matmul("lru_in_da_x", dux, p["w_in"], mode="nt", tm=1024, tn=1024, b_koff=D, resid=dh)
    grads["w_in"] = jnp.concatenate([_matmul("lru_in_dw_y", h, duy, mode="tn", tm=1024, tn=1024),
                                     _matmul("lru_in_dw_x", h, dux, mode="tn", tm=1024, tn=1024)], axis=1)
    row = (tr, D), lambda i: (i, 0)
    (dx,), (grads["norm"], grads["b_out"]) = _stage_bwd(
        "lru_pre_bwd", _f_lru_pre, (n // tr,), [_row_item(x, tr)], [_vec_item(p["norm"]), _vec_item(p["b_out"])],
        [([dh], *row), ([dy], *row)], [F32])
    return dx, grads


_RWKV_PRE_VECS = ("w0", "a0", "k_k", "k_a", "w2", "a2", "g2")
_RWKV_POST_VECS = ("ln_w", "ln_b", "r_k")


def _rwkv_fwd(x, p, t, carry=None):
    n = x.shape[0]
    bsz = n // t
    tr = _tile(n, 512)
    ts = _tile(n, 128)
    (h,) = _stage_fwd("rwkv_norm", _f_norm, (n // tr,), [_row_item(x, tr)], [_vec_item(p["norm"])],
                      [((n, D), F32, (tr, D), lambda i: (i, 0))])
    mixed = _stage_fwd("rwkv_mix", _f_rwkv_mix, (D // CT, bsz), [_seq_item(h, t, CT)], [_col_item(p["mix"], CT)],
                       [((n, D), BF16, (t, CT), lambda j, b: (b, j))] * 6)
    xr, xk, xv, xw, xa, xg = mixed
    r = _matmul("rwkv_r", xr, p["w_r"], tm=1024, tn=1024)
    k = _matmul("rwkv_k", xk, p["w_k"], tm=1024, tn=1024)
    v = _matmul("rwkv_v", xv, p["w_v"], tm=1024, tn=1024)
    lw1 = _matmul("rwkv_w1", xw, p["w1"], tm=1024)
    la1 = _matmul("rwkv_a1", xa, p["a1"], tm=1024)
    lg1 = _matmul("rwkv_g1", xg, p["g1"], tm=1024)
    pre_ins = [_row_item(a, ts) for a in (k, lw1, la1, lg1)]
    pre_par = [_vec_item(p[q]) for q in _RWKV_PRE_VECS]
    row = ((n, D), F32, (ts, D), lambda i: (i, 0))
    lw, kf, aa, bb, g = _stage_fwd("rwkv_pre", _f_rwkv_pre, (n // ts,), pre_ins, pre_par, [row] * 5)
    y, zs = _rwkv_scan_fwd([r, lw, kf, v, aa, bb], t, carry=carry)
    post_ins = [_row_item(a, ts) for a in (y, r, kf, v, g)]
    post_par = [_vec_item(p[q]) for q in _RWKV_POST_VECS]
    (yg,) = _stage_fwd("rwkv_post", _f_rwkv_post, (n // ts,), post_ins, post_par,
                       [((n, D), BF16, (ts, D), lambda i: (i, 0))])
    out = _matmul("rwkv_out", yg, p["w_out"], resid=x, tm=1024, tn=1024)
    return out, (x, h, mixed, r, k, v, lw1, la1, lg1, lw, kf, aa, bb, g, zs, y, yg)


def _rwkv_bwd(dout, saved, p, t, carry=None):
    x, h, mixed, r, k, v, lw1, la1, lg1, lw, kf, aa, bb, g, zs, y, yg = saved
    xr, xk, xv, xw, xa, xg = mixed
    n = x.shape[0]
    bsz = n // t
    tr = _tile(n, 512)
    ts = _tile(n, 128)
    grads = {}
    row_s = (ts, D), lambda i: (i, 0)
    dyg = _matmul("rwkv_out_da", dout, p["w_out"], mode="nt", tm=1024, tn=1024)
    grads["w_out"] = _matmul("rwkv_out_dw", yg, dout, mode="tn", tm=1024, tn=1024)
    post_ins = [_row_item(a, ts) for a in (y, r, kf, v, g)]
    post_par = [_vec_item(p[q]) for q in _RWKV_POST_VECS]
    (dy, dr_a, dkf_a, dv_a, dg), post_grads = _stage_bwd(
        "rwkv_post_bwd", _f_rwkv_post, (n // ts,), post_ins, post_par, [([dyg], *row_s)], [F32] * 5)
    grads.update(zip(_RWKV_POST_VECS, post_grads))
    dr_b, dlw, dkf_b, dv_b, daa, dbb = _rwkv_scan_bwd([r, lw, kf, v, aa, bb], zs, dy, t, carry=carry)
    dr = _add("rwkv_dr_sum", dr_a, dr_b)
    dv = _add("rwkv_dv_sum", dv_a, dv_b)
    pre_ins = [_row_item(a, ts) for a in (k, lw1, la1, lg1)]
    pre_par = [_vec_item(p[q]) for q in _RWKV_PRE_VECS]
    (dk, dlw1, dla1, dlg1), pre_grads = _stage_bwd(
        "rwkv_pre_bwd", _f_rwkv_pre, (n // ts,), pre_ins, pre_par,
        [([dlw], *row_s), ([dkf_a, dkf_b], *row_s), ([daa], *row_s), ([dbb], *row_s), ([dg], *row_s)], [F32] * 4)
    grads.update(zip(_RWKV_PRE_VECS, pre_grads))
    dmixed = []
    for tag, xin, dz, wname in (("r", xr, dr, "w_r"), ("k", xk, dk, "w_k"), ("v", xv, dv, "w_v"),
                                ("w1", xw, dlw1, "w1"), ("a1", xa, dla1, "a1"), ("g1", xg, dlg1, "g1")):
        dmixed.append(_matmul(f"rwkv_{tag}_da", dz, p[wname], mode="nt", tm=1024, tn=1024))
        grads[wname] = _matmul(f"rwkv_{tag}_dw", xin, dz, mode="tn", tm=1024, tn=1024)
    seq_blk = (t, CT), lambda j, b: (b, j)
    (dh,), (grads["mix"],) = _stage_bwd(
        "rwkv_mix_bwd", _f_rwkv_mix, (D // CT, bsz), [_seq_item(h, t, CT)], [_col_item(p["mix"], CT)],
        [([a], *seq_blk) for a in dmixed], [F32])
    (dx,), (grads["norm"],) = _stage_bwd(
        "rwkv_norm_bwd", _f_norm, (n // tr,), [_row_item(x, tr)], [_vec_item(p["norm"])],
        [([dh], (tr, D), lambda i: (i, 0))], [F32], adds={0: dout})
    return dx, grads


def _block_diag_gates(gate_w):
    z = jnp.zeros((2, D // CT, HEAD, HEAD), gate_w.dtype)
    even, odd = gate_w[:, 0::2], gate_w[:, 1::2]
    top = jnp.concatenate([even, z], axis=-1)
    bot = jnp.concatenate([z, odd], axis=-1)
    return jnp.concatenate([top, bot], axis=-2)


def _gate_blocks_grad(dg):
    even, odd = dg[:, :, :HEAD, :HEAD], dg[:, :, HEAD:, HEAD:]
    return jnp.stack([even, odd], axis=2).reshape(2, N_HEAD, HEAD, HEAD)


def _step(local, x, target):
    bsz, t, _ = x.shape
    n = bsz * t

    def block(name, pre=""):
        base, _, layer = name.partition("@")
        a = local[pre + base]
        return a[int(layer):int(layer) + 1] if layer else a

    def gather_pack(group):
        flat = [block(k).astype(BF16).reshape(-1) if _base(k) in GATHER_BF16
                else lax.bitcast_convert_type(block(k), BF16).reshape(-1) for k in group]
        return _pack(flat, BF16)

    def gathered(group, buf):
        sizes = [(1 if _base(k) in GATHER_BF16 else 2) * math.prod(block(k).shape) for k in group]
        out = {}
        for k, g in zip(group, _unpack(buf, [(s,) for s in sizes], (N_DEV,))):
            if _base(k) not in GATHER_BF16:
                g = lax.bitcast_convert_type(g.reshape(N_DEV, -1, 2), F32)
            out[k] = _unshard(g.reshape((N_DEV,) + block(k).shape), SHARD_AXIS[_base(k)])
        return out

    def scatter_pack(group, grads):
        return _pack([_reshard(grads[k], SHARD_AXIS[_base(k)]) for k in group], F32, (N_DEV,))

    def ffn_params(l, full):
        return dict(norm=local["ffn_norm"][l:l + 1], w_up=full[f"ffn_w_up@{l}"][0], conv_w=full[f"ffn_conv_w@{l}"][0],
                    conv_b=local["ffn_conv_b"][l:l + 1], w_down=full[f"ffn_w_down@{l}"][0])

    def ffn_grads(l, g):
        return {f"ffn_w_up@{l}": g["w_up"][None], f"ffn_w_down@{l}": g["w_down"][None], f"ffn_conv_w@{l}": g["conv_w"][None]}

    (landed,) = _exchange("gather_lru", [(gather_pack(GROUPS["lru"]), False)])
    full = gathered(GROUPS["lru"], landed)
    lru = dict(norm=local["lru_norm"], b_out=local["lru_b_out"], w_in=full["lru_w_in"][0],
               b_y=local["lru_b_in"][:, :D], b_x=local["lru_b_in"][:, D:], conv_w=full["lru_conv_w"][0],
               conv_b=local["lru_conv_b"], gate_w=_block_diag_gates(local["lru_gate_w"][0]),
               gate_b=local["lru_gate_b"][0].reshape(2, D), lam=local["lru_lambda"], w_out=full["lru_w_out"][0])
    x0 = x.reshape(n, D)
    ride = _Carry([(gather_pack(GROUPS["ffn0"]), False)])
    x1, s_lru = _lru_fwd(x0, lru, t, carry=ride)
    ffn0 = ffn_params(0, gathered(GROUPS["ffn0"], ride.landed[0]))
    ride = _Carry([(gather_pack(GROUPS["rwkv"]), False)])
    x2, s_ffn0 = _ffn_fwd("ffn0", x1, ffn0, t, carry=ride)
    full = gathered(GROUPS["rwkv"], ride.landed[0])
    rwkv = dict(norm=full["rwkv_norm"], mix=full["rwkv_mix"][0], w_r=full["rwkv_w_rkv"][0, 0],
                w_k=full["rwkv_w_rkv"][0, 1], w_v=full["rwkv_w_rkv"][0, 2], w0=full["rwkv_w0"], w1=full["rwkv_w1"][0],
                w2=full["rwkv_w2"][0], a0=full["rwkv_a0"], a1=full["rwkv_a1"][0], a2=full["rwkv_a2"][0],
                g1=full["rwkv_g1"][0], g2=full["rwkv_g2"][0], k_k=full["rwkv_k_k"], k_a=full["rwkv_k_a"],
                r_k=local["rwkv_r_k"].reshape(1, D), ln_w=full["rwkv_ln_w"], ln_b=full["rwkv_ln_b"],
                w_out=full["rwkv_w_out"][0])
    ride = _Carry([(gather_pack(GROUPS["ffn1"]), False)])
    x3, s_rwkv = _rwkv_fwd(x2, rwkv, t, carry=ride)
    ffn1 = ffn_params(1, gathered(GROUPS["ffn1"], ride.landed[0]))
    x4, s_ffn1 = _ffn_fwd("ffn1", x3, ffn1, t)
    loss, dx4, d_final = _loss_head(x4, target.reshape(n, D), local["final_norm"].reshape(1, D))

    parts = {}
    dx3, g_ffn1 = _ffn_bwd("ffn1", dx4, s_ffn1, ffn1, t)
    ride = _Carry([(scatter_pack(GROUPS["ffn1"], ffn_grads(1, g_ffn1)), True)])
    dx2, g_rwkv = _rwkv_bwd(dx3, s_rwkv, rwkv, t, carry=ride)
    parts["ffn1"] = ride.landed[0]
    rwkv_grads = {
        "rwkv_norm": g_rwkv["norm"], "rwkv_mix": g_rwkv["mix"][None],
        "rwkv_w_rkv": jnp.stack([g_rwkv["w_r"], g_rwkv["w_k"], g_rwkv["w_v"]])[None],
        "rwkv_w0": g_rwkv["w0"], "rwkv_w1": g_rwkv["w1"][None], "rwkv_w2": g_rwkv["w2"][None],
        "rwkv_a0": g_rwkv["a0"], "rwkv_a1": g_rwkv["a1"][None], "rwkv_a2": g_rwkv["a2"][None],
        "rwkv_g1": g_rwkv["g1"][None], "rwkv_g2": g_rwkv["g2"][None], "rwkv_k_k": g_rwkv["k_k"],
        "rwkv_k_a": g_rwkv["k_a"], "rwkv_ln_w": g_rwkv["ln_w"], "rwkv_ln_b": g_rwkv["ln_b"],
        "rwkv_w_out": g_rwkv["w_out"][None],
    }
    ride = _Carry([(scatter_pack(GROUPS["rwkv"], rwkv_grads), True)])
    dx1, g_ffn0 = _ffn_bwd("ffn0", dx2, s_ffn0, ffn0, t, carry=ride)
    parts["rwkv"] = ride.landed[0]
    ride = _Carry([(scatter_pack(GROUPS["ffn0"], ffn_grads(0, g_ffn0)), True)])
    dx0, g_lru = _lru_bwd(dx1, s_lru, lru, t, carry=ride)
    parts["ffn0"] = ride.landed[0]

    g_ffn = [g_ffn0, g_ffn1]
    lru_grads = {"lru_w_in": g_lru["w_in"][None], "lru_conv_w": g_lru["conv_w"][None], "lru_w_out": g_lru["w_out"][None]}
    gfull = {
        "lru_norm": g_lru["norm"], "lru_b_in": jnp.concatenate([g_lru["b_y"], g_lru["b_x"]], axis=1),
        "lru_conv_b": g_lru["conv_b"], "lru_gate_w": _gate_blocks_grad(g_lru["gate_w"])[None],
        "lru_gate_b": g_lru["gate_b"].reshape(1, 2, N_HEAD, HEAD), "lru_lambda": g_lru["lam"],
        "lru_b_out": g_lru["b_out"], "rwkv_r_k": g_rwkv["r_k"].reshape(1, N_HEAD, HEAD),
        "ffn_norm": jnp.concatenate([g["norm"] for g in g_ffn]),
        "ffn_conv_b": jnp.concatenate([g["conv_b"] for g in g_ffn]),
        "final_norm": d_final.reshape(D),
    }
    parts["lru"], parts["replicated"] = _exchange(
        "grad_tail", [(scatter_pack(GROUPS["lru"], lru_grads), True), (_pack([gfull[k] for k in REPLICATED], F32), False)])

    out = {}
    for tag, names in (*GROUPS.items(), ("replicated", REPLICATED)):
        packs = [_pack([block(k, pre) for k in names], F32) for pre in ("", "m_", "v_")]
        res = _adamw(f"adamw_{tag}", parts[tag], *packs)
        for kind, buf in zip(("grad", "delta", "new_m", "new_v"), res):
            for k, a in zip(names, _unpack(buf, [block(k).shape for k in names])):
                out[(kind, k)] = a
    for kind in ("grad", "delta", "new_m", "new_v"):
        for k in STACKED:
            out[(kind, k)] = jnp.concatenate([out[(kind, f"{k}@0")], out[(kind, f"{k}@1")]], axis=0)
    return loss, dx0.reshape(x.shape), out


def kernel(x, lru_norm, lru_w_in, lru_b_in, lru_conv_w, lru_conv_b, lru_gate_w, lru_gate_b, lru_lambda, lru_w_out, lru_b_out, rwkv_norm, rwkv_mix, rwkv_w_rkv, rwkv_w0, rwkv_w1, rwkv_w2, rwkv_a0, rwkv_a1, rwkv_a2, rwkv_g1, rwkv_g2, rwkv_k_k, rwkv_k_a, rwkv_r_k, rwkv_ln_w, rwkv_ln_b, rwkv_w_out, ffn_norm, ffn_w_up, ffn_conv_w, ffn_conv_b, ffn_w_down, final_norm, loss_target, m_lru_norm, m_lru_w_in, m_lru_b_in, m_lru_conv_w, m_lru_conv_b, m_lru_gate_w, m_lru_gate_b, m_lru_lambda, m_lru_w_out, m_lru_b_out, m_rwkv_norm, m_rwkv_mix, m_rwkv_w_rkv, m_rwkv_w0, m_rwkv_w1, m_rwkv_w2, m_rwkv_a0, m_rwkv_a1, m_rwkv_a2, m_rwkv_g1, m_rwkv_g2, m_rwkv_k_k, m_rwkv_k_a, m_rwkv_r_k, m_rwkv_ln_w, m_rwkv_ln_b, m_rwkv_w_out, m_ffn_norm, m_ffn_w_up, m_ffn_conv_w, m_ffn_conv_b, m_ffn_w_down, m_final_norm, v_lru_norm, v_lru_w_in, v_lru_b_in, v_lru_conv_w, v_lru_conv_b, v_lru_gate_w, v_lru_gate_b, v_lru_lambda, v_lru_w_out, v_lru_b_out, v_rwkv_norm, v_rwkv_mix, v_rwkv_w_rkv, v_rwkv_w0, v_rwkv_w1, v_rwkv_w2, v_rwkv_a0, v_rwkv_a1, v_rwkv_a2, v_rwkv_g1, v_rwkv_g2, v_rwkv_k_k, v_rwkv_k_a, v_rwkv_r_k, v_rwkv_ln_w, v_rwkv_ln_b, v_rwkv_w_out, v_ffn_norm, v_ffn_w_up, v_ffn_conv_w, v_ffn_conv_b, v_ffn_w_down, v_final_norm):
    args = locals()
    local = {(pre + k): args[pre + k] for pre in ("", "m_", "v_") for k in WEIGHTS}
    loss_local, grad_x, out = _step(local, x, loss_target)
    loss = lax.psum(loss_local, ("x", "y", "c"))
    return (loss, grad_x, *[out[(kind, k)] for kind in ("grad", "delta", "new_m", "new_v") for k in WEIGHTS])
```

```python
import functools
import math

import jax
import jax.numpy as jnp
from jax import lax
from jax.experimental import pallas as pl
from jax.experimental.pallas import tpu as pltpu

F32 = jnp.float32
BF16 = jnp.bfloat16
HI = lax.Precision.HIGHEST

N_DEV = 8
D = 1024
HEAD = 64
N_HEAD = D // HEAD
D_FF = 3 * D
LANE = 128
V7X_VMEM_BYTES = 64 * 1024 * 1024
VMEM_LIMIT = V7X_VMEM_BYTES - 8 * 1024 * 1024
CT = LANE
CHUNK = 64
SCAN_HEADS = 8
PACK_W = 1024
PACK_ROWS = 256

ADAM_LR, ADAM_B1, ADAM_B2, ADAM_EPS, ADAM_WD, ADAM_STEP = 0.001, 0.9, 0.999, 1e-08, 0.01, 10
RMS_EPS = 1e-6
GN_EPS = 64e-5
LRU_C = 8.0

WEIGHTS = ['lru_norm', 'lru_w_in', 'lru_b_in', 'lru_conv_w', 'lru_conv_b', 'lru_gate_w', 'lru_gate_b',
           'lru_lambda', 'lru_w_out', 'lru_b_out', 'rwkv_norm', 'rwkv_mix', 'rwkv_w_rkv', 'rwkv_w0', 'rwkv_w1',
           'rwkv_w2', 'rwkv_a0', 'rwkv_a1', 'rwkv_a2', 'rwkv_g1', 'rwkv_g2', 'rwkv_k_k', 'rwkv_k_a', 'rwkv_r_k',
           'rwkv_ln_w', 'rwkv_ln_b', 'rwkv_w_out', 'ffn_norm', 'ffn_w_up', 'ffn_conv_w', 'ffn_conv_b',
           'ffn_w_down', 'final_norm']
SHARD_AXIS = {'lru_w_in': 2, 'lru_conv_w': 2, 'lru_w_out': 1, 'rwkv_norm': 1, 'rwkv_mix': 2, 'rwkv_w_rkv': 2,
              'rwkv_w0': 1, 'rwkv_w1': 1, 'rwkv_w2': 2, 'rwkv_a0': 1, 'rwkv_a1': 1, 'rwkv_a2': 2, 'rwkv_g1': 1,
              'rwkv_g2': 2, 'rwkv_k_k': 1, 'rwkv_k_a': 1, 'rwkv_ln_w': 1, 'rwkv_ln_b': 1, 'rwkv_w_out': 1,
              'ffn_w_up': 2, 'ffn_conv_w': 2, 'ffn_w_down': 1}
GATHER_BF16 = ['lru_w_in', 'lru_w_out', 'rwkv_w_rkv', 'rwkv_w1', 'rwkv_a1', 'rwkv_g1', 'rwkv_w_out', 'ffn_w_up',
               'ffn_w_down']
REPLICATED = [n for n in WEIGHTS if n not in SHARD_AXIS]
STACKED = ['ffn_w_up', 'ffn_w_down', 'ffn_conv_w']
GROUPS = {
    "lru": ['lru_w_in', 'lru_w_out', 'lru_conv_w'],
    "ffn0": [f"{n}@0" for n in STACKED],
    "rwkv": [n for n in WEIGHTS if n.startswith("rwkv_") and n in SHARD_AXIS],
    "ffn1": [f"{n}@1" for n in STACKED],
}


def _base(name):
    return name.partition("@")[0]


def _pcall(body, **kw):
    return pl.pallas_call(body, **kw)


class _Carry:
    def __init__(self, items):
        self.items = list(items)
        self.landed = None


def _exchange_copies(src_ref, dst_ref, send_sems, recv_sems, local_sem, scatter):
    x, y, c = lax.axis_index("x"), lax.axis_index("y"), lax.axis_index("c")
    me = 4 * x + 2 * y + c
    mine = pltpu.make_async_copy(src_ref.at[me] if scatter else src_ref, dst_ref.at[me], local_sem)
    copies = []
    for m in range(1, N_DEV):
        px = 1 - x if m & 4 else x
        py = 1 - y if m & 2 else y
        pc = 1 - c if m & 1 else c
        part = src_ref.at[4 * px + 2 * py + pc] if scatter else src_ref
        copies.append(pltpu.make_async_remote_copy(
            src_ref=part, dst_ref=dst_ref.at[me], send_sem=send_sems.at[m - 1], recv_sem=recv_sems.at[m - 1],
            device_id=(px, py, pc), device_id_type=pl.DeviceIdType.MESH))
    return mine, copies


def _start_exchange(mine, copies):
    mine.start()
    for cp in copies:
        cp.start()


def _wait_exchange(mine, copies):
    for cp in copies:
        cp.wait_recv()
    for cp in copies:
        cp.wait_send()
    mine.wait()


_EXCHANGE_SEMS = [pltpu.SemaphoreType.DMA((N_DEV - 1,)), pltpu.SemaphoreType.DMA((N_DEV - 1,)), pltpu.SemaphoreType.DMA]


def _landing_shape(src, scatter):
    return jax.ShapeDtypeStruct((N_DEV,) + tuple(src.shape[1:] if scatter else src.shape), src.dtype)


def _call(body, operands, *, carry=None, name, grid, in_specs, out_specs, out_shape, scratch_shapes=(),
          compiler_params=None):
    if carry is None:
        return _pcall(body, name=name, grid=grid, in_specs=in_specs, out_specs=out_specs, out_shape=out_shape,
                      scratch_shapes=list(scratch_shapes), compiler_params=compiler_params)(*operands)
    single = not isinstance(out_specs, (list, tuple))
    out_specs_l = [out_specs] if single else list(out_specs)
    out_shape_l = [out_shape] if single else list(out_shape)
    n_in, n_out, n_scr, n_x = len(in_specs), len(out_specs_l), len(scratch_shapes), len(carry.items)
    flags = [sc for (_, sc) in carry.items]

    def wrapped(*refs):
        ins, refs = refs[:n_in], refs[n_in:]
        xsrc, refs = refs[:n_x], refs[n_x:]
        outs, refs = refs[:n_out], refs[n_out:]
        xdst, refs = refs[:n_x], refs[n_x:]
        scr, sems = refs[:n_scr], refs[n_scr:]
        first = functools.reduce(jnp.logical_and, [pl.program_id(i) == 0 for i in range(len(grid))])
        last = functools.reduce(jnp.logical_and, [pl.program_id(i) == grid[i] - 1 for i in range(len(grid))])

        def exchanges():
            return [_exchange_copies(xsrc[i], xdst[i], *sems[3 * i:3 * i + 3], flags[i]) for i in range(n_x)]

        @pl.when(first)
        def _():
            for mine, copies in exchanges():
                _start_exchange(mine, copies)

        body(*ins, *outs, *scr)

        @pl.when(last)
        def _():
            for mine, copies in exchanges():
                _wait_exchange(mine, copies)

    hbm = pl.BlockSpec(memory_space=pl.ANY)
    res = _pcall(
        wrapped, name=name, grid=grid, in_specs=list(in_specs) + [hbm] * n_x,
        out_specs=out_specs_l + [hbm] * n_x,
        out_shape=out_shape_l + [_landing_shape(a, sc) for (a, sc) in carry.items],
        scratch_shapes=list(scratch_shapes) + _EXCHANGE_SEMS * n_x, compiler_params=compiler_params,
    )(*operands, *[a for (a, _) in carry.items])
    carry.landed = list(res[n_out:])
    return res[0] if single else list(res[:n_out])


def _params(n_grid):
    return pltpu.CompilerParams(dimension_semantics=("arbitrary",) * n_grid, vmem_limit_bytes=VMEM_LIMIT)


def _shift_rows(x, d, up):
    n = x.shape[0]
    idx = lax.broadcasted_iota(jnp.int32, x.shape, 0)
    if up:
        return jnp.where(idx < n - d, pltpu.roll(x, n - d, 0), 0.0)
    return jnp.where(idx >= d, pltpu.roll(x, d, 0), 0.0)


@functools.partial(jax.custom_vjp, nondiff_argnums=(1,))
def _shift_down(x, d):
    return _shift_rows(x, d, False)


def _shift_down_fwd(x, d):
    return _shift_rows(x, d, False), None


def _shift_down_bwd(d, _, g):
    return (_shift_rows(g, d, True),)


_shift_down.defvjp(_shift_down_fwd, _shift_down_bwd)


def _scan_doubling(a, b, up):
    n = a.shape[0]
    d = 1
    while d < n:
        b = b + a * _shift_rows(b, d, up)
        a = a * _shift_rows(a, d, up)
        d *= 2
    return b


@jax.custom_vjp
def _linear_scan(a, b):
    return _scan_doubling(a, b, False)


def _linear_scan_fwd(a, b):
    h = _scan_doubling(a, b, False)
    return h, (a, h)


def _linear_scan_bwd(res, dh):
    a, h = res
    g = _scan_doubling(_shift_rows(a, 1, True), dh, True)
    return g * _shift_rows(h, 1, False), g


_linear_scan.defvjp(_linear_scan_fwd, _linear_scan_bwd)


def _causal_conv(x, w, b):
    k = w.shape[0]
    out = b + x * w[k - 1:k]
    for j in range(k - 1):
        out = out + _shift_down(x, k - 1 - j) * w[j:j + 1]
    return out


_GELU_C1 = math.sqrt(2.0 / math.pi)
_GELU_C2 = 0.044715 * _GELU_C1


@jax.custom_vjp
def _gelu(x):
    return 0.5 * x * (1.0 + jnp.tanh(x * (_GELU_C1 + _GELU_C2 * (x * x))))


def _gelu_fwd(x):
    x2 = x * x
    t = jnp.tanh(x * (_GELU_C1 + _GELU_C2 * x2))
    return 0.5 * x * (1.0 + t), (x, x2, t)


def _gelu_bwd(res, g):
    x, x2, t = res
    return (g * (0.5 * (1.0 + t) + (0.5 * x) * (1.0 - t * t) * (_GELU_C1 + (3.0 * _GELU_C2) * x2)),)


_gelu.defvjp(_gelu_fwd, _gelu_bwd)


def _rmsnorm(x, g):
    return x * lax.rsqrt(jnp.mean(x * x, axis=-1, keepdims=True) + RMS_EPS) * g


def _neg_expm1(x):
    series = x * (1.0 + x * 0.5 * (1.0 + x * (1.0 / 3.0) * (1.0 + x * 0.25 * (1.0 + x * 0.2))))
    return -jnp.where(x > -0.1, series, jnp.exp(x) - 1.0)


def _dot(a, b, ca=1, cb=0, precision=None):
    return lax.dot_general(a, b, (((ca,), (cb,)), ((), ())), precision=precision, preferred_element_type=F32)


def _bdot(a, b):
    return _dot(a.astype(BF16), b.astype(BF16))


def _split_bf16(x):
    hi = x.astype(BF16)
    return hi, (x - hi.astype(F32)).astype(BF16)


def _head_sum_impl(x):
    row = lax.broadcasted_iota(jnp.int32, (D, LANE), 0)
    col = lax.broadcasted_iota(jnp.int32, (D, LANE), 1)
    e = (lax.shift_right_logical(row, 6) == col).astype(BF16)
    hi, lo = _split_bf16(x)
    s_hi, s_lo = _split_bf16(_dot(hi, e) + _dot(lo, e))
    return _dot(s_hi, e, 1, 1) + _dot(s_lo, e, 1, 1)


@jax.custom_vjp
def _head_sum(x):
    return _head_sum_impl(x)


def _head_sum_fwd(x):
    return _head_sum_impl(x), None


def _head_sum_bwd(_, g):
    return (_head_sum(g),)


_head_sum.defvjp(_head_sum_fwd, _head_sum_bwd)


def _specs(items):
    return [pl.BlockSpec(it[1], it[2]) for it in items]


def _stage_fwd(name, f, grid, ins, params, outs, carry=None):
    n_in = len(ins) + len(params)

    def body(*refs):
        res = f(*[r[...] for r in refs[:n_in]])
        for o, v in zip(refs[n_in:], res):
            o[...] = v.astype(o.dtype)

    return _call(
        body, [it[0] for it in ins + params], carry=carry, name=name, grid=grid, in_specs=_specs(ins + params),
        out_specs=[pl.BlockSpec(bs, im) for (_, _, bs, im) in outs],
        out_shape=[jax.ShapeDtypeStruct(s, dt) for (s, dt, _, _) in outs],
        compiler_params=_params(len(grid)),
    )


def _stage_bwd(name, f, grid, ins, params, douts, din_dtypes, adds=None, carry=None):
    adds = adds or {}
    n_in, n_par = len(ins), len(params)
    dout_items = [(a, bs, im) for (arrs, bs, im) in douts for a in arrs]
    add_items = [(adds[i], ins[i][1], ins[i][2]) for i in sorted(adds)]
    n_do, n_add = len(dout_items), len(add_items)

    def body(*refs):
        vals = [r[...] for r in refs[:n_in + n_par]]
        do_refs = list(refs[n_in + n_par:n_in + n_par + n_do])
        add_refs = dict(zip(sorted(adds), refs[n_in + n_par + n_do:n_in + n_par + n_do + n_add]))
        din_refs = refs[n_in + n_par + n_do + n_add:n_in + n_par + n_do + n_add + n_in]
        dpar_refs = refs[n_in + n_par + n_do + n_add + n_in:]
        cts = []
        for (arrs, _, _) in douts:
            ct = do_refs.pop(0)[...].astype(F32)
            for _ in arrs[1:]:
                ct = ct + do_refs.pop(0)[...].astype(F32)
            cts.append(ct)
        _, vjp = jax.vjp(f, *vals)
        grads = vjp(tuple(cts))
        for i, r in enumerate(din_refs):
            g = grads[i]
            if i in add_refs:
                g = g + add_refs[i][...]
            r[...] = g.astype(r.dtype)

        @pl.when(pl.program_id(len(grid) - 1) == 0)
        def _():
            for r in dpar_refs:
                r[...] = jnp.zeros(r.shape, r.dtype)

        for j, r in enumerate(dpar_refs):
            r[...] += grads[n_in + j]

    din_shapes = [it[3][0] if len(it) > 3 else it[0].shape for it in ins]
    din_specs = [pl.BlockSpec(it[1], it[3][1] if len(it) > 3 else it[2]) for it in ins]
    res = _call(
        body, [it[0] for it in ins + params + dout_items + add_items], carry=carry, name=name, grid=grid,
        in_specs=_specs(ins + params + dout_items + add_items), out_specs=din_specs + _specs(params),
        out_shape=[jax.ShapeDtypeStruct(s, dt) for s, dt in zip(din_shapes, din_dtypes)]
        + [jax.ShapeDtypeStruct(it[0].shape, F32) for it in params],
        compiler_params=_params(len(grid)),
    )
    return list(res[:n_in]), list(res[n_in:])


def _tile(n, want):
    t = min(n, want)
    while n % t:
        t //= 2
    return t


def _matmul(name, a, b, *, mode="nn", resid=None, out_dtype=F32, tm=512, tn=512, tk=1024, b_koff=0, carry=None):
    if mode == "tn":
        (kdim, m), n = a.shape, b.shape[1]
    else:
        (m, kdim), n = a.shape, (b.shape[1] if mode == "nn" else b.shape[0])
    tm, tn, tk = _tile(m, tm), _tile(n, tn), _tile(kdim, tk)
    nk = kdim // tk
    ko = b_koff // tk
    assert ko * tk == b_koff
    a_spec = pl.BlockSpec((tk, tm), lambda i, j, k: (k, i)) if mode == "tn" else pl.BlockSpec((tm, tk), lambda i, j, k: (i, k))
    b_spec = pl.BlockSpec((tn, tk), lambda i, j, k: (j, k + ko)) if mode == "nt" else pl.BlockSpec((tk, tn), lambda i, j, k: (k + ko, j))
    ca = 0 if mode == "tn" else 1
    cb = 1 if mode == "nt" else 0
    operands = [a, b]
    in_specs = [a_spec, b_spec]
    if resid is not None:
        operands.append(resid)
        in_specs.append(pl.BlockSpec((tm, tn), lambda i, j, k: (i, j)))

    def finish(r, refs, o_ref):
        if resid is not None:
            r = r + refs[2][...]
        o_ref[...] = r.astype(o_ref.dtype)

    def body_one_step(*refs):
        finish(_dot(refs[0][...].astype(BF16), refs[1][...].astype(BF16), ca, cb), refs, refs[-1])

    def body(*refs):
        a_ref, b_ref = refs[0], refs[1]
        o_ref, acc_ref = refs[-2], refs[-1]
        k = pl.program_id(2)

        @pl.when(k == 0)
        def _():
            acc_ref[...] = jnp.zeros(acc_ref.shape, F32)

        acc_ref[...] += _dot(a_ref[...].astype(BF16), b_ref[...].astype(BF16), ca, cb)

        @pl.when(k == nk - 1)
        def _():
            finish(acc_ref[...], refs, o_ref)

    return _call(
        body_one_step if nk == 1 else body, operands, carry=carry, name=name, grid=(m // tm, n // tn, nk),
        in_specs=in_specs, out_specs=pl.BlockSpec((tm, tn), lambda i, j, k: (i, j)),
        out_shape=jax.ShapeDtypeStruct((m, n), out_dtype),
        scratch_shapes=[] if nk == 1 else [pltpu.VMEM((tm, tn), F32)],
        compiler_params=_params(3),
    )


def _add(name, a, b):
    rows, cols = a.shape
    tr = _tile(rows, 512)

    def body(a_ref, b_ref, o_ref):
        o_ref[...] = a_ref[...] + b_ref[...]

    spec = pl.BlockSpec((tr, cols), lambda i: (i, 0))
    return _pcall(body, name=name, grid=(rows // tr,), in_specs=[spec, spec], out_specs=spec,
                  out_shape=jax.ShapeDtypeStruct(a.shape, a.dtype), compiler_params=_params(1))(a, b)


def _f_lru_pre(x, norm, b_out):
    return _rmsnorm(x, norm), x + b_out


def _f_lru_core(uy, ux, b_y, b_x, cw, cb, gw, gb, lam):
    yb = _gelu(uy + b_y)
    xr = _causal_conv(ux + b_x, cw, cb)
    gr = jax.nn.sigmoid(_bdot(xr, gw[0, 0]) + gb[0:1])
    gi = jax.nn.sigmoid(_bdot(xr, gw[1, 0]) + gb[1:2])
    log_a = -LRU_C * gr * jax.nn.softplus(-lam)
    a = jnp.exp(log_a)
    bterm = jnp.sqrt(_neg_expm1(2.0 * log_a)) * (gi * xr)
    return (_linear_scan(a, bterm) * yb,)


def _f_norm(x, norm):
    return (_rmsnorm(x, norm),)


def _f_ffn_core(ug, uv, cw, cb):
    return (_gelu(_causal_conv(ug, cw, cb)) * uv,)


def _f_rwkv_mix(h, mix):
    xx = _shift_down(h, 1) - h
    return tuple(h + xx * mix[i:i + 1] for i in range(6))


def _f_rwkv_pre(k, lw1, la1, lg1, w0, a0, k_k, k_a, w2, a2, g2):
    wpre = w0 + _bdot(jnp.tanh(lw1), w2)
    apre = a0 + _bdot(la1, a2)
    g = _bdot(jax.nn.sigmoid(lg1), g2)
    log_decay = -jnp.exp(-jax.nn.softplus(-wpre) - 0.5)
    a = jax.nn.sigmoid(apre)
    kk = k * k_k
    kk = kk / jnp.maximum(jnp.sqrt(_head_sum(kk * kk)), 1e-12)
    kf = k * (1.0 + (a - 1.0) * k_a)
    return log_decay, kf, -kk, kk * a, g


def _f_rwkv_post(y, r, kf, v, g, ln_w, ln_b, r_k):
    inv = 1.0 / HEAD
    yc = y - _head_sum(y) * inv
    var = _head_sum(yc * yc) * inv
    yn = yc * lax.rsqrt(var + GN_EPS) * ln_w + ln_b
    bonus = _head_sum(r * kf * r_k) * v
    return ((yn + bonus) * g,)


def _hdot(a, b, ca, cb):
    return lax.dot_general(a, b, (((ca,), (cb,)), ((0,), (0,))), preferred_element_type=F32)


def _mm3_impl(a, b, ca, cb):
    a_hi, a_lo = _split_bf16(a)
    b_hi, b_lo = _split_bf16(b)
    lhs = jnp.concatenate([a_hi, a_hi, a_lo], axis=ca)
    rhs = jnp.concatenate([b_hi, b_lo, b_hi], axis=cb)
    return _hdot(lhs, rhs, ca, cb)


@functools.partial(jax.custom_vjp, nondiff_argnums=(2, 3))
def _mm3(a, b, ca, cb):
    return _mm3_impl(a, b, ca, cb)


def _mm3_fwd(a, b, ca, cb):
    return _mm3_impl(a, b, ca, cb), (a, b)


def _mm3_bwd(ca, cb, res, dc):
    a, b = res
    fa, fb = 3 - ca, 3 - cb
    da = _mm3(dc, b, 2, fb) if ca == 2 else _mm3(b, dc, fb, 2)
    db = _mm3(a, dc, fa, 1) if cb == 1 else _mm3(dc, a, 1, fa)
    return da, db


_mm3.defvjp(_mm3_fwd, _mm3_bwd)


def _tri_dot_impl(x, upper):
    g, n, _ = x.shape
    row = lax.broadcasted_iota(jnp.int32, (g, n, 3 * n), 1)
    col = lax.broadcasted_iota(jnp.int32, (g, n, 3 * n), 2)
    col = jnp.where(col >= 2 * n, col - 2 * n, jnp.where(col >= n, col - n, col))
    tri = (row <= col if upper else row >= col).astype(BF16)
    hi = x.astype(BF16)
    rem = x - hi.astype(F32)
    mid = rem.astype(BF16)
    lo = (rem - mid.astype(F32)).astype(BF16)
    return _hdot(tri, jnp.concatenate([hi, mid, lo], axis=1), 2, 1)


@functools.partial(jax.custom_vjp, nondiff_argnums=(1,))
def _tri_dot(x, upper):
    return _tri_dot_impl(x, upper)


def _tri_dot_fwd(x, upper):
    return _tri_dot_impl(x, upper), None


def _tri_dot_bwd(upper, _, g):
    return (_tri_dot(g, not upper),)


_tri_dot.defvjp(_tri_dot_fwd, _tri_dot_bwd)


def _unit_lower_inverse_impl(a):
    g, n, _ = a.shape
    row = lax.broadcasted_iota(jnp.int32, (g, n, n), 1)
    col = lax.broadcasted_iota(jnp.int32, (g, n, n), 2)
    inv = (row == col).astype(F32) + a
    p = a
    for _ in range(int(math.log2(n)) - 1):
        p = _mm3(p, p, 2, 1)
        inv = inv + _mm3(inv, p, 2, 1)
    return inv


@jax.custom_vjp
def _unit_lower_inverse(a):
    return _unit_lower_inverse_impl(a)


def _unit_lower_inverse_fwd(a):
    inv = _unit_lower_inverse_impl(a)
    return inv, inv


def _unit_lower_inverse_bwd(inv, d_inv):
    return (_mm3(_mm3(inv, d_inv, 1, 1), inv, 2, 2),)


_unit_lower_inverse.defvjp(_unit_lower_inverse_fwd, _unit_lower_inverse_bwd)


def _rwkv_chunk(z0, r, lw, k, v, a, b):
    g, n, _ = r.shape
    row = lax.broadcasted_iota(jnp.int32, (g, n, n), 1)
    col = lax.broadcasted_iota(jnp.int32, (g, n, n), 2)
    incl, strict = row >= col, row > col
    cs = _tri_dot(lw, False)
    c_last = cs[:, n - 1:n]
    inv = jnp.exp(-cs)
    ar = jnp.concatenate([a * jnp.exp(cs - lw), r * jnp.exp(cs)], axis=1)
    bk = jnp.concatenate([b * inv, k * inv], axis=1)
    pair = _mm3(ar, bk, 2, 2)
    a_ab = jnp.where(strict, pair[:, :n, :n], 0.0)
    a_ak = jnp.where(strict, pair[:, :n, n:], 0.0)
    a_rbk = jnp.concatenate([jnp.where(incl, pair[:, n:, :n], 0.0), jnp.where(incl, pair[:, n:, n:], 0.0)], axis=2)
    arz = _mm3(ar, z0, 2, 1)
    u = _mm3(_unit_lower_inverse(a_ab), arz[:, :n] + _mm3(a_ak, v, 2, 1), 2, 1)
    uv = jnp.concatenate([u, v], axis=1)
    y = arz[:, n:] + _mm3(a_rbk, uv, 2, 1)
    tail = jnp.exp(c_last - cs)
    er = lax.broadcasted_iota(jnp.int32, (g, HEAD, HEAD), 1)
    ec = lax.broadcasted_iota(jnp.int32, (g, HEAD, HEAD), 2)
    decay_all = jnp.where(er == ec, jnp.exp(c_last), 0.0)
    z_l = _mm3(jnp.concatenate([b * tail, k * tail, decay_all], axis=1), jnp.concatenate([uv, z0], axis=1), 1, 1)
    return y, z_l


def _heads_in(ref, g):
    x = ref[...]
    return jnp.stack([x[:, h * HEAD:(h + 1) * HEAD] for h in range(g)], axis=0)


def _heads_out(ref, x):
    ref[...] = jnp.concatenate([x[h] for h in range(x.shape[0])], axis=1)


def _scan_geometry(n_rows, t):
    g, n = _tile(N_HEAD, SCAN_HEADS), _tile(t, CHUNK)
    return g, n, t // n, N_HEAD // g, n_rows // t


def _rwkv_scan_fwd(seqs, t, carry=None):
    rows = seqs[0].shape[0]
    g, n, nc, hpg, bsz = _scan_geometry(rows, t)

    def body(r_ref, lw_ref, k_ref, v_ref, a_ref, b_ref, y_ref, zs_ref, z_ref):
        @pl.when(pl.program_id(1) == 0)
        def _():
            z_ref[...] = jnp.zeros(z_ref.shape, F32)

        z0 = z_ref[...]
        zs_ref[:, 0] = z0
        y, z_l = _rwkv_chunk(z0, *[_heads_in(ref, g) for ref in (r_ref, lw_ref, k_ref, v_ref, a_ref, b_ref)])
        _heads_out(y_ref, y)
        z_ref[...] = z_l

    seq_spec = pl.BlockSpec((n, g * HEAD), lambda i, c: ((i // hpg) * nc + c, i % hpg))
    return _call(
        body, list(seqs), carry=carry, name="rwkv_scan_fwd", grid=(bsz * hpg, nc), in_specs=[seq_spec] * 6,
        out_specs=[seq_spec, pl.BlockSpec((g, 1, HEAD, HEAD), lambda i, c: (i, c, 0, 0))],
        out_shape=[jax.ShapeDtypeStruct((rows, D), F32), jax.ShapeDtypeStruct((bsz * N_HEAD, nc, HEAD, HEAD), F32)],
        scratch_shapes=[pltpu.VMEM((g, HEAD, HEAD), F32)],
        compiler_params=_params(2),
    )


def _rwkv_scan_bwd(seqs, zs, dy, t, carry=None):
    rows = seqs[0].shape[0]
    g, n, nc, hpg, bsz = _scan_geometry(rows, t)

    def body(r_ref, lw_ref, k_ref, v_ref, a_ref, b_ref, zs_ref, dy_ref, dr, dlw, dk, dv, da, db, dz_ref):
        @pl.when(pl.program_id(1) == 0)
        def _():
            dz_ref[...] = jnp.zeros(dz_ref.shape, F32)

        _, vjp = jax.vjp(_rwkv_chunk, zs_ref[:, 0],
                         *[_heads_in(ref, g) for ref in (r_ref, lw_ref, k_ref, v_ref, a_ref, b_ref)])
        grads = vjp((_heads_in(dy_ref, g), dz_ref[...]))
        dz_ref[...] = grads[0]
        for o, gr in zip((dr, dlw, dk, dv, da, db), grads[1:]):
            _heads_out(o, gr)

    seq_spec = pl.BlockSpec((n, g * HEAD), lambda i, c: ((i // hpg) * nc + nc - 1 - c, i % hpg))
    return _call(
        body, [*seqs, zs, dy], carry=carry, name="rwkv_scan_bwd", grid=(bsz * hpg, nc),
        in_specs=[seq_spec] * 6 + [pl.BlockSpec((g, 1, HEAD, HEAD), lambda i, c: (i, nc - 1 - c, 0, 0)), seq_spec],
        out_specs=[seq_spec] * 6,
        out_shape=[jax.ShapeDtypeStruct((rows, D), F32)] * 6,
        scratch_shapes=[pltpu.VMEM((g, HEAD, HEAD), F32)],
        compiler_params=_params(2),
    )


def _loss_head(x, target, norm):
    n = x.shape[0]
    tr = _tile(n, 256)

    def f(xv, gv, tv):
        err = _rmsnorm(xv, gv) - tv
        return 0.5 * jnp.sum(jnp.mean(err * err, axis=-1, keepdims=True), axis=0, keepdims=True)

    def body(x_ref, t_ref, g_ref, dx_ref, dg_ref, loss_ref):
        val, vjp = jax.vjp(lambda xv, gv: f(xv, gv, t_ref[...]), x_ref[...], g_ref[...])
        dx, dg = vjp(jnp.ones((1, 1), F32))
        dx_ref[...] = dx

        @pl.when(pl.program_id(0) == 0)
        def _():
            dg_ref[...] = jnp.zeros(dg_ref.shape, F32)
            loss_ref[...] = jnp.zeros(loss_ref.shape, F32)

        dg_ref[...] += dg
        loss_ref[...] += jnp.broadcast_to(val, loss_ref.shape)

    row = pl.BlockSpec((tr, D), lambda i: (i, 0))
    vec = pl.BlockSpec((1, D), lambda i: (0, 0))
    dx, dg, loss = _pcall(
        body, name="loss_head", grid=(n // tr,), in_specs=[row, row, vec],
        out_specs=[row, vec, pl.BlockSpec((8, LANE), lambda i: (0, 0))],
        out_shape=[jax.ShapeDtypeStruct((n, D), F32), jax.ShapeDtypeStruct((1, D), F32),
                   jax.ShapeDtypeStruct((8, LANE), F32)],
        compiler_params=_params(1),
    )(x, target, norm)
    return loss[0, 0], dx, dg


def _exchange(name, items):
    n_x = len(items)

    def body(*refs):
        srcs, dsts, sems = refs[:n_x], refs[n_x:2 * n_x], refs[2 * n_x:]
        started = [_exchange_copies(srcs[i], dsts[i], *sems[3 * i:3 * i + 3], items[i][1]) for i in range(n_x)]
        for mine, copies in started:
            _start_exchange(mine, copies)
        for mine, copies in started:
            _wait_exchange(mine, copies)

    hbm = pl.BlockSpec(memory_space=pl.ANY)
    return _pcall(
        body, name=name, in_specs=[hbm] * n_x, out_specs=[hbm] * n_x,
        out_shape=[_landing_shape(a, sc) for (a, sc) in items], scratch_shapes=_EXCHANGE_SEMS * n_x,
    )(*[a for (a, _) in items])


def _adamw(name, parts, w, m, v):
    rows = w.shape[0]
    tr = _tile(rows, PACK_ROWS)

    def body(p_ref, w_ref, m_ref, v_ref, g_ref, d_ref, nm_ref, nv_ref):
        g = p_ref[0]
        for k in range(1, N_DEV):
            g = g + p_ref[k]
        nm = ADAM_B1 * m_ref[...] + (1.0 - ADAM_B1) * g
        nv = ADAM_B2 * v_ref[...] + (1.0 - ADAM_B2) * jnp.square(g)
        m_hat = nm / (1.0 - ADAM_B1 ** ADAM_STEP)
        v_hat = nv / (1.0 - ADAM_B2 ** ADAM_STEP)
        g_ref[...] = g
        d_ref[...] = -ADAM_LR * (m_hat / (jnp.sqrt(v_hat) + ADAM_EPS) + ADAM_WD * w_ref[...])
        nm_ref[...] = nm
        nv_ref[...] = nv

    row = pl.BlockSpec((tr, PACK_W), lambda i: (i, 0))
    return _pcall(
        body, name=name, grid=(rows // tr,),
        in_specs=[pl.BlockSpec((N_DEV, tr, PACK_W), lambda i: (0, i, 0)), row, row, row],
        out_specs=[row] * 4, out_shape=[jax.ShapeDtypeStruct((rows, PACK_W), F32)] * 4,
        compiler_params=_params(1),
    )(parts, w, m, v)


def _pack(arrs, dtype, lead=()):
    flat = jnp.concatenate([a.astype(dtype).reshape(lead + (-1,)) for a in arrs], axis=-1)
    n = flat.shape[-1]
    quantum = PACK_W * PACK_ROWS
    total = -(-n // quantum) * quantum
    flat = jnp.pad(flat, [(0, 0)] * len(lead) + [(0, total - n)])
    return flat.reshape(lead + (total // PACK_W, PACK_W))


def _unpack(buf, shapes, lead=()):
    flat = buf.reshape(lead + (-1,))
    out, off = [], 0
    for s in shapes:
        n = math.prod(s)
        out.append(flat[..., off:off + n].reshape(lead + tuple(s)))
        off += n
    return out


def _unshard(g, axis):
    local = g.shape[1:]
    full = jnp.moveaxis(g, 0, axis)
    return full.reshape(local[:axis] + (N_DEV * local[axis],) + local[axis + 1:])


def _reshard(full, axis):
    s = full.shape
    blocked = full.reshape(s[:axis] + (N_DEV, s[axis] // N_DEV) + s[axis + 1:])
    return jnp.moveaxis(blocked, axis, 0)


def _row_item(a, tr):
    return (a, (tr, a.shape[1]), lambda i: (i, 0))


def _vec_item(a):
    return (a, a.shape, lambda i: (0,) * a.ndim)


def _seq_item(a, t, width):
    return (a, (t, width), lambda j, b: (b, j))


def _seq_halves(a, t, width):
    half = a.shape[1] // 2
    off = half // width
    grad = ((a.shape[0], half), lambda j, b: (b, j))
    return [(a, (t, width), lambda j, b: (b, j), grad), (a, (t, width), lambda j, b: (b, j + off), grad)]


def _col_item(a, width):
    return (a, (a.shape[0], width), lambda j, b: (0, j))


def _ffn_fwd(tag, x, p, t, carry=None):
    n = x.shape[0]
    tr = _tile(n, 512)
    norm_ins, norm_par = [_row_item(x, tr)], [_vec_item(p["norm"])]
    (h,) = _stage_fwd(f"{tag}_norm", _f_norm, (n // tr,), norm_ins, norm_par, [((n, D), BF16, (tr, D), lambda i: (i, 0))])
    u = _matmul(f"{tag}_up", h, p["w_up"], tm=2048, tn=768, carry=carry)
    grid = (D_FF // CT, n // t)
    core_par = [_col_item(p["conv_w"], CT), _col_item(p["conv_b"], CT)]
    (hid,) = _stage_fwd(f"{tag}_core", _f_ffn_core, grid, _seq_halves(u, t, CT), core_par,
                        [((n, D_FF), BF16, (t, CT), lambda j, b: (b, j))])
    y = _matmul(f"{tag}_down", hid, p["w_down"], resid=x, tm=1024, tn=1024, tk=D_FF)
    return y, (x, h, u, hid)


def _ffn_bwd(tag, dy, saved, p, t, carry=None):
    x, h, u, hid = saved
    n = x.shape[0]
    tr = _tile(n, 512)
    grads = {}
    dhid = _matmul(f"{tag}_down_da", dy, p["w_down"], mode="nt", tm=1024, tn=768)
    grads["w_down"] = _matmul(f"{tag}_down_dw", hid, dy, mode="tn", tm=768, tn=1024, tk=2048)
    grid = (D_FF // CT, n // t)
    core_par = [_col_item(p["conv_w"], CT), _col_item(p["conv_b"], CT)]
    (dug, duv), (grads["conv_w"], grads["conv_b"]) = _stage_bwd(
        f"{tag}_core_bwd", _f_ffn_core, grid, _seq_halves(u, t, CT), core_par,
        [([dhid], (t, CT), lambda j, b: (b, j))], [BF16, BF16], carry=carry)
    dh = _matmul(f"{tag}_up_da_g", dug, p["w_up"], mode="nt", tm=1024, tn=1024, tk=D_FF)
    dh = _matmul(f"{tag}_up_da_v", duv, p["w_up"], mode="nt", tm=1024, tn=1024, tk=D_FF, b_koff=D_FF, resid=dh)
    grads["w_up"] = jnp.concatenate([_matmul(f"{tag}_up_dw_g", h, dug, mode="tn", tm=1024, tn=768, tk=2048),
                                     _matmul(f"{tag}_up_dw_v", h, duv, mode="tn", tm=1024, tn=768, tk=2048)], axis=1)
    (dx,), (grads["norm"],) = _stage_bwd(
        f"{tag}_norm_bwd", _f_norm, (n // tr,), [_row_item(x, tr)], [_vec_item(p["norm"])],
        [([dh], (tr, D), lambda i: (i, 0))], [F32], adds={0: dy})
    return dx, grads


def _lru_fwd(x, p, t, carry=None):
    n = x.shape[0]
    tr = _tile(n, 512)
    row = lambda dt: ((n, D), dt, (tr, D), lambda i: (i, 0))
    h, xb = _stage_fwd("lru_pre", _f_lru_pre, (n // tr,), [_row_item(x, tr)],
                       [_vec_item(p["norm"]), _vec_item(p["b_out"])], [row(BF16), row(F32)])
    u = _matmul("lru_in", h, p["w_in"], tm=1024, tn=1024)
    out, = _stage_fwd("lru_core", _f_lru_core, (D // CT, n // t), _seq_halves(u, t, CT), _lru_core_params(p),
                      [((n, D), BF16, (t, CT), lambda j, b: (b, j))], carry=carry)
    y = _matmul("lru_out", out, p["w_out"], resid=xb, tm=1024, tn=1024)
    return y, (x, h, u, out)


def _lru_core_params(p):
    return [_col_item(p["b_y"], CT), _col_item(p["b_x"], CT), _col_item(p["conv_w"], CT), _col_item(p["conv_b"], CT),
            (p["gate_w"], (2, 1, CT, CT), lambda j, b: (0, j, 0, 0)), _col_item(p["gate_b"], CT), _col_item(p["lam"], CT)]


def _lru_bwd(dy, saved, p, t, carry=None):
    x, h, u, out = saved
    n = x.shape[0]
    tr = _tile(n, 512)
    grads = {}
    dout = _matmul("lru_out_da", dy, p["w_out"], mode="nt", tm=1024, tn=1024)
    grads["w_out"] = _matmul("lru_out_dw", out, dy, mode="tn", tm=1024, tn=1024)
    (duy, dux), core_grads = _stage_bwd("lru_core_bwd", _f_lru_core, (D // CT, n // t), _seq_halves(u, t, CT),
                                        _lru_core_params(p), [([dout], (t, CT), lambda j, b: (b, j))], [BF16, BF16],
                                        carry=carry)
    for k, g in zip(("b_y", "b_x", "conv_w", "conv_b", "gate_w", "gate_b", "lam"), core_grads):
        grads[k] = g
    dh = _matmul("lru_in_da_y", duy, p["w_in"], mode="nt", tm=1024, tn=1024)
    dh = _matmul("lru_in_da_x", dux, p["w_in"], mode="nt", tm=1024, tn=1024, b_koff=D, resid=dh)
    grads["w_in"] = jnp.concatenate([_matmul("lru_in_dw_y", h, duy, mode="tn", tm=1024, tn=1024),
                                     _matmul("lru_in_dw_x", h, dux, mode="tn", tm=1024, tn=1024)], axis=1)
    row = (tr, D), lambda i: (i, 0)
    (dx,), (grads["norm"], grads["b_out"]) = _stage_bwd(
        "lru_pre_bwd", _f_lru_pre, (n // tr,), [_row_item(x, tr)], [_vec_item(p["norm"]), _vec_item(p["b_out"])],
        [([dh], *row), ([dy], *row)], [F32])
    return dx, grads


_RWKV_PRE_VECS = ("w0", "a0", "k_k", "k_a", "w2", "a2", "g2")
_RWKV_POST_VECS = ("ln_w", "ln_b", "r_k")


def _rwkv_fwd(x, p, t, carry=None):
    n = x.shape[0]
    bsz = n // t
    tr = _tile(n, 512)
    ts = _tile(n, 128)
    (h,) = _stage_fwd("rwkv_norm", _f_norm, (n // tr,), [_row_item(x, tr)], [_vec_item(p["norm"])],
                      [((n, D), F32, (tr, D), lambda i: (i, 0))])
    mixed = _stage_fwd("rwkv_mix", _f_rwkv_mix, (D // CT, bsz), [_seq_item(h, t, CT)], [_col_item(p["mix"], CT)],
                       [((n, D), BF16, (t, CT), lambda j, b: (b, j))] * 6)
    xr, xk, xv, xw, xa, xg = mixed
    r = _matmul("rwkv_r", xr, p["w_r"], tm=1024, tn=1024)
    k = _matmul("rwkv_k", xk, p["w_k"], tm=1024, tn=1024)
    v = _matmul("rwkv_v", xv, p["w_v"], tm=1024, tn=1024)
    lw1 = _matmul("rwkv_w1", xw, p["w1"], tm=1024)
    la1 = _matmul("rwkv_a1", xa, p["a1"], tm=1024)
    lg1 = _matmul("rwkv_g1", xg, p["g1"], tm=1024)
    pre_ins = [_row_item(a, ts) for a in (k, lw1, la1, lg1)]
    pre_par = [_vec_item(p[q]) for q in _RWKV_PRE_VECS]
    row = ((n, D), F32, (ts, D), lambda i: (i, 0))
    lw, kf, aa, bb, g = _stage_fwd("rwkv_pre", _f_rwkv_pre, (n // ts,), pre_ins, pre_par, [row] * 5)
    y, zs = _rwkv_scan_fwd([r, lw, kf, v, aa, bb], t, carry=carry)
    post_ins = [_row_item(a, ts) for a in (y, r, kf, v, g)]
    post_par = [_vec_item(p[q]) for q in _RWKV_POST_VECS]
    (yg,) = _stage_fwd("rwkv_post", _f_rwkv_post, (n // ts,), post_ins, post_par,
                       [((n, D), BF16, (ts, D), lambda i: (i, 0))])
    out = _matmul("rwkv_out", yg, p["w_out"], resid=x, tm=1024, tn=1024)
    return out, (x, h, mixed, r, k, v, lw1, la1, lg1, lw, kf, aa, bb, g, zs, y, yg)


def _rwkv_bwd(dout, saved, p, t, carry=None):
    x, h, mixed, r, k, v, lw1, la1, lg1, lw, kf, aa, bb, g, zs, y, yg = saved
    xr, xk, xv, xw, xa, xg = mixed
    n = x.shape[0]
    bsz = n // t
    tr = _tile(n, 512)
    ts = _tile(n, 128)
    grads = {}
    row_s = (ts, D), lambda i: (i, 0)
    dyg = _matmul("rwkv_out_da", dout, p["w_out"], mode="nt", tm=1024, tn=1024)
    grads["w_out"] = _matmul("rwkv_out_dw", yg, dout, mode="tn", tm=1024, tn=1024)
    post_ins = [_row_item(a, ts) for a in (y, r, kf, v, g)]
    post_par = [_vec_item(p[q]) for q in _RWKV_POST_VECS]
    (dy, dr_a, dkf_a, dv_a, dg), post_grads = _stage_bwd(
        "rwkv_post_bwd", _f_rwkv_post, (n // ts,), post_ins, post_par, [([dyg], *row_s)], [F32] * 5)
    grads.update(zip(_RWKV_POST_VECS, post_grads))
    dr_b, dlw, dkf_b, dv_b, daa, dbb = _rwkv_scan_bwd([r, lw, kf, v, aa, bb], zs, dy, t, carry=carry)
    dr = _add("rwkv_dr_sum", dr_a, dr_b)
    dv = _add("rwkv_dv_sum", dv_a, dv_b)
    pre_ins = [_row_item(a, ts) for a in (k, lw1, la1, lg1)]
    pre_par = [_vec_item(p[q]) for q in _RWKV_PRE_VECS]
    (dk, dlw1, dla1, dlg1), pre_grads = _stage_bwd(
        "rwkv_pre_bwd", _f_rwkv_pre, (n // ts,), pre_ins, pre_par,
        [([dlw], *row_s), ([dkf_a, dkf_b], *row_s), ([daa], *row_s), ([dbb], *row_s), ([dg], *row_s)], [F32] * 4)
    grads.update(zip(_RWKV_PRE_VECS, pre_grads))
    dmixed = []
    for tag, xin, dz, wname in (("r", xr, dr, "w_r"), ("k", xk, dk, "w_k"), ("v", xv, dv, "w_v"),
                                ("w1", xw, dlw1, "w1"), ("a1", xa, dla1, "a1"), ("g1", xg, dlg1, "g1")):
        dmixed.append(_matmul(f"rwkv_{tag}_da", dz, p[wname], mode="nt", tm=1024, tn=1024))
        grads[wname] = _matmul(f"rwkv_{tag}_dw", xin, dz, mode="tn", tm=1024, tn=1024)
    seq_blk = (t, CT), lambda j, b: (b, j)
    (dh,), (grads["mix"],) = _stage_bwd(
        "rwkv_mix_bwd", _f_rwkv_mix, (D // CT, bsz), [_seq_item(h, t, CT)], [_col_item(p["mix"], CT)],
        [([a], *seq_blk) for a in dmixed], [F32])
    (dx,), (grads["norm"],) = _stage_bwd(
        "rwkv_norm_bwd", _f_norm, (n // tr,), [_row_item(x, tr)], [_vec_item(p["norm"])],
        [([dh], (tr, D), lambda i: (i, 0))], [F32], adds={0: dout})
    return dx, grads


def _block_diag_gates(gate_w):
    z = jnp.zeros((2, D // CT, HEAD, HEAD), gate_w.dtype)
    even, odd = gate_w[:, 0::2], gate_w[:, 1::2]
    top = jnp.concatenate([even, z], axis=-1)
    bot = jnp.concatenate([z, odd], axis=-1)
    return jnp.concatenate([top, bot], axis=-2)


def _gate_blocks_grad(dg):
    even, odd = dg[:, :, :HEAD, :HEAD], dg[:, :, HEAD:, HEAD:]
    return jnp.stack([even, odd], axis=2).reshape(2, N_HEAD, HEAD, HEAD)


def _step(local, x, target):
    bsz, t, _ = x.shape
    n = bsz * t

    def block(name, pre=""):
        base, _, layer = name.partition("@")
        a = local[pre + base]
        return a[int(layer):int(layer) + 1] if layer else a

    def gather_pack(group):
        flat = [block(k).astype(BF16).reshape(-1) if _base(k) in GATHER_BF16
                else lax.bitcast_convert_type(block(k), BF16).reshape(-1) for k in group]
        return _pack(flat, BF16)

    def gathered(group, buf):
        sizes = [(1 if _base(k) in GATHER_BF16 else 2) * math.prod(block(k).shape) for k in group]
        out = {}
        for k, g in zip(group, _unpack(buf, [(s,) for s in sizes], (N_DEV,))):
            if _base(k) not in GATHER_BF16:
                g = lax.bitcast_convert_type(g.reshape(N_DEV, -1, 2), F32)
            out[k] = _unshard(g.reshape((N_DEV,) + block(k).shape), SHARD_AXIS[_base(k)])
        return out

    def scatter_pack(group, grads):
        return _pack([_reshard(grads[k], SHARD_AXIS[_base(k)]) for k in group], F32, (N_DEV,))

    def ffn_params(l, full):
        return dict(norm=local["ffn_norm"][l:l + 1], w_up=full[f"ffn_w_up@{l}"][0], conv_w=full[f"ffn_conv_w@{l}"][0],
                    conv_b=local["ffn_conv_b"][l:l + 1], w_down=full[f"ffn_w_down@{l}"][0])

    def ffn_grads(l, g):
        return {f"ffn_w_up@{l}": g["w_up"][None], f"ffn_w_down@{l}": g["w_down"][None], f"ffn_conv_w@{l}": g["conv_w"][None]}

    (landed,) = _exchange("gather_lru", [(gather_pack(GROUPS["lru"]), False)])
    full = gathered(GROUPS["lru"], landed)
    lru = dict(norm=local["lru_norm"], b_out=local["lru_b_out"], w_in=full["lru_w_in"][0],
               b_y=local["lru_b_in"][:, :D], b_x=local["lru_b_in"][:, D:], conv_w=full["lru_conv_w"][0],
               conv_b=local["lru_conv_b"], gate_w=_block_diag_gates(local["lru_gate_w"][0]),
               gate_b=local["lru_gate_b"][0].reshape(2, D), lam=local["lru_lambda"], w_out=full["lru_w_out"][0])
    x0 = x.reshape(n, D)
    ride = _Carry([(gather_pack(GROUPS["ffn0"]), False)])
    x1, s_lru = _lru_fwd(x0, lru, t, carry=ride)
    ffn0 = ffn_params(0, gathered(GROUPS["ffn0"], ride.landed[0]))
    ride = _Carry([(gather_pack(GROUPS["rwkv"]), False)])
    x2, s_ffn0 = _ffn_fwd("ffn0", x1, ffn0, t, carry=ride)
    full = gathered(GROUPS["rwkv"], ride.landed[0])
    rwkv = dict(norm=full["rwkv_norm"], mix=full["rwkv_mix"][0], w_r=full["rwkv_w_rkv"][0, 0],
                w_k=full["rwkv_w_rkv"][0, 1], w_v=full["rwkv_w_rkv"][0, 2], w0=full["rwkv_w0"], w1=full["rwkv_w1"][0],
                w2=full["rwkv_w2"][0], a0=full["rwkv_a0"], a1=full["rwkv_a1"][0], a2=full["rwkv_a2"][0],
                g1=full["rwkv_g1"][0], g2=full["rwkv_g2"][0], k_k=full["rwkv_k_k"], k_a=full["rwkv_k_a"],
                r_k=local["rwkv_r_k"].reshape(1, D), ln_w=full["rwkv_ln_w"], ln_b=full["rwkv_ln_b"],
                w_out=full["rwkv_w_out"][0])
    ride = _Carry([(gather_pack(GROUPS["ffn1"]), False)])
    x3, s_rwkv = _rwkv_fwd(x2, rwkv, t, carry=ride)
    ffn1 = ffn_params(1, gathered(GROUPS["ffn1"], ride.landed[0]))
    x4, s_ffn1 = _ffn_fwd("ffn1", x3, ffn1, t)
    loss, dx4, d_final = _loss_head(x4, target.reshape(n, D), local["final_norm"].reshape(1, D))

    parts = {}
    dx3, g_ffn1 = _ffn_bwd("ffn1", dx4, s_ffn1, ffn1, t)
    ride = _Carry([(scatter_pack(GROUPS["ffn1"], ffn_grads(1, g_ffn1)), True)])
    dx2, g_rwkv = _rwkv_bwd(dx3, s_rwkv, rwkv, t, carry=ride)
    parts["ffn1"] = ride.landed[0]
    rwkv_grads = {
        "rwkv_norm": g_rwkv["norm"], "rwkv_mix": g_rwkv["mix"][None],
        "rwkv_w_rkv": jnp.stack([g_rwkv["w_r"], g_rwkv["w_k"], g_rwkv["w_v"]])[None],
        "rwkv_w0": g_rwkv["w0"], "rwkv_w1": g_rwkv["w1"][None], "rwkv_w2": g_rwkv["w2"][None],
        "rwkv_a0": g_rwkv["a0"], "rwkv_a1": g_rwkv["a1"][None], "rwkv_a2": g_rwkv["a2"][None],
        "rwkv_g1": g_rwkv["g1"][None], "rwkv_g2": g_rwkv["g2"][None], "rwkv_k_k": g_rwkv["k_k"],
        "rwkv_k_a": g_rwkv["k_a"], "rwkv_ln_w": g_rwkv["ln_w"], "rwkv_ln_b": g_rwkv["ln_b"],
        "rwkv_w_out": g_rwkv["w_out"][None],
    }
    ride = _Carry([(scatter_pack(GROUPS["rwkv"], rwkv_grads), True)])
    dx1, g_ffn0 = _ffn_bwd("ffn0", dx2, s_ffn0, ffn0, t, carry=ride)
    parts["rwkv"] = ride.landed[0]
    ride = _Carry([(scatter_pack(GROUPS["ffn0"], ffn_grads(0, g_ffn0)), True)])
    dx0, g_lru = _lru_bwd(dx1, s_lru, lru, t, carry=ride)
    parts["ffn0"] = ride.landed[0]

    g_ffn = [g_ffn0, g_ffn1]
    lru_grads = {"lru_w_in": g_lru["w_in"][None], "lru_conv_w": g_lru["conv_w"][None], "lru_w_out": g_lru["w_out"][None]}
    gfull = {
        "lru_norm": g_lru["norm"], "lru_b_in": jnp.concatenate([g_lru["b_y"], g_lru["b_x"]], axis=1),
        "lru_conv_b": g_lru["conv_b"], "lru_gate_w": _gate_blocks_grad(g_lru["gate_w"])[None],
        "lru_gate_b": g_lru["gate_b"].reshape(1, 2, N_HEAD, HEAD), "lru_lambda": g_lru["lam"],
        "lru_b_out": g_lru["b_out"], "rwkv_r_k": g_rwkv["r_k"].reshape(1, N_HEAD, HEAD),
        "ffn_norm": jnp.concatenate([g["norm"] for g in g_ffn]),
        "ffn_conv_b": jnp.concatenate([g["conv_b"] for g in g_ffn]),
        "final_norm": d_final.reshape(D),
    }
    parts["lru"], parts["replicated"] = _exchange(
        "grad_tail", [(scatter_pack(GROUPS["lru"], lru_grads), True), (_pack([gfull[k] for k in REPLICATED], F32), False)])

    out = {}
    for tag, names in (*GROUPS.items(), ("replicated", REPLICATED)):
        packs = [_pack([block(k, pre) for k in names], F32) for pre in ("", "m_", "v_")]
        res = _adamw(f"adamw_{tag}", parts[tag], *packs)
        for kind, buf in zip(("grad", "delta", "new_m", "new_v"), res):
            for k, a in zip(names, _unpack(buf, [block(k).shape for k in names])):
                out[(kind, k)] = a
    for kind in ("grad", "delta", "new_m", "new_v"):
        for k in STACKED:
            out[(kind, k)] = jnp.concatenate([out[(kind, f"{k}@0")], out[(kind, f"{k}@1")]], axis=0)
    return loss, dx0.reshape(x.shape), out


def kernel(x, lru_norm, lru_w_in, lru_b_in, lru_conv_w, lru_conv_b, lru_gate_w, lru_gate_b, lru_lambda, lru_w_out, lru_b_out, rwkv_norm, rwkv_mix, rwkv_w_rkv, rwkv_w0, rwkv_w1, rwkv_w2, rwkv_a0, rwkv_a1, rwkv_a2, rwkv_g1, rwkv_g2, rwkv_k_k, rwkv_k_a, rwkv_r_k, rwkv_ln_w, rwkv_ln_b, rwkv_w_out, ffn_norm, ffn_w_up, ffn_conv_w, ffn_conv_b, ffn_w_down, final_norm, loss_target, m_lru_norm, m_lru_w_in, m_lru_b_in, m_lru_conv_w, m_lru_conv_b, m_lru_gate_w, m_lru_gate_b, m_lru_lambda, m_lru_w_out, m_lru_b_out, m_rwkv_norm, m_rwkv_mix, m_rwkv_w_rkv, m_rwkv_w0, m_rwkv_w1, m_rwkv_w2, m_rwkv_a0, m_rwkv_a1, m_rwkv_a2, m_rwkv_g1, m_rwkv_g2, m_rwkv_k_k, m_rwkv_k_a, m_rwkv_r_k, m_rwkv_ln_w, m_rwkv_ln_b, m_rwkv_w_out, m_ffn_norm, m_ffn_w_up, m_ffn_conv_w, m_ffn_conv_b, m_ffn_w_down, m_final_norm, v_lru_norm, v_lru_w_in, v_lru_b_in, v_lru_conv_w, v_lru_conv_b, v_lru_gate_w, v_lru_gate_b, v_lru_lambda, v_lru_w_out, v_lru_b_out, v_rwkv_norm, v_rwkv_mix, v_rwkv_w_rkv, v_rwkv_w0, v_rwkv_w1, v_rwkv_w2, v_rwkv_a0, v_rwkv_a1, v_rwkv_a2, v_rwkv_g1, v_rwkv_g2, v_rwkv_k_k, v_rwkv_k_a, v_rwkv_r_k, v_rwkv_ln_w, v_rwkv_ln_b, v_rwkv_w_out, v_ffn_norm, v_ffn_w_up, v_ffn_conv_w, v_ffn_conv_b, v_ffn_w_down, v_final_norm):
    args = locals()
    local = {(pre + k): args[pre + k] for pre in ("", "m_", "v_") for k in WEIGHTS}
    loss_local, grad_x, out = _step(local, x, loss_target)
    loss = lax.psum(loss_local, ("x", "y", "c"))
    return (loss, grad_x, *[out[(kind, k)] for kind in ("grad", "delta", "new_m", "new_v") for k in WEIGHTS])
```

```python
import functools
import math

import jax
import jax.numpy as jnp
from jax import lax
from jax.experimental import pallas as pl
from jax.experimental.pallas import tpu as pltpu

F32 = jnp.float32
BF16 = jnp.bfloat16
HI = lax.Precision.HIGHEST

N_DEV = 8
D = 1024
HEAD = 64
N_HEAD = D // HEAD
D_FF = 3 * D
LANE = 128
V7X_VMEM_BYTES = 64 * 1024 * 1024
VMEM_LIMIT = V7X_VMEM_BYTES - 8 * 1024 * 1024
CT = LANE
CHUNK = 64
SCAN_HEADS = 16
PACK_W = 1024
PACK_ROWS = 256

ADAM_LR, ADAM_B1, ADAM_B2, ADAM_EPS, ADAM_WD, ADAM_STEP = 0.001, 0.9, 0.999, 1e-08, 0.01, 10
RMS_EPS = 1e-6
GN_EPS = 64e-5
LRU_C = 8.0

WEIGHTS = ['lru_norm', 'lru_w_in', 'lru_b_in', 'lru_conv_w', 'lru_conv_b', 'lru_gate_w', 'lru_gate_b',
           'lru_lambda', 'lru_w_out', 'lru_b_out', 'rwkv_norm', 'rwkv_mix', 'rwkv_w_rkv', 'rwkv_w0', 'rwkv_w1',
           'rwkv_w2', 'rwkv_a0', 'rwkv_a1', 'rwkv_a2', 'rwkv_g1', 'rwkv_g2', 'rwkv_k_k', 'rwkv_k_a', 'rwkv_r_k',
           'rwkv_ln_w', 'rwkv_ln_b', 'rwkv_w_out', 'ffn_norm', 'ffn_w_up', 'ffn_conv_w', 'ffn_conv_b',
           'ffn_w_down', 'final_norm']
SHARD_AXIS = {'lru_w_in': 2, 'lru_conv_w': 2, 'lru_w_out': 1, 'rwkv_norm': 1, 'rwkv_mix': 2, 'rwkv_w_rkv': 2,
              'rwkv_w0': 1, 'rwkv_w1': 1, 'rwkv_w2': 2, 'rwkv_a0': 1, 'rwkv_a1': 1, 'rwkv_a2': 2, 'rwkv_g1': 1,
              'rwkv_g2': 2, 'rwkv_k_k': 1, 'rwkv_k_a': 1, 'rwkv_ln_w': 1, 'rwkv_ln_b': 1, 'rwkv_w_out': 1,
              'ffn_w_up': 2, 'ffn_conv_w': 2, 'ffn_w_down': 1}
GATHER_BF16 = ['lru_w_in', 'lru_w_out', 'rwkv_w_rkv', 'rwkv_w1', 'rwkv_a1', 'rwkv_g1', 'rwkv_w_out', 'ffn_w_up',
               'ffn_w_down']
REPLICATED = [n for n in WEIGHTS if n not in SHARD_AXIS]
STACKED = ['ffn_w_up', 'ffn_w_down', 'ffn_conv_w']
GROUPS = {
    "lru": ['lru_w_in', 'lru_w_out', 'lru_conv_w'],
    "ffn0": [f"{n}@0" for n in STACKED],
    "rwkv": [n for n in WEIGHTS if n.startswith("rwkv_") and n in SHARD_AXIS],
    "ffn1": [f"{n}@1" for n in STACKED],
}


def _base(name):
    return name.partition("@")[0]


def _pcall(body, **kw):
    return pl.pallas_call(body, **kw)


class _Carry:
    def __init__(self, items):
        self.items = list(items)
        self.landed = None


def _exchange_copies(src_ref, dst_ref, send_sems, recv_sems, local_sem, scatter):
    x, y, c = lax.axis_index("x"), lax.axis_index("y"), lax.axis_index("c")
    me = 4 * x + 2 * y + c
    mine = pltpu.make_async_copy(src_ref.at[me] if scatter else src_ref, dst_ref.at[me], local_sem)
    copies = []
    for m in range(1, N_DEV):
        px = 1 - x if m & 4 else x
        py = 1 - y if m & 2 else y
        pc = 1 - c if m & 1 else c
        part = src_ref.at[4 * px + 2 * py + pc] if scatter else src_ref
        copies.append(pltpu.make_async_remote_copy(
            src_ref=part, dst_ref=dst_ref.at[me], send_sem=send_sems.at[m - 1], recv_sem=recv_sems.at[m - 1],
            device_id=(px, py, pc), device_id_type=pl.DeviceIdType.MESH))
    return mine, copies


def _start_exchange(mine, copies):
    mine.start()
    for cp in copies:
        cp.start()


def _wait_exchange(mine, copies):
    for cp in copies:
        cp.wait_recv()
    for cp in copies:
        cp.wait_send()
    mine.wait()


_EXCHANGE_SEMS = [pltpu.SemaphoreType.DMA((N_DEV - 1,)), pltpu.SemaphoreType.DMA((N_DEV - 1,)), pltpu.SemaphoreType.DMA]


def _landing_shape(src, scatter):
    return jax.ShapeDtypeStruct((N_DEV,) + tuple(src.shape[1:] if scatter else src.shape), src.dtype)


def _call(body, operands, *, carry=None, name, grid, in_specs, out_specs, out_shape, scratch_shapes=(),
          compiler_params=None):
    if carry is None:
        return _pcall(body, name=name, grid=grid, in_specs=in_specs, out_specs=out_specs, out_shape=out_shape,
                      scratch_shapes=list(scratch_shapes), compiler_params=compiler_params)(*operands)
    single = not isinstance(out_specs, (list, tuple))
    out_specs_l = [out_specs] if single else list(out_specs)
    out_shape_l = [out_shape] if single else list(out_shape)
    n_in, n_out, n_scr, n_x = len(in_specs), len(out_specs_l), len(scratch_shapes), len(carry.items)
    flags = [sc for (_, sc) in carry.items]

    def wrapped(*refs):
        ins, refs = refs[:n_in], refs[n_in:]
        xsrc, refs = refs[:n_x], refs[n_x:]
        outs, refs = refs[:n_out], refs[n_out:]
        xdst, refs = refs[:n_x], refs[n_x:]
        scr, sems = refs[:n_scr], refs[n_scr:]
        first = functools.reduce(jnp.logical_and, [pl.program_id(i) == 0 for i in range(len(grid))])
        last = functools.reduce(jnp.logical_and, [pl.program_id(i) == grid[i] - 1 for i in range(len(grid))])

        def exchanges():
            return [_exchange_copies(xsrc[i], xdst[i], *sems[3 * i:3 * i + 3], flags[i]) for i in range(n_x)]

        @pl.when(first)
        def _():
            for mine, copies in exchanges():
                _start_exchange(mine, copies)

        body(*ins, *outs, *scr)

        @pl.when(last)
        def _():
            for mine, copies in exchanges():
                _wait_exchange(mine, copies)

    hbm = pl.BlockSpec(memory_space=pl.ANY)
    res = _pcall(
        wrapped, name=name, grid=grid, in_specs=list(in_specs) + [hbm] * n_x,
        out_specs=out_specs_l + [hbm] * n_x,
        out_shape=out_shape_l + [_landing_shape(a, sc) for (a, sc) in carry.items],
        scratch_shapes=list(scratch_shapes) + _EXCHANGE_SEMS * n_x, compiler_params=compiler_params,
    )(*operands, *[a for (a, _) in carry.items])
    carry.landed = list(res[n_out:])
    return res[0] if single else list(res[:n_out])


def _params(n_grid):
    return pltpu.CompilerParams(dimension_semantics=("arbitrary",) * n_grid, vmem_limit_bytes=VMEM_LIMIT)


def _shift_rows(x, d, up):
    n = x.shape[0]
    idx = lax.broadcasted_iota(jnp.int32, x.shape, 0)
    if up:
        return jnp.where(idx < n - d, pltpu.roll(x, n - d, 0), 0.0)
    return jnp.where(idx >= d, pltpu.roll(x, d, 0), 0.0)


@functools.partial(jax.custom_vjp, nondiff_argnums=(1,))
def _shift_down(x, d):
    return _shift_rows(x, d, False)


def _shift_down_fwd(x, d):
    return _shift_rows(x, d, False), None


def _shift_down_bwd(d, _, g):
    return (_shift_rows(g, d, True),)


_shift_down.defvjp(_shift_down_fwd, _shift_down_bwd)


def _scan_doubling(a, b, up):
    n = a.shape[0]
    d = 1
    while d < n:
        b = b + a * _shift_rows(b, d, up)
        a = a * _shift_rows(a, d, up)
        d *= 2
    return b


@jax.custom_vjp
def _linear_scan(a, b):
    return _scan_doubling(a, b, False)


def _linear_scan_fwd(a, b):
    h = _scan_doubling(a, b, False)
    return h, (a, h)


def _linear_scan_bwd(res, dh):
    a, h = res
    g = _scan_doubling(_shift_rows(a, 1, True), dh, True)
    return g * _shift_rows(h, 1, False), g


_linear_scan.defvjp(_linear_scan_fwd, _linear_scan_bwd)


def _causal_conv(x, w, b):
    k = w.shape[0]
    out = b + x * w[k - 1:k]
    for j in range(k - 1):
        out = out + _shift_down(x, k - 1 - j) * w[j:j + 1]
    return out


_GELU_C1 = math.sqrt(2.0 / math.pi)
_GELU_C2 = 0.044715 * _GELU_C1


@jax.custom_vjp
def _gelu(x):
    return 0.5 * x * (1.0 + jnp.tanh(x * (_GELU_C1 + _GELU_C2 * (x * x))))


def _gelu_fwd(x):
    x2 = x * x
    t = jnp.tanh(x * (_GELU_C1 + _GELU_C2 * x2))
    return 0.5 * x * (1.0 + t), (x, x2, t)


def _gelu_bwd(res, g):
    x, x2, t = res
    return (g * (0.5 * (1.0 + t) + (0.5 * x) * (1.0 - t * t) * (_GELU_C1 + (3.0 * _GELU_C2) * x2)),)


_gelu.defvjp(_gelu_fwd, _gelu_bwd)


def _rmsnorm(x, g):
    return x * lax.rsqrt(jnp.mean(x * x, axis=-1, keepdims=True) + RMS_EPS) * g


def _neg_expm1(x):
    series = x * (1.0 + x * 0.5 * (1.0 + x * (1.0 / 3.0) * (1.0 + x * 0.25 * (1.0 + x * 0.2))))
    return -jnp.where(x > -0.1, series, jnp.exp(x) - 1.0)


def _dot(a, b, ca=1, cb=0, precision=None):
    return lax.dot_general(a, b, (((ca,), (cb,)), ((), ())), precision=precision, preferred_element_type=F32)


def _bdot(a, b):
    return _dot(a.astype(BF16), b.astype(BF16))


def _split_bf16(x):
    hi = x.astype(BF16)
    return hi, (x - hi.astype(F32)).astype(BF16)


def _head_sum_impl(x):
    row = lax.broadcasted_iota(jnp.int32, (D, LANE), 0)
    col = lax.broadcasted_iota(jnp.int32, (D, LANE), 1)
    e = (lax.shift_right_logical(row, 6) == col).astype(BF16)
    hi, lo = _split_bf16(x)
    s_hi, s_lo = _split_bf16(_dot(hi, e) + _dot(lo, e))
    return _dot(s_hi, e, 1, 1) + _dot(s_lo, e, 1, 1)


@jax.custom_vjp
def _head_sum(x):
    return _head_sum_impl(x)


def _head_sum_fwd(x):
    return _head_sum_impl(x), None


def _head_sum_bwd(_, g):
    return (_head_sum(g),)


_head_sum.defvjp(_head_sum_fwd, _head_sum_bwd)


def _specs(items):
    return [pl.BlockSpec(it[1], it[2]) for it in items]


def _stage_fwd(name, f, grid, ins, params, outs, carry=None):
    n_in = len(ins) + len(params)

    def body(*refs):
        res = f(*[r[...] for r in refs[:n_in]])
        for o, v in zip(refs[n_in:], res):
            o[...] = v.astype(o.dtype)

    return _call(
        body, [it[0] for it in ins + params], carry=carry, name=name, grid=grid, in_specs=_specs(ins + params),
        out_specs=[pl.BlockSpec(bs, im) for (_, _, bs, im) in outs],
        out_shape=[jax.ShapeDtypeStruct(s, dt) for (s, dt, _, _) in outs],
        compiler_params=_params(len(grid)),
    )


def _stage_bwd(name, f, grid, ins, params, douts, din_dtypes, adds=None, carry=None):
    adds = adds or {}
    n_in, n_par = len(ins), len(params)
    dout_items = [(a, bs, im) for (arrs, bs, im) in douts for a in arrs]
    add_items = [(adds[i], ins[i][1], ins[i][2]) for i in sorted(adds)]
    n_do, n_add = len(dout_items), len(add_items)

    def body(*refs):
        vals = [r[...] for r in refs[:n_in + n_par]]
        do_refs = list(refs[n_in + n_par:n_in + n_par + n_do])
        add_refs = dict(zip(sorted(adds), refs[n_in + n_par + n_do:n_in + n_par + n_do + n_add]))
        din_refs = refs[n_in + n_par + n_do + n_add:n_in + n_par + n_do + n_add + n_in]
        dpar_refs = refs[n_in + n_par + n_do + n_add + n_in:]
        cts = []
        for (arrs, _, _) in douts:
            ct = do_refs.pop(0)[...].astype(F32)
            for _ in arrs[1:]:
                ct = ct + do_refs.pop(0)[...].astype(F32)
            cts.append(ct)
        _, vjp = jax.vjp(f, *vals)
        grads = vjp(tuple(cts))
        for i, r in enumerate(din_refs):
            g = grads[i]
            if i in add_refs:
                g = g + add_refs[i][...]
            r[...] = g.astype(r.dtype)

        @pl.when(pl.program_id(len(grid) - 1) == 0)
        def _():
            for r in dpar_refs:
                r[...] = jnp.zeros(r.shape, r.dtype)

        for j, r in enumerate(dpar_refs):
            r[...] += grads[n_in + j]

    din_shapes = [it[3][0] if len(it) > 3 else it[0].shape for it in ins]
    din_specs = [pl.BlockSpec(it[1], it[3][1] if len(it) > 3 else it[2]) for it in ins]
    res = _call(
        body, [it[0] for it in ins + params + dout_items + add_items], carry=carry, name=name, grid=grid,
        in_specs=_specs(ins + params + dout_items + add_items), out_specs=din_specs + _specs(params),
        out_shape=[jax.ShapeDtypeStruct(s, dt) for s, dt in zip(din_shapes, din_dtypes)]
        + [jax.ShapeDtypeStruct(it[0].shape, F32) for it in params],
        compiler_params=_params(len(grid)),
    )
    return list(res[:n_in]), list(res[n_in:])


def _tile(n, want):
    t = min(n, want)
    while n % t:
        t //= 2
    return t


def _matmul(name, a, b, *, mode="nn", resid=None, out_dtype=F32, tm=512, tn=512, tk=1024, b_koff=0, carry=None):
    if mode == "tn":
        (kdim, m), n = a.shape, b.shape[1]
    else:
        (m, kdim), n = a.shape, (b.shape[1] if mode == "nn" else b.shape[0])
    tm, tn, tk = _tile(m, tm), _tile(n, tn), _tile(kdim, tk)
    nk = kdim // tk
    ko = b_koff // tk
    assert ko * tk == b_koff
    a_spec = pl.BlockSpec((tk, tm), lambda i, j, k: (k, i)) if mode == "tn" else pl.BlockSpec((tm, tk), lambda i, j, k: (i, k))
    b_spec = pl.BlockSpec((tn, tk), lambda i, j, k: (j, k + ko)) if mode == "nt" else pl.BlockSpec((tk, tn), lambda i, j, k: (k + ko, j))
    ca = 0 if mode == "tn" else 1
    cb = 1 if mode == "nt" else 0
    operands = [a, b]
    in_specs = [a_spec, b_spec]
    if resid is not None:
        operands.append(resid)
        in_specs.append(pl.BlockSpec((tm, tn), lambda i, j, k: (i, j)))

    def finish(r, refs, o_ref):
        if resid is not None:
            r = r + refs[2][...]
        o_ref[...] = r.astype(o_ref.dtype)

    def body_one_step(*refs):
        finish(_dot(refs[0][...].astype(BF16), refs[1][...].astype(BF16), ca, cb), refs, refs[-1])

    def body(*refs):
        a_ref, b_ref = refs[0], refs[1]
        o_ref, acc_ref = refs[-2], refs[-1]
        k = pl.program_id(2)

        @pl.when(k == 0)
        def _():
            acc_ref[...] = jnp.zeros(acc_ref.shape, F32)

        acc_ref[...] += _dot(a_ref[...].astype(BF16), b_ref[...].astype(BF16), ca, cb)

        @pl.when(k == nk - 1)
        def _():
            finish(acc_ref[...], refs, o_ref)

    return _call(
        body_one_step if nk == 1 else body, operands, carry=carry, name=name, grid=(m // tm, n // tn, nk),
        in_specs=in_specs, out_specs=pl.BlockSpec((tm, tn), lambda i, j, k: (i, j)),
        out_shape=jax.ShapeDtypeStruct((m, n), out_dtype),
        scratch_shapes=[] if nk == 1 else [pltpu.VMEM((tm, tn), F32)],
        compiler_params=_params(3),
    )


def _add(name, a, b):
    rows, cols = a.shape
    tr = _tile(rows, 512)

    def body(a_ref, b_ref, o_ref):
        o_ref[...] = a_ref[...] + b_ref[...]

    spec = pl.BlockSpec((tr, cols), lambda i: (i, 0))
    return _pcall(body, name=name, grid=(rows // tr,), in_specs=[spec, spec], out_specs=spec,
                  out_shape=jax.ShapeDtypeStruct(a.shape, a.dtype), compiler_params=_params(1))(a, b)


def _f_lru_pre(x, norm, b_out):
    return _rmsnorm(x, norm), x + b_out


def _f_lru_core(uy, ux, b_y, b_x, cw, cb, gw, gb, lam):
    yb = _gelu(uy + b_y)
    xr = _causal_conv(ux + b_x, cw, cb)
    gr = jax.nn.sigmoid(_bdot(xr, gw[0, 0]) + gb[0:1])
    gi = jax.nn.sigmoid(_bdot(xr, gw[1, 0]) + gb[1:2])
    log_a = -LRU_C * gr * jax.nn.softplus(-lam)
    a = jnp.exp(log_a)
    bterm = jnp.sqrt(_neg_expm1(2.0 * log_a)) * (gi * xr)
    return (_linear_scan(a, bterm) * yb,)


def _f_norm(x, norm):
    return (_rmsnorm(x, norm),)


def _f_ffn_core(ug, uv, cw, cb):
    return (_gelu(_causal_conv(ug, cw, cb)) * uv,)


def _f_rwkv_mix(h, mix):
    xx = _shift_down(h, 1) - h
    return tuple(h + xx * mix[i:i + 1] for i in range(6))


def _f_rwkv_pre(k, lw1, la1, lg1, w0, a0, k_k, k_a, w2, a2, g2):
    wpre = w0 + _bdot(jnp.tanh(lw1), w2)
    apre = a0 + _bdot(la1, a2)
    g = _bdot(jax.nn.sigmoid(lg1), g2)
    log_decay = -jnp.exp(-jax.nn.softplus(-wpre) - 0.5)
    a = jax.nn.sigmoid(apre)
    kk = k * k_k
    kk = kk / jnp.maximum(jnp.sqrt(_head_sum(kk * kk)), 1e-12)
    kf = k * (1.0 + (a - 1.0) * k_a)
    return log_decay, kf, -kk, kk * a, g


def _f_rwkv_post(y, r, kf, v, g, ln_w, ln_b, r_k):
    inv = 1.0 / HEAD
    yc = y - _head_sum(y) * inv
    var = _head_sum(yc * yc) * inv
    yn = yc * lax.rsqrt(var + GN_EPS) * ln_w + ln_b
    bonus = _head_sum(r * kf * r_k) * v
    return ((yn + bonus) * g,)


def _hdot(a, b, ca, cb):
    return lax.dot_general(a, b, (((ca,), (cb,)), ((0,), (0,))), preferred_element_type=F32)


def _hmm_impl(a, b, ca, cb):
    return _hdot(a.astype(BF16), b.astype(BF16), ca, cb)


@functools.partial(jax.custom_vjp, nondiff_argnums=(2, 3))
def _hmm(a, b, ca, cb):
    return _hmm_impl(a, b, ca, cb)


def _hmm_fwd(a, b, ca, cb):
    return _hmm_impl(a, b, ca, cb), (a, b)


def _hmm_bwd(ca, cb, res, dc):
    a, b = res
    fa, fb = 3 - ca, 3 - cb
    da = _hmm(dc, b, 2, fb) if ca == 2 else _hmm(b, dc, fb, 2)
    db = _hmm(a, dc, fa, 1) if cb == 1 else _hmm(dc, a, 1, fa)
    return da, db


_hmm.defvjp(_hmm_fwd, _hmm_bwd)


def _tri_dot_impl(x, upper):
    g, n, _ = x.shape
    row = lax.broadcasted_iota(jnp.int32, (g, n, 3 * n), 1)
    col = lax.broadcasted_iota(jnp.int32, (g, n, 3 * n), 2)
    col = jnp.where(col >= 2 * n, col - 2 * n, jnp.where(col >= n, col - n, col))
    tri = (row <= col if upper else row >= col).astype(BF16)
    hi = x.astype(BF16)
    rem = x - hi.astype(F32)
    mid = rem.astype(BF16)
    lo = (rem - mid.astype(F32)).astype(BF16)
    return _hdot(tri, jnp.concatenate([hi, mid, lo], axis=1), 2, 1)


@functools.partial(jax.custom_vjp, nondiff_argnums=(1,))
def _tri_dot(x, upper):
    return _tri_dot_impl(x, upper)


def _tri_dot_fwd(x, upper):
    return _tri_dot_impl(x, upper), None


def _tri_dot_bwd(upper, _, g):
    return (_tri_dot(g, not upper),)


_tri_dot.defvjp(_tri_dot_fwd, _tri_dot_bwd)


def _unit_lower_inverse_impl(a):
    g, n, _ = a.shape
    row = lax.broadcasted_iota(jnp.int32, (g, n, n), 1)
    col = lax.broadcasted_iota(jnp.int32, (g, n, n), 2)
    inv = (row == col).astype(F32) + a
    p = a
    for _ in range(int(math.log2(n)) - 1):
        p = _hmm(p, p, 2, 1)
        inv = inv + _hmm(inv, p, 2, 1)
    return inv


@jax.custom_vjp
def _unit_lower_inverse(a):
    return _unit_lower_inverse_impl(a)


def _unit_lower_inverse_fwd(a):
    inv = _unit_lower_inverse_impl(a)
    return inv, inv


def _unit_lower_inverse_bwd(inv, d_inv):
    return (_hmm(_hmm(inv, d_inv, 1, 1), inv, 2, 2),)


_unit_lower_inverse.defvjp(_unit_lower_inverse_fwd, _unit_lower_inverse_bwd)


def _rwkv_chunk(z0, r, lw, k, v, a, b):
    g, n, _ = r.shape
    row = lax.broadcasted_iota(jnp.int32, (g, n, n), 1)
    col = lax.broadcasted_iota(jnp.int32, (g, n, n), 2)
    incl, strict = row >= col, row > col
    cs = _tri_dot(lw, False)
    c_last = cs[:, n - 1:n]
    inv = jnp.exp(-cs)
    ar = jnp.concatenate([a * jnp.exp(cs - lw), r * jnp.exp(cs)], axis=1)
    bk = jnp.concatenate([b * inv, k * inv], axis=1)
    pair = _hmm(ar, bk, 2, 2)
    a_ab = jnp.where(strict, pair[:, :n, :n], 0.0)
    a_ak = jnp.where(strict, pair[:, :n, n:], 0.0)
    a_rbk = jnp.concatenate([jnp.where(incl, pair[:, n:, :n], 0.0), jnp.where(incl, pair[:, n:, n:], 0.0)], axis=2)
    arz = _hmm(ar, z0, 2, 1)
    u = _hmm(_unit_lower_inverse(a_ab), arz[:, :n] + _hmm(a_ak, v, 2, 1), 2, 1)
    uv = jnp.concatenate([u, v], axis=1)
    y = arz[:, n:] + _hmm(a_rbk, uv, 2, 1)
    tail = jnp.exp(c_last - cs)
    er = lax.broadcasted_iota(jnp.int32, (g, HEAD, HEAD), 1)
    ec = lax.broadcasted_iota(jnp.int32, (g, HEAD, HEAD), 2)
    decay_all = jnp.where(er == ec, jnp.exp(c_last), 0.0)
    z_l = _hmm(jnp.concatenate([b * tail, k * tail, decay_all], axis=1), jnp.concatenate([uv, z0], axis=1), 1, 1)
    return y, z_l


def _heads_in(ref, g):
    x = ref[...]
    return jnp.stack([x[:, h * HEAD:(h + 1) * HEAD] for h in range(g)], axis=0)


def _heads_out(ref, x):
    ref[...] = jnp.concatenate([x[h] for h in range(x.shape[0])], axis=1)


def _scan_geometry(n_rows, t):
    g, n = _tile(N_HEAD, SCAN_HEADS), _tile(t, CHUNK)
    return g, n, t // n, N_HEAD // g, n_rows // t


def _rwkv_scan_fwd(seqs, t, carry=None):
    rows = seqs[0].shape[0]
    g, n, nc, hpg, bsz = _scan_geometry(rows, t)

    def body(r_ref, lw_ref, k_ref, v_ref, a_ref, b_ref, y_ref, zs_ref, z_ref):
        @pl.when(pl.program_id(1) == 0)
        def _():
            z_ref[...] = jnp.zeros(z_ref.shape, F32)

        z0 = z_ref[...]
        zs_ref[:, 0] = z0
        y, z_l = _rwkv_chunk(z0, *[_heads_in(ref, g) for ref in (r_ref, lw_ref, k_ref, v_ref, a_ref, b_ref)])
        _heads_out(y_ref, y)
        z_ref[...] = z_l

    seq_spec = pl.BlockSpec((n, g * HEAD), lambda i, c: ((i // hpg) * nc + c, i % hpg))
    return _call(
        body, list(seqs), carry=carry, name="rwkv_scan_fwd", grid=(bsz * hpg, nc), in_specs=[seq_spec] * 6,
        out_specs=[seq_spec, pl.BlockSpec((g, 1, HEAD, HEAD), lambda i, c: (i, c, 0, 0))],
        out_shape=[jax.ShapeDtypeStruct((rows, D), F32), jax.ShapeDtypeStruct((bsz * N_HEAD, nc, HEAD, HEAD), F32)],
        scratch_shapes=[pltpu.VMEM((g, HEAD, HEAD), F32)],
        compiler_params=_params(2),
    )


def _rwkv_scan_bwd(seqs, zs, dy, t, carry=None):
    rows = seqs[0].shape[0]
    g, n, nc, hpg, bsz = _scan_geometry(rows, t)

    def body(r_ref, lw_ref, k_ref, v_ref, a_ref, b_ref, zs_ref, dy_ref, dr, dlw, dk, dv, da, db, dz_ref):
        @pl.when(pl.program_id(1) == 0)
        def _():
            dz_ref[...] = jnp.zeros(dz_ref.shape, F32)

        _, vjp = jax.vjp(_rwkv_chunk, zs_ref[:, 0],
                         *[_heads_in(ref, g) for ref in (r_ref, lw_ref, k_ref, v_ref, a_ref, b_ref)])
        grads = vjp((_heads_in(dy_ref, g), dz_ref[...]))
        dz_ref[...] = grads[0]
        for o, gr in zip((dr, dlw, dk, dv, da, db), grads[1:]):
            _heads_out(o, gr)

    seq_spec = pl.BlockSpec((n, g * HEAD), lambda i, c: ((i // hpg) * nc + nc - 1 - c, i % hpg))
    return _call(
        body, [*seqs, zs, dy], carry=carry, name="rwkv_scan_bwd", grid=(bsz * hpg, nc),
        in_specs=[seq_spec] * 6 + [pl.BlockSpec((g, 1, HEAD, HEAD), lambda i, c: (i, nc - 1 - c, 0, 0)), seq_spec],
        out_specs=[seq_spec] * 6,
        out_shape=[jax.ShapeDtypeStruct((rows, D), F32)] * 6,
        scratch_shapes=[pltpu.VMEM((g, HEAD, HEAD), F32)],
        compiler_params=_params(2),
    )


def _loss_head(x, target, norm):
    n = x.shape[0]
    tr = _tile(n, 256)

    def f(xv, gv, tv):
        err = _rmsnorm(xv, gv) - tv
        return 0.5 * jnp.sum(jnp.mean(err * err, axis=-1, keepdims=True), axis=0, keepdims=True)

    def body(x_ref, t_ref, g_ref, dx_ref, dg_ref, loss_ref):
        val, vjp = jax.vjp(lambda xv, gv: f(xv, gv, t_ref[...]), x_ref[...], g_ref[...])
        dx, dg = vjp(jnp.ones((1, 1), F32))
        dx_ref[...] = dx

        @pl.when(pl.program_id(0) == 0)
        def _():
            dg_ref[...] = jnp.zeros(dg_ref.shape, F32)
            loss_ref[...] = jnp.zeros(loss_ref.shape, F32)

        dg_ref[...] += dg
        loss_ref[...] += jnp.broadcast_to(val, loss_ref.shape)

    row = pl.BlockSpec((tr, D), lambda i: (i, 0))
    vec = pl.BlockSpec((1, D), lambda i: (0, 0))
    dx, dg, loss = _pcall(
        body, name="loss_head", grid=(n // tr,), in_specs=[row, row, vec],
        out_specs=[row, vec, pl.BlockSpec((8, LANE), lambda i: (0, 0))],
        out_shape=[jax.ShapeDtypeStruct((n, D), F32), jax.ShapeDtypeStruct((1, D), F32),
                   jax.ShapeDtypeStruct((8, LANE), F32)],
        compiler_params=_params(1),
    )(x, target, norm)
    return loss[0, 0], dx, dg


def _exchange(name, items):
    n_x = len(items)

    def body(*refs):
        srcs, dsts, sems = refs[:n_x], refs[n_x:2 * n_x], refs[2 * n_x:]
        started = [_exchange_copies(srcs[i], dsts[i], *sems[3 * i:3 * i + 3], items[i][1]) for i in range(n_x)]
        for mine, copies in started:
            _start_exchange(mine, copies)
        for mine, copies in started:
            _wait_exchange(mine, copies)

    hbm = pl.BlockSpec(memory_space=pl.ANY)
    return _pcall(
        body, name=name, in_specs=[hbm] * n_x, out_specs=[hbm] * n_x,
        out_shape=[_landing_shape(a, sc) for (a, sc) in items], scratch_shapes=_EXCHANGE_SEMS * n_x,
    )(*[a for (a, _) in items])


def _adamw(name, parts, w, m, v):
    rows = w.shape[0]
    tr = _tile(rows, PACK_ROWS)

    def body(p_ref, w_ref, m_ref, v_ref, g_ref, d_ref, nm_ref, nv_ref):
        g = p_ref[0]
        for k in range(1, N_DEV):
            g = g + p_ref[k]
        nm = ADAM_B1 * m_ref[...] + (1.0 - ADAM_B1) * g
        nv = ADAM_B2 * v_ref[...] + (1.0 - ADAM_B2) * jnp.square(g)
        m_hat = nm / (1.0 - ADAM_B1 ** ADAM_STEP)
        v_hat = nv / (1.0 - ADAM_B2 ** ADAM_STEP)
        g_ref[...] = g
        d_ref[...] = -ADAM_LR * (m_hat / (jnp.sqrt(v_hat) + ADAM_EPS) + ADAM_WD * w_ref[...])
        nm_ref[...] = nm
        nv_ref[...] = nv

    row = pl.BlockSpec((tr, PACK_W), lambda i: (i, 0))
    return _pcall(
        body, name=name, grid=(rows // tr,),
        in_specs=[pl.BlockSpec((N_DEV, tr, PACK_W), lambda i: (0, i, 0)), row, row, row],
        out_specs=[row] * 4, out_shape=[jax.ShapeDtypeStruct((rows, PACK_W), F32)] * 4,
        compiler_params=_params(1),
    )(parts, w, m, v)


def _pack(arrs, dtype, lead=()):
    flat = jnp.concatenate([a.astype(dtype).reshape(lead + (-1,)) for a in arrs], axis=-1)
    n = flat.shape[-1]
    quantum = PACK_W * PACK_ROWS
    total = -(-n // quantum) * quantum
    flat = jnp.pad(flat, [(0, 0)] * len(lead) + [(0, total - n)])
    return flat.reshape(lead + (total // PACK_W, PACK_W))


def _unpack(buf, shapes, lead=()):
    flat = buf.reshape(lead + (-1,))
    out, off = [], 0
    for s in shapes:
        n = math.prod(s)
        out.append(flat[..., off:off + n].reshape(lead + tuple(s)))
        off += n
    return out


def _unshard(g, axis):
    local = g.shape[1:]
    full = jnp.moveaxis(g, 0, axis)
    return full.reshape(local[:axis] + (N_DEV * local[axis],) + local[axis + 1:])


def _reshard(full, axis):
    s = full.shape
    blocked = full.reshape(s[:axis] + (N_DEV, s[axis] // N_DEV) + s[axis + 1:])
    return jnp.moveaxis(blocked, axis, 0)


def _row_item(a, tr):
    return (a, (tr, a.shape[1]), lambda i: (i, 0))


def _vec_item(a):
    return (a, a.shape, lambda i: (0,) * a.ndim)


def _seq_item(a, t, width):
    return (a, (t, width), lambda j, b: (b, j))


def _seq_halves(a, t, width):
    half = a.shape[1] // 2
    off = half // width
    grad = ((a.shape[0], half), lambda j, b: (b, j))
    return [(a, (t, width), lambda j, b: (b, j), grad), (a, (t, width), lambda j, b: (b, j + off), grad)]


def _col_item(a, width):
    return (a, (a.shape[0], width), lambda j, b: (0, j))


def _ffn_fwd(tag, x, p, t, carry=None):
    n = x.shape[0]
    tr = _tile(n, 512)
    norm_ins, norm_par = [_row_item(x, tr)], [_vec_item(p["norm"])]
    (h,) = _stage_fwd(f"{tag}_norm", _f_norm, (n // tr,), norm_ins, norm_par, [((n, D), BF16, (tr, D), lambda i: (i, 0))])
    u = _matmul(f"{tag}_up", h, p["w_up"], tm=2048, tn=768, carry=carry)
    grid = (D_FF // CT, n // t)
    core_par = [_col_item(p["conv_w"], CT), _col_item(p["conv_b"], CT)]
    (hid,) = _stage_fwd(f"{tag}_core", _f_ffn_core, grid, _seq_halves(u, t, CT), core_par,
                        [((n, D_FF), BF16, (t, CT), lambda j, b: (b, j))])
    y = _matmul(f"{tag}_down", hid, p["w_down"], resid=x, tm=1024, tn=1024, tk=D_FF)
    return y, (x, h, u, hid)


def _ffn_bwd(tag, dy, saved, p, t, carry=None):
    x, h, u, hid = saved
    n = x.shape[0]
    tr = _tile(n, 512)
    grads = {}
    dhid = _matmul(f"{tag}_down_da", dy, p["w_down"], mode="nt", tm=1024, tn=768)
    grads["w_down"] = _matmul(f"{tag}_down_dw", hid, dy, mode="tn", tm=768, tn=1024, tk=2048)
    grid = (D_FF // CT, n // t)
    core_par = [_col_item(p["conv_w"], CT), _col_item(p["conv_b"], CT)]
    (dug, duv), (grads["conv_w"], grads["conv_b"]) = _stage_bwd(
        f"{tag}_core_bwd", _f_ffn_core, grid, _seq_halves(u, t, CT), core_par,
        [([dhid], (t, CT), lambda j, b: (b, j))], [BF16, BF16], carry=carry)
    dh = _matmul(f"{tag}_up_da_g", dug, p["w_up"], mode="nt", tm=1024, tn=1024, tk=D_FF)
    dh = _matmul(f"{tag}_up_da_v", duv, p["w_up"], mode="nt", tm=1024, tn=1024, tk=D_FF, b_koff=D_FF, resid=dh)
    grads["w_up"] = jnp.concatenate([_matmul(f"{tag}_up_dw_g", h, dug, mode="tn", tm=1024, tn=768, tk=2048),
                                     _matmul(f"{tag}_up_dw_v", h, duv, mode="tn", tm=1024, tn=768, tk=2048)], axis=1)
    (dx,), (grads["norm"],) = _stage_bwd(
        f"{tag}_norm_bwd", _f_norm, (n // tr,), [_row_item(x, tr)], [_vec_item(p["norm"])],
        [([dh], (tr, D), lambda i: (i, 0))], [F32], adds={0: dy})
    return dx, grads


def _lru_fwd(x, p, t, carry=None):
    n = x.shape[0]
    tr = _tile(n, 512)
    row = lambda dt: ((n, D), dt, (tr, D), lambda i: (i, 0))
    h, xb = _stage_fwd("lru_pre", _f_lru_pre, (n // tr,), [_row_item(x, tr)],
                       [_vec_item(p["norm"]), _vec_item(p["b_out"])], [row(BF16), row(F32)])
    u = _matmul("lru_in", h, p["w_in"], tm=1024, tn=1024)
    out, = _stage_fwd("lru_core", _f_lru_core, (D // CT, n // t), _seq_halves(u, t, CT), _lru_core_params(p),
                      [((n, D), BF16, (t, CT), lambda j, b: (b, j))], carry=carry)
    y = _matmul("lru_out", out, p["w_out"], resid=xb, tm=1024, tn=1024)
    return y, (x, h, u, out)


def _lru_core_params(p):
    return [_col_item(p["b_y"], CT), _col_item(p["b_x"], CT), _col_item(p["conv_w"], CT), _col_item(p["conv_b"], CT),
            (p["gate_w"], (2, 1, CT, CT), lambda j, b: (0, j, 0, 0)), _col_item(p["gate_b"], CT), _col_item(p["lam"], CT)]


def _lru_bwd(dy, saved, p, t, carry=None):
    x, h, u, out = saved
    n = x.shape[0]
    tr = _tile(n, 512)
    grads = {}
    dout = _matmul("lru_out_da", dy, p["w_out"], mode="nt", tm=1024, tn=1024)
    grads["w_out"] = _matmul("lru_out_dw", out, dy, mode="tn", tm=1024, tn=1024)
    (duy, dux), core_grads = _stage_bwd("lru_core_bwd", _f_lru_core, (D // CT, n // t), _seq_halves(u, t, CT),
                                        _lru_core_params(p), [([dout], (t, CT), lambda j, b: (b, j))], [BF16, BF16],
                                        carry=carry)
    for k, g in zip(("b_y", "b_x", "conv_w", "conv_b", "gate_w", "gate_b", "lam"), core_grads):
        grads[k] = g
    dh = _matmul("lru_in_da_y", duy, p["w_in"], mode="nt", tm=1024, tn=1024)
    dh = _matmul("lru_in_da_x", dux, p["w_in"], mode="nt", tm=1024, tn=1024, b_koff=D, resid=dh)
    grads["w_in"] = jnp.concatenate([_matmul("lru_in_dw_y", h, duy, mode="tn", tm=1024, tn=1024),
                                     _matmul("lru_in_dw_x", h, dux, mode="tn", tm=1024, tn=1024)], axis=1)
    row = (tr, D), lambda i: (i, 0)
    (dx,), (grads["norm"], grads["b_out"]) = _stage_bwd(
        "lru_pre_bwd", _f_lru_pre, (n // tr,), [_row_item(x, tr)], [_vec_item(p["norm"]), _vec_item(p["b_out"])],
        [([dh], *row), ([dy], *row)], [F32])
    return dx, grads


_RWKV_PRE_VECS = ("w0", "a0", "k_k", "k_a", "w2", "a2", "g2")
_RWKV_POST_VECS = ("ln_w", "ln_b", "r_k")


def _rwkv_fwd(x, p, t, carry=None):
    n = x.shape[0]
    bsz = n // t
    tr = _tile(n, 512)
    ts = _tile(n, 128)
    (h,) = _stage_fwd("rwkv_norm", _f_norm, (n // tr,), [_row_item(x, tr)], [_vec_item(p["norm"])],
                      [((n, D), F32, (tr, D), lambda i: (i, 0))])
    mixed = _stage_fwd("rwkv_mix", _f_rwkv_mix, (D // CT, bsz), [_seq_item(h, t, CT)], [_col_item(p["mix"], CT)],
                       [((n, D), BF16, (t, CT), lambda j, b: (b, j))] * 6)
    xr, xk, xv, xw, xa, xg = mixed
    r = _matmul("rwkv_r", xr, p["w_r"], tm=1024, tn=1024)
    k = _matmul("rwkv_k", xk, p["w_k"], tm=1024, tn=1024)
    v = _matmul("rwkv_v", xv, p["w_v"], tm=1024, tn=1024)
    lw1 = _matmul("rwkv_w1", xw, p["w1"], tm=1024)
    la1 = _matmul("rwkv_a1", xa, p["a1"], tm=1024)
    lg1 = _matmul("rwkv_g1", xg, p["g1"], tm=1024)
    pre_ins = [_row_item(a, ts) for a in (k, lw1, la1, lg1)]
    pre_par = [_vec_item(p[q]) for q in _RWKV_PRE_VECS]
    row = ((n, D), F32, (ts, D), lambda i: (i, 0))
    lw, kf, aa, bb, g = _stage_fwd("rwkv_pre", _f_rwkv_pre, (n // ts,), pre_ins, pre_par, [row] * 5)
    y, zs = _rwkv_scan_fwd([r, lw, kf, v, aa, bb], t, carry=carry)
    post_ins = [_row_item(a, ts) for a in (y, r, kf, v, g)]
    post_par = [_vec_item(p[q]) for q in _RWKV_POST_VECS]
    (yg,) = _stage_fwd("rwkv_post", _f_rwkv_post, (n // ts,), post_ins, post_par,
                       [((n, D), BF16, (ts, D), lambda i: (i, 0))])
    out = _matmul("rwkv_out", yg, p["w_out"], resid=x, tm=1024, tn=1024)
    return out, (x, h, mixed, r, k, v, lw1, la1, lg1, lw, kf, aa, bb, g, zs, y, yg)


def _rwkv_bwd(dout, saved, p, t, carry=None):
    x, h, mixed, r, k, v, lw1, la1, lg1, lw, kf, aa, bb, g, zs, y, yg = saved
    xr, xk, xv, xw, xa, xg = mixed
    n = x.shape[0]
    bsz = n // t
    tr = _tile(n, 512)
    ts = _tile(n, 128)
    grads = {}
    row_s = (ts, D), lambda i: (i, 0)
    dyg = _matmul("rwkv_out_da", dout, p["w_out"], mode="nt", tm=1024, tn=1024)
    grads["w_out"] = _matmul("rwkv_out_dw", yg, dout, mode="tn", tm=1024, tn=1024)
    post_ins = [_row_item(a, ts) for a in (y, r, kf, v, g)]
    post_par = [_vec_item(p[q]) for q in _RWKV_POST_VECS]
    (dy, dr_a, dkf_a, dv_a, dg), post_grads = _stage_bwd(
        "rwkv_post_bwd", _f_rwkv_post, (n // ts,), post_ins, post_par, [([dyg], *row_s)], [F32] * 5)
    grads.update(zip(_RWKV_POST_VECS, post_grads))
    dr_b, dlw, dkf_b, dv_b, daa, dbb = _rwkv_scan_bwd([r, lw, kf, v, aa, bb], zs, dy, t, carry=carry)
    dr = _add("rwkv_dr_sum", dr_a, dr_b)
    dv = _add("rwkv_dv_sum", dv_a, dv_b)
    pre_ins = [_row_item(a, ts) for a in (k, lw1, la1, lg1)]
    pre_par = [_vec_item(p[q]) for q in _RWKV_PRE_VECS]
    (dk, dlw1, dla1, dlg1), pre_grads = _stage_bwd(
        "rwkv_pre_bwd", _f_rwkv_pre, (n // ts,), pre_ins, pre_par,
        [([dlw], *row_s), ([dkf_a, dkf_b], *row_s), ([daa], *row_s), ([dbb], *row_s), ([dg], *row_s)], [F32] * 4)
    grads.update(zip(_RWKV_PRE_VECS, pre_grads))
    dmixed = []
    for tag, xin, dz, wname in (("r", xr, dr, "w_r"), ("k", xk, dk, "w_k"), ("v", xv, dv, "w_v"),
                                ("w1", xw, dlw1, "w1"), ("a1", xa, dla1, "a1"), ("g1", xg, dlg1, "g1")):
        dmixed.append(_matmul(f"rwkv_{tag}_da", dz, p[wname], mode="nt", tm=1024, tn=1024))
        grads[wname] = _matmul(f"rwkv_{tag}_dw", xin, dz, mode="tn", tm=1024, tn=1024)
    seq_blk = (t, CT), lambda j, b: (b, j)
    (dh,), (grads["mix"],) = _stage_bwd(
        "rwkv_mix_bwd", _f_rwkv_mix, (D // CT, bsz), [_seq_item(h, t, CT)], [_col_item(p["mix"], CT)],
        [([a], *seq_blk) for a in dmixed], [F32])
    (dx,), (grads["norm"],) = _stage_bwd(
        "rwkv_norm_bwd", _f_norm, (n // tr,), [_row_item(x, tr)], [_vec_item(p["norm"])],
        [([dh], (tr, D), lambda i: (i, 0))], [F32], adds={0: dout})
    return dx, grads


def _block_diag_gates(gate_w):
    z = jnp.zeros((2, D // CT, HEAD, HEAD), gate_w.dtype)
    even, odd = gate_w[:, 0::2], gate_w[:, 1::2]
    top = jnp.concatenate([even, z], axis=-1)
    bot = jnp.concatenate([z, odd], axis=-1)
    return jnp.concatenate([top, bot], axis=-2)


def _gate_blocks_grad(dg):
    even, odd = dg[:, :, :HEAD, :HEAD], dg[:, :, HEAD:, HEAD:]
    return jnp.stack([even, odd], axis=2).reshape(2, N_HEAD, HEAD, HEAD)


def _step(local, x, target):
    bsz, t, _ = x.shape
    n = bsz * t

    def block(name, pre=""):
        base, _, layer = name.partition("@")
        a = local[pre + base]
        return a[int(layer):int(layer) + 1] if layer else a

    def gather_pack(group):
        flat = [block(k).astype(BF16).reshape(-1) if _base(k) in GATHER_BF16
                else lax.bitcast_convert_type(block(k), BF16).reshape(-1) for k in group]
        return _pack(flat, BF16)

    def gathered(group, buf):
        sizes = [(1 if _base(k) in GATHER_BF16 else 2) * math.prod(block(k).shape) for k in group]
        out = {}
        for k, g in zip(group, _unpack(buf, [(s,) for s in sizes], (N_DEV,))):
            if _base(k) not in GATHER_BF16:
                g = lax.bitcast_convert_type(g.reshape(N_DEV, -1, 2), F32)
            out[k] = _unshard(g.reshape((N_DEV,) + block(k).shape), SHARD_AXIS[_base(k)])
        return out

    def scatter_pack(group, grads):
        return _pack([_reshard(grads[k], SHARD_AXIS[_base(k)]) for k in group], F32, (N_DEV,))

    def ffn_params(l, full):
        return dict(norm=local["ffn_norm"][l:l + 1], w_up=full[f"ffn_w_up@{l}"][0], conv_w=full[f"ffn_conv_w@{l}"][0],
                    conv_b=local["ffn_conv_b"][l:l + 1], w_down=full[f"ffn_w_down@{l}"][0])

    def ffn_grads(l, g):
        return {f"ffn_w_up@{l}": g["w_up"][None], f"ffn_w_down@{l}": g["w_down"][None], f"ffn_conv_w@{l}": g["conv_w"][None]}

    (landed,) = _exchange("gather_lru", [(gather_pack(GROUPS["lru"]), False)])
    full = gathered(GROUPS["lru"], landed)
    lru = dict(norm=local["lru_norm"], b_out=local["lru_b_out"], w_in=full["lru_w_in"][0],
               b_y=local["lru_b_in"][:, :D], b_x=local["lru_b_in"][:, D:], conv_w=full["lru_conv_w"][0],
               conv_b=local["lru_conv_b"], gate_w=_block_diag_gates(local["lru_gate_w"][0]),
               gate_b=local["lru_gate_b"][0].reshape(2, D), lam=local["lru_lambda"], w_out=full["lru_w_out"][0])
    x0 = x.reshape(n, D)
    ride = _Carry([(gather_pack(GROUPS["ffn0"]), False)])
    x1, s_lru = _lru_fwd(x0, lru, t, carry=ride)
    ffn0 = ffn_params(0, gathered(GROUPS["ffn0"], ride.landed[0]))
    ride = _Carry([(gather_pack(GROUPS["rwkv"]), False)])
    x2, s_ffn0 = _ffn_fwd("ffn0", x1, ffn0, t, carry=ride)
    full = gathered(GROUPS["rwkv"], ride.landed[0])
    rwkv = dict(norm=full["rwkv_norm"], mix=full["rwkv_mix"][0], w_r=full["rwkv_w_rkv"][0, 0],
                w_k=full["rwkv_w_rkv"][0, 1], w_v=full["rwkv_w_rkv"][0, 2], w0=full["rwkv_w0"], w1=full["rwkv_w1"][0],
                w2=full["rwkv_w2"][0], a0=full["rwkv_a0"], a1=full["rwkv_a1"][0], a2=full["rwkv_a2"][0],
                g1=full["rwkv_g1"][0], g2=full["rwkv_g2"][0], k_k=full["rwkv_k_k"], k_a=full["rwkv_k_a"],
                r_k=local["rwkv_r_k"].reshape(1, D), ln_w=full["rwkv_ln_w"], ln_b=full["rwkv_ln_b"],
                w_out=full["rwkv_w_out"][0])
    ride = _Carry([(gather_pack(GROUPS["ffn1"]), False)])
    x3, s_rwkv = _rwkv_fwd(x2, rwkv, t, carry=ride)
    ffn1 = ffn_params(1, gathered(GROUPS["ffn1"], ride.landed[0]))
    x4, s_ffn1 = _ffn_fwd("ffn1", x3, ffn1, t)
    loss, dx4, d_final = _loss_head(x4, target.reshape(n, D), local["final_norm"].reshape(1, D))

    parts = {}
    dx3, g_ffn1 = _ffn_bwd("ffn1", dx4, s_ffn1, ffn1, t)
    ride = _Carry([(scatter_pack(GROUPS["ffn1"], ffn_grads(1, g_ffn1)), True)])
    dx2, g_rwkv = _rwkv_bwd(dx3, s_rwkv, rwkv, t, carry=ride)
    parts["ffn1"] = ride.landed[0]
    rwkv_grads = {
        "rwkv_norm": g_rwkv["norm"], "rwkv_mix": g_rwkv["mix"][None],
        "rwkv_w_rkv": jnp.stack([g_rwkv["w_r"], g_rwkv["w_k"], g_rwkv["w_v"]])[None],
        "rwkv_w0": g_rwkv["w0"], "rwkv_w1": g_rwkv["w1"][None], "rwkv_w2": g_rwkv["w2"][None],
        "rwkv_a0": g_rwkv["a0"], "rwkv_a1": g_rwkv["a1"][None], "rwkv_a2": g_rwkv["a2"][None],
        "rwkv_g1": g_rwkv["g1"][None], "rwkv_g2": g_rwkv["g2"][None], "rwkv_k_k": g_rwkv["k_k"],
        "rwkv_k_a": g_rwkv["k_a"], "rwkv_ln_w": g_rwkv["ln_w"], "rwkv_ln_b": g_rwkv["ln_b"],
        "rwkv_w_out": g_rwkv["w_out"][None],
    }
    ride = _Carry([(scatter_pack(GROUPS["rwkv"], rwkv_grads), True)])
    dx1, g_ffn0 = _ffn_bwd("ffn0", dx2, s_ffn0, ffn0, t, carry=ride)
    parts["rwkv"] = ride.landed[0]
    ride = _Carry([(scatter_pack(GROUPS["ffn0"], ffn_grads(0, g_ffn0)), True)])
    dx0, g_lru = _lru_bwd(dx1, s_lru, lru, t, carry=ride)
    parts["ffn0"] = ride.landed[0]

    g_ffn = [g_ffn0, g_ffn1]
    lru_grads = {"lru_w_in": g_lru["w_in"][None], "lru_conv_w": g_lru["conv_w"][None], "lru_w_out": g_lru["w_out"][None]}
    gfull = {
        "lru_norm": g_lru["norm"], "lru_b_in": jnp.concatenate([g_lru["b_y"], g_lru["b_x"]], axis=1),
        "lru_conv_b": g_lru["conv_b"], "lru_gate_w": _gate_blocks_grad(g_lru["gate_w"])[None],
        "lru_gate_b": g_lru["gate_b"].reshape(1, 2, N_HEAD, HEAD), "lru_lambda": g_lru["lam"],
        "lru_b_out": g_lru["b_out"], "rwkv_r_k": g_rwkv["r_k"].reshape(1, N_HEAD, HEAD),
        "ffn_norm": jnp.concatenate([g["norm"] for g in g_ffn]),
        "ffn_conv_b": jnp.concatenate([g["conv_b"] for g in g_ffn]),
        "final_norm": d_final.reshape(D),
    }
    parts["lru"], parts["replicated"] = _exchange(
        "grad_tail", [(scatter_pack(GROUPS["lru"], lru_grads), True), (_pack([gfull[k] for k in REPLICATED], F32), False)])

    out = {}
    for tag, names in (*GROUPS.items(), ("replicated", REPLICATED)):
        packs = [_pack([block(k, pre) for k in names], F32) for pre in ("", "m_", "v_")]
        res = _adamw(f"adamw_{tag}", parts[tag], *packs)
        for kind, buf in zip(("grad", "delta", "new_m", "new_v"), res):
            for k, a in zip(names, _unpack(buf, [block(k).shape for k in names])):
                out[(kind, k)] = a
    for kind in ("grad", "delta", "new_m", "new_v"):
        for k in STACKED:
            out[(kind, k)] = jnp.concatenate([out[(kind, f"{k}@0")], out[(kind, f"{k}@1")]], axis=0)
    return loss, dx0.reshape(x.shape), out


def kernel(x, lru_norm, lru_w_in, lru_b_in, lru_conv_w, lru_conv_b, lru_gate_w, lru_gate_b, lru_lambda, lru_w_out, lru_b_out, rwkv_norm, rwkv_mix, rwkv_w_rkv, rwkv_w0, rwkv_w1, rwkv_w2, rwkv_a0, rwkv_a1, rwkv_a2, rwkv_g1, rwkv_g2, rwkv_k_k, rwkv_k_a, rwkv_r_k, rwkv_ln_w, rwkv_ln_b, rwkv_w_out, ffn_norm, ffn_w_up, ffn_conv_w, ffn_conv_b, ffn_w_down, final_norm, loss_target, m_lru_norm, m_lru_w_in, m_lru_b_in, m_lru_conv_w, m_lru_conv_b, m_lru_gate_w, m_lru_gate_b, m_lru_lambda, m_lru_w_out, m_lru_b_out, m_rwkv_norm, m_rwkv_mix, m_rwkv_w_rkv, m_rwkv_w0, m_rwkv_w1, m_rwkv_w2, m_rwkv_a0, m_rwkv_a1, m_rwkv_a2, m_rwkv_g1, m_rwkv_g2, m_rwkv_k_k, m_rwkv_k_a, m_rwkv_r_k, m_rwkv_ln_w, m_rwkv_ln_b, m_rwkv_w_out, m_ffn_norm, m_ffn_w_up, m_ffn_conv_w, m_ffn_conv_b, m_ffn_w_down, m_final_norm, v_lru_norm, v_lru_w_in, v_lru_b_in, v_lru_conv_w, v_lru_conv_b, v_lru_gate_w, v_lru_gate_b, v_lru_lambda, v_lru_w_out, v_lru_b_out, v_rwkv_norm, v_rwkv_mix, v_rwkv_w_rkv, v_rwkv_w0, v_rwkv_w1, v_rwkv_w2, v_rwkv_a0, v_rwkv_a1, v_rwkv_a2, v_rwkv_g1, v_rwkv_g2, v_rwkv_k_k, v_rwkv_k_a, v_rwkv_r_k, v_rwkv_ln_w, v_rwkv_ln_b, v_rwkv_w_out, v_ffn_norm, v_ffn_w_up, v_ffn_conv_w, v_ffn_conv_b, v_ffn_w_down, v_final_norm):
    args = locals()
    local = {(pre + k): args[pre + k] for pre in ("", "m_", "v_") for k in WEIGHTS}
    loss_local, grad_x, out = _step(local, x, loss_target)
    loss = lax.psum(loss_local, ("x", "y", "c"))
    return (loss, grad_x, *[out[(kind, k)] for kind in ("grad", "delta", "new_m", "new_v") for k in WEIGHTS])
```

```python
import functools
import math

import jax
import jax.numpy as jnp
from jax import lax
from jax.experimental import pallas as pl
from jax.experimental.pallas import tpu as pltpu

F32 = jnp.float32
BF16 = jnp.bfloat16
HI = lax.Precision.HIGHEST

N_DEV = 8
D = 1024
HEAD = 64
N_HEAD = D // HEAD
D_FF = 3 * D
LANE = 128
V7X_VMEM_BYTES = 64 * 1024 * 1024
VMEM_LIMIT = V7X_VMEM_BYTES - 8 * 1024 * 1024
CT = LANE
CHUNK = 64
SCAN_HEADS = 16
PACK_W = 1024
PACK_ROWS = 256

ADAM_LR, ADAM_B1, ADAM_B2, ADAM_EPS, ADAM_WD, ADAM_STEP = 0.001, 0.9, 0.999, 1e-08, 0.01, 10
RMS_EPS = 1e-6
GN_EPS = 64e-5
LRU_C = 8.0

WEIGHTS = ['lru_norm', 'lru_w_in', 'lru_b_in', 'lru_conv_w', 'lru_conv_b', 'lru_gate_w', 'lru_gate_b',
           'lru_lambda', 'lru_w_out', 'lru_b_out', 'rwkv_norm', 'rwkv_mix', 'rwkv_w_rkv', 'rwkv_w0', 'rwkv_w1',
           'rwkv_w2', 'rwkv_a0', 'rwkv_a1', 'rwkv_a2', 'rwkv_g1', 'rwkv_g2', 'rwkv_k_k', 'rwkv_k_a', 'rwkv_r_k',
           'rwkv_ln_w', 'rwkv_ln_b', 'rwkv_w_out', 'ffn_norm', 'ffn_w_up', 'ffn_conv_w', 'ffn_conv_b',
           'ffn_w_down', 'final_norm']
SHARD_AXIS = {'lru_w_in': 2, 'lru_conv_w': 2, 'lru_w_out': 1, 'rwkv_norm': 1, 'rwkv_mix': 2, 'rwkv_w_rkv': 2,
              'rwkv_w0': 1, 'rwkv_w1': 1, 'rwkv_w2': 2, 'rwkv_a0': 1, 'rwkv_a1': 1, 'rwkv_a2': 2, 'rwkv_g1': 1,
              'rwkv_g2': 2, 'rwkv_k_k': 1, 'rwkv_k_a': 1, 'rwkv_ln_w': 1, 'rwkv_ln_b': 1, 'rwkv_w_out': 1,
              'ffn_w_up': 2, 'ffn_conv_w': 2, 'ffn_w_down': 1}
GATHER_BF16 = ['lru_w_in', 'lru_w_out', 'rwkv_w_rkv', 'rwkv_w1', 'rwkv_a1', 'rwkv_g1', 'rwkv_w_out', 'ffn_w_up',
               'ffn_w_down']
REPLICATED = [n for n in WEIGHTS if n not in SHARD_AXIS]
STACKED = ['ffn_w_up', 'ffn_w_down', 'ffn_conv_w']
LARGE = ['lru_w_in', 'lru_w_out', 'rwkv_w_rkv', 'rwkv_w_out', 'ffn_w_up', 'ffn_w_down']
GROUPS = {
    "lru": ['lru_w_in', 'lru_w_out', 'lru_conv_w'],
    "ffn0": [f"{n}@0" for n in STACKED],
    "rwkv": [n for n in WEIGHTS if n.startswith("rwkv_") and n in SHARD_AXIS],
    "ffn1": [f"{n}@1" for n in STACKED],
}


def _base(name):
    return name.partition("@")[0]


def _pcall(body, **kw):
    return pl.pallas_call(body, **kw)


class _Carry:
    def __init__(self, items):
        self.items = list(items)
        self.landed = None


def _exchange_copies(src_ref, dst_ref, send_sems, recv_sems, local_sem, scatter):
    x, y, c = lax.axis_index("x"), lax.axis_index("y"), lax.axis_index("c")
    me = 4 * x + 2 * y + c
    mine = pltpu.make_async_copy(src_ref.at[me] if scatter else src_ref, dst_ref.at[me], local_sem)
    copies = []
    for m in range(1, N_DEV):
        px = 1 - x if m & 4 else x
        py = 1 - y if m & 2 else y
        pc = 1 - c if m & 1 else c
        part = src_ref.at[4 * px + 2 * py + pc] if scatter else src_ref
        copies.append(pltpu.make_async_remote_copy(
            src_ref=part, dst_ref=dst_ref.at[me], send_sem=send_sems.at[m - 1], recv_sem=recv_sems.at[m - 1],
            device_id=(px, py, pc), device_id_type=pl.DeviceIdType.MESH))
    return mine, copies


def _start_exchange(mine, copies):
    mine.start()
    for cp in copies:
        cp.start()


def _wait_exchange(mine, copies):
    for cp in copies:
        cp.wait_recv()
    for cp in copies:
        cp.wait_send()
    mine.wait()


_EXCHANGE_SEMS = [pltpu.SemaphoreType.DMA((N_DEV - 1,)), pltpu.SemaphoreType.DMA((N_DEV - 1,)), pltpu.SemaphoreType.DMA]


def _landing_shape(src, scatter):
    return jax.ShapeDtypeStruct((N_DEV,) + tuple(src.shape[1:] if scatter else src.shape), src.dtype)


def _call(body, operands, *, carry=None, name, grid, in_specs, out_specs, out_shape, scratch_shapes=(),
          compiler_params=None):
    if carry is None:
        return _pcall(body, name=name, grid=grid, in_specs=in_specs, out_specs=out_specs, out_shape=out_shape,
                      scratch_shapes=list(scratch_shapes), compiler_params=compiler_params)(*operands)
    single = not isinstance(out_specs, (list, tuple))
    out_specs_l = [out_specs] if single else list(out_specs)
    out_shape_l = [out_shape] if single else list(out_shape)
    n_in, n_out, n_scr, n_x = len(in_specs), len(out_specs_l), len(scratch_shapes), len(carry.items)
    flags = [sc for (_, sc) in carry.items]

    def wrapped(*refs):
        ins, refs = refs[:n_in], refs[n_in:]
        xsrc, refs = refs[:n_x], refs[n_x:]
        outs, refs = refs[:n_out], refs[n_out:]
        xdst, refs = refs[:n_x], refs[n_x:]
        scr, sems = refs[:n_scr], refs[n_scr:]
        first = functools.reduce(jnp.logical_and, [pl.program_id(i) == 0 for i in range(len(grid))])
        last = functools.reduce(jnp.logical_and, [pl.program_id(i) == grid[i] - 1 for i in range(len(grid))])

        def exchanges():
            return [_exchange_copies(xsrc[i], xdst[i], *sems[3 * i:3 * i + 3], flags[i]) for i in range(n_x)]

        @pl.when(first)
        def _():
            for mine, copies in exchanges():
                _start_exchange(mine, copies)

        body(*ins, *outs, *scr)

        @pl.when(last)
        def _():
            for mine, copies in exchanges():
                _wait_exchange(mine, copies)

    hbm = pl.BlockSpec(memory_space=pl.ANY)
    res = _pcall(
        wrapped, name=name, grid=grid, in_specs=list(in_specs) + [hbm] * n_x,
        out_specs=out_specs_l + [hbm] * n_x,
        out_shape=out_shape_l + [_landing_shape(a, sc) for (a, sc) in carry.items],
        scratch_shapes=list(scratch_shapes) + _EXCHANGE_SEMS * n_x, compiler_params=compiler_params,
    )(*operands, *[a for (a, _) in carry.items])
    carry.landed = list(res[n_out:])
    return res[0] if single else list(res[:n_out])


def _params(n_grid):
    return pltpu.CompilerParams(dimension_semantics=("arbitrary",) * n_grid, vmem_limit_bytes=VMEM_LIMIT)


def _shift_rows(x, d, up):
    n = x.shape[0]
    idx = lax.broadcasted_iota(jnp.int32, x.shape, 0)
    if up:
        return jnp.where(idx < n - d, pltpu.roll(x, n - d, 0), 0.0)
    return jnp.where(idx >= d, pltpu.roll(x, d, 0), 0.0)


@functools.partial(jax.custom_vjp, nondiff_argnums=(1,))
def _shift_down(x, d):
    return _shift_rows(x, d, False)


def _shift_down_fwd(x, d):
    return _shift_rows(x, d, False), None


def _shift_down_bwd(d, _, g):
    return (_shift_rows(g, d, True),)


_shift_down.defvjp(_shift_down_fwd, _shift_down_bwd)


def _scan_doubling(a, b, up):
    n = a.shape[0]
    d = 1
    while d < n:
        b = b + a * _shift_rows(b, d, up)
        a = a * _shift_rows(a, d, up)
        d *= 2
    return b


@jax.custom_vjp
def _linear_scan(a, b):
    return _scan_doubling(a, b, False)


def _linear_scan_fwd(a, b):
    h = _scan_doubling(a, b, False)
    return h, (a, h)


def _linear_scan_bwd(res, dh):
    a, h = res
    g = _scan_doubling(_shift_rows(a, 1, True), dh, True)
    return g * _shift_rows(h, 1, False), g


_linear_scan.defvjp(_linear_scan_fwd, _linear_scan_bwd)


def _causal_conv(x, w, b):
    k = w.shape[0]
    out = b + x * w[k - 1:k]
    for j in range(k - 1):
        out = out + _shift_down(x, k - 1 - j) * w[j:j + 1]
    return out


_GELU_C1 = math.sqrt(2.0 / math.pi)
_GELU_C2 = 0.044715 * _GELU_C1


@jax.custom_vjp
def _gelu(x):
    return 0.5 * x * (1.0 + jnp.tanh(x * (_GELU_C1 + _GELU_C2 * (x * x))))


def _gelu_fwd(x):
    x2 = x * x
    t = jnp.tanh(x * (_GELU_C1 + _GELU_C2 * x2))
    return 0.5 * x * (1.0 + t), (x, x2, t)


def _gelu_bwd(res, g):
    x, x2, t = res
    return (g * (0.5 * (1.0 + t) + (0.5 * x) * (1.0 - t * t) * (_GELU_C1 + (3.0 * _GELU_C2) * x2)),)


_gelu.defvjp(_gelu_fwd, _gelu_bwd)


def _rmsnorm(x, g):
    return x * lax.rsqrt(jnp.mean(x * x, axis=-1, keepdims=True) + RMS_EPS) * g


def _neg_expm1(x):
    series = x * (1.0 + x * 0.5 * (1.0 + x * (1.0 / 3.0) * (1.0 + x * 0.25 * (1.0 + x * 0.2))))
    return -jnp.where(x > -0.1, series, jnp.exp(x) - 1.0)


def _dot(a, b, ca=1, cb=0, precision=None):
    return lax.dot_general(a, b, (((ca,), (cb,)), ((), ())), precision=precision, preferred_element_type=F32)


def _bdot(a, b):
    return _dot(a.astype(BF16), b.astype(BF16))


def _split_bf16(x):
    hi = x.astype(BF16)
    return hi, (x - hi.astype(F32)).astype(BF16)


def _head_sum_impl(x):
    row = lax.broadcasted_iota(jnp.int32, (D, LANE), 0)
    col = lax.broadcasted_iota(jnp.int32, (D, LANE), 1)
    e = (lax.shift_right_logical(row, 6) == col).astype(BF16)
    hi, lo = _split_bf16(x)
    s_hi, s_lo = _split_bf16(_dot(hi, e) + _dot(lo, e))
    return _dot(s_hi, e, 1, 1) + _dot(s_lo, e, 1, 1)


@jax.custom_vjp
def _head_sum(x):
    return _head_sum_impl(x)


def _head_sum_fwd(x):
    return _head_sum_impl(x), None


def _head_sum_bwd(_, g):
    return (_head_sum(g),)


_head_sum.defvjp(_head_sum_fwd, _head_sum_bwd)


def _specs(items):
    return [pl.BlockSpec(it[1], it[2]) for it in items]


def _stage_fwd(name, f, grid, ins, params, outs, carry=None):
    n_in = len(ins) + len(params)

    def body(*refs):
        res = f(*[r[...] for r in refs[:n_in]])
        for o, v in zip(refs[n_in:], res):
            o[...] = v.astype(o.dtype)

    return _call(
        body, [it[0] for it in ins + params], carry=carry, name=name, grid=grid, in_specs=_specs(ins + params),
        out_specs=[pl.BlockSpec(bs, im) for (_, _, bs, im) in outs],
        out_shape=[jax.ShapeDtypeStruct(s, dt) for (s, dt, _, _) in outs],
        compiler_params=_params(len(grid)),
    )


def _stage_bwd(name, f, grid, ins, params, douts, din_dtypes, adds=None, carry=None):
    adds = adds or {}
    n_in, n_par = len(ins), len(params)
    dout_items = [(a, bs, im) for (arrs, bs, im) in douts for a in arrs]
    add_items = [(adds[i], ins[i][1], ins[i][2]) for i in sorted(adds)]
    n_do, n_add = len(dout_items), len(add_items)

    def body(*refs):
        vals = [r[...] for r in refs[:n_in + n_par]]
        do_refs = list(refs[n_in + n_par:n_in + n_par + n_do])
        add_refs = dict(zip(sorted(adds), refs[n_in + n_par + n_do:n_in + n_par + n_do + n_add]))
        din_refs = refs[n_in + n_par + n_do + n_add:n_in + n_par + n_do + n_add + n_in]
        dpar_refs = refs[n_in + n_par + n_do + n_add + n_in:]
        cts = []
        for (arrs, _, _) in douts:
            ct = do_refs.pop(0)[...].astype(F32)
            for _ in arrs[1:]:
                ct = ct + do_refs.pop(0)[...].astype(F32)
            cts.append(ct)
        _, vjp = jax.vjp(f, *vals)
        grads = vjp(tuple(cts))
        for i, r in enumerate(din_refs):
            g = grads[i]
            if i in add_refs:
                g = g + add_refs[i][...]
            r[...] = g.astype(r.dtype)

        @pl.when(pl.program_id(len(grid) - 1) == 0)
        def _():
            for r in dpar_refs:
                r[...] = jnp.zeros(r.shape, r.dtype)

        for j, r in enumerate(dpar_refs):
            r[...] += grads[n_in + j]

    din_shapes = [it[3][0] if len(it) > 3 else it[0].shape for it in ins]
    din_specs = [pl.BlockSpec(it[1], it[3][1] if len(it) > 3 else it[2]) for it in ins]
    res = _call(
        body, [it[0] for it in ins + params + dout_items + add_items], carry=carry, name=name, grid=grid,
        in_specs=_specs(ins + params + dout_items + add_items), out_specs=din_specs + _specs(params),
        out_shape=[jax.ShapeDtypeStruct(s, dt) for s, dt in zip(din_shapes, din_dtypes)]
        + [jax.ShapeDtypeStruct(it[0].shape, F32) for it in params],
        compiler_params=_params(len(grid)),
    )
    return list(res[:n_in]), list(res[n_in:])


def _tile(n, want):
    t = min(n, want)
    while n % t:
        t //= 2
    return t


def _matmul(name, a, b, *, mode="nn", resid=None, out_dtype=F32, tm=512, tn=512, tk=1024, b_koff=0, carry=None):
    if mode == "tn":
        (kdim, m), n = a.shape, b.shape[1]
    else:
        (m, kdim), n = a.shape, (b.shape[1] if mode == "nn" else b.shape[0])
    tm, tn, tk = _tile(m, tm), _tile(n, tn), _tile(kdim, tk)
    nk = kdim // tk
    ko = b_koff // tk
    assert ko * tk == b_koff
    a_spec = pl.BlockSpec((tk, tm), lambda i, j, k: (k, i)) if mode == "tn" else pl.BlockSpec((tm, tk), lambda i, j, k: (i, k))
    b_spec = pl.BlockSpec((tn, tk), lambda i, j, k: (j, k + ko)) if mode == "nt" else pl.BlockSpec((tk, tn), lambda i, j, k: (k + ko, j))
    ca = 0 if mode == "tn" else 1
    cb = 1 if mode == "nt" else 0
    operands = [a, b]
    in_specs = [a_spec, b_spec]
    if resid is not None:
        operands.append(resid)
        in_specs.append(pl.BlockSpec((tm, tn), lambda i, j, k: (i, j)))

    def finish(r, refs, o_ref):
        if resid is not None:
            r = r + refs[2][...]
        o_ref[...] = r.astype(o_ref.dtype)

    def body_one_step(*refs):
        finish(_dot(refs[0][...].astype(BF16), refs[1][...].astype(BF16), ca, cb), refs, refs[-1])

    def body(*refs):
        a_ref, b_ref = refs[0], refs[1]
        o_ref, acc_ref = refs[-2], refs[-1]
        k = pl.program_id(2)

        @pl.when(k == 0)
        def _():
            acc_ref[...] = jnp.zeros(acc_ref.shape, F32)

        acc_ref[...] += _dot(a_ref[...].astype(BF16), b_ref[...].astype(BF16), ca, cb)

        @pl.when(k == nk - 1)
        def _():
            finish(acc_ref[...], refs, o_ref)

    return _call(
        body_one_step if nk == 1 else body, operands, carry=carry, name=name, grid=(m // tm, n // tn, nk),
        in_specs=in_specs, out_specs=pl.BlockSpec((tm, tn), lambda i, j, k: (i, j)),
        out_shape=jax.ShapeDtypeStruct((m, n), out_dtype),
        scratch_shapes=[] if nk == 1 else [pltpu.VMEM((tm, tn), F32)],
        compiler_params=_params(3),
    )


def _add(name, a, b):
    rows, cols = a.shape
    tr = _tile(rows, 512)

    def body(a_ref, b_ref, o_ref):
        o_ref[...] = a_ref[...] + b_ref[...]

    spec = pl.BlockSpec((tr, cols), lambda i: (i, 0))
    return _pcall(body, name=name, grid=(rows // tr,), in_specs=[spec, spec], out_specs=spec,
                  out_shape=jax.ShapeDtypeStruct(a.shape, a.dtype), compiler_params=_params(1))(a, b)


def _f_lru_pre(x, norm, b_out):
    return _rmsnorm(x, norm), x + b_out


def _f_lru_core(uy, ux, b_y, b_x, cw, cb, gw, gb, lam):
    yb = _gelu(uy + b_y)
    xr = _causal_conv(ux + b_x, cw, cb)
    gr = jax.nn.sigmoid(_bdot(xr, gw[0, 0]) + gb[0:1])
    gi = jax.nn.sigmoid(_bdot(xr, gw[1, 0]) + gb[1:2])
    log_a = -LRU_C * gr * jax.nn.softplus(-lam)
    a = jnp.exp(log_a)
    bterm = jnp.sqrt(_neg_expm1(2.0 * log_a)) * (gi * xr)
    return (_linear_scan(a, bterm) * yb,)


def _f_norm(x, norm):
    return (_rmsnorm(x, norm),)


def _f_ffn_core(ug, uv, cw, cb):
    return (_gelu(_causal_conv(ug, cw, cb)) * uv,)


def _f_rwkv_mix(h, mix):
    xx = _shift_down(h, 1) - h
    return tuple(h + xx * mix[i:i + 1] for i in range(6))


def _f_rwkv_pre(k, lw1, la1, lg1, w0, a0, k_k, k_a, w2, a2, g2):
    wpre = w0 + _bdot(jnp.tanh(lw1), w2)
    apre = a0 + _bdot(la1, a2)
    g = _bdot(jax.nn.sigmoid(lg1), g2)
    log_decay = -jnp.exp(-jax.nn.softplus(-wpre) - 0.5)
    a = jax.nn.sigmoid(apre)
    kk = k * k_k
    kk = kk / jnp.maximum(jnp.sqrt(_head_sum(kk * kk)), 1e-12)
    kf = k * (1.0 + (a - 1.0) * k_a)
    return log_decay, kf, -kk, kk * a, g


def _f_rwkv_post(y, r, kf, v, g, ln_w, ln_b, r_k):
    inv = 1.0 / HEAD
    yc = y - _head_sum(y) * inv
    var = _head_sum(yc * yc) * inv
    yn = yc * lax.rsqrt(var + GN_EPS) * ln_w + ln_b
    bonus = _head_sum(r * kf * r_k) * v
    return ((yn + bonus) * g,)


def _hdot(a, b, ca, cb):
    return lax.dot_general(a, b, (((ca,), (cb,)), ((0,), (0,))), preferred_element_type=F32)


def _hmm_impl(a, b, ca, cb):
    return _hdot(a.astype(BF16), b.astype(BF16), ca, cb)


@functools.partial(jax.custom_vjp, nondiff_argnums=(2, 3))
def _hmm(a, b, ca, cb):
    return _hmm_impl(a, b, ca, cb)


def _hmm_fwd(a, b, ca, cb):
    return _hmm_impl(a, b, ca, cb), (a, b)


def _hmm_bwd(ca, cb, res, dc):
    a, b = res
    fa, fb = 3 - ca, 3 - cb
    da = _hmm(dc, b, 2, fb) if ca == 2 else _hmm(b, dc, fb, 2)
    db = _hmm(a, dc, fa, 1) if cb == 1 else _hmm(dc, a, 1, fa)
    return da, db


_hmm.defvjp(_hmm_fwd, _hmm_bwd)


def _tri_dot_impl(x, upper):
    g, n, _ = x.shape
    row = lax.broadcasted_iota(jnp.int32, (g, n, 3 * n), 1)
    col = lax.broadcasted_iota(jnp.int32, (g, n, 3 * n), 2)
    col = jnp.where(col >= 2 * n, col - 2 * n, jnp.where(col >= n, col - n, col))
    tri = (row <= col if upper else row >= col).astype(BF16)
    hi = x.astype(BF16)
    rem = x - hi.astype(F32)
    mid = rem.astype(BF16)
    lo = (rem - mid.astype(F32)).astype(BF16)
    return _hdot(tri, jnp.concatenate([hi, mid, lo], axis=1), 2, 1)


@functools.partial(jax.custom_vjp, nondiff_argnums=(1,))
def _tri_dot(x, upper):
    return _tri_dot_impl(x, upper)


def _tri_dot_fwd(x, upper):
    return _tri_dot_impl(x, upper), None


def _tri_dot_bwd(upper, _, g):
    return (_tri_dot(g, not upper),)


_tri_dot.defvjp(_tri_dot_fwd, _tri_dot_bwd)


def _unit_lower_inverse_impl(a):
    g, n, _ = a.shape
    row = lax.broadcasted_iota(jnp.int32, (g, n, n), 1)
    col = lax.broadcasted_iota(jnp.int32, (g, n, n), 2)
    inv = (row == col).astype(F32) + a
    p = a
    for _ in range(int(math.log2(n)) - 1):
        p = _hmm(p, p, 2, 1)
        inv = inv + _hmm(inv, p, 2, 1)
    return inv


@jax.custom_vjp
def _unit_lower_inverse(a):
    return _unit_lower_inverse_impl(a)


def _unit_lower_inverse_fwd(a):
    inv = _unit_lower_inverse_impl(a)
    return inv, inv


def _unit_lower_inverse_bwd(inv, d_inv):
    return (_hmm(_hmm(inv, d_inv, 1, 1), inv, 2, 2),)


_unit_lower_inverse.defvjp(_unit_lower_inverse_fwd, _unit_lower_inverse_bwd)


def _rwkv_chunk(z0, r, lw, k, v, a, b):
    g, n, _ = r.shape
    row = lax.broadcasted_iota(jnp.int32, (g, n, n), 1)
    col = lax.broadcasted_iota(jnp.int32, (g, n, n), 2)
    incl, strict = row >= col, row > col
    cs = _tri_dot(lw, False)
    c_last = cs[:, n - 1:n]
    inv = jnp.exp(-cs)
    ar = jnp.concatenate([a * jnp.exp(cs - lw), r * jnp.exp(cs)], axis=1)
    bk = jnp.concatenate([b * inv, k * inv], axis=1)
    pair = _hmm(ar, bk, 2, 2)
    a_ab = jnp.where(strict, pair[:, :n, :n], 0.0)
    a_ak = jnp.where(strict, pair[:, :n, n:], 0.0)
    a_rbk = jnp.concatenate([jnp.where(incl, pair[:, n:, :n], 0.0), jnp.where(incl, pair[:, n:, n:], 0.0)], axis=2)
    arz = _hmm(ar, z0, 2, 1)
    u = _hmm(_unit_lower_inverse(a_ab), arz[:, :n] + _hmm(a_ak, v, 2, 1), 2, 1)
    uv = jnp.concatenate([u, v], axis=1)
    y = arz[:, n:] + _hmm(a_rbk, uv, 2, 1)
    tail = jnp.exp(c_last - cs)
    er = lax.broadcasted_iota(jnp.int32, (g, HEAD, HEAD), 1)
    ec = lax.broadcasted_iota(jnp.int32, (g, HEAD, HEAD), 2)
    decay_all = jnp.where(er == ec, jnp.exp(c_last), 0.0)
    z_l = _hmm(jnp.concatenate([b * tail, k * tail, decay_all], axis=1), jnp.concatenate([uv, z0], axis=1), 1, 1)
    return y, z_l


def _heads_in(ref, g):
    x = ref[...]
    return jnp.stack([x[:, h * HEAD:(h + 1) * HEAD] for h in range(g)], axis=0)


def _heads_out(ref, x):
    ref[...] = jnp.concatenate([x[h] for h in range(x.shape[0])], axis=1)


def _scan_geometry(n_rows, t):
    g, n = _tile(N_HEAD, SCAN_HEADS), _tile(t, CHUNK)
    return g, n, t // n, N_HEAD // g, n_rows // t


def _rwkv_scan_fwd(seqs, t, carry=None):
    rows = seqs[0].shape[0]
    g, n, nc, hpg, bsz = _scan_geometry(rows, t)

    def body(r_ref, lw_ref, k_ref, v_ref, a_ref, b_ref, y_ref, zs_ref, z_ref):
        @pl.when(pl.program_id(1) == 0)
        def _():
            z_ref[...] = jnp.zeros(z_ref.shape, F32)

        z0 = z_ref[...]
        zs_ref[:, 0] = z0
        y, z_l = _rwkv_chunk(z0, *[_heads_in(ref, g) for ref in (r_ref, lw_ref, k_ref, v_ref, a_ref, b_ref)])
        _heads_out(y_ref, y)
        z_ref[...] = z_l

    seq_spec = pl.BlockSpec((n, g * HEAD), lambda i, c: ((i // hpg) * nc + c, i % hpg))
    return _call(
        body, list(seqs), carry=carry, name="rwkv_scan_fwd", grid=(bsz * hpg, nc), in_specs=[seq_spec] * 6,
        out_specs=[seq_spec, pl.BlockSpec((g, 1, HEAD, HEAD), lambda i, c: (i, c, 0, 0))],
        out_shape=[jax.ShapeDtypeStruct((rows, D), F32), jax.ShapeDtypeStruct((bsz * N_HEAD, nc, HEAD, HEAD), F32)],
        scratch_shapes=[pltpu.VMEM((g, HEAD, HEAD), F32)],
        compiler_params=_params(2),
    )


def _rwkv_scan_bwd(seqs, zs, dy, t, carry=None):
    rows = seqs[0].shape[0]
    g, n, nc, hpg, bsz = _scan_geometry(rows, t)

    def body(r_ref, lw_ref, k_ref, v_ref, a_ref, b_ref, zs_ref, dy_ref, dr, dlw, dk, dv, da, db, dz_ref):
        @pl.when(pl.program_id(1) == 0)
        def _():
            dz_ref[...] = jnp.zeros(dz_ref.shape, F32)

        _, vjp = jax.vjp(_rwkv_chunk, zs_ref[:, 0],
                         *[_heads_in(ref, g) for ref in (r_ref, lw_ref, k_ref, v_ref, a_ref, b_ref)])
        grads = vjp((_heads_in(dy_ref, g), dz_ref[...]))
        dz_ref[...] = grads[0]
        for o, gr in zip((dr, dlw, dk, dv, da, db), grads[1:]):
            _heads_out(o, gr)

    seq_spec = pl.BlockSpec((n, g * HEAD), lambda i, c: ((i // hpg) * nc + nc - 1 - c, i % hpg))
    return _call(
        body, [*seqs, zs, dy], carry=carry, name="rwkv_scan_bwd", grid=(bsz * hpg, nc),
        in_specs=[seq_spec] * 6 + [pl.BlockSpec((g, 1, HEAD, HEAD), lambda i, c: (i, nc - 1 - c, 0, 0)), seq_spec],
        out_specs=[seq_spec] * 6,
        out_shape=[jax.ShapeDtypeStruct((rows, D), F32)] * 6,
        scratch_shapes=[pltpu.VMEM((g, HEAD, HEAD), F32)],
        compiler_params=_params(2),
    )


def _loss_head(x, target, norm):
    n = x.shape[0]
    tr = _tile(n, 256)

    def f(xv, gv, tv):
        err = _rmsnorm(xv, gv) - tv
        return 0.5 * jnp.sum(jnp.mean(err * err, axis=-1, keepdims=True), axis=0, keepdims=True)

    def body(x_ref, t_ref, g_ref, dx_ref, dg_ref, loss_ref):
        val, vjp = jax.vjp(lambda xv, gv: f(xv, gv, t_ref[...]), x_ref[...], g_ref[...])
        dx, dg = vjp(jnp.ones((1, 1), F32))
        dx_ref[...] = dx

        @pl.when(pl.program_id(0) == 0)
        def _():
            dg_ref[...] = jnp.zeros(dg_ref.shape, F32)
            loss_ref[...] = jnp.zeros(loss_ref.shape, F32)

        dg_ref[...] += dg
        loss_ref[...] += jnp.broadcast_to(val, loss_ref.shape)

    row = pl.BlockSpec((tr, D), lambda i: (i, 0))
    vec = pl.BlockSpec((1, D), lambda i: (0, 0))
    dx, dg, loss = _pcall(
        body, name="loss_head", grid=(n // tr,), in_specs=[row, row, vec],
        out_specs=[row, vec, pl.BlockSpec((8, LANE), lambda i: (0, 0))],
        out_shape=[jax.ShapeDtypeStruct((n, D), F32), jax.ShapeDtypeStruct((1, D), F32),
                   jax.ShapeDtypeStruct((8, LANE), F32)],
        compiler_params=_params(1),
    )(x, target, norm)
    return loss[0, 0], dx, dg


def _exchange(name, items):
    n_x = len(items)

    def body(*refs):
        srcs, dsts, sems = refs[:n_x], refs[n_x:2 * n_x], refs[2 * n_x:]
        started = [_exchange_copies(srcs[i], dsts[i], *sems[3 * i:3 * i + 3], items[i][1]) for i in range(n_x)]
        for mine, copies in started:
            _start_exchange(mine, copies)
        for mine, copies in started:
            _wait_exchange(mine, copies)

    hbm = pl.BlockSpec(memory_space=pl.ANY)
    return _pcall(
        body, name=name, in_specs=[hbm] * n_x, out_specs=[hbm] * n_x,
        out_shape=[_landing_shape(a, sc) for (a, sc) in items], scratch_shapes=_EXCHANGE_SEMS * n_x,
    )(*[a for (a, _) in items])


def _adamw(name, parts, w, m, v):
    rows, cols = w.shape
    tr = _tile(rows, PACK_ROWS)

    def body(p_ref, w_ref, m_ref, v_ref, g_ref, d_ref, nm_ref, nv_ref):
        g = p_ref[0]
        for k in range(1, N_DEV):
            g = g + p_ref[k]
        nm = ADAM_B1 * m_ref[...] + (1.0 - ADAM_B1) * g
        nv = ADAM_B2 * v_ref[...] + (1.0 - ADAM_B2) * jnp.square(g)
        m_hat = nm / (1.0 - ADAM_B1 ** ADAM_STEP)
        v_hat = nv / (1.0 - ADAM_B2 ** ADAM_STEP)
        g_ref[...] = g
        d_ref[...] = -ADAM_LR * (m_hat / (jnp.sqrt(v_hat) + ADAM_EPS) + ADAM_WD * w_ref[...])
        nm_ref[...] = nm
        nv_ref[...] = nv

    row = pl.BlockSpec((tr, cols), lambda i: (i, 0))
    return _pcall(
        body, name=name, grid=(rows // tr,),
        in_specs=[pl.BlockSpec((N_DEV, tr, cols), lambda i: (0, i, 0)), row, row, row],
        out_specs=[row] * 4, out_shape=[jax.ShapeDtypeStruct((rows, cols), F32)] * 4,
        compiler_params=_params(1),
    )(parts, w, m, v)


def _pack(arrs, dtype, lead=()):
    flat = jnp.concatenate([a.astype(dtype).reshape(lead + (-1,)) for a in arrs], axis=-1)
    n = flat.shape[-1]
    quantum = PACK_W * PACK_ROWS
    total = -(-n // quantum) * quantum
    flat = jnp.pad(flat, [(0, 0)] * len(lead) + [(0, total - n)])
    return flat.reshape(lead + (total // PACK_W, PACK_W))


def _unpack(buf, shapes, lead=()):
    flat = buf.reshape(lead + (-1,))
    out, off = [], 0
    for s in shapes:
        n = math.prod(s)
        out.append(flat[..., off:off + n].reshape(lead + tuple(s)))
        off += n
    return out


def _unshard(g, axis):
    local = g.shape[1:]
    full = jnp.moveaxis(g, 0, axis)
    return full.reshape(local[:axis] + (N_DEV * local[axis],) + local[axis + 1:])


def _reshard(full, axis):
    s = full.shape
    blocked = full.reshape(s[:axis] + (N_DEV, s[axis] // N_DEV) + s[axis + 1:])
    return jnp.moveaxis(blocked, axis, 0)


def _row_item(a, tr):
    return (a, (tr, a.shape[1]), lambda i: (i, 0))


def _vec_item(a):
    return (a, a.shape, lambda i: (0,) * a.ndim)


def _seq_item(a, t, width):
    return (a, (t, width), lambda j, b: (b, j))


def _seq_halves(a, t, width):
    half = a.shape[1] // 2
    off = half // width
    grad = ((a.shape[0], half), lambda j, b: (b, j))
    return [(a, (t, width), lambda j, b: (b, j), grad), (a, (t, width), lambda j, b: (b, j + off), grad)]


def _col_item(a, width):
    return (a, (a.shape[0], width), lambda j, b: (0, j))


def _ffn_fwd(tag, x, p, t, carry=None):
    n = x.shape[0]
    tr = _tile(n, 512)
    norm_ins, norm_par = [_row_item(x, tr)], [_vec_item(p["norm"])]
    (h,) = _stage_fwd(f"{tag}_norm", _f_norm, (n // tr,), norm_ins, norm_par, [((n, D), BF16, (tr, D), lambda i: (i, 0))])
    u = _matmul(f"{tag}_up", h, p["w_up"], tm=2048, tn=768, carry=carry)
    grid = (D_FF // CT, n // t)
    core_par = [_col_item(p["conv_w"], CT), _col_item(p["conv_b"], CT)]
    (hid,) = _stage_fwd(f"{tag}_core", _f_ffn_core, grid, _seq_halves(u, t, CT), core_par,
                        [((n, D_FF), BF16, (t, CT), lambda j, b: (b, j))])
    y = _matmul(f"{tag}_down", hid, p["w_down"], resid=x, tm=1024, tn=1024, tk=D_FF)
    return y, (x, h, u, hid)


def _ffn_bwd(tag, dy, saved, p, t, carry=None):
    x, h, u, hid = saved
    n = x.shape[0]
    tr = _tile(n, 512)
    grads = {}
    dhid = _matmul(f"{tag}_down_da", dy, p["w_down"], mode="nt", tm=1024, tn=768)
    grads["w_down"] = _matmul(f"{tag}_down_dw", hid, dy, mode="tn", tm=768, tn=1024, tk=2048)
    grid = (D_FF // CT, n // t)
    core_par = [_col_item(p["conv_w"], CT), _col_item(p["conv_b"], CT)]
    (dug, duv), (grads["conv_w"], grads["conv_b"]) = _stage_bwd(
        f"{tag}_core_bwd", _f_ffn_core, grid, _seq_halves(u, t, CT), core_par,
        [([dhid], (t, CT), lambda j, b: (b, j))], [BF16, BF16], carry=carry)
    dh = _matmul(f"{tag}_up_da_g", dug, p["w_up"], mode="nt", tm=1024, tn=1024, tk=D_FF)
    dh = _matmul(f"{tag}_up_da_v", duv, p["w_up"], mode="nt", tm=1024, tn=1024, tk=D_FF, b_koff=D_FF, resid=dh)
    grads["w_up"] = jnp.concatenate([_matmul(f"{tag}_up_dw_g", h, dug, mode="tn", tm=1024, tn=768, tk=2048),
                                     _matmul(f"{tag}_up_dw_v", h, duv, mode="tn", tm=1024, tn=768, tk=2048)], axis=1)
    (dx,), (grads["norm"],) = _stage_bwd(
        f"{tag}_norm_bwd", _f_norm, (n // tr,), [_row_item(x, tr)], [_vec_item(p["norm"])],
        [([dh], (tr, D), lambda i: (i, 0))], [F32], adds={0: dy})
    return dx, grads


def _lru_fwd(x, p, t, carry=None):
    n = x.shape[0]
    tr = _tile(n, 512)
    row = lambda dt: ((n, D), dt, (tr, D), lambda i: (i, 0))
    h, xb = _stage_fwd("lru_pre", _f_lru_pre, (n // tr,), [_row_item(x, tr)],
                       [_vec_item(p["norm"]), _vec_item(p["b_out"])], [row(BF16), row(F32)])
    u = _matmul("lru_in", h, p["w_in"], tm=1024, tn=1024)
    out, = _stage_fwd("lru_core", _f_lru_core, (D // CT, n // t), _seq_halves(u, t, CT), _lru_core_params(p),
                      [((n, D), BF16, (t, CT), lambda j, b: (b, j))], carry=carry)
    y = _matmul("lru_out", out, p["w_out"], resid=xb, tm=1024, tn=1024)
    return y, (x, h, u, out)


def _lru_core_params(p):
    return [_col_item(p["b_y"], CT), _col_item(p["b_x"], CT), _col_item(p["conv_w"], CT), _col_item(p["conv_b"], CT),
            (p["gate_w"], (2, 1, CT, CT), lambda j, b: (0, j, 0, 0)), _col_item(p["gate_b"], CT), _col_item(p["lam"], CT)]


def _lru_bwd(dy, saved, p, t, carry=None):
    x, h, u, out = saved
    n = x.shape[0]
    tr = _tile(n, 512)
    grads = {}
    dout = _matmul("lru_out_da", dy, p["w_out"], mode="nt", tm=1024, tn=1024)
    grads["w_out"] = _matmul("lru_out_dw", out, dy, mode="tn", tm=1024, tn=1024)
    (duy, dux), core_grads = _stage_bwd("lru_core_bwd", _f_lru_core, (D // CT, n // t), _seq_halves(u, t, CT),
                                        _lru_core_params(p), [([dout], (t, CT), lambda j, b: (b, j))], [BF16, BF16],
                                        carry=carry)
    for k, g in zip(("b_y", "b_x", "conv_w", "conv_b", "gate_w", "gate_b", "lam"), core_grads):
        grads[k] = g
    dh = _matmul("lru_in_da_y", duy, p["w_in"], mode="nt", tm=1024, tn=1024)
    dh = _matmul("lru_in_da_x", dux, p["w_in"], mode="nt", tm=1024, tn=1024, b_koff=D, resid=dh)
    grads["w_in"] = jnp.concatenate([_matmul("lru_in_dw_y", h, duy, mode="tn", tm=1024, tn=1024),
                                     _matmul("lru_in_dw_x", h, dux, mode="tn", tm=1024, tn=1024)], axis=1)
    row = (tr, D), lambda i: (i, 0)
    (dx,), (grads["norm"], grads["b_out"]) = _stage_bwd(
        "lru_pre_bwd", _f_lru_pre, (n // tr,), [_row_item(x, tr)], [_vec_item(p["norm"]), _vec_item(p["b_out"])],
        [([dh], *row), ([dy], *row)], [F32])
    return dx, grads


_RWKV_PRE_VECS = ("w0", "a0", "k_k", "k_a", "w2", "a2", "g2")
_RWKV_POST_VECS = ("ln_w", "ln_b", "r_k")


def _rwkv_fwd(x, p, t, carry=None):
    n = x.shape[0]
    bsz = n // t
    tr = _tile(n, 512)
    ts = _tile(n, 128)
    (h,) = _stage_fwd("rwkv_norm", _f_norm, (n // tr,), [_row_item(x, tr)], [_vec_item(p["norm"])],
                      [((n, D), F32, (tr, D), lambda i: (i, 0))])
    mixed = _stage_fwd("rwkv_mix", _f_rwkv_mix, (D // CT, bsz), [_seq_item(h, t, CT)], [_col_item(p["mix"], CT)],
                       [((n, D), BF16, (t, CT), lambda j, b: (b, j))] * 6)
    xr, xk, xv, xw, xa, xg = mixed
    r = _matmul("rwkv_r", xr, p["w_r"], tm=1024, tn=1024)
    k = _matmul("rwkv_k", xk, p["w_k"], tm=1024, tn=1024)
    v = _matmul("rwkv_v", xv, p["w_v"], tm=1024, tn=1024)
    lw1 = _matmul("rwkv_w1", xw, p["w1"], tm=1024)
    la1 = _matmul("rwkv_a1", xa, p["a1"], tm=1024)
    lg1 = _matmul("rwkv_g1", xg, p["g1"], tm=1024)
    pre_ins = [_row_item(a, ts) for a in (k, lw1, la1, lg1)]
    pre_par = [_vec_item(p[q]) for q in _RWKV_PRE_VECS]
    row = ((n, D), F32, (ts, D), lambda i: (i, 0))
    lw, kf, aa, bb, g = _stage_fwd("rwkv_pre", _f_rwkv_pre, (n // ts,), pre_ins, pre_par, [row] * 5)
    y, zs = _rwkv_scan_fwd([r, lw, kf, v, aa, bb], t, carry=carry)
    post_ins = [_row_item(a, ts) for a in (y, r, kf, v, g)]
    post_par = [_vec_item(p[q]) for q in _RWKV_POST_VECS]
    (yg,) = _stage_fwd("rwkv_post", _f_rwkv_post, (n // ts,), post_ins, post_par,
                       [((n, D), BF16, (ts, D), lambda i: (i, 0))])
    out = _matmul("rwkv_out", yg, p["w_out"], resid=x, tm=1024, tn=1024)
    return out, (x, h, mixed, r, k, v, lw1, la1, lg1, lw, kf, aa, bb, g, zs, y, yg)


def _rwkv_bwd(dout, saved, p, t, carry=None):
    x, h, mixed, r, k, v, lw1, la1, lg1, lw, kf, aa, bb, g, zs, y, yg = saved
    xr, xk, xv, xw, xa, xg = mixed
    n = x.shape[0]
    bsz = n // t
    tr = _tile(n, 512)
    ts = _tile(n, 128)
    grads = {}
    row_s = (ts, D), lambda i: (i, 0)
    dyg = _matmul("rwkv_out_da", dout, p["w_out"], mode="nt", tm=1024, tn=1024)
    grads["w_out"] = _matmul("rwkv_out_dw", yg, dout, mode="tn", tm=1024, tn=1024)
    post_ins = [_row_item(a, ts) for a in (y, r, kf, v, g)]
    post_par = [_vec_item(p[q]) for q in _RWKV_POST_VECS]
    (dy, dr_a, dkf_a, dv_a, dg), post_grads = _stage_bwd(
        "rwkv_post_bwd", _f_rwkv_post, (n // ts,), post_ins, post_par, [([dyg], *row_s)], [F32] * 5)
    grads.update(zip(_RWKV_POST_VECS, post_grads))
    dr_b, dlw, dkf_b, dv_b, daa, dbb = _rwkv_scan_bwd([r, lw, kf, v, aa, bb], zs, dy, t, carry=carry)
    dr = _add("rwkv_dr_sum", dr_a, dr_b)
    dv = _add("rwkv_dv_sum", dv_a, dv_b)
    pre_ins = [_row_item(a, ts) for a in (k, lw1, la1, lg1)]
    pre_par = [_vec_item(p[q]) for q in _RWKV_PRE_VECS]
    (dk, dlw1, dla1, dlg1), pre_grads = _stage_bwd(
        "rwkv_pre_bwd", _f_rwkv_pre, (n // ts,), pre_ins, pre_par,
        [([dlw], *row_s), ([dkf_a, dkf_b], *row_s), ([daa], *row_s), ([dbb], *row_s), ([dg], *row_s)], [F32] * 4)
    grads.update(zip(_RWKV_PRE_VECS, pre_grads))
    dmixed = []
    for tag, xin, dz, wname in (("r", xr, dr, "w_r"), ("k", xk, dk, "w_k"), ("v", xv, dv, "w_v"),
                                ("w1", xw, dlw1, "w1"), ("a1", xa, dla1, "a1"), ("g1", xg, dlg1, "g1")):
        dmixed.append(_matmul(f"rwkv_{tag}_da", dz, p[wname], mode="nt", tm=1024, tn=1024))
        grads[wname] = _matmul(f"rwkv_{tag}_dw", xin, dz, mode="tn", tm=1024, tn=1024)
    seq_blk = (t, CT), lambda j, b: (b, j)
    (dh,), (grads["mix"],) = _stage_bwd(
        "rwkv_mix_bwd", _f_rwkv_mix, (D // CT, bsz), [_seq_item(h, t, CT)], [_col_item(p["mix"], CT)],
        [([a], *seq_blk) for a in dmixed], [F32])
    (dx,), (grads["norm"],) = _stage_bwd(
        "rwkv_norm_bwd", _f_norm, (n // tr,), [_row_item(x, tr)], [_vec_item(p["norm"])],
        [([dh], (tr, D), lambda i: (i, 0))], [F32], adds={0: dout})
    return dx, grads


def _block_diag_gates(gate_w):
    z = jnp.zeros((2, D // CT, HEAD, HEAD), gate_w.dtype)
    even, odd = gate_w[:, 0::2], gate_w[:, 1::2]
    top = jnp.concatenate([even, z], axis=-1)
    bot = jnp.concatenate([z, odd], axis=-1)
    return jnp.concatenate([top, bot], axis=-2)


def _gate_blocks_grad(dg):
    even, odd = dg[:, :, :HEAD, :HEAD], dg[:, :, HEAD:, HEAD:]
    return jnp.stack([even, odd], axis=2).reshape(2, N_HEAD, HEAD, HEAD)


def _step(local, x, target):
    bsz, t, _ = x.shape
    n = bsz * t

    def block(name, pre=""):
        base, _, layer = name.partition("@")
        a = local[pre + base]
        return a[int(layer):int(layer) + 1] if layer else a

    def split(group):
        return [k for k in group if _base(k) in LARGE], [k for k in group if _base(k) not in LARGE]

    def gather_items(group):
        large, small = split(group)
        flat = [block(k).astype(BF16).reshape(-1) if _base(k) in GATHER_BF16
                else lax.bitcast_convert_type(block(k), BF16).reshape(-1) for k in small]
        return [(block(k).astype(BF16), False) for k in large] + [(_pack(flat, BF16), False)]

    def gathered(group, landed):
        large, small = split(group)
        out = {k: _unshard(g, SHARD_AXIS[_base(k)]) for k, g in zip(large, landed)}
        sizes = [(1 if _base(k) in GATHER_BF16 else 2) * math.prod(block(k).shape) for k in small]
        for k, g in zip(small, _unpack(landed[-1], [(s,) for s in sizes], (N_DEV,))):
            if _base(k) not in GATHER_BF16:
                g = lax.bitcast_convert_type(g.reshape(N_DEV, -1, 2), F32)
            out[k] = _unshard(g.reshape((N_DEV,) + block(k).shape), SHARD_AXIS[_base(k)])
        return out

    def scatter_items(group, grads):
        large, small = split(group)
        blocked = {k: _reshard(grads[k], SHARD_AXIS[_base(k)]) for k in group}
        return [(blocked[k], True) for k in large] + [(_pack([blocked[k] for k in small], F32, (N_DEV,)), True)]

    def ffn_params(l, full):
        return dict(norm=local["ffn_norm"][l:l + 1], w_up=full[f"ffn_w_up@{l}"][0], conv_w=full[f"ffn_conv_w@{l}"][0],
                    conv_b=local["ffn_conv_b"][l:l + 1], w_down=full[f"ffn_w_down@{l}"][0])

    def ffn_grads(l, g):
        return {f"ffn_w_up@{l}": g["w_up"][None], f"ffn_w_down@{l}": g["w_down"][None], f"ffn_conv_w@{l}": g["conv_w"][None]}

    full = gathered(GROUPS["lru"], _exchange("gather_lru", gather_items(GROUPS["lru"])))
    lru = dict(norm=local["lru_norm"], b_out=local["lru_b_out"], w_in=full["lru_w_in"][0],
               b_y=local["lru_b_in"][:, :D], b_x=local["lru_b_in"][:, D:], conv_w=full["lru_conv_w"][0],
               conv_b=local["lru_conv_b"], gate_w=_block_diag_gates(local["lru_gate_w"][0]),
               gate_b=local["lru_gate_b"][0].reshape(2, D), lam=local["lru_lambda"], w_out=full["lru_w_out"][0])
    x0 = x.reshape(n, D)
    ride = _Carry(gather_items(GROUPS["ffn0"]))
    x1, s_lru = _lru_fwd(x0, lru, t, carry=ride)
    ffn0 = ffn_params(0, gathered(GROUPS["ffn0"], ride.landed))
    ride = _Carry(gather_items(GROUPS["rwkv"]))
    x2, s_ffn0 = _ffn_fwd("ffn0", x1, ffn0, t, carry=ride)
    full = gathered(GROUPS["rwkv"], ride.landed)
    rwkv = dict(norm=full["rwkv_norm"], mix=full["rwkv_mix"][0], w_r=full["rwkv_w_rkv"][0, 0],
                w_k=full["rwkv_w_rkv"][0, 1], w_v=full["rwkv_w_rkv"][0, 2], w0=full["rwkv_w0"], w1=full["rwkv_w1"][0],
                w2=full["rwkv_w2"][0], a0=full["rwkv_a0"], a1=full["rwkv_a1"][0], a2=full["rwkv_a2"][0],
                g1=full["rwkv_g1"][0], g2=full["rwkv_g2"][0], k_k=full["rwkv_k_k"], k_a=full["rwkv_k_a"],
                r_k=local["rwkv_r_k"].reshape(1, D), ln_w=full["rwkv_ln_w"], ln_b=full["rwkv_ln_b"],
                w_out=full["rwkv_w_out"][0])
    ride = _Carry(gather_items(GROUPS["ffn1"]))
    x3, s_rwkv = _rwkv_fwd(x2, rwkv, t, carry=ride)
    ffn1 = ffn_params(1, gathered(GROUPS["ffn1"], ride.landed))
    x4, s_ffn1 = _ffn_fwd("ffn1", x3, ffn1, t)
    loss, dx4, d_final = _loss_head(x4, target.reshape(n, D), local["final_norm"].reshape(1, D))

    parts = {}
    dx3, g_ffn1 = _ffn_bwd("ffn1", dx4, s_ffn1, ffn1, t)
    ride = _Carry(scatter_items(GROUPS["ffn1"], ffn_grads(1, g_ffn1)))
    dx2, g_rwkv = _rwkv_bwd(dx3, s_rwkv, rwkv, t, carry=ride)
    parts["ffn1"] = ride.landed
    rwkv_grads = {
        "rwkv_norm": g_rwkv["norm"], "rwkv_mix": g_rwkv["mix"][None],
        "rwkv_w_rkv": jnp.stack([g_rwkv["w_r"], g_rwkv["w_k"], g_rwkv["w_v"]])[None],
        "rwkv_w0": g_rwkv["w0"], "rwkv_w1": g_rwkv["w1"][None], "rwkv_w2": g_rwkv["w2"][None],
        "rwkv_a0": g_rwkv["a0"], "rwkv_a1": g_rwkv["a1"][None], "rwkv_a2": g_rwkv["a2"][None],
        "rwkv_g1": g_rwkv["g1"][None], "rwkv_g2": g_rwkv["g2"][None], "rwkv_k_k": g_rwkv["k_k"],
        "rwkv_k_a": g_rwkv["k_a"], "rwkv_ln_w": g_rwkv["ln_w"], "rwkv_ln_b": g_rwkv["ln_b"],
        "rwkv_w_out": g_rwkv["w_out"][None],
    }
    ride = _Carry(scatter_items(GROUPS["rwkv"], rwkv_grads))
    dx1, g_ffn0 = _ffn_bwd("ffn0", dx2, s_ffn0, ffn0, t, carry=ride)
    parts["rwkv"] = ride.landed
    ride = _Carry(scatter_items(GROUPS["ffn0"], ffn_grads(0, g_ffn0)))
    dx0, g_lru = _lru_bwd(dx1, s_lru, lru, t, carry=ride)
    parts["ffn0"] = ride.landed

    g_ffn = [g_ffn0, g_ffn1]
    lru_grads = {"lru_w_in": g_lru["w_in"][None], "lru_conv_w": g_lru["conv_w"][None], "lru_w_out": g_lru["w_out"][None]}
    gfull = {
        "lru_norm": g_lru["norm"], "lru_b_in": jnp.concatenate([g_lru["b_y"], g_lru["b_x"]], axis=1),
        "lru_conv_b": g_lru["conv_b"], "lru_gate_w": _gate_blocks_grad(g_lru["gate_w"])[None],
        "lru_gate_b": g_lru["gate_b"].reshape(1, 2, N_HEAD, HEAD), "lru_lambda": g_lru["lam"],
        "lru_b_out": g_lru["b_out"], "rwkv_r_k": g_rwkv["r_k"].reshape(1, N_HEAD, HEAD),
        "ffn_norm": jnp.concatenate([g["norm"] for g in g_ffn]),
        "ffn_conv_b": jnp.concatenate([g["conv_b"] for g in g_ffn]),
        "final_norm": d_final.reshape(D),
    }
    *parts["lru"], replicated = _exchange(
        "grad_tail", scatter_items(GROUPS["lru"], lru_grads) + [(_pack([gfull[k] for k in REPLICATED], F32), False)])

    out = {}
    kinds = ("grad", "delta", "new_m", "new_v")
    for tag, group in GROUPS.items():
        large, small = split(group)
        for k, landed in zip(large, parts[tag]):
            flat = (-1, block(k).shape[-1])
            res = _adamw(f"adamw_{k.replace('@', '_')}", landed.reshape((N_DEV,) + (math.prod(block(k).shape[:-1]), flat[1])),
                         *[block(k, pre).reshape(flat) for pre in ("", "m_", "v_")])
            for kind, a in zip(kinds, res):
                out[(kind, k)] = a.reshape(block(k).shape)
    for tag, names, landed in [(t_, split(g_)[1], parts[t_][-1]) for t_, g_ in GROUPS.items()] + [("replicated", REPLICATED, replicated)]:
        packs = [_pack([block(k, pre) for k in names], F32) for pre in ("", "m_", "v_")]
        res = _adamw(f"adamw_{tag}", landed, *packs)
        for kind, buf in zip(kinds, res):
            for k, a in zip(names, _unpack(buf, [block(k).shape for k in names])):
                out[(kind, k)] = a
    for kind in ("grad", "delta", "new_m", "new_v"):
        for k in STACKED:
            out[(kind, k)] = jnp.concatenate([out[(kind, f"{k}@0")], out[(kind, f"{k}@1")]], axis=0)
    return loss, dx0.reshape(x.shape), out


def kernel(x, lru_norm, lru_w_in, lru_b_in, lru_conv_w, lru_conv_b, lru_gate_w, lru_gate_b, lru_lambda, lru_w_out, lru_b_out, rwkv_norm, rwkv_mix, rwkv_w_rkv, rwkv_w0, rwkv_w1, rwkv_w2, rwkv_a0, rwkv_a1, rwkv_a2, rwkv_g1, rwkv_g2, rwkv_k_k, rwkv_k_a, rwkv_r_k, rwkv_ln_w, rwkv_ln_b, rwkv_w_out, ffn_norm, ffn_w_up, ffn_conv_w, ffn_conv_b, ffn_w_down, final_norm, loss_target, m_lru_norm, m_lru_w_in, m_lru_b_in, m_lru_conv_w, m_lru_conv_b, m_lru_gate_w, m_lru_gate_b, m_lru_lambda, m_lru_w_out, m_lru_b_out, m_rwkv_norm, m_rwkv_mix, m_rwkv_w_rkv, m_rwkv_w0, m_rwkv_w1, m_rwkv_w2, m_rwkv_a0, m_rwkv_a1, m_rwkv_a2, m_rwkv_g1, m_rwkv_g2, m_rwkv_k_k, m_rwkv_k_a, m_rwkv_r_k, m_rwkv_ln_w, m_rwkv_ln_b, m_rwkv_w_out, m_ffn_norm, m_ffn_w_up, m_ffn_conv_w, m_ffn_conv_b, m_ffn_w_down, m_final_norm, v_lru_norm, v_lru_w_in, v_lru_b_in, v_lru_conv_w, v_lru_conv_b, v_lru_gate_w, v_lru_gate_b, v_lru_lambda, v_lru_w_out, v_lru_b_out, v_rwkv_norm, v_rwkv_mix, v_rwkv_w_rkv, v_rwkv_w0, v_rwkv_w1, v_rwkv_w2, v_rwkv_a0, v_rwkv_a1, v_rwkv_a2, v_rwkv_g1, v_rwkv_g2, v_rwkv_k_k, v_rwkv_k_a, v_rwkv_r_k, v_rwkv_ln_w, v_rwkv_ln_b, v_rwkv_w_out, v_ffn_norm, v_ffn_w_up, v_ffn_conv_w, v_ffn_conv_b, v_ffn_w_down, v_final_norm):
    args = locals()
    local = {(pre + k): args[pre + k] for pre in ("", "m_", "v_") for k in WEIGHTS}
    loss_local, grad_x, out = _step(local, x, loss_target)
    loss = lax.psum(loss_local, ("x", "y", "c"))
    return (loss, grad_x, *[out[(kind, k)] for kind in ("grad", "delta", "new_m", "new_v") for k in WEIGHTS])
```

```python
import functools
import math

import jax
import jax.numpy as jnp
from jax import lax
from jax.experimental import pallas as pl
from jax.experimental.pallas import tpu as pltpu

F32 = jnp.float32
BF16 = jnp.bfloat16
HI = lax.Precision.HIGHEST

N_DEV = 8
D = 1024
HEAD = 64
N_HEAD = D // HEAD
D_FF = 3 * D
LANE = 128
V7X_VMEM_BYTES = 64 * 1024 * 1024
VMEM_LIMIT = V7X_VMEM_BYTES - 8 * 1024 * 1024
CT = LANE
CHUNK = 64
SCAN_HEADS = 16
PACK_W = 1024
PACK_ROWS = 256

ADAM_LR, ADAM_B1, ADAM_B2, ADAM_EPS, ADAM_WD, ADAM_STEP = 0.001, 0.9, 0.999, 1e-08, 0.01, 10
RMS_EPS = 1e-6
GN_EPS = 64e-5
LRU_C = 8.0

WEIGHTS = ['lru_norm', 'lru_w_in', 'lru_b_in', 'lru_conv_w', 'lru_conv_b', 'lru_gate_w', 'lru_gate_b',
           'lru_lambda', 'lru_w_out', 'lru_b_out', 'rwkv_norm', 'rwkv_mix', 'rwkv_w_rkv', 'rwkv_w0', 'rwkv_w1',
           'rwkv_w2', 'rwkv_a0', 'rwkv_a1', 'rwkv_a2', 'rwkv_g1', 'rwkv_g2', 'rwkv_k_k', 'rwkv_k_a', 'rwkv_r_k',
           'rwkv_ln_w', 'rwkv_ln_b', 'rwkv_w_out', 'ffn_norm', 'ffn_w_up', 'ffn_conv_w', 'ffn_conv_b',
           'ffn_w_down', 'final_norm']
SHARD_AXIS = {'lru_w_in': 2, 'lru_conv_w': 2, 'lru_w_out': 1, 'rwkv_norm': 1, 'rwkv_mix': 2, 'rwkv_w_rkv': 2,
              'rwkv_w0': 1, 'rwkv_w1': 1, 'rwkv_w2': 2, 'rwkv_a0': 1, 'rwkv_a1': 1, 'rwkv_a2': 2, 'rwkv_g1': 1,
              'rwkv_g2': 2, 'rwkv_k_k': 1, 'rwkv_k_a': 1, 'rwkv_ln_w': 1, 'rwkv_ln_b': 1, 'rwkv_w_out': 1,
              'ffn_w_up': 2, 'ffn_conv_w': 2, 'ffn_w_down': 1}
GATHER_BF16 = ['lru_w_in', 'lru_w_out', 'rwkv_w_rkv', 'rwkv_w1', 'rwkv_a1', 'rwkv_g1', 'rwkv_w_out', 'ffn_w_up',
               'ffn_w_down']
REPLICATED = [n for n in WEIGHTS if n not in SHARD_AXIS]
STACKED = ['ffn_w_up', 'ffn_w_down', 'ffn_conv_w']
LARGE = ['lru_w_in', 'lru_w_out', 'rwkv_w_rkv', 'rwkv_w_out', 'ffn_w_up', 'ffn_w_down']
GROUPS = {
    "lru": ['lru_w_in', 'lru_w_out', 'lru_conv_w'],
    "ffn0": [f"{n}@0" for n in STACKED],
    "rwkv": [n for n in WEIGHTS if n.startswith("rwkv_") and n in SHARD_AXIS],
    "ffn1": [f"{n}@1" for n in STACKED],
}


def _base(name):
    return name.partition("@")[0]


def _pcall(body, **kw):
    return pl.pallas_call(body, **kw)


class _Carry:
    def __init__(self, items):
        self.items = list(items)
        self.landed = None


def _exchange_copies(src_ref, dst_ref, send_sems, recv_sems, local_sem, scatter):
    x, y, c = lax.axis_index("x"), lax.axis_index("y"), lax.axis_index("c")
    me = 4 * x + 2 * y + c
    mine = pltpu.make_async_copy(src_ref.at[me] if scatter else src_ref, dst_ref.at[me], local_sem)
    copies = []
    for m in range(1, N_DEV):
        px = 1 - x if m & 4 else x
        py = 1 - y if m & 2 else y
        pc = 1 - c if m & 1 else c
        part = src_ref.at[4 * px + 2 * py + pc] if scatter else src_ref
        copies.append(pltpu.make_async_remote_copy(
            src_ref=part, dst_ref=dst_ref.at[me], send_sem=send_sems.at[m - 1], recv_sem=recv_sems.at[m - 1],
            device_id=(px, py, pc), device_id_type=pl.DeviceIdType.MESH))
    return mine, copies


def _start_exchange(mine, copies):
    mine.start()
    for cp in copies:
        cp.start()


def _wait_exchange(mine, copies):
    for cp in copies:
        cp.wait_recv()
    for cp in copies:
        cp.wait_send()
    mine.wait()


_EXCHANGE_SEMS = [pltpu.SemaphoreType.DMA((N_DEV - 1,)), pltpu.SemaphoreType.DMA((N_DEV - 1,)), pltpu.SemaphoreType.DMA]


def _landing_shape(src, scatter):
    return jax.ShapeDtypeStruct((N_DEV,) + tuple(src.shape[1:] if scatter else src.shape), src.dtype)


def _call(body, operands, *, carry=None, name, grid, in_specs, out_specs, out_shape, scratch_shapes=(),
          compiler_params=None):
    if carry is None:
        return _pcall(body, name=name, grid=grid, in_specs=in_specs, out_specs=out_specs, out_shape=out_shape,
                      scratch_shapes=list(scratch_shapes), compiler_params=compiler_params)(*operands)
    single = not isinstance(out_specs, (list, tuple))
    out_specs_l = [out_specs] if single else list(out_specs)
    out_shape_l = [out_shape] if single else list(out_shape)
    n_in, n_out, n_scr, n_x = len(in_specs), len(out_specs_l), len(scratch_shapes), len(carry.items)
    flags = [sc for (_, sc) in carry.items]

    def wrapped(*refs):
        ins, refs = refs[:n_in], refs[n_in:]
        xsrc, refs = refs[:n_x], refs[n_x:]
        outs, refs = refs[:n_out], refs[n_out:]
        xdst, refs = refs[:n_x], refs[n_x:]
        scr, sems = refs[:n_scr], refs[n_scr:]
        first = functools.reduce(jnp.logical_and, [pl.program_id(i) == 0 for i in range(len(grid))])
        last = functools.reduce(jnp.logical_and, [pl.program_id(i) == grid[i] - 1 for i in range(len(grid))])

        def exchanges():
            return [_exchange_copies(xsrc[i], xdst[i], *sems[3 * i:3 * i + 3], flags[i]) for i in range(n_x)]

        @pl.when(first)
        def _():
            for mine, copies in exchanges():
                _start_exchange(mine, copies)

        body(*ins, *outs, *scr)

        @pl.when(last)
        def _():
            for mine, copies in exchanges():
                _wait_exchange(mine, copies)

    hbm = pl.BlockSpec(memory_space=pl.ANY)
    res = _pcall(
        wrapped, name=name, grid=grid, in_specs=list(in_specs) + [hbm] * n_x,
        out_specs=out_specs_l + [hbm] * n_x,
        out_shape=out_shape_l + [_landing_shape(a, sc) for (a, sc) in carry.items],
        scratch_shapes=list(scratch_shapes) + _EXCHANGE_SEMS * n_x, compiler_params=compiler_params,
    )(*operands, *[a for (a, _) in carry.items])
    carry.landed = list(res[n_out:])
    return res[0] if single else list(res[:n_out])


def _params(n_grid):
    return pltpu.CompilerParams(dimension_semantics=("arbitrary",) * n_grid, vmem_limit_bytes=VMEM_LIMIT)


def _shift_rows(x, d, up):
    n = x.shape[0]
    idx = lax.broadcasted_iota(jnp.int32, x.shape, 0)
    if up:
        return jnp.where(idx < n - d, pltpu.roll(x, n - d, 0), 0.0)
    return jnp.where(idx >= d, pltpu.roll(x, d, 0), 0.0)


@functools.partial(jax.custom_vjp, nondiff_argnums=(1,))
def _shift_down(x, d):
    return _shift_rows(x, d, False)


def _shift_down_fwd(x, d):
    return _shift_rows(x, d, False), None


def _shift_down_bwd(d, _, g):
    return (_shift_rows(g, d, True),)


_shift_down.defvjp(_shift_down_fwd, _shift_down_bwd)


def _scan_doubling(a, b, up):
    n = a.shape[0]
    d = 1
    while d < n:
        b = b + a * _shift_rows(b, d, up)
        a = a * _shift_rows(a, d, up)
        d *= 2
    return b


@jax.custom_vjp
def _linear_scan(a, b):
    return _scan_doubling(a, b, False)


def _linear_scan_fwd(a, b):
    h = _scan_doubling(a, b, False)
    return h, (a, h)


def _linear_scan_bwd(res, dh):
    a, h = res
    g = _scan_doubling(_shift_rows(a, 1, True), dh, True)
    return g * _shift_rows(h, 1, False), g


_linear_scan.defvjp(_linear_scan_fwd, _linear_scan_bwd)


def _causal_conv(x, w, b):
    k = w.shape[0]
    out = b + x * w[k - 1:k]
    for j in range(k - 1):
        out = out + _shift_down(x, k - 1 - j) * w[j:j + 1]
    return out


_GELU_C1 = math.sqrt(2.0 / math.pi)
_GELU_C2 = 0.044715 * _GELU_C1


@jax.custom_vjp
def _gelu(x):
    return 0.5 * x * (1.0 + jnp.tanh(x * (_GELU_C1 + _GELU_C2 * (x * x))))


def _gelu_fwd(x):
    x2 = x * x
    t = jnp.tanh(x * (_GELU_C1 + _GELU_C2 * x2))
    return 0.5 * x * (1.0 + t), (x, x2, t)


def _gelu_bwd(res, g):
    x, x2, t = res
    return (g * (0.5 * (1.0 + t) + (0.5 * x) * (1.0 - t * t) * (_GELU_C1 + (3.0 * _GELU_C2) * x2)),)


_gelu.defvjp(_gelu_fwd, _gelu_bwd)


def _rmsnorm(x, g):
    return x * lax.rsqrt(jnp.mean(x * x, axis=-1, keepdims=True) + RMS_EPS) * g


def _neg_expm1(x):
    series = x * (1.0 + x * 0.5 * (1.0 + x * (1.0 / 3.0) * (1.0 + x * 0.25 * (1.0 + x * 0.2))))
    return -jnp.where(x > -0.1, series, jnp.exp(x) - 1.0)


def _dot(a, b, ca=1, cb=0, precision=None):
    return lax.dot_general(a, b, (((ca,), (cb,)), ((), ())), precision=precision, preferred_element_type=F32)


def _bdot(a, b):
    return _dot(a.astype(BF16), b.astype(BF16))


def _split_bf16(x):
    hi = x.astype(BF16)
    return hi, (x - hi.astype(F32)).astype(BF16)


def _head_sum_impl(x):
    row = lax.broadcasted_iota(jnp.int32, (D, LANE), 0)
    col = lax.broadcasted_iota(jnp.int32, (D, LANE), 1)
    e = (lax.shift_right_logical(row, 6) == col).astype(BF16)
    hi, lo = _split_bf16(x)
    s_hi, s_lo = _split_bf16(_dot(hi, e) + _dot(lo, e))
    return _dot(s_hi, e, 1, 1) + _dot(s_lo, e, 1, 1)


@jax.custom_vjp
def _head_sum(x):
    return _head_sum_impl(x)


def _head_sum_fwd(x):
    return _head_sum_impl(x), None


def _head_sum_bwd(_, g):
    return (_head_sum(g),)


_head_sum.defvjp(_head_sum_fwd, _head_sum_bwd)


def _specs(items):
    return [pl.BlockSpec(it[1], it[2]) for it in items]


def _stage_fwd(name, f, grid, ins, params, outs, carry=None):
    n_in = len(ins) + len(params)

    def body(*refs):
        res = f(*[r[...] for r in refs[:n_in]])
        for o, v in zip(refs[n_in:], res):
            o[...] = v.astype(o.dtype)

    return _call(
        body, [it[0] for it in ins + params], carry=carry, name=name, grid=grid, in_specs=_specs(ins + params),
        out_specs=[pl.BlockSpec(bs, im) for (_, _, bs, im) in outs],
        out_shape=[jax.ShapeDtypeStruct(s, dt) for (s, dt, _, _) in outs],
        compiler_params=_params(len(grid)),
    )


def _stage_bwd(name, f, grid, ins, params, douts, din_dtypes, adds=None, carry=None):
    adds = adds or {}
    n_in, n_par = len(ins), len(params)
    dout_items = [(a, bs, im) for (arrs, bs, im) in douts for a in arrs]
    add_items = [(adds[i], ins[i][1], ins[i][2]) for i in sorted(adds)]
    n_do, n_add = len(dout_items), len(add_items)

    def body(*refs):
        vals = [r[...] for r in refs[:n_in + n_par]]
        do_refs = list(refs[n_in + n_par:n_in + n_par + n_do])
        add_refs = dict(zip(sorted(adds), refs[n_in + n_par + n_do:n_in + n_par + n_do + n_add]))
        din_refs = refs[n_in + n_par + n_do + n_add:n_in + n_par + n_do + n_add + n_in]
        dpar_refs = refs[n_in + n_par + n_do + n_add + n_in:]
        cts = []
        for (arrs, _, _) in douts:
            ct = do_refs.pop(0)[...].astype(F32)
            for _ in arrs[1:]:
                ct = ct + do_refs.pop(0)[...].astype(F32)
            cts.append(ct)
        _, vjp = jax.vjp(f, *vals)
        grads = vjp(tuple(cts))
        for i, r in enumerate(din_refs):
            g = grads[i]
            if i in add_refs:
                g = g + add_refs[i][...]
            r[...] = g.astype(r.dtype)

        @pl.when(pl.program_id(len(grid) - 1) == 0)
        def _():
            for r in dpar_refs:
                r[...] = jnp.zeros(r.shape, r.dtype)

        for j, r in enumerate(dpar_refs):
            r[...] += grads[n_in + j]

    din_shapes = [it[3][0] if len(it) > 3 else it[0].shape for it in ins]
    din_specs = [pl.BlockSpec(it[1], it[3][1] if len(it) > 3 else it[2]) for it in ins]
    res = _call(
        body, [it[0] for it in ins + params + dout_items + add_items], carry=carry, name=name, grid=grid,
        in_specs=_specs(ins + params + dout_items + add_items), out_specs=din_specs + _specs(params),
        out_shape=[jax.ShapeDtypeStruct(s, dt) for s, dt in zip(din_shapes, din_dtypes)]
        + [jax.ShapeDtypeStruct(it[0].shape, F32) for it in params],
        compiler_params=_params(len(grid)),
    )
    return list(res[:n_in]), list(res[n_in:])


def _tile(n, want):
    t = min(n, want)
    while n % t:
        t //= 2
    return t


def _matmul(name, a, b, *, mode="nn", resid=None, out_dtype=F32, tm=512, tn=512, tk=1024, b_koff=0, carry=None):
    if mode == "tn":
        (kdim, m), n = a.shape, b.shape[1]
    else:
        (m, kdim), n = a.shape, (b.shape[1] if mode == "nn" else b.shape[0])
    tm, tn, tk = _tile(m, tm), _tile(n, tn), _tile(kdim, tk)
    nk = kdim // tk
    ko = b_koff // tk
    assert ko * tk == b_koff
    a_spec = pl.BlockSpec((tk, tm), lambda i, j, k: (k, i)) if mode == "tn" else pl.BlockSpec((tm, tk), lambda i, j, k: (i, k))
    b_spec = pl.BlockSpec((tn, tk), lambda i, j, k: (j, k + ko)) if mode == "nt" else pl.BlockSpec((tk, tn), lambda i, j, k: (k + ko, j))
    ca = 0 if mode == "tn" else 1
    cb = 1 if mode == "nt" else 0
    operands = [a, b]
    in_specs = [a_spec, b_spec]
    if resid is not None:
        operands.append(resid)
        in_specs.append(pl.BlockSpec((tm, tn), lambda i, j, k: (i, j)))

    def finish(r, refs, o_ref):
        if resid is not None:
            r = r + refs[2][...]
        o_ref[...] = r.astype(o_ref.dtype)

    def body_one_step(*refs):
        finish(_dot(refs[0][...].astype(BF16), refs[1][...].astype(BF16), ca, cb), refs, refs[-1])

    def body(*refs):
        a_ref, b_ref = refs[0], refs[1]
        o_ref, acc_ref = refs[-2], refs[-1]
        k = pl.program_id(2)

        @pl.when(k == 0)
        def _():
            acc_ref[...] = jnp.zeros(acc_ref.shape, F32)

        acc_ref[...] += _dot(a_ref[...].astype(BF16), b_ref[...].astype(BF16), ca, cb)

        @pl.when(k == nk - 1)
        def _():
            finish(acc_ref[...], refs, o_ref)

    return _call(
        body_one_step if nk == 1 else body, operands, carry=carry, name=name, grid=(m // tm, n // tn, nk),
        in_specs=in_specs, out_specs=pl.BlockSpec((tm, tn), lambda i, j, k: (i, j)),
        out_shape=jax.ShapeDtypeStruct((m, n), out_dtype),
        scratch_shapes=[] if nk == 1 else [pltpu.VMEM((tm, tn), F32)],
        compiler_params=_params(3),
    )


def _add(name, a, b):
    rows, cols = a.shape
    tr = _tile(rows, 512)

    def body(a_ref, b_ref, o_ref):
        o_ref[...] = a_ref[...] + b_ref[...]

    spec = pl.BlockSpec((tr, cols), lambda i: (i, 0))
    return _pcall(body, name=name, grid=(rows // tr,), in_specs=[spec, spec], out_specs=spec,
                  out_shape=jax.ShapeDtypeStruct(a.shape, a.dtype), compiler_params=_params(1))(a, b)


def _f_lru_pre(x, norm, b_out):
    return _rmsnorm(x, norm), x + b_out


def _f_lru_core(uy, ux, b_y, b_x, cw, cb, gw, gb, lam):
    yb = _gelu(uy + b_y)
    xr = _causal_conv(ux + b_x, cw, cb)
    gr = jax.nn.sigmoid(_bdot(xr, gw[0, 0]) + gb[0:1])
    gi = jax.nn.sigmoid(_bdot(xr, gw[1, 0]) + gb[1:2])
    log_a = -LRU_C * gr * jax.nn.softplus(-lam)
    a = jnp.exp(log_a)
    bterm = jnp.sqrt(_neg_expm1(2.0 * log_a)) * (gi * xr)
    return (_linear_scan(a, bterm) * yb,)


def _f_norm(x, norm):
    return (_rmsnorm(x, norm),)


def _f_ffn_core(ug, uv, cw, cb):
    return (_gelu(_causal_conv(ug, cw, cb)) * uv,)


def _f_rwkv_mix(h, mix):
    xx = _shift_down(h, 1) - h
    return tuple(h + xx * mix[i:i + 1] for i in range(6))


def _f_rwkv_pre(k, lw1, la1, lg1, w0, a0, k_k, k_a, w2, a2, g2):
    wpre = w0 + _bdot(jnp.tanh(lw1), w2)
    apre = a0 + _bdot(la1, a2)
    g = _bdot(jax.nn.sigmoid(lg1), g2)
    log_decay = -jnp.exp(-jax.nn.softplus(-wpre) - 0.5)
    a = jax.nn.sigmoid(apre)
    kk = k * k_k
    kk = kk / jnp.maximum(jnp.sqrt(_head_sum(kk * kk)), 1e-12)
    kf = k * (1.0 + (a - 1.0) * k_a)
    return log_decay, kf, -kk, kk * a, g


def _f_rwkv_post(y, r, kf, v, g, ln_w, ln_b, r_k):
    inv = 1.0 / HEAD
    yc = y - _head_sum(y) * inv
    var = _head_sum(yc * yc) * inv
    yn = yc * lax.rsqrt(var + GN_EPS) * ln_w + ln_b
    bonus = _head_sum(r * kf * r_k) * v
    return ((yn + bonus) * g,)


def _hdot(a, b, ca, cb):
    return lax.dot_general(a, b, (((ca,), (cb,)), ((0,), (0,))), preferred_element_type=F32)


def _hmm_impl(a, b, ca, cb):
    return _hdot(a.astype(BF16), b.astype(BF16), ca, cb)


@functools.partial(jax.custom_vjp, nondiff_argnums=(2, 3))
def _hmm(a, b, ca, cb):
    return _hmm_impl(a, b, ca, cb)


def _hmm_fwd(a, b, ca, cb):
    return _hmm_impl(a, b, ca, cb), (a, b)


def _hmm_bwd(ca, cb, res, dc):
    a, b = res
    fa, fb = 3 - ca, 3 - cb
    da = _hmm(dc, b, 2, fb) if ca == 2 else _hmm(b, dc, fb, 2)
    db = _hmm(a, dc, fa, 1) if cb == 1 else _hmm(dc, a, 1, fa)
    return da, db


_hmm.defvjp(_hmm_fwd, _hmm_bwd)


def _tri_dot_impl(x, upper):
    g, n, _ = x.shape
    row = lax.broadcasted_iota(jnp.int32, (g, n, 3 * n), 1)
    col = lax.broadcasted_iota(jnp.int32, (g, n, 3 * n), 2)
    col = jnp.where(col >= 2 * n, col - 2 * n, jnp.where(col >= n, col - n, col))
    tri = (row <= col if upper else row >= col).astype(BF16)
    hi = x.astype(BF16)
    rem = x - hi.astype(F32)
    mid = rem.astype(BF16)
    lo = (rem - mid.astype(F32)).astype(BF16)
    return _hdot(tri, jnp.concatenate([hi, mid, lo], axis=1), 2, 1)


@functools.partial(jax.custom_vjp, nondiff_argnums=(1,))
def _tri_dot(x, upper):
    return _tri_dot_impl(x, upper)


def _tri_dot_fwd(x, upper):
    return _tri_dot_impl(x, upper), None


def _tri_dot_bwd(upper, _, g):
    return (_tri_dot(g, not upper),)


_tri_dot.defvjp(_tri_dot_fwd, _tri_dot_bwd)


def _unit_lower_inverse_impl(a):
    g, n, _ = a.shape
    row = lax.broadcasted_iota(jnp.int32, (g, n, n), 1)
    col = lax.broadcasted_iota(jnp.int32, (g, n, n), 2)
    inv = (row == col).astype(F32) + a
    p = a
    for _ in range(int(math.log2(n)) - 1):
        p = _hmm(p, p, 2, 1)
        inv = inv + _hmm(inv, p, 2, 1)
    return inv


@jax.custom_vjp
def _unit_lower_inverse(a):
    return _unit_lower_inverse_impl(a)


def _unit_lower_inverse_fwd(a):
    inv = _unit_lower_inverse_impl(a)
    return inv, inv


def _unit_lower_inverse_bwd(inv, d_inv):
    return (_hmm(_hmm(inv, d_inv, 1, 1), inv, 2, 2),)


_unit_lower_inverse.defvjp(_unit_lower_inverse_fwd, _unit_lower_inverse_bwd)


def _rwkv_chunk(z0, r, lw, k, v, a, b):
    g, n, _ = r.shape
    row = lax.broadcasted_iota(jnp.int32, (g, n, n), 1)
    col = lax.broadcasted_iota(jnp.int32, (g, n, n), 2)
    incl, strict = row >= col, row > col
    cs = _tri_dot(lw, False)
    c_last = cs[:, n - 1:n]
    inv = jnp.exp(-cs)
    ar = jnp.concatenate([a * jnp.exp(cs - lw), r * jnp.exp(cs)], axis=1)
    bk = jnp.concatenate([b * inv, k * inv], axis=1)
    pair = _hmm(ar, bk, 2, 2)
    a_ab = jnp.where(strict, pair[:, :n, :n], 0.0)
    a_ak = jnp.where(strict, pair[:, :n, n:], 0.0)
    a_rbk = jnp.concatenate([jnp.where(incl, pair[:, n:, :n], 0.0), jnp.where(incl, pair[:, n:, n:], 0.0)], axis=2)
    arz = _hmm(ar, z0, 2, 1)
    u = _hmm(_unit_lower_inverse(a_ab), arz[:, :n] + _hmm(a_ak, v, 2, 1), 2, 1)
    uv = jnp.concatenate([u, v], axis=1)
    y = arz[:, n:] + _hmm(a_rbk, uv, 2, 1)
    tail = jnp.exp(c_last - cs)
    er = lax.broadcasted_iota(jnp.int32, (g, HEAD, HEAD), 1)
    ec = lax.broadcasted_iota(jnp.int32, (g, HEAD, HEAD), 2)
    decay_all = jnp.where(er == ec, jnp.exp(c_last), 0.0)
    z_l = _hmm(jnp.concatenate([b * tail, k * tail, decay_all], axis=1), jnp.concatenate([uv, z0], axis=1), 1, 1)
    return y, z_l


def _heads_in(ref, g):
    x = ref[...]
    return jnp.stack([x[:, h * HEAD:(h + 1) * HEAD] for h in range(g)], axis=0)


def _heads_out(ref, x):
    ref[...] = jnp.concatenate([x[h] for h in range(x.shape[0])], axis=1)


def _scan_geometry(n_rows, t):
    g, n = _tile(N_HEAD, SCAN_HEADS), _tile(t, CHUNK)
    return g, n, t // n, N_HEAD // g, n_rows // t


def _rwkv_scan_fwd(seqs, t, carry=None):
    rows = seqs[0].shape[0]
    g, n, nc, hpg, bsz = _scan_geometry(rows, t)

    def body(r_ref, lw_ref, k_ref, v_ref, a_ref, b_ref, y_ref, zs_ref, z_ref):
        @pl.when(pl.program_id(1) == 0)
        def _():
            z_ref[...] = jnp.zeros(z_ref.shape, F32)

        z0 = z_ref[...]
        zs_ref[:, 0] = z0
        y, z_l = _rwkv_chunk(z0, *[_heads_in(ref, g) for ref in (r_ref, lw_ref, k_ref, v_ref, a_ref, b_ref)])
        _heads_out(y_ref, y)
        z_ref[...] = z_l

    seq_spec = pl.BlockSpec((n, g * HEAD), lambda i, c: ((i // hpg) * nc + c, i % hpg))
    return _call(
        body, list(seqs), carry=carry, name="rwkv_scan_fwd", grid=(bsz * hpg, nc), in_specs=[seq_spec] * 6,
        out_specs=[seq_spec, pl.BlockSpec((g, 1, HEAD, HEAD), lambda i, c: (i, c, 0, 0))],
        out_shape=[jax.ShapeDtypeStruct((rows, D), F32), jax.ShapeDtypeStruct((bsz * N_HEAD, nc, HEAD, HEAD), F32)],
        scratch_shapes=[pltpu.VMEM((g, HEAD, HEAD), F32)],
        compiler_params=_params(2),
    )


def _rwkv_scan_bwd(seqs, zs, dy, t, carry=None):
    rows = seqs[0].shape[0]
    g, n, nc, hpg, bsz = _scan_geometry(rows, t)

    def body(r_ref, lw_ref, k_ref, v_ref, a_ref, b_ref, zs_ref, dy_ref, dr, dlw, dk, dv, da, db, dz_ref):
        @pl.when(pl.program_id(1) == 0)
        def _():
            dz_ref[...] = jnp.zeros(dz_ref.shape, F32)

        _, vjp = jax.vjp(_rwkv_chunk, zs_ref[:, 0],
                         *[_heads_in(ref, g) for ref in (r_ref, lw_ref, k_ref, v_ref, a_ref, b_ref)])
        grads = vjp((_heads_in(dy_ref, g), dz_ref[...]))
        dz_ref[...] = grads[0]
        for o, gr in zip((dr, dlw, dk, dv, da, db), grads[1:]):
            _heads_out(o, gr)

    seq_spec = pl.BlockSpec((n, g * HEAD), lambda i, c: ((i // hpg) * nc + nc - 1 - c, i % hpg))
    return _call(
        body, [*seqs, zs, dy], carry=carry, name="rwkv_scan_bwd", grid=(bsz * hpg, nc),
        in_specs=[seq_spec] * 6 + [pl.BlockSpec((g, 1, HEAD, HEAD), lambda i, c: (i, nc - 1 - c, 0, 0)), seq_spec],
        out_specs=[seq_spec] * 6,
        out_shape=[jax.ShapeDtypeStruct((rows, D), F32)] * 6,
        scratch_shapes=[pltpu.VMEM((g, HEAD, HEAD), F32)],
        compiler_params=_params(2),
    )


def _loss_head(x, target, norm):
    n = x.shape[0]
    tr = _tile(n, 256)

    def f(xv, gv, tv):
        err = _rmsnorm(xv, gv) - tv
        return 0.5 * jnp.sum(jnp.mean(err * err, axis=-1, keepdims=True), axis=0, keepdims=True)

    def body(x_ref, t_ref, g_ref, dx_ref, dg_ref, loss_ref):
        val, vjp = jax.vjp(lambda xv, gv: f(xv, gv, t_ref[...]), x_ref[...], g_ref[...])
        dx, dg = vjp(jnp.ones((1, 1), F32))
        dx_ref[...] = dx

        @pl.when(pl.program_id(0) == 0)
        def _():
            dg_ref[...] = jnp.zeros(dg_ref.shape, F32)
            loss_ref[...] = jnp.zeros(loss_ref.shape, F32)

        dg_ref[...] += dg
        loss_ref[...] += jnp.broadcast_to(val, loss_ref.shape)

    row = pl.BlockSpec((tr, D), lambda i: (i, 0))
    vec = pl.BlockSpec((1, D), lambda i: (0, 0))
    dx, dg, loss = _pcall(
        body, name="loss_head", grid=(n // tr,), in_specs=[row, row, vec],
        out_specs=[row, vec, pl.BlockSpec((8, LANE), lambda i: (0, 0))],
        out_shape=[jax.ShapeDtypeStruct((n, D), F32), jax.ShapeDtypeStruct((1, D), F32),
                   jax.ShapeDtypeStruct((8, LANE), F32)],
        compiler_params=_params(1),
    )(x, target, norm)
    return loss[0, 0], dx, dg


def _exchange(name, items):
    n_x = len(items)

    def body(*refs):
        srcs, dsts, sems = refs[:n_x], refs[n_x:2 * n_x], refs[2 * n_x:]
        started = [_exchange_copies(srcs[i], dsts[i], *sems[3 * i:3 * i + 3], items[i][1]) for i in range(n_x)]
        for mine, copies in started:
            _start_exchange(mine, copies)
        for mine, copies in started:
            _wait_exchange(mine, copies)

    hbm = pl.BlockSpec(memory_space=pl.ANY)
    return _pcall(
        body, name=name, in_specs=[hbm] * n_x, out_specs=[hbm] * n_x,
        out_shape=[_landing_shape(a, sc) for (a, sc) in items], scratch_shapes=_EXCHANGE_SEMS * n_x,
    )(*[a for (a, _) in items])


def _adamw(name, parts, w, m, v):
    rows, cols = w.shape
    tr = _tile(rows, PACK_ROWS)

    def body(p_ref, w_ref, m_ref, v_ref, g_ref, d_ref, nm_ref, nv_ref):
        g = p_ref[0]
        for k in range(1, N_DEV):
            g = g + p_ref[k]
        nm = ADAM_B1 * m_ref[...] + (1.0 - ADAM_B1) * g
        nv = ADAM_B2 * v_ref[...] + (1.0 - ADAM_B2) * jnp.square(g)
        m_hat = nm / (1.0 - ADAM_B1 ** ADAM_STEP)
        v_hat = nv / (1.0 - ADAM_B2 ** ADAM_STEP)
        g_ref[...] = g
        d_ref[...] = -ADAM_LR * (m_hat / (jnp.sqrt(v_hat) + ADAM_EPS) + ADAM_WD * w_ref[...])
        nm_ref[...] = nm
        nv_ref[...] = nv

    row = pl.BlockSpec((tr, cols), lambda i: (i, 0))
    return _pcall(
        body, name=name, grid=(rows // tr,),
        in_specs=[pl.BlockSpec((N_DEV, tr, cols), lambda i: (0, i, 0)), row, row, row],
        out_specs=[row] * 4, out_shape=[jax.ShapeDtypeStruct((rows, cols), F32)] * 4,
        compiler_params=_params(1),
    )(parts, w, m, v)


def _pack(arrs, dtype, lead=()):
    flat = jnp.concatenate([a.astype(dtype).reshape(lead + (-1,)) for a in arrs], axis=-1)
    n = flat.shape[-1]
    quantum = PACK_W * PACK_ROWS
    total = -(-n // quantum) * quantum
    flat = jnp.pad(flat, [(0, 0)] * len(lead) + [(0, total - n)])
    return flat.reshape(lead + (total // PACK_W, PACK_W))


def _unpack(buf, shapes, lead=()):
    flat = buf.reshape(lead + (-1,))
    out, off = [], 0
    for s in shapes:
        n = math.prod(s)
        out.append(flat[..., off:off + n].reshape(lead + tuple(s)))
        off += n
    return out


def _unshard(g, axis):
    local = g.shape[1:]
    full = jnp.moveaxis(g, 0, axis)
    return full.reshape(local[:axis] + (N_DEV * local[axis],) + local[axis + 1:])


def _reshard(full, axis):
    s = full.shape
    blocked = full.reshape(s[:axis] + (N_DEV, s[axis] // N_DEV) + s[axis + 1:])
    return jnp.moveaxis(blocked, axis, 0)


def _row_item(a, tr):
    return (a, (tr, a.shape[1]), lambda i: (i, 0))


def _vec_item(a):
    return (a, a.shape, lambda i: (0,) * a.ndim)


def _seq_item(a, t, width):
    return (a, (t, width), lambda j, b: (b, j))


def _seq_halves(a, t, width):
    half = a.shape[1] // 2
    off = half // width
    grad = ((a.shape[0], half), lambda j, b: (b, j))
    return [(a, (t, width), lambda j, b: (b, j), grad), (a, (t, width), lambda j, b: (b, j + off), grad)]


def _col_item(a, width):
    return (a, (a.shape[0], width), lambda j, b: (0, j))


def _ffn_fwd(tag, x, p, t, carries=None, late=None):
    carries = carries or {}
    n = x.shape[0]
    tr = _tile(n, 512)
    norm_ins, norm_par = [_row_item(x, tr)], [_vec_item(p["norm"])]
    (h,) = _stage_fwd(f"{tag}_norm", _f_norm, (n // tr,), norm_ins, norm_par, [((n, D), BF16, (tr, D), lambda i: (i, 0))])
    u = _matmul(f"{tag}_up", h, p["w_up"], tm=2048, tn=768, carry=carries.get("up"))
    grid = (D_FF // CT, n // t)
    core_par = [_col_item(p["conv_w"], CT), _col_item(p["conv_b"], CT)]
    (hid,) = _stage_fwd(f"{tag}_core", _f_ffn_core, grid, _seq_halves(u, t, CT), core_par,
                        [((n, D_FF), BF16, (t, CT), lambda j, b: (b, j))], carry=carries.get("core"))
    if late is not None:
        late()
    y = _matmul(f"{tag}_down", hid, p["w_down"], resid=x, tm=1024, tn=1024, tk=D_FF)
    return y, (x, h, u, hid)


def _ffn_bwd(tag, dy, saved, p, t, carries=None, down_rides=None):
    carries = carries or {}
    x, h, u, hid = saved
    n = x.shape[0]
    tr = _tile(n, 512)
    grads = {}
    dhid = _matmul(f"{tag}_down_da", dy, p["w_down"], mode="nt", tm=1024, tn=768, carry=carries.get("down_da"))
    grads["w_down"] = _matmul(f"{tag}_down_dw", hid, dy, mode="tn", tm=768, tn=1024, tk=2048)
    after = down_rides(grads["w_down"]) if down_rides is not None else [None] * 3
    grid = (D_FF // CT, n // t)
    core_par = [_col_item(p["conv_w"], CT), _col_item(p["conv_b"], CT)]
    (dug, duv), (grads["conv_w"], grads["conv_b"]) = _stage_bwd(
        f"{tag}_core_bwd", _f_ffn_core, grid, _seq_halves(u, t, CT), core_par,
        [([dhid], (t, CT), lambda j, b: (b, j))], [BF16, BF16], carry=carries.get("core"))
    dh = _matmul(f"{tag}_up_da_g", dug, p["w_up"], mode="nt", tm=1024, tn=1024, tk=D_FF, carry=after[0])
    dh = _matmul(f"{tag}_up_da_v", duv, p["w_up"], mode="nt", tm=1024, tn=1024, tk=D_FF, b_koff=D_FF, resid=dh,
                 carry=after[1])
    grads["w_up"] = jnp.concatenate([
        _matmul(f"{tag}_up_dw_g", h, dug, mode="tn", tm=1024, tn=768, tk=2048, carry=after[2]),
        _matmul(f"{tag}_up_dw_v", h, duv, mode="tn", tm=1024, tn=768, tk=2048)], axis=1)
    (dx,), (grads["norm"],) = _stage_bwd(
        f"{tag}_norm_bwd", _f_norm, (n // tr,), [_row_item(x, tr)], [_vec_item(p["norm"])],
        [([dh], (tr, D), lambda i: (i, 0))], [F32], adds={0: dy})
    return dx, grads


def _lru_fwd(x, p, t, carries=None, late=None):
    carries = carries or {}
    n = x.shape[0]
    tr = _tile(n, 512)
    row = lambda dt: ((n, D), dt, (tr, D), lambda i: (i, 0))
    h, xb = _stage_fwd("lru_pre", _f_lru_pre, (n // tr,), [_row_item(x, tr)],
                       [_vec_item(p["norm"]), _vec_item(p["b_out"])], [row(BF16), row(F32)], carry=carries.get("pre"))
    if late is not None:
        late()
    u = _matmul("lru_in", h, p["w_in"], tm=1024, tn=1024)
    out, = _stage_fwd("lru_core", _f_lru_core, (D // CT, n // t), _seq_halves(u, t, CT), _lru_core_params(p),
                      [((n, D), BF16, (t, CT), lambda j, b: (b, j))], carry=carries.get("core"))
    y = _matmul("lru_out", out, p["w_out"], resid=xb, tm=1024, tn=1024)
    return y, (x, h, u, out)


def _lru_core_params(p):
    return [_col_item(p["b_y"], CT), _col_item(p["b_x"], CT), _col_item(p["conv_w"], CT), _col_item(p["conv_b"], CT),
            (p["gate_w"], (2, 1, CT, CT), lambda j, b: (0, j, 0, 0)), _col_item(p["gate_b"], CT), _col_item(p["lam"], CT)]


def _lru_bwd(dy, saved, p, t, carry=None):
    x, h, u, out = saved
    n = x.shape[0]
    tr = _tile(n, 512)
    grads = {}
    dout = _matmul("lru_out_da", dy, p["w_out"], mode="nt", tm=1024, tn=1024)
    grads["w_out"] = _matmul("lru_out_dw", out, dy, mode="tn", tm=1024, tn=1024)
    (duy, dux), core_grads = _stage_bwd("lru_core_bwd", _f_lru_core, (D // CT, n // t), _seq_halves(u, t, CT),
                                        _lru_core_params(p), [([dout], (t, CT), lambda j, b: (b, j))], [BF16, BF16],
                                        carry=carry)
    for k, g in zip(("b_y", "b_x", "conv_w", "conv_b", "gate_w", "gate_b", "lam"), core_grads):
        grads[k] = g
    dh = _matmul("lru_in_da_y", duy, p["w_in"], mode="nt", tm=1024, tn=1024)
    dh = _matmul("lru_in_da_x", dux, p["w_in"], mode="nt", tm=1024, tn=1024, b_koff=D, resid=dh)
    grads["w_in"] = jnp.concatenate([_matmul("lru_in_dw_y", h, duy, mode="tn", tm=1024, tn=1024),
                                     _matmul("lru_in_dw_x", h, dux, mode="tn", tm=1024, tn=1024)], axis=1)
    row = (tr, D), lambda i: (i, 0)
    (dx,), (grads["norm"], grads["b_out"]) = _stage_bwd(
        "lru_pre_bwd", _f_lru_pre, (n // tr,), [_row_item(x, tr)], [_vec_item(p["norm"]), _vec_item(p["b_out"])],
        [([dh], *row), ([dy], *row)], [F32])
    return dx, grads


_RWKV_PRE_VECS = ("w0", "a0", "k_k", "k_a", "w2", "a2", "g2")
_RWKV_POST_VECS = ("ln_w", "ln_b", "r_k")


def _rwkv_fwd(x, p, t, carry=None):
    n = x.shape[0]
    bsz = n // t
    tr = _tile(n, 512)
    ts = _tile(n, 256)
    (h,) = _stage_fwd("rwkv_norm", _f_norm, (n // tr,), [_row_item(x, tr)], [_vec_item(p["norm"])],
                      [((n, D), F32, (tr, D), lambda i: (i, 0))])
    mixed = _stage_fwd("rwkv_mix", _f_rwkv_mix, (D // CT, bsz), [_seq_item(h, t, CT)], [_col_item(p["mix"], CT)],
                       [((n, D), BF16, (t, CT), lambda j, b: (b, j))] * 6)
    xr, xk, xv, xw, xa, xg = mixed
    r = _matmul("rwkv_r", xr, p["w_r"], tm=1024, tn=1024)
    k = _matmul("rwkv_k", xk, p["w_k"], tm=1024, tn=1024)
    v = _matmul("rwkv_v", xv, p["w_v"], tm=1024, tn=1024)
    lw1 = _matmul("rwkv_w1", xw, p["w1"], tm=1024)
    la1 = _matmul("rwkv_a1", xa, p["a1"], tm=1024)
    lg1 = _matmul("rwkv_g1", xg, p["g1"], tm=1024)
    pre_ins = [_row_item(a, ts) for a in (k, lw1, la1, lg1)]
    pre_par = [_vec_item(p[q]) for q in _RWKV_PRE_VECS]
    row = ((n, D), F32, (ts, D), lambda i: (i, 0))
    lw, kf, aa, bb, g = _stage_fwd("rwkv_pre", _f_rwkv_pre, (n // ts,), pre_ins, pre_par, [row] * 5)
    y, zs = _rwkv_scan_fwd([r, lw, kf, v, aa, bb], t, carry=carry)
    post_ins = [_row_item(a, ts) for a in (y, r, kf, v, g)]
    post_par = [_vec_item(p[q]) for q in _RWKV_POST_VECS]
    (yg,) = _stage_fwd("rwkv_post", _f_rwkv_post, (n // ts,), post_ins, post_par,
                       [((n, D), BF16, (ts, D), lambda i: (i, 0))])
    out = _matmul("rwkv_out", yg, p["w_out"], resid=x, tm=1024, tn=1024)
    return out, (x, h, mixed, r, k, v, lw1, la1, lg1, lw, kf, aa, bb, g, zs, y, yg)


def _rwkv_bwd(dout, saved, p, t, carry=None):
    x, h, mixed, r, k, v, lw1, la1, lg1, lw, kf, aa, bb, g, zs, y, yg = saved
    xr, xk, xv, xw, xa, xg = mixed
    n = x.shape[0]
    bsz = n // t
    tr = _tile(n, 512)
    ts = _tile(n, 256)
    grads = {}
    row_s = (ts, D), lambda i: (i, 0)
    dyg = _matmul("rwkv_out_da", dout, p["w_out"], mode="nt", tm=1024, tn=1024)
    grads["w_out"] = _matmul("rwkv_out_dw", yg, dout, mode="tn", tm=1024, tn=1024)
    post_ins = [_row_item(a, ts) for a in (y, r, kf, v, g)]
    post_par = [_vec_item(p[q]) for q in _RWKV_POST_VECS]
    (dy, dr_a, dkf_a, dv_a, dg), post_grads = _stage_bwd(
        "rwkv_post_bwd", _f_rwkv_post, (n // ts,), post_ins, post_par, [([dyg], *row_s)], [F32] * 5)
    grads.update(zip(_RWKV_POST_VECS, post_grads))
    dr_b, dlw, dkf_b, dv_b, daa, dbb = _rwkv_scan_bwd([r, lw, kf, v, aa, bb], zs, dy, t, carry=carry)
    dr = _add("rwkv_dr_sum", dr_a, dr_b)
    dv = _add("rwkv_dv_sum", dv_a, dv_b)
    pre_ins = [_row_item(a, ts) for a in (k, lw1, la1, lg1)]
    pre_par = [_vec_item(p[q]) for q in _RWKV_PRE_VECS]
    (dk, dlw1, dla1, dlg1), pre_grads = _stage_bwd(
        "rwkv_pre_bwd", _f_rwkv_pre, (n // ts,), pre_ins, pre_par,
        [([dlw], *row_s), ([dkf_a, dkf_b], *row_s), ([daa], *row_s), ([dbb], *row_s), ([dg], *row_s)], [F32] * 4)
    grads.update(zip(_RWKV_PRE_VECS, pre_grads))
    dmixed = []
    for tag, xin, dz, wname in (("r", xr, dr, "w_r"), ("k", xk, dk, "w_k"), ("v", xv, dv, "w_v"),
                                ("w1", xw, dlw1, "w1"), ("a1", xa, dla1, "a1"), ("g1", xg, dlg1, "g1")):
        dmixed.append(_matmul(f"rwkv_{tag}_da", dz, p[wname], mode="nt", tm=1024, tn=1024))
        grads[wname] = _matmul(f"rwkv_{tag}_dw", xin, dz, mode="tn", tm=1024, tn=1024)
    seq_blk = (t, CT), lambda j, b: (b, j)
    (dh,), (grads["mix"],) = _stage_bwd(
        "rwkv_mix_bwd", _f_rwkv_mix, (D // CT, bsz), [_seq_item(h, t, CT)], [_col_item(p["mix"], CT)],
        [([a], *seq_blk) for a in dmixed], [F32])
    (dx,), (grads["norm"],) = _stage_bwd(
        "rwkv_norm_bwd", _f_norm, (n // tr,), [_row_item(x, tr)], [_vec_item(p["norm"])],
        [([dh], (tr, D), lambda i: (i, 0))], [F32], adds={0: dout})
    return dx, grads


def _block_diag_gates(gate_w):
    z = jnp.zeros((2, D // CT, HEAD, HEAD), gate_w.dtype)
    even, odd = gate_w[:, 0::2], gate_w[:, 1::2]
    top = jnp.concatenate([even, z], axis=-1)
    bot = jnp.concatenate([z, odd], axis=-1)
    return jnp.concatenate([top, bot], axis=-2)


def _gate_blocks_grad(dg):
    even, odd = dg[:, :, :HEAD, :HEAD], dg[:, :, HEAD:, HEAD:]
    return jnp.stack([even, odd], axis=2).reshape(2, N_HEAD, HEAD, HEAD)


def _step(local, x, target):
    bsz, t, _ = x.shape
    n = bsz * t

    def block(name, pre=""):
        base, _, layer = name.partition("@")
        a = local[pre + base]
        return a[int(layer):int(layer) + 1] if layer else a

    def split(names):
        return [k for k in names if _base(k) in LARGE], [k for k in names if _base(k) not in LARGE]

    def gather_items(names):
        large, small = split(names)
        flat = [block(k).astype(BF16).reshape(-1) if _base(k) in GATHER_BF16
                else lax.bitcast_convert_type(block(k), BF16).reshape(-1) for k in small]
        return [(block(k).astype(BF16), False) for k in large] + ([(_pack(flat, BF16), False)] if small else [])

    def gathered(names, landed):
        large, small = split(names)
        out = {k: _unshard(g, SHARD_AXIS[_base(k)]) for k, g in zip(large, landed)}
        sizes = [(1 if _base(k) in GATHER_BF16 else 2) * math.prod(block(k).shape) for k in small]
        for k, g in zip(small, _unpack(landed[-1], [(s,) for s in sizes], (N_DEV,)) if small else []):
            if _base(k) not in GATHER_BF16:
                g = lax.bitcast_convert_type(g.reshape(N_DEV, -1, 2), F32)
            out[k] = _unshard(g.reshape((N_DEV,) + block(k).shape), SHARD_AXIS[_base(k)])
        return out

    landed_large, landed_small = {}, {}

    def scatter_items(names, grads):
        large, small = split(names)
        blocked = {k: _reshard(grads[k], SHARD_AXIS[_base(k)]) for k in names}
        return [(blocked[k], True) for k in large] + ([(_pack([blocked[k] for k in small], F32, (N_DEV,)), True)] if small else [])

    def scattered(names, landed):
        large, small = split(names)
        landed_large.update(zip(large, landed))
        if small:
            landed_small[tuple(small)] = landed[-1]

    def ffn_grads(l, g):
        return {f"ffn_w_up@{l}": g["w_up"][None], f"ffn_w_down@{l}": g["w_down"][None], f"ffn_conv_w@{l}": g["conv_w"][None]}

    lru = dict(norm=local["lru_norm"], b_out=local["lru_b_out"], b_y=local["lru_b_in"][:, :D], b_x=local["lru_b_in"][:, D:],
               conv_b=local["lru_conv_b"], gate_w=_block_diag_gates(local["lru_gate_w"][0]),
               gate_b=local["lru_gate_b"][0].reshape(2, D), lam=local["lru_lambda"])
    ride_pre = _Carry(gather_items(GROUPS["lru"]))
    ffn0_first = ["ffn_w_up@0", "ffn_conv_w@0"]
    ride_core = _Carry(gather_items(ffn0_first))

    def lru_late():
        full = gathered(GROUPS["lru"], ride_pre.landed)
        lru.update(w_in=full["lru_w_in"][0], conv_w=full["lru_conv_w"][0], w_out=full["lru_w_out"][0])

    x0 = x.reshape(n, D)
    x1, s_lru = _lru_fwd(x0, lru, t, carries={"pre": ride_pre, "core": ride_core}, late=lru_late)
    full = gathered(ffn0_first, ride_core.landed)
    ffn0 = dict(norm=local["ffn_norm"][0:1], conv_b=local["ffn_conv_b"][0:1], w_up=full["ffn_w_up@0"][0],
                conv_w=full["ffn_conv_w@0"][0])
    ride_up = _Carry(gather_items(GROUPS["rwkv"]))
    ride_down = _Carry(gather_items(["ffn_w_down@0"]))

    def ffn0_late():
        ffn0["w_down"] = gathered(["ffn_w_down@0"], ride_down.landed)["ffn_w_down@0"][0]

    x2, s_ffn0 = _ffn_fwd("ffn0", x1, ffn0, t, carries={"up": ride_up, "core": ride_down}, late=ffn0_late)
    full = gathered(GROUPS["rwkv"], ride_up.landed)
    rwkv = dict(norm=full["rwkv_norm"], mix=full["rwkv_mix"][0], w_r=full["rwkv_w_rkv"][0, 0],
                w_k=full["rwkv_w_rkv"][0, 1], w_v=full["rwkv_w_rkv"][0, 2], w0=full["rwkv_w0"], w1=full["rwkv_w1"][0],
                w2=full["rwkv_w2"][0], a0=full["rwkv_a0"], a1=full["rwkv_a1"][0], a2=full["rwkv_a2"][0],
                g1=full["rwkv_g1"][0], g2=full["rwkv_g2"][0], k_k=full["rwkv_k_k"], k_a=full["rwkv_k_a"],
                r_k=local["rwkv_r_k"].reshape(1, D), ln_w=full["rwkv_ln_w"], ln_b=full["rwkv_ln_b"],
                w_out=full["rwkv_w_out"][0])
    ride = _Carry(gather_items(GROUPS["ffn1"]))
    x3, s_rwkv = _rwkv_fwd(x2, rwkv, t, carry=ride)
    full = gathered(GROUPS["ffn1"], ride.landed)
    ffn1 = dict(norm=local["ffn_norm"][1:2], conv_b=local["ffn_conv_b"][1:2], w_up=full["ffn_w_up@1"][0],
                conv_w=full["ffn_conv_w@1"][0], w_down=full["ffn_w_down@1"][0])
    x4, s_ffn1 = _ffn_fwd("ffn1", x3, ffn1, t)
    loss, dx4, d_final = _loss_head(x4, target.reshape(n, D), local["final_norm"].reshape(1, D))

    dx3, g_ffn1 = _ffn_bwd("ffn1", dx4, s_ffn1, ffn1, t)
    ride = _Carry(scatter_items(GROUPS["ffn1"], ffn_grads(1, g_ffn1)))
    dx2, g_rwkv = _rwkv_bwd(dx3, s_rwkv, rwkv, t, carry=ride)
    scattered(GROUPS["ffn1"], ride.landed)
    rwkv_grads = {
        "rwkv_norm": g_rwkv["norm"], "rwkv_mix": g_rwkv["mix"][None],
        "rwkv_w_rkv": jnp.stack([g_rwkv["w_r"], g_rwkv["w_k"], g_rwkv["w_v"]])[None],
        "rwkv_w0": g_rwkv["w0"], "rwkv_w1": g_rwkv["w1"][None], "rwkv_w2": g_rwkv["w2"][None],
        "rwkv_a0": g_rwkv["a0"], "rwkv_a1": g_rwkv["a1"][None], "rwkv_a2": g_rwkv["a2"][None],
        "rwkv_g1": g_rwkv["g1"][None], "rwkv_g2": g_rwkv["g2"][None], "rwkv_k_k": g_rwkv["k_k"],
        "rwkv_k_a": g_rwkv["k_a"], "rwkv_ln_w": g_rwkv["ln_w"], "rwkv_ln_b": g_rwkv["ln_b"],
        "rwkv_w_out": g_rwkv["w_out"][None],
    }
    rwkv_first = [k for k in GROUPS["rwkv"] if k != "rwkv_w_rkv"]
    ride_da = _Carry(scatter_items(rwkv_first, rwkv_grads))
    ride_core = _Carry(scatter_items(["rwkv_w_rkv"], rwkv_grads))
    down_pieces = []

    def down_rides(g_w_down):
        blocked = _reshard(g_w_down[None], SHARD_AXIS["ffn_w_down"])
        rows = blocked.shape[2] // 3
        down_pieces.extend(_Carry([(blocked[:, :, i * rows:(i + 1) * rows], True)]) for i in range(3))
        return down_pieces

    dx1, g_ffn0 = _ffn_bwd("ffn0", dx2, s_ffn0, ffn0, t, carries={"down_da": ride_da, "core": ride_core},
                           down_rides=down_rides)
    scattered(rwkv_first, ride_da.landed)
    scattered(["rwkv_w_rkv"], ride_core.landed)
    landed_large["ffn_w_down@0"] = jnp.concatenate([c.landed[0] for c in down_pieces], axis=2)
    ride = _Carry(scatter_items(ffn0_first, ffn_grads(0, g_ffn0)))
    dx0, g_lru = _lru_bwd(dx1, s_lru, lru, t, carry=ride)
    scattered(ffn0_first, ride.landed)

    g_ffn = [g_ffn0, g_ffn1]
    lru_grads = {"lru_w_in": g_lru["w_in"][None], "lru_conv_w": g_lru["conv_w"][None], "lru_w_out": g_lru["w_out"][None]}
    gfull = {
        "lru_norm": g_lru["norm"], "lru_b_in": jnp.concatenate([g_lru["b_y"], g_lru["b_x"]], axis=1),
        "lru_conv_b": g_lru["conv_b"], "lru_gate_w": _gate_blocks_grad(g_lru["gate_w"])[None],
        "lru_gate_b": g_lru["gate_b"].reshape(1, 2, N_HEAD, HEAD), "lru_lambda": g_lru["lam"],
        "lru_b_out": g_lru["b_out"], "rwkv_r_k": g_rwkv["r_k"].reshape(1, N_HEAD, HEAD),
        "ffn_norm": jnp.concatenate([g["norm"] for g in g_ffn]),
        "ffn_conv_b": jnp.concatenate([g["conv_b"] for g in g_ffn]),
        "final_norm": d_final.reshape(D),
    }
    *tail, replicated = _exchange(
        "grad_tail", scatter_items(GROUPS["lru"], lru_grads) + [(_pack([gfull[k] for k in REPLICATED], F32), False)])
    scattered(GROUPS["lru"], tail)

    out = {}
    kinds = ("grad", "delta", "new_m", "new_v")
    for k, landed in landed_large.items():
        flat = (math.prod(block(k).shape[:-1]), block(k).shape[-1])
        res = _adamw(f"adamw_{k.replace('@', '_')}", landed.reshape((N_DEV,) + flat),
                     *[block(k, pre).reshape(flat) for pre in ("", "m_", "v_")])
        for kind, a in zip(kinds, res):
            out[(kind, k)] = a.reshape(block(k).shape)
    small_sets = [(f"adamw_small_{i}", list(names), landed) for i, (names, landed) in enumerate(landed_small.items())]
    for tag, names, landed in small_sets + [("adamw_replicated", REPLICATED, replicated)]:
        packs = [_pack([block(k, pre) for k in names], F32) for pre in ("", "m_", "v_")]
        res = _adamw(tag, landed, *packs)
        for kind, buf in zip(kinds, res):
            for k, a in zip(names, _unpack(buf, [block(k).shape for k in names])):
                out[(kind, k)] = a
    for kind in ("grad", "delta", "new_m", "new_v"):
        for k in STACKED:
            out[(kind, k)] = jnp.concatenate([out[(kind, f"{k}@0")], out[(kind, f"{k}@1")]], axis=0)
    return loss, dx0.reshape(x.shape), out


def kernel(x, lru_norm, lru_w_in, lru_b_in, lru_conv_w, lru_conv_b, lru_gate_w, lru_gate_b, lru_lambda, lru_w_out, lru_b_out, rwkv_norm, rwkv_mix, rwkv_w_rkv, rwkv_w0, rwkv_w1, rwkv_w2, rwkv_a0, rwkv_a1, rwkv_a2, rwkv_g1, rwkv_g2, rwkv_k_k, rwkv_k_a, rwkv_r_k, rwkv_ln_w, rwkv_ln_b, rwkv_w_out, ffn_norm, ffn_w_up, ffn_conv_w, ffn_conv_b, ffn_w_down, final_norm, loss_target, m_lru_norm, m_lru_w_in, m_lru_b_in, m_lru_conv_w, m_lru_conv_b, m_lru_gate_w, m_lru_gate_b, m_lru_lambda, m_lru_w_out, m_lru_b_out, m_rwkv_norm, m_rwkv_mix, m_rwkv_w_rkv, m_rwkv_w0, m_rwkv_w1, m_rwkv_w2, m_rwkv_a0, m_rwkv_a1, m_rwkv_a2, m_rwkv_g1, m_rwkv_g2, m_rwkv_k_k, m_rwkv_k_a, m_rwkv_r_k, m_rwkv_ln_w, m_rwkv_ln_b, m_rwkv_w_out, m_ffn_norm, m_ffn_w_up, m_ffn_conv_w, m_ffn_conv_b, m_ffn_w_down, m_final_norm, v_lru_norm, v_lru_w_in, v_lru_b_in, v_lru_conv_w, v_lru_conv_b, v_lru_gate_w, v_lru_gate_b, v_lru_lambda, v_lru_w_out, v_lru_b_out, v_rwkv_norm, v_rwkv_mix, v_rwkv_w_rkv, v_rwkv_w0, v_rwkv_w1, v_rwkv_w2, v_rwkv_a0, v_rwkv_a1, v_rwkv_a2, v_rwkv_g1, v_rwkv_g2, v_rwkv_k_k, v_rwkv_k_a, v_rwkv_r_k, v_rwkv_ln_w, v_rwkv_ln_b, v_rwkv_w_out, v_ffn_norm, v_ffn_w_up, v_ffn_conv_w, v_ffn_conv_b, v_ffn_w_down, v_final_norm):
    args = locals()
    local = {(pre + k): args[pre + k] for pre in ("", "m_", "v_") for k in WEIGHTS}
    loss_local, grad_x, out = _step(local, x, loss_target)
    loss = lax.psum(loss_local, ("x", "y", "c"))
    return (loss, grad_x, *[out[(kind, k)] for kind in ("grad", "delta", "new_m", "new_v") for k in WEIGHTS])
```

```python
import functools
import math

import jax
import jax.numpy as jnp
from jax import lax
from jax.experimental import pallas as pl
from jax.experimental.pallas import tpu as pltpu

F32 = jnp.float32
BF16 = jnp.bfloat16
HI = lax.Precision.HIGHEST

N_DEV = 8
D = 1024
HEAD = 64
N_HEAD = D // HEAD
D_FF = 3 * D
LANE = 128
V7X_VMEM_BYTES = 64 * 1024 * 1024
VMEM_LIMIT = V7X_VMEM_BYTES - 8 * 1024 * 1024
CT = LANE
CHUNK = 64
SCAN_HEADS = 16
PACK_W = 1024
PACK_ROWS = 256

ADAM_LR, ADAM_B1, ADAM_B2, ADAM_EPS, ADAM_WD, ADAM_STEP = 0.001, 0.9, 0.999, 1e-08, 0.01, 10
RMS_EPS = 1e-6
GN_EPS = 64e-5
LRU_C = 8.0

WEIGHTS = ['lru_norm', 'lru_w_in', 'lru_b_in', 'lru_conv_w', 'lru_conv_b', 'lru_gate_w', 'lru_gate_b',
           'lru_lambda', 'lru_w_out', 'lru_b_out', 'rwkv_norm', 'rwkv_mix', 'rwkv_w_rkv', 'rwkv_w0', 'rwkv_w1',
           'rwkv_w2', 'rwkv_a0', 'rwkv_a1', 'rwkv_a2', 'rwkv_g1', 'rwkv_g2', 'rwkv_k_k', 'rwkv_k_a', 'rwkv_r_k',
           'rwkv_ln_w', 'rwkv_ln_b', 'rwkv_w_out', 'ffn_norm', 'ffn_w_up', 'ffn_conv_w', 'ffn_conv_b',
           'ffn_w_down', 'final_norm']
SHARD_AXIS = {'lru_w_in': 2, 'lru_conv_w': 2, 'lru_w_out': 1, 'rwkv_norm': 1, 'rwkv_mix': 2, 'rwkv_w_rkv': 2,
              'rwkv_w0': 1, 'rwkv_w1': 1, 'rwkv_w2': 2, 'rwkv_a0': 1, 'rwkv_a1': 1, 'rwkv_a2': 2, 'rwkv_g1': 1,
              'rwkv_g2': 2, 'rwkv_k_k': 1, 'rwkv_k_a': 1, 'rwkv_ln_w': 1, 'rwkv_ln_b': 1, 'rwkv_w_out': 1,
              'ffn_w_up': 2, 'ffn_conv_w': 2, 'ffn_w_down': 1}
GATHER_BF16 = ['lru_w_in', 'lru_w_out', 'rwkv_w_rkv', 'rwkv_w1', 'rwkv_a1', 'rwkv_g1', 'rwkv_w_out', 'ffn_w_up',
               'ffn_w_down']
REPLICATED = [n for n in WEIGHTS if n not in SHARD_AXIS]
STACKED = ['ffn_w_up', 'ffn_w_down', 'ffn_conv_w']
LARGE = ['lru_w_in', 'lru_w_out', 'rwkv_w_rkv', 'rwkv_w_out', 'ffn_w_up', 'ffn_w_down']
GROUPS = {
    "lru": ['lru_w_in', 'lru_w_out', 'lru_conv_w'],
    "ffn0": [f"{n}@0" for n in STACKED],
    "rwkv": [n for n in WEIGHTS if n.startswith("rwkv_") and n in SHARD_AXIS],
    "ffn1": [f"{n}@1" for n in STACKED],
}


def _base(name):
    return name.partition("@")[0]


def _pcall(body, **kw):
    return pl.pallas_call(body, **kw)


class _Carry:
    def __init__(self, items):
        self.items = list(items)
        self.landed = None


def _exchange_copies(src_ref, dst_ref, send_sems, recv_sems, local_sem, scatter):
    x, y, c = lax.axis_index("x"), lax.axis_index("y"), lax.axis_index("c")
    me = 4 * x + 2 * y + c
    mine = pltpu.make_async_copy(src_ref.at[me] if scatter else src_ref, dst_ref.at[me], local_sem)
    copies = []
    for m in range(1, N_DEV):
        px = 1 - x if m & 4 else x
        py = 1 - y if m & 2 else y
        pc = 1 - c if m & 1 else c
        part = src_ref.at[4 * px + 2 * py + pc] if scatter else src_ref
        copies.append(pltpu.make_async_remote_copy(
            src_ref=part, dst_ref=dst_ref.at[me], send_sem=send_sems.at[m - 1], recv_sem=recv_sems.at[m - 1],
            device_id=(px, py, pc), device_id_type=pl.DeviceIdType.MESH))
    return mine, copies


def _start_exchange(mine, copies):
    mine.start()
    for cp in copies:
        cp.start()


def _wait_exchange(mine, copies):
    for cp in copies:
        cp.wait_recv()
    for cp in copies:
        cp.wait_send()
    mine.wait()


_EXCHANGE_SEMS = [pltpu.SemaphoreType.DMA((N_DEV - 1,)), pltpu.SemaphoreType.DMA((N_DEV - 1,)), pltpu.SemaphoreType.DMA]


def _landing_shape(src, scatter):
    return jax.ShapeDtypeStruct((N_DEV,) + tuple(src.shape[1:] if scatter else src.shape), src.dtype)


def _call(body, operands, *, carry=None, name, grid, in_specs, out_specs, out_shape, scratch_shapes=(),
          compiler_params=None):
    if carry is None:
        return _pcall(body, name=name, grid=grid, in_specs=in_specs, out_specs=out_specs, out_shape=out_shape,
                      scratch_shapes=list(scratch_shapes), compiler_params=compiler_params)(*operands)
    single = not isinstance(out_specs, (list, tuple))
    out_specs_l = [out_specs] if single else list(out_specs)
    out_shape_l = [out_shape] if single else list(out_shape)
    n_in, n_out, n_scr, n_x = len(in_specs), len(out_specs_l), len(scratch_shapes), len(carry.items)
    flags = [sc for (_, sc) in carry.items]

    def wrapped(*refs):
        ins, refs = refs[:n_in], refs[n_in:]
        xsrc, refs = refs[:n_x], refs[n_x:]
        outs, refs = refs[:n_out], refs[n_out:]
        xdst, refs = refs[:n_x], refs[n_x:]
        scr, sems = refs[:n_scr], refs[n_scr:]
        first = functools.reduce(jnp.logical_and, [pl.program_id(i) == 0 for i in range(len(grid))])
        last = functools.reduce(jnp.logical_and, [pl.program_id(i) == grid[i] - 1 for i in range(len(grid))])

        def exchanges():
            return [_exchange_copies(xsrc[i], xdst[i], *sems[3 * i:3 * i + 3], flags[i]) for i in range(n_x)]

        @pl.when(first)
        def _():
            for mine, copies in exchanges():
                _start_exchange(mine, copies)

        body(*ins, *outs, *scr)

        @pl.when(last)
        def _():
            for mine, copies in exchanges():
                _wait_exchange(mine, copies)

    hbm = pl.BlockSpec(memory_space=pl.ANY)
    res = _pcall(
        wrapped, name=name, grid=grid, in_specs=list(in_specs) + [hbm] * n_x,
        out_specs=out_specs_l + [hbm] * n_x,
        out_shape=out_shape_l + [_landing_shape(a, sc) for (a, sc) in carry.items],
        scratch_shapes=list(scratch_shapes) + _EXCHANGE_SEMS * n_x, compiler_params=compiler_params,
    )(*operands, *[a for (a, _) in carry.items])
    carry.landed = list(res[n_out:])
    return res[0] if single else list(res[:n_out])


def _params(n_grid):
    return pltpu.CompilerParams(dimension_semantics=("arbitrary",) * n_grid, vmem_limit_bytes=VMEM_LIMIT)


def _shift_rows(x, d, up):
    n = x.shape[0]
    idx = lax.broadcasted_iota(jnp.int32, x.shape, 0)
    if up:
        return jnp.where(idx < n - d, pltpu.roll(x, n - d, 0), 0.0)
    return jnp.where(idx >= d, pltpu.roll(x, d, 0), 0.0)


@functools.partial(jax.custom_vjp, nondiff_argnums=(1,))
def _shift_down(x, d):
    return _shift_rows(x, d, False)


def _shift_down_fwd(x, d):
    return _shift_rows(x, d, False), None


def _shift_down_bwd(d, _, g):
    return (_shift_rows(g, d, True),)


_shift_down.defvjp(_shift_down_fwd, _shift_down_bwd)


def _scan_doubling(a, b, up):
    n = a.shape[0]
    d = 1
    while d < n:
        b = b + a * _shift_rows(b, d, up)
        a = a * _shift_rows(a, d, up)
        d *= 2
    return b


@jax.custom_vjp
def _linear_scan(a, b):
    return _scan_doubling(a, b, False)


def _linear_scan_fwd(a, b):
    h = _scan_doubling(a, b, False)
    return h, (a, h)


def _linear_scan_bwd(res, dh):
    a, h = res
    g = _scan_doubling(_shift_rows(a, 1, True), dh, True)
    return g * _shift_rows(h, 1, False), g


_linear_scan.defvjp(_linear_scan_fwd, _linear_scan_bwd)


def _causal_conv(x, w, b):
    k = w.shape[0]
    out = b + x * w[k - 1:k]
    for j in range(k - 1):
        out = out + _shift_down(x, k - 1 - j) * w[j:j + 1]
    return out


_GELU_C1 = math.sqrt(2.0 / math.pi)
_GELU_C2 = 0.044715 * _GELU_C1


@jax.custom_vjp
def _gelu(x):
    return 0.5 * x * (1.0 + jnp.tanh(x * (_GELU_C1 + _GELU_C2 * (x * x))))


def _gelu_fwd(x):
    x2 = x * x
    t = jnp.tanh(x * (_GELU_C1 + _GELU_C2 * x2))
    return 0.5 * x * (1.0 + t), (x, x2, t)


def _gelu_bwd(res, g):
    x, x2, t = res
    return (g * (0.5 * (1.0 + t) + (0.5 * x) * (1.0 - t * t) * (_GELU_C1 + (3.0 * _GELU_C2) * x2)),)


_gelu.defvjp(_gelu_fwd, _gelu_bwd)


def _rmsnorm(x, g):
    return x * lax.rsqrt(jnp.mean(x * x, axis=-1, keepdims=True) + RMS_EPS) * g


def _neg_expm1(x):
    series = x * (1.0 + x * 0.5 * (1.0 + x * (1.0 / 3.0) * (1.0 + x * 0.25 * (1.0 + x * 0.2))))
    return -jnp.where(x > -0.1, series, jnp.exp(x) - 1.0)


def _dot(a, b, ca=1, cb=0, precision=None):
    return lax.dot_general(a, b, (((ca,), (cb,)), ((), ())), precision=precision, preferred_element_type=F32)


def _bdot(a, b):
    return _dot(a.astype(BF16), b.astype(BF16))


def _split_bf16(x):
    hi = x.astype(BF16)
    return hi, (x - hi.astype(F32)).astype(BF16)


def _head_sum_impl(x):
    row = lax.broadcasted_iota(jnp.int32, (D, LANE), 0)
    col = lax.broadcasted_iota(jnp.int32, (D, LANE), 1)
    e = (lax.shift_right_logical(row, 6) == col).astype(BF16)
    hi, lo = _split_bf16(x)
    s_hi, s_lo = _split_bf16(_dot(hi, e) + _dot(lo, e))
    return _dot(s_hi, e, 1, 1) + _dot(s_lo, e, 1, 1)


@jax.custom_vjp
def _head_sum(x):
    return _head_sum_impl(x)


def _head_sum_fwd(x):
    return _head_sum_impl(x), None


def _head_sum_bwd(_, g):
    return (_head_sum(g),)


_head_sum.defvjp(_head_sum_fwd, _head_sum_bwd)


def _specs(items):
    return [pl.BlockSpec(it[1], it[2]) for it in items]


def _stage_fwd(name, f, grid, ins, params, outs, carry=None):
    n_in = len(ins) + len(params)

    def body(*refs):
        res = f(*[r[...] for r in refs[:n_in]])
        for o, v in zip(refs[n_in:], res):
            o[...] = v.astype(o.dtype)

    return _call(
        body, [it[0] for it in ins + params], carry=carry, name=name, grid=grid, in_specs=_specs(ins + params),
        out_specs=[pl.BlockSpec(bs, im) for (_, _, bs, im) in outs],
        out_shape=[jax.ShapeDtypeStruct(s, dt) for (s, dt, _, _) in outs],
        compiler_params=_params(len(grid)),
    )


def _stage_bwd(name, f, grid, ins, params, douts, din_dtypes, adds=None, carry=None):
    adds = adds or {}
    n_in, n_par = len(ins), len(params)
    dout_items = [(a, bs, im) for (arrs, bs, im) in douts for a in arrs]
    add_items = [(adds[i], ins[i][1], ins[i][2]) for i in sorted(adds)]
    n_do, n_add = len(dout_items), len(add_items)

    def body(*refs):
        vals = [r[...] for r in refs[:n_in + n_par]]
        do_refs = list(refs[n_in + n_par:n_in + n_par + n_do])
        add_refs = dict(zip(sorted(adds), refs[n_in + n_par + n_do:n_in + n_par + n_do + n_add]))
        din_refs = refs[n_in + n_par + n_do + n_add:n_in + n_par + n_do + n_add + n_in]
        dpar_refs = refs[n_in + n_par + n_do + n_add + n_in:]
        cts = []
        for (arrs, _, _) in douts:
            ct = do_refs.pop(0)[...].astype(F32)
            for _ in arrs[1:]:
                ct = ct + do_refs.pop(0)[...].astype(F32)
            cts.append(ct)
        _, vjp = jax.vjp(f, *vals)
        grads = vjp(tuple(cts))
        for i, r in enumerate(din_refs):
            g = grads[i]
            if i in add_refs:
                g = g + add_refs[i][...]
            r[...] = g.astype(r.dtype)

        @pl.when(pl.program_id(len(grid) - 1) == 0)
        def _():
            for r in dpar_refs:
                r[...] = jnp.zeros(r.shape, r.dtype)

        for j, r in enumerate(dpar_refs):
            r[...] += grads[n_in + j]

    din_shapes = [it[3][0] if len(it) > 3 else it[0].shape for it in ins]
    din_specs = [pl.BlockSpec(it[1], it[3][1] if len(it) > 3 else it[2]) for it in ins]
    res = _call(
        body, [it[0] for it in ins + params + dout_items + add_items], carry=carry, name=name, grid=grid,
        in_specs=_specs(ins + params + dout_items + add_items), out_specs=din_specs + _specs(params),
        out_shape=[jax.ShapeDtypeStruct(s, dt) for s, dt in zip(din_shapes, din_dtypes)]
        + [jax.ShapeDtypeStruct(it[0].shape, F32) for it in params],
        compiler_params=_params(len(grid)),
    )
    return list(res[:n_in]), list(res[n_in:])


def _tile(n, want):
    t = min(n, want)
    while n % t:
        t //= 2
    return t


def _matmul(name, a, b, *, mode="nn", resid=None, out_dtype=F32, tm=512, tn=512, tk=1024, b_koff=0, carry=None,
            a_plus=None, b_plus=None):
    if mode == "tn":
        (kdim, m), n = a.shape, b.shape[1]
    else:
        (m, kdim), n = a.shape, (b.shape[1] if mode == "nn" else b.shape[0])
    tm, tn, tk = _tile(m, tm), _tile(n, tn), _tile(kdim, tk)
    nk = kdim // tk
    ko = b_koff // tk
    assert ko * tk == b_koff
    a_spec = pl.BlockSpec((tk, tm), lambda i, j, k: (k, i)) if mode == "tn" else pl.BlockSpec((tm, tk), lambda i, j, k: (i, k))
    b_spec = pl.BlockSpec((tn, tk), lambda i, j, k: (j, k + ko)) if mode == "nt" else pl.BlockSpec((tk, tn), lambda i, j, k: (k + ko, j))
    ca = 0 if mode == "tn" else 1
    cb = 1 if mode == "nt" else 0
    operands = [a, b]
    in_specs = [a_spec, b_spec]
    extra = {}
    for key, arr, spec in (("a", a_plus, a_spec), ("b", b_plus, b_spec), ("resid", resid, pl.BlockSpec((tm, tn), lambda i, j, k: (i, j)))):
        if arr is not None:
            extra[key] = len(operands)
            operands.append(arr)
            in_specs.append(spec)

    def product(refs):
        av, bv = refs[0][...], refs[1][...]
        if "a" in extra:
            av = av + refs[extra["a"]][...]
        if "b" in extra:
            bv = bv + refs[extra["b"]][...]
        return _dot(av.astype(BF16), bv.astype(BF16), ca, cb)

    def finish(r, refs, o_ref):
        if resid is not None:
            r = r + refs[extra["resid"]][...]
        o_ref[...] = r.astype(o_ref.dtype)

    def body_one_step(*refs):
        finish(product(refs), refs, refs[-1])

    def body(*refs):
        o_ref, acc_ref = refs[-2], refs[-1]
        k = pl.program_id(2)

        @pl.when(k == 0)
        def _():
            acc_ref[...] = jnp.zeros(acc_ref.shape, F32)

        acc_ref[...] += product(refs)

        @pl.when(k == nk - 1)
        def _():
            finish(acc_ref[...], refs, o_ref)

    return _call(
        body_one_step if nk == 1 else body, operands, carry=carry, name=name, grid=(m // tm, n // tn, nk),
        in_specs=in_specs, out_specs=pl.BlockSpec((tm, tn), lambda i, j, k: (i, j)),
        out_shape=jax.ShapeDtypeStruct((m, n), out_dtype),
        scratch_shapes=[] if nk == 1 else [pltpu.VMEM((tm, tn), F32)],
        compiler_params=_params(3),
    )


def _f_lru_pre(x, norm, b_out):
    return _rmsnorm(x, norm), x + b_out


def _f_lru_core(uy, ux, b_y, b_x, cw, cb, gw, gb, lam):
    yb = _gelu(uy + b_y)
    xr = _causal_conv(ux + b_x, cw, cb)
    gr = jax.nn.sigmoid(_bdot(xr, gw[0, 0]) + gb[0:1])
    gi = jax.nn.sigmoid(_bdot(xr, gw[1, 0]) + gb[1:2])
    log_a = -LRU_C * gr * jax.nn.softplus(-lam)
    a = jnp.exp(log_a)
    bterm = jnp.sqrt(_neg_expm1(2.0 * log_a)) * (gi * xr)
    return (_linear_scan(a, bterm) * yb,)


def _f_norm(x, norm):
    return (_rmsnorm(x, norm),)


def _f_ffn_core(ug, uv, cw, cb):
    return (_gelu(_causal_conv(ug, cw, cb)) * uv,)


def _f_rwkv_mix(h, mix):
    xx = _shift_down(h, 1) - h
    return tuple(h + xx * mix[i:i + 1] for i in range(6))


def _f_rwkv_pre(k, lw1, la1, lg1, w0, a0, k_k, k_a, w2, a2, g2):
    wpre = w0 + _bdot(jnp.tanh(lw1), w2)
    apre = a0 + _bdot(la1, a2)
    g = _bdot(jax.nn.sigmoid(lg1), g2)
    log_decay = -jnp.exp(-jax.nn.softplus(-wpre) - 0.5)
    a = jax.nn.sigmoid(apre)
    kk = k * k_k
    kk = kk / jnp.maximum(jnp.sqrt(_head_sum(kk * kk)), 1e-12)
    kf = k * (1.0 + (a - 1.0) * k_a)
    return log_decay, kf, -kk, kk * a, g


def _f_rwkv_post(y, r, kf, v, g, ln_w, ln_b, r_k):
    inv = 1.0 / HEAD
    yc = y - _head_sum(y) * inv
    var = _head_sum(yc * yc) * inv
    yn = yc * lax.rsqrt(var + GN_EPS) * ln_w + ln_b
    bonus = _head_sum(r * kf * r_k) * v
    return ((yn + bonus) * g,)


def _hdot(a, b, ca, cb):
    return lax.dot_general(a, b, (((ca,), (cb,)), ((0,), (0,))), preferred_element_type=F32)


def _hmm_impl(a, b, ca, cb):
    return _hdot(a.astype(BF16), b.astype(BF16), ca, cb)


@functools.partial(jax.custom_vjp, nondiff_argnums=(2, 3))
def _hmm(a, b, ca, cb):
    return _hmm_impl(a, b, ca, cb)


def _hmm_fwd(a, b, ca, cb):
    return _hmm_impl(a, b, ca, cb), (a, b)


def _hmm_bwd(ca, cb, res, dc):
    a, b = res
    fa, fb = 3 - ca, 3 - cb
    da = _hmm(dc, b, 2, fb) if ca == 2 else _hmm(b, dc, fb, 2)
    db = _hmm(a, dc, fa, 1) if cb == 1 else _hmm(dc, a, 1, fa)
    return da, db


_hmm.defvjp(_hmm_fwd, _hmm_bwd)


def _tri_dot_impl(x, upper):
    g, n, _ = x.shape
    row = lax.broadcasted_iota(jnp.int32, (g, n, 3 * n), 1)
    col = lax.broadcasted_iota(jnp.int32, (g, n, 3 * n), 2)
    col = jnp.where(col >= 2 * n, col - 2 * n, jnp.where(col >= n, col - n, col))
    tri = (row <= col if upper else row >= col).astype(BF16)
    hi = x.astype(BF16)
    rem = x - hi.astype(F32)
    mid = rem.astype(BF16)
    lo = (rem - mid.astype(F32)).astype(BF16)
    return _hdot(tri, jnp.concatenate([hi, mid, lo], axis=1), 2, 1)


@functools.partial(jax.custom_vjp, nondiff_argnums=(1,))
def _tri_dot(x, upper):
    return _tri_dot_impl(x, upper)


def _tri_dot_fwd(x, upper):
    return _tri_dot_impl(x, upper), None


def _tri_dot_bwd(upper, _, g):
    return (_tri_dot(g, not upper),)


_tri_dot.defvjp(_tri_dot_fwd, _tri_dot_bwd)


def _unit_lower_inverse_impl(a):
    g, n, _ = a.shape
    row = lax.broadcasted_iota(jnp.int32, (g, n, n), 1)
    col = lax.broadcasted_iota(jnp.int32, (g, n, n), 2)
    inv = (row == col).astype(F32) + a
    p = a
    for _ in range(int(math.log2(n)) - 1):
        p = _hmm(p, p, 2, 1)
        inv = inv + _hmm(inv, p, 2, 1)
    return inv


@jax.custom_vjp
def _unit_lower_inverse(a):
    return _unit_lower_inverse_impl(a)


def _unit_lower_inverse_fwd(a):
    inv = _unit_lower_inverse_impl(a)
    return inv, inv


def _unit_lower_inverse_bwd(inv, d_inv):
    return (_hmm(_hmm(inv, d_inv, 1, 1), inv, 2, 2),)


_unit_lower_inverse.defvjp(_unit_lower_inverse_fwd, _unit_lower_inverse_bwd)


def _rwkv_chunk(z0, r, lw, k, v, a, b):
    g, n, _ = r.shape
    row = lax.broadcasted_iota(jnp.int32, (g, n, n), 1)
    col = lax.broadcasted_iota(jnp.int32, (g, n, n), 2)
    incl, strict = row >= col, row > col
    cs = _tri_dot(lw, False)
    c_last = cs[:, n - 1:n]
    inv = jnp.exp(-cs)
    ar = jnp.concatenate([a * jnp.exp(cs - lw), r * jnp.exp(cs)], axis=1)
    bk = jnp.concatenate([b * inv, k * inv], axis=1)
    pair = _hmm(ar, bk, 2, 2)
    a_ab = jnp.where(strict, pair[:, :n, :n], 0.0)
    a_ak = jnp.where(strict, pair[:, :n, n:], 0.0)
    a_rbk = jnp.concatenate([jnp.where(incl, pair[:, n:, :n], 0.0), jnp.where(incl, pair[:, n:, n:], 0.0)], axis=2)
    arz = _hmm(ar, z0, 2, 1)
    u = _hmm(_unit_lower_inverse(a_ab), arz[:, :n] + _hmm(a_ak, v, 2, 1), 2, 1)
    uv = jnp.concatenate([u, v], axis=1)
    y = arz[:, n:] + _hmm(a_rbk, uv, 2, 1)
    tail = jnp.exp(c_last - cs)
    er = lax.broadcasted_iota(jnp.int32, (g, HEAD, HEAD), 1)
    ec = lax.broadcasted_iota(jnp.int32, (g, HEAD, HEAD), 2)
    decay_all = jnp.where(er == ec, jnp.exp(c_last), 0.0)
    z_l = _hmm(jnp.concatenate([b * tail, k * tail, decay_all], axis=1), jnp.concatenate([uv, z0], axis=1), 1, 1)
    return y, z_l


def _heads_in(ref, g):
    x = ref[...]
    return jnp.stack([x[:, h * HEAD:(h + 1) * HEAD] for h in range(g)], axis=0)


def _heads_out(ref, x):
    ref[...] = jnp.concatenate([x[h] for h in range(x.shape[0])], axis=1)


def _scan_geometry(n_rows, t):
    g, n = _tile(N_HEAD, SCAN_HEADS), _tile(t, CHUNK)
    return g, n, t // n, N_HEAD // g, n_rows // t


def _rwkv_scan_fwd(seqs, t, carry=None):
    rows = seqs[0].shape[0]
    g, n, nc, hpg, bsz = _scan_geometry(rows, t)

    def body(r_ref, lw_ref, k_ref, v_ref, a_ref, b_ref, y_ref, zs_ref, z_ref):
        @pl.when(pl.program_id(1) == 0)
        def _():
            z_ref[...] = jnp.zeros(z_ref.shape, F32)

        z0 = z_ref[...]
        zs_ref[:, 0] = z0
        y, z_l = _rwkv_chunk(z0, *[_heads_in(ref, g) for ref in (r_ref, lw_ref, k_ref, v_ref, a_ref, b_ref)])
        _heads_out(y_ref, y)
        z_ref[...] = z_l

    seq_spec = pl.BlockSpec((n, g * HEAD), lambda i, c: ((i // hpg) * nc + c, i % hpg))
    return _call(
        body, list(seqs), carry=carry, name="rwkv_scan_fwd", grid=(bsz * hpg, nc), in_specs=[seq_spec] * 6,
        out_specs=[seq_spec, pl.BlockSpec((g, 1, HEAD, HEAD), lambda i, c: (i, c, 0, 0))],
        out_shape=[jax.ShapeDtypeStruct((rows, D), F32), jax.ShapeDtypeStruct((bsz * N_HEAD, nc, HEAD, HEAD), F32)],
        scratch_shapes=[pltpu.VMEM((g, HEAD, HEAD), F32)],
        compiler_params=_params(2),
    )


def _rwkv_scan_bwd(seqs, zs, dy, t, carry=None):
    rows = seqs[0].shape[0]
    g, n, nc, hpg, bsz = _scan_geometry(rows, t)

    def body(r_ref, lw_ref, k_ref, v_ref, a_ref, b_ref, zs_ref, dy_ref, dr, dlw, dk, dv, da, db, dz_ref):
        @pl.when(pl.program_id(1) == 0)
        def _():
            dz_ref[...] = jnp.zeros(dz_ref.shape, F32)

        _, vjp = jax.vjp(_rwkv_chunk, zs_ref[:, 0],
                         *[_heads_in(ref, g) for ref in (r_ref, lw_ref, k_ref, v_ref, a_ref, b_ref)])
        grads = vjp((_heads_in(dy_ref, g), dz_ref[...]))
        dz_ref[...] = grads[0]
        for o, gr in zip((dr, dlw, dk, dv, da, db), grads[1:]):
            _heads_out(o, gr)

    seq_spec = pl.BlockSpec((n, g * HEAD), lambda i, c: ((i // hpg) * nc + nc - 1 - c, i % hpg))
    return _call(
        body, [*seqs, zs, dy], carry=carry, name="rwkv_scan_bwd", grid=(bsz * hpg, nc),
        in_specs=[seq_spec] * 6 + [pl.BlockSpec((g, 1, HEAD, HEAD), lambda i, c: (i, nc - 1 - c, 0, 0)), seq_spec],
        out_specs=[seq_spec] * 6,
        out_shape=[jax.ShapeDtypeStruct((rows, D), F32)] * 6,
        scratch_shapes=[pltpu.VMEM((g, HEAD, HEAD), F32)],
        compiler_params=_params(2),
    )


def _loss_head(x, target, norm):
    n = x.shape[0]
    tr = _tile(n, 256)

    def f(xv, gv, tv):
        err = _rmsnorm(xv, gv) - tv
        return 0.5 * jnp.sum(jnp.mean(err * err, axis=-1, keepdims=True), axis=0, keepdims=True)

    def body(x_ref, t_ref, g_ref, dx_ref, dg_ref, loss_ref):
        val, vjp = jax.vjp(lambda xv, gv: f(xv, gv, t_ref[...]), x_ref[...], g_ref[...])
        dx, dg = vjp(jnp.ones((1, 1), F32))
        dx_ref[...] = dx

        @pl.when(pl.program_id(0) == 0)
        def _():
            dg_ref[...] = jnp.zeros(dg_ref.shape, F32)
            loss_ref[...] = jnp.zeros(loss_ref.shape, F32)

        dg_ref[...] += dg
        loss_ref[...] += jnp.broadcast_to(val, loss_ref.shape)

    row = pl.BlockSpec((tr, D), lambda i: (i, 0))
    vec = pl.BlockSpec((1, D), lambda i: (0, 0))
    dx, dg, loss = _pcall(
        body, name="loss_head", grid=(n // tr,), in_specs=[row, row, vec],
        out_specs=[row, vec, pl.BlockSpec((8, LANE), lambda i: (0, 0))],
        out_shape=[jax.ShapeDtypeStruct((n, D), F32), jax.ShapeDtypeStruct((1, D), F32),
                   jax.ShapeDtypeStruct((8, LANE), F32)],
        compiler_params=_params(1),
    )(x, target, norm)
    return loss[0, 0], dx, dg


def _exchange(name, items):
    n_x = len(items)

    def body(*refs):
        srcs, dsts, sems = refs[:n_x], refs[n_x:2 * n_x], refs[2 * n_x:]
        started = [_exchange_copies(srcs[i], dsts[i], *sems[3 * i:3 * i + 3], items[i][1]) for i in range(n_x)]
        for mine, copies in started:
            _start_exchange(mine, copies)
        for mine, copies in started:
            _wait_exchange(mine, copies)

    hbm = pl.BlockSpec(memory_space=pl.ANY)
    return _pcall(
        body, name=name, in_specs=[hbm] * n_x, out_specs=[hbm] * n_x,
        out_shape=[_landing_shape(a, sc) for (a, sc) in items], scratch_shapes=_EXCHANGE_SEMS * n_x,
    )(*[a for (a, _) in items])


def _adamw(name, parts, w, m, v):
    rows, cols = w.shape
    tr = _tile(rows, PACK_ROWS)

    def body(p_ref, w_ref, m_ref, v_ref, g_ref, d_ref, nm_ref, nv_ref):
        g = p_ref[0]
        for k in range(1, N_DEV):
            g = g + p_ref[k]
        nm = ADAM_B1 * m_ref[...] + (1.0 - ADAM_B1) * g
        nv = ADAM_B2 * v_ref[...] + (1.0 - ADAM_B2) * jnp.square(g)
        m_hat = nm / (1.0 - ADAM_B1 ** ADAM_STEP)
        v_hat = nv / (1.0 - ADAM_B2 ** ADAM_STEP)
        g_ref[...] = g
        d_ref[...] = -ADAM_LR * (m_hat / (jnp.sqrt(v_hat) + ADAM_EPS) + ADAM_WD * w_ref[...])
        nm_ref[...] = nm
        nv_ref[...] = nv

    row = pl.BlockSpec((tr, cols), lambda i: (i, 0))
    return _pcall(
        body, name=name, grid=(rows // tr,),
        in_specs=[pl.BlockSpec((N_DEV, tr, cols), lambda i: (0, i, 0)), row, row, row],
        out_specs=[row] * 4, out_shape=[jax.ShapeDtypeStruct((rows, cols), F32)] * 4,
        compiler_params=_params(1),
    )(parts, w, m, v)


def _pack(arrs, dtype, lead=()):
    flat = jnp.concatenate([a.astype(dtype).reshape(lead + (-1,)) for a in arrs], axis=-1)
    n = flat.shape[-1]
    quantum = PACK_W * PACK_ROWS
    total = -(-n // quantum) * quantum
    flat = jnp.pad(flat, [(0, 0)] * len(lead) + [(0, total - n)])
    return flat.reshape(lead + (total // PACK_W, PACK_W))


def _unpack(buf, shapes, lead=()):
    flat = buf.reshape(lead + (-1,))
    out, off = [], 0
    for s in shapes:
        n = math.prod(s)
        out.append(flat[..., off:off + n].reshape(lead + tuple(s)))
        off += n
    return out


def _unshard(g, axis):
    local = g.shape[1:]
    full = jnp.moveaxis(g, 0, axis)
    return full.reshape(local[:axis] + (N_DEV * local[axis],) + local[axis + 1:])


def _reshard(full, axis):
    s = full.shape
    blocked = full.reshape(s[:axis] + (N_DEV, s[axis] // N_DEV) + s[axis + 1:])
    return jnp.moveaxis(blocked, axis, 0)


def _row_item(a, tr):
    return (a, (tr, a.shape[1]), lambda i: (i, 0))


def _vec_item(a):
    return (a, a.shape, lambda i: (0,) * a.ndim)


def _seq_item(a, t, width):
    return (a, (t, width), lambda j, b: (b, j))


def _seq_halves(a, t, width):
    half = a.shape[1] // 2
    off = half // width
    grad = ((a.shape[0], half), lambda j, b: (b, j))
    return [(a, (t, width), lambda j, b: (b, j), grad), (a, (t, width), lambda j, b: (b, j + off), grad)]


def _col_item(a, width):
    return (a, (a.shape[0], width), lambda j, b: (0, j))


def _ffn_fwd(tag, x, p, t, carries=None, late=None):
    carries = carries or {}
    n = x.shape[0]
    tr = _tile(n, 512)
    norm_ins, norm_par = [_row_item(x, tr)], [_vec_item(p["norm"])]
    (h,) = _stage_fwd(f"{tag}_norm", _f_norm, (n // tr,), norm_ins, norm_par, [((n, D), BF16, (tr, D), lambda i: (i, 0))])
    u = _matmul(f"{tag}_up", h, p["w_up"], tm=2048, tn=768, carry=carries.get("up"))
    grid = (D_FF // CT, n // t)
    core_par = [_col_item(p["conv_w"], CT), _col_item(p["conv_b"], CT)]
    (hid,) = _stage_fwd(f"{tag}_core", _f_ffn_core, grid, _seq_halves(u, t, CT), core_par,
                        [((n, D_FF), BF16, (t, CT), lambda j, b: (b, j))], carry=carries.get("core"))
    if late is not None:
        late()
    y = _matmul(f"{tag}_down", hid, p["w_down"], resid=x, tm=1024, tn=1024, tk=D_FF)
    return y, (x, h, u, hid)


def _ffn_bwd(tag, dy, saved, p, t, carries=None, down_rides=None):
    carries = carries or {}
    x, h, u, hid = saved
    n = x.shape[0]
    tr = _tile(n, 512)
    grads = {}
    dhid = _matmul(f"{tag}_down_da", dy, p["w_down"], mode="nt", tm=1024, tn=768, carry=carries.get("down_da"))
    grads["w_down"] = _matmul(f"{tag}_down_dw", hid, dy, mode="tn", tm=768, tn=1024, tk=2048,
                              carry=carries.get("down_dw"))
    after = down_rides(grads["w_down"]) if down_rides is not None else [None] * 3
    grid = (D_FF // CT, n // t)
    core_par = [_col_item(p["conv_w"], CT), _col_item(p["conv_b"], CT)]
    (dug, duv), (grads["conv_w"], grads["conv_b"]) = _stage_bwd(
        f"{tag}_core_bwd", _f_ffn_core, grid, _seq_halves(u, t, CT), core_par,
        [([dhid], (t, CT), lambda j, b: (b, j))], [BF16, BF16], carry=carries.get("core"))
    dh = _matmul(f"{tag}_up_da_g", dug, p["w_up"], mode="nt", tm=1024, tn=1024, tk=D_FF, carry=after[0])
    dh = _matmul(f"{tag}_up_da_v", duv, p["w_up"], mode="nt", tm=1024, tn=1024, tk=D_FF, b_koff=D_FF, resid=dh,
                 carry=after[1])
    grads["w_up"] = jnp.concatenate([
        _matmul(f"{tag}_up_dw_g", h, dug, mode="tn", tm=1024, tn=768, tk=2048, carry=after[2]),
        _matmul(f"{tag}_up_dw_v", h, duv, mode="tn", tm=1024, tn=768, tk=2048)], axis=1)
    (dx,), (grads["norm"],) = _stage_bwd(
        f"{tag}_norm_bwd", _f_norm, (n // tr,), [_row_item(x, tr)], [_vec_item(p["norm"])],
        [([dh], (tr, D), lambda i: (i, 0))], [F32], adds={0: dy})
    return dx, grads


def _lru_fwd(x, p, t, carries=None, late=None):
    carries = carries or {}
    n = x.shape[0]
    tr = _tile(n, 512)
    row = lambda dt: ((n, D), dt, (tr, D), lambda i: (i, 0))
    h, xb = _stage_fwd("lru_pre", _f_lru_pre, (n // tr,), [_row_item(x, tr)],
                       [_vec_item(p["norm"]), _vec_item(p["b_out"])], [row(BF16), row(F32)], carry=carries.get("pre"))
    if late is not None:
        late("pre")
    u = _matmul("lru_in", h, p["w_in"], tm=1024, tn=1024, carry=carries.get("in"))
    if late is not None:
        late("in")
    out, = _stage_fwd("lru_core", _f_lru_core, (D // CT, n // t), _seq_halves(u, t, CT), _lru_core_params(p),
                      [((n, D), BF16, (t, CT), lambda j, b: (b, j))], carry=carries.get("core"))
    y = _matmul("lru_out", out, p["w_out"], resid=xb, tm=1024, tn=1024)
    return y, (x, h, u, out)


def _lru_core_params(p):
    return [_col_item(p["b_y"], CT), _col_item(p["b_x"], CT), _col_item(p["conv_w"], CT), _col_item(p["conv_b"], CT),
            (p["gate_w"], (2, 1, CT, CT), lambda j, b: (0, j, 0, 0)), _col_item(p["gate_b"], CT), _col_item(p["lam"], CT)]


def _lru_bwd(dy, saved, p, t, carry=None, out_rides=None):
    x, h, u, out = saved
    n = x.shape[0]
    tr = _tile(n, 512)
    grads = {}
    dout = _matmul("lru_out_da", dy, p["w_out"], mode="nt", tm=1024, tn=1024)
    grads["w_out"] = _matmul("lru_out_dw", out, dy, mode="tn", tm=1024, tn=1024)
    after = out_rides(grads["w_out"]) if out_rides is not None else [None] * 2
    (duy, dux), core_grads = _stage_bwd("lru_core_bwd", _f_lru_core, (D // CT, n // t), _seq_halves(u, t, CT),
                                        _lru_core_params(p), [([dout], (t, CT), lambda j, b: (b, j))], [BF16, BF16],
                                        carry=carry)
    for k, g in zip(("b_y", "b_x", "conv_w", "conv_b", "gate_w", "gate_b", "lam"), core_grads):
        grads[k] = g
    dh = _matmul("lru_in_da_y", duy, p["w_in"], mode="nt", tm=1024, tn=1024, carry=after[0])
    dh = _matmul("lru_in_da_x", dux, p["w_in"], mode="nt", tm=1024, tn=1024, b_koff=D, resid=dh, carry=after[1])
    grads["w_in"] = jnp.concatenate([_matmul("lru_in_dw_y", h, duy, mode="tn", tm=1024, tn=1024),
                                     _matmul("lru_in_dw_x", h, dux, mode="tn", tm=1024, tn=1024)], axis=1)
    row = (tr, D), lambda i: (i, 0)
    (dx,), (grads["norm"], grads["b_out"]) = _stage_bwd(
        "lru_pre_bwd", _f_lru_pre, (n // tr,), [_row_item(x, tr)], [_vec_item(p["norm"]), _vec_item(p["b_out"])],
        [([dh], *row), ([dy], *row)], [F32])
    return dx, grads


_RWKV_PRE_VECS = ("w0", "a0", "k_k", "k_a", "w2", "a2", "g2")
_RWKV_POST_VECS = ("ln_w", "ln_b", "r_k")


def _rwkv_fwd(x, p, t, carry=None):
    n = x.shape[0]
    bsz = n // t
    tr = _tile(n, 512)
    ts = _tile(n, 256)
    (h,) = _stage_fwd("rwkv_norm", _f_norm, (n // tr,), [_row_item(x, tr)], [_vec_item(p["norm"])],
                      [((n, D), F32, (tr, D), lambda i: (i, 0))])
    mixed = _stage_fwd("rwkv_mix", _f_rwkv_mix, (D // CT, bsz), [_seq_item(h, t, CT)], [_col_item(p["mix"], CT)],
                       [((n, D), BF16, (t, CT), lambda j, b: (b, j))] * 6)
    xr, xk, xv, xw, xa, xg = mixed
    r = _matmul("rwkv_r", xr, p["w_r"], tm=1024, tn=1024)
    k = _matmul("rwkv_k", xk, p["w_k"], tm=1024, tn=1024)
    v = _matmul("rwkv_v", xv, p["w_v"], tm=1024, tn=1024)
    lw1 = _matmul("rwkv_w1", xw, p["w1"], tm=1024)
    la1 = _matmul("rwkv_a1", xa, p["a1"], tm=1024)
    lg1 = _matmul("rwkv_g1", xg, p["g1"], tm=1024)
    pre_ins = [_row_item(a, ts) for a in (k, lw1, la1, lg1)]
    pre_par = [_vec_item(p[q]) for q in _RWKV_PRE_VECS]
    row = ((n, D), F32, (ts, D), lambda i: (i, 0))
    lw, kf, aa, bb, g = _stage_fwd("rwkv_pre", _f_rwkv_pre, (n // ts,), pre_ins, pre_par, [row] * 5)
    y, zs = _rwkv_scan_fwd([r, lw, kf, v, aa, bb], t, carry=carry)
    post_ins = [_row_item(a, ts) for a in (y, r, kf, v, g)]
    post_par = [_vec_item(p[q]) for q in _RWKV_POST_VECS]
    (yg,) = _stage_fwd("rwkv_post", _f_rwkv_post, (n // ts,), post_ins, post_par,
                       [((n, D), BF16, (ts, D), lambda i: (i, 0))])
    out = _matmul("rwkv_out", yg, p["w_out"], resid=x, tm=1024, tn=1024)
    return out, (x, h, mixed, r, k, v, lw1, la1, lg1, lw, kf, aa, bb, g, zs, y, yg)


def _rwkv_bwd(dout, saved, p, t, carry=None):
    x, h, mixed, r, k, v, lw1, la1, lg1, lw, kf, aa, bb, g, zs, y, yg = saved
    xr, xk, xv, xw, xa, xg = mixed
    n = x.shape[0]
    bsz = n // t
    tr = _tile(n, 512)
    ts = _tile(n, 256)
    grads = {}
    row_s = (ts, D), lambda i: (i, 0)
    dyg = _matmul("rwkv_out_da", dout, p["w_out"], mode="nt", tm=1024, tn=1024)
    grads["w_out"] = _matmul("rwkv_out_dw", yg, dout, mode="tn", tm=1024, tn=1024)
    post_ins = [_row_item(a, ts) for a in (y, r, kf, v, g)]
    post_par = [_vec_item(p[q]) for q in _RWKV_POST_VECS]
    (dy, dr_a, dkf_a, dv_a, dg), post_grads = _stage_bwd(
        "rwkv_post_bwd", _f_rwkv_post, (n // ts,), post_ins, post_par, [([dyg], *row_s)], [F32] * 5)
    grads.update(zip(_RWKV_POST_VECS, post_grads))
    dr_b, dlw, dkf_b, dv_b, daa, dbb = _rwkv_scan_bwd([r, lw, kf, v, aa, bb], zs, dy, t, carry=carry)
    pre_ins = [_row_item(a, ts) for a in (k, lw1, la1, lg1)]
    pre_par = [_vec_item(p[q]) for q in _RWKV_PRE_VECS]
    (dk, dlw1, dla1, dlg1), pre_grads = _stage_bwd(
        "rwkv_pre_bwd", _f_rwkv_pre, (n // ts,), pre_ins, pre_par,
        [([dlw], *row_s), ([dkf_a, dkf_b], *row_s), ([daa], *row_s), ([dbb], *row_s), ([dg], *row_s)], [F32] * 4)
    grads.update(zip(_RWKV_PRE_VECS, pre_grads))
    dmixed = []
    for tag, xin, dz, dz2, wname in (("r", xr, dr_a, dr_b, "w_r"), ("k", xk, dk, None, "w_k"), ("v", xv, dv_a, dv_b, "w_v"),
                                     ("w1", xw, dlw1, None, "w1"), ("a1", xa, dla1, None, "a1"), ("g1", xg, dlg1, None, "g1")):
        dmixed.append(_matmul(f"rwkv_{tag}_da", dz, p[wname], mode="nt", tm=1024, tn=1024, a_plus=dz2))
        grads[wname] = _matmul(f"rwkv_{tag}_dw", xin, dz, mode="tn", tm=1024, tn=1024, b_plus=dz2)
    seq_blk = (t, CT), lambda j, b: (b, j)
    (dh,), (grads["mix"],) = _stage_bwd(
        "rwkv_mix_bwd", _f_rwkv_mix, (D // CT, bsz), [_seq_item(h, t, CT)], [_col_item(p["mix"], CT)],
        [([a], *seq_blk) for a in dmixed], [F32])
    (dx,), (grads["norm"],) = _stage_bwd(
        "rwkv_norm_bwd", _f_norm, (n // tr,), [_row_item(x, tr)], [_vec_item(p["norm"])],
        [([dh], (tr, D), lambda i: (i, 0))], [F32], adds={0: dout})
    return dx, grads


def _block_diag_gates(gate_w):
    z = jnp.zeros((2, D // CT, HEAD, HEAD), gate_w.dtype)
    even, odd = gate_w[:, 0::2], gate_w[:, 1::2]
    top = jnp.concatenate([even, z], axis=-1)
    bot = jnp.concatenate([z, odd], axis=-1)
    return jnp.concatenate([top, bot], axis=-2)


def _gate_blocks_grad(dg):
    even, odd = dg[:, :, :HEAD, :HEAD], dg[:, :, HEAD:, HEAD:]
    return jnp.stack([even, odd], axis=2).reshape(2, N_HEAD, HEAD, HEAD)


def _step(local, x, target):
    bsz, t, _ = x.shape
    n = bsz * t

    def block(name, pre=""):
        base, _, layer = name.partition("@")
        a = local[pre + base]
        return a[int(layer):int(layer) + 1] if layer else a

    def split(names):
        return [k for k in names if _base(k) in LARGE], [k for k in names if _base(k) not in LARGE]

    def gather_items(names):
        large, small = split(names)
        flat = [block(k).astype(BF16).reshape(-1) if _base(k) in GATHER_BF16
                else lax.bitcast_convert_type(block(k), BF16).reshape(-1) for k in small]
        return [(block(k).astype(BF16), False) for k in large] + ([(_pack(flat, BF16), False)] if small else [])

    def gathered(names, landed):
        large, small = split(names)
        out = {k: _unshard(g, SHARD_AXIS[_base(k)]) for k, g in zip(large, landed)}
        sizes = [(1 if _base(k) in GATHER_BF16 else 2) * math.prod(block(k).shape) for k in small]
        for k, g in zip(small, _unpack(landed[-1], [(s,) for s in sizes], (N_DEV,)) if small else []):
            if _base(k) not in GATHER_BF16:
                g = lax.bitcast_convert_type(g.reshape(N_DEV, -1, 2), F32)
            out[k] = _unshard(g.reshape((N_DEV,) + block(k).shape), SHARD_AXIS[_base(k)])
        return out

    landed_large, landed_small = {}, {}

    def scatter_items(names, grads):
        large, small = split(names)
        blocked = {k: _reshard(grads[k], SHARD_AXIS[_base(k)]) for k in names}
        return [(blocked[k], True) for k in large] + ([(_pack([blocked[k] for k in small], F32, (N_DEV,)), True)] if small else [])

    def scattered(names, landed):
        large, small = split(names)
        landed_large.update(zip(large, landed))
        if small:
            landed_small[tuple(small)] = landed[-1]

    def ffn_grads(l, g):
        return {f"ffn_w_up@{l}": g["w_up"][None], f"ffn_w_down@{l}": g["w_down"][None], f"ffn_conv_w@{l}": g["conv_w"][None]}

    lru = dict(norm=local["lru_norm"], b_out=local["lru_b_out"], b_y=local["lru_b_in"][:, :D], b_x=local["lru_b_in"][:, D:],
               conv_b=local["lru_conv_b"], gate_w=_block_diag_gates(local["lru_gate_w"][0]),
               gate_b=local["lru_gate_b"][0].reshape(2, D), lam=local["lru_lambda"])
    lru_first = ["lru_w_in", "lru_conv_w"]
    ride_pre = _Carry(gather_items(lru_first))
    ride_in = _Carry(gather_items(["lru_w_out"]))
    ffn0_first = ["ffn_w_up@0", "ffn_conv_w@0"]
    ride_core = _Carry(gather_items(ffn0_first))

    def lru_late(which):
        if which == "pre":
            full = gathered(lru_first, ride_pre.landed)
            lru.update(w_in=full["lru_w_in"][0], conv_w=full["lru_conv_w"][0])
        else:
            lru.update(w_out=gathered(["lru_w_out"], ride_in.landed)["lru_w_out"][0])

    x0 = x.reshape(n, D)
    x1, s_lru = _lru_fwd(x0, lru, t, carries={"pre": ride_pre, "in": ride_in, "core": ride_core}, late=lru_late)
    full = gathered(ffn0_first, ride_core.landed)
    ffn0 = dict(norm=local["ffn_norm"][0:1], conv_b=local["ffn_conv_b"][0:1], w_up=full["ffn_w_up@0"][0],
                conv_w=full["ffn_conv_w@0"][0])
    ride_up = _Carry(gather_items(GROUPS["rwkv"]))
    ride_down = _Carry(gather_items(["ffn_w_down@0"]))

    def ffn0_late():
        ffn0["w_down"] = gathered(["ffn_w_down@0"], ride_down.landed)["ffn_w_down@0"][0]

    x2, s_ffn0 = _ffn_fwd("ffn0", x1, ffn0, t, carries={"up": ride_up, "core": ride_down}, late=ffn0_late)
    full = gathered(GROUPS["rwkv"], ride_up.landed)
    rwkv = dict(norm=full["rwkv_norm"], mix=full["rwkv_mix"][0], w_r=full["rwkv_w_rkv"][0, 0],
                w_k=full["rwkv_w_rkv"][0, 1], w_v=full["rwkv_w_rkv"][0, 2], w0=full["rwkv_w0"], w1=full["rwkv_w1"][0],
                w2=full["rwkv_w2"][0], a0=full["rwkv_a0"], a1=full["rwkv_a1"][0], a2=full["rwkv_a2"][0],
                g1=full["rwkv_g1"][0], g2=full["rwkv_g2"][0], k_k=full["rwkv_k_k"], k_a=full["rwkv_k_a"],
                r_k=local["rwkv_r_k"].reshape(1, D), ln_w=full["rwkv_ln_w"], ln_b=full["rwkv_ln_b"],
                w_out=full["rwkv_w_out"][0])
    ride = _Carry(gather_items(GROUPS["ffn1"]))
    x3, s_rwkv = _rwkv_fwd(x2, rwkv, t, carry=ride)
    full = gathered(GROUPS["ffn1"], ride.landed)
    ffn1 = dict(norm=local["ffn_norm"][1:2], conv_b=local["ffn_conv_b"][1:2], w_up=full["ffn_w_up@1"][0],
                conv_w=full["ffn_conv_w@1"][0], w_down=full["ffn_w_down@1"][0])
    x4, s_ffn1 = _ffn_fwd("ffn1", x3, ffn1, t)
    loss, dx4, d_final = _loss_head(x4, target.reshape(n, D), local["final_norm"].reshape(1, D))

    dx3, g_ffn1 = _ffn_bwd("ffn1", dx4, s_ffn1, ffn1, t)
    ride = _Carry(scatter_items(GROUPS["ffn1"], ffn_grads(1, g_ffn1)))
    dx2, g_rwkv = _rwkv_bwd(dx3, s_rwkv, rwkv, t, carry=ride)
    scattered(GROUPS["ffn1"], ride.landed)
    rwkv_grads = {
        "rwkv_norm": g_rwkv["norm"], "rwkv_mix": g_rwkv["mix"][None],
        "rwkv_w_rkv": jnp.stack([g_rwkv["w_r"], g_rwkv["w_k"], g_rwkv["w_v"]])[None],
        "rwkv_w0": g_rwkv["w0"], "rwkv_w1": g_rwkv["w1"][None], "rwkv_w2": g_rwkv["w2"][None],
        "rwkv_a0": g_rwkv["a0"], "rwkv_a1": g_rwkv["a1"][None], "rwkv_a2": g_rwkv["a2"][None],
        "rwkv_g1": g_rwkv["g1"][None], "rwkv_g2": g_rwkv["g2"][None], "rwkv_k_k": g_rwkv["k_k"],
        "rwkv_k_a": g_rwkv["k_a"], "rwkv_ln_w": g_rwkv["ln_w"], "rwkv_ln_b": g_rwkv["ln_b"],
        "rwkv_w_out": g_rwkv["w_out"][None],
    }
    rwkv_small = split(GROUPS["rwkv"])[1]
    ride_da = _Carry(scatter_items(["rwkv_w_out"], rwkv_grads))
    ride_dw = _Carry(scatter_items(rwkv_small, rwkv_grads))
    ride_core = _Carry(scatter_items(["rwkv_w_rkv"], rwkv_grads))
    down_pieces = []

    def down_rides(g_w_down):
        blocked = _reshard(g_w_down[None], SHARD_AXIS["ffn_w_down"])
        rows = blocked.shape[2] // 3
        down_pieces.extend(_Carry([(blocked[:, :, i * rows:(i + 1) * rows], True)]) for i in range(3))
        return down_pieces

    dx1, g_ffn0 = _ffn_bwd("ffn0", dx2, s_ffn0, ffn0, t,
                           carries={"down_da": ride_da, "down_dw": ride_dw, "core": ride_core}, down_rides=down_rides)
    scattered(["rwkv_w_out"], ride_da.landed)
    scattered(rwkv_small, ride_dw.landed)
    scattered(["rwkv_w_rkv"], ride_core.landed)
    landed_large["ffn_w_down@0"] = jnp.concatenate([c.landed[0] for c in down_pieces], axis=2)
    ride = _Carry(scatter_items(ffn0_first, ffn_grads(0, g_ffn0)))
    out_pieces = []

    def out_rides(g_w_out):
        blocked = _reshard(g_w_out[None], SHARD_AXIS["lru_w_out"])
        rows = blocked.shape[2] // 2
        out_pieces.extend(_Carry([(blocked[:, :, i * rows:(i + 1) * rows], True)]) for i in range(2))
        return out_pieces

    dx0, g_lru = _lru_bwd(dx1, s_lru, lru, t, carry=ride, out_rides=out_rides)
    scattered(ffn0_first, ride.landed)
    landed_large["lru_w_out"] = jnp.concatenate([c.landed[0] for c in out_pieces], axis=2)

    g_ffn = [g_ffn0, g_ffn1]
    lru_grads = {"lru_w_in": g_lru["w_in"][None], "lru_conv_w": g_lru["conv_w"][None], "lru_w_out": g_lru["w_out"][None]}
    gfull = {
        "lru_norm": g_lru["norm"], "lru_b_in": jnp.concatenate([g_lru["b_y"], g_lru["b_x"]], axis=1),
        "lru_conv_b": g_lru["conv_b"], "lru_gate_w": _gate_blocks_grad(g_lru["gate_w"])[None],
        "lru_gate_b": g_lru["gate_b"].reshape(1, 2, N_HEAD, HEAD), "lru_lambda": g_lru["lam"],
        "lru_b_out": g_lru["b_out"], "rwkv_r_k": g_rwkv["r_k"].reshape(1, N_HEAD, HEAD),
        "ffn_norm": jnp.concatenate([g["norm"] for g in g_ffn]),
        "ffn_conv_b": jnp.concatenate([g["conv_b"] for g in g_ffn]),
        "final_norm": d_final.reshape(D),
    }
    *tail, replicated = _exchange(
        "grad_tail", scatter_items(lru_first, lru_grads) + [(_pack([gfull[k] for k in REPLICATED], F32), False)])
    scattered(lru_first, tail)

    out = {}
    kinds = ("grad", "delta", "new_m", "new_v")
    for k, landed in landed_large.items():
        flat = (math.prod(block(k).shape[:-1]), block(k).shape[-1])
        res = _adamw(f"adamw_{k.replace('@', '_')}", landed.reshape((N_DEV,) + flat),
                     *[block(k, pre).reshape(flat) for pre in ("", "m_", "v_")])
        for kind, a in zip(kinds, res):
            out[(kind, k)] = a.reshape(block(k).shape)
    small_sets = [(f"adamw_small_{i}", list(names), landed) for i, (names, landed) in enumerate(landed_small.items())]
    for tag, names, landed in small_sets + [("adamw_replicated", REPLICATED, replicated)]:
        packs = [_pack([block(k, pre) for k in names], F32) for pre in ("", "m_", "v_")]
        res = _adamw(tag, landed, *packs)
        for kind, buf in zip(kinds, res):
            for k, a in zip(names, _unpack(buf, [block(k).shape for k in names])):
                out[(kind, k)] = a
    for kind in ("grad", "delta", "new_m", "new_v"):
        for k in STACKED:
            out[(kind, k)] = jnp.concatenate([out[(kind, f"{k}@0")], out[(kind, f"{k}@1")]], axis=0)
    return loss, dx0.reshape(x.shape), out


def kernel(x, lru_norm, lru_w_in, lru_b_in, lru_conv_w, lru_conv_b, lru_gate_w, lru_gate_b, lru_lambda, lru_w_out, lru_b_out, rwkv_norm, rwkv_mix, rwkv_w_rkv, rwkv_w0, rwkv_w1, rwkv_w2, rwkv_a0, rwkv_a1, rwkv_a2, rwkv_g1, rwkv_g2, rwkv_k_k, rwkv_k_a, rwkv_r_k, rwkv_ln_w, rwkv_ln_b, rwkv_w_out, ffn_norm, ffn_w_up, ffn_conv_w, ffn_conv_b, ffn_w_down, final_norm, loss_target, m_lru_norm, m_lru_w_in, m_lru_b_in, m_lru_conv_w, m_lru_conv_b, m_lru_gate_w, m_lru_gate_b, m_lru_lambda, m_lru_w_out, m_lru_b_out, m_rwkv_norm, m_rwkv_mix, m_rwkv_w_rkv, m_rwkv_w0, m_rwkv_w1, m_rwkv_w2, m_rwkv_a0, m_rwkv_a1, m_rwkv_a2, m_rwkv_g1, m_rwkv_g2, m_rwkv_k_k, m_rwkv_k_a, m_rwkv_r_k, m_rwkv_ln_w, m_rwkv_ln_b, m_rwkv_w_out, m_ffn_norm, m_ffn_w_up, m_ffn_conv_w, m_ffn_conv_b, m_ffn_w_down, m_final_norm, v_lru_norm, v_lru_w_in, v_lru_b_in, v_lru_conv_w, v_lru_conv_b, v_lru_gate_w, v_lru_gate_b, v_lru_lambda, v_lru_w_out, v_lru_b_out, v_rwkv_norm, v_rwkv_mix, v_rwkv_w_rkv, v_rwkv_w0, v_rwkv_w1, v_rwkv_w2, v_rwkv_a0, v_rwkv_a1, v_rwkv_a2, v_rwkv_g1, v_rwkv_g2, v_rwkv_k_k, v_rwkv_k_a, v_rwkv_r_k, v_rwkv_ln_w, v_rwkv_ln_b, v_rwkv_w_out, v_ffn_norm, v_ffn_w_up, v_ffn_conv_w, v_ffn_conv_b, v_ffn_w_down, v_final_norm):
    args = locals()
    local = {(pre + k): args[pre + k] for pre in ("", "m_", "v_") for k in WEIGHTS}
    loss_local, grad_x, out = _step(local, x, loss_target)
    loss = lax.psum(loss_local, ("x", "y", "c"))
    return (loss, grad_x, *[out[(kind, k)] for kind in ("grad", "delta", "new_m", "new_v") for k in WEIGHTS])
```

```python
import functools
import math

import jax
import jax.numpy as jnp
from jax import lax
from jax.experimental import pallas as pl
from jax.experimental.pallas import tpu as pltpu

F32 = jnp.float32
BF16 = jnp.bfloat16
HI = lax.Precision.HIGHEST

N_DEV = 8
D = 1024
HEAD = 64
N_HEAD = D // HEAD
D_FF = 3 * D
LANE = 128
V7X_VMEM_BYTES = 64 * 1024 * 1024
VMEM_LIMIT = V7X_VMEM_BYTES - 8 * 1024 * 1024
CT = LANE
CHUNK = 64
SCAN_HEADS = 16
PACK_W = 1024
PACK_ROWS = 256

ADAM_LR, ADAM_B1, ADAM_B2, ADAM_EPS, ADAM_WD, ADAM_STEP = 0.001, 0.9, 0.999, 1e-08, 0.01, 10
RMS_EPS = 1e-6
GN_EPS = 64e-5
LRU_C = 8.0

WEIGHTS = ['lru_norm', 'lru_w_in', 'lru_b_in', 'lru_conv_w', 'lru_conv_b', 'lru_gate_w', 'lru_gate_b',
           'lru_lambda', 'lru_w_out', 'lru_b_out', 'rwkv_norm', 'rwkv_mix', 'rwkv_w_rkv', 'rwkv_w0', 'rwkv_w1',
           'rwkv_w2', 'rwkv_a0', 'rwkv_a1', 'rwkv_a2', 'rwkv_g1', 'rwkv_g2', 'rwkv_k_k', 'rwkv_k_a', 'rwkv_r_k',
           'rwkv_ln_w', 'rwkv_ln_b', 'rwkv_w_out', 'ffn_norm', 'ffn_w_up', 'ffn_conv_w', 'ffn_conv_b',
           'ffn_w_down', 'final_norm']
SHARD_AXIS = {'lru_w_in': 2, 'lru_conv_w': 2, 'lru_w_out': 1, 'rwkv_norm': 1, 'rwkv_mix': 2, 'rwkv_w_rkv': 2,
              'rwkv_w0': 1, 'rwkv_w1': 1, 'rwkv_w2': 2, 'rwkv_a0': 1, 'rwkv_a1': 1, 'rwkv_a2': 2, 'rwkv_g1': 1,
              'rwkv_g2': 2, 'rwkv_k_k': 1, 'rwkv_k_a': 1, 'rwkv_ln_w': 1, 'rwkv_ln_b': 1, 'rwkv_w_out': 1,
              'ffn_w_up': 2, 'ffn_conv_w': 2, 'ffn_w_down': 1}
GATHER_BF16 = ['lru_w_in', 'lru_w_out', 'rwkv_w_rkv', 'rwkv_w1', 'rwkv_a1', 'rwkv_g1', 'rwkv_w_out', 'ffn_w_up',
               'ffn_w_down']
REPLICATED = [n for n in WEIGHTS if n not in SHARD_AXIS]
STACKED = ['ffn_w_up', 'ffn_w_down', 'ffn_conv_w']
LARGE = ['lru_w_in', 'lru_w_out', 'rwkv_w_rkv', 'rwkv_w_out', 'ffn_w_up', 'ffn_w_down']
GROUPS = {
    "lru": ['lru_w_in', 'lru_w_out', 'lru_conv_w'],
    "ffn0": [f"{n}@0" for n in STACKED],
    "rwkv": [n for n in WEIGHTS if n.startswith("rwkv_") and n in SHARD_AXIS],
    "ffn1": [f"{n}@1" for n in STACKED],
}


def _base(name):
    return name.partition("@")[0]


def _pcall(body, **kw):
    return pl.pallas_call(body, **kw)


class _Carry:
    def __init__(self, items):
        self.items = list(items)
        self.landed = None


def _exchange_copies(src_ref, dst_ref, send_sems, recv_sems, local_sem, scatter):
    x, y, c = lax.axis_index("x"), lax.axis_index("y"), lax.axis_index("c")
    me = 4 * x + 2 * y + c
    mine = pltpu.make_async_copy(src_ref.at[me] if scatter else src_ref, dst_ref.at[me], local_sem)
    copies = []
    for m in range(1, N_DEV):
        px = 1 - x if m & 4 else x
        py = 1 - y if m & 2 else y
        pc = 1 - c if m & 1 else c
        part = src_ref.at[4 * px + 2 * py + pc] if scatter else src_ref
        copies.append(pltpu.make_async_remote_copy(
            src_ref=part, dst_ref=dst_ref.at[me], send_sem=send_sems.at[m - 1], recv_sem=recv_sems.at[m - 1],
            device_id=(px, py, pc), device_id_type=pl.DeviceIdType.MESH))
    return mine, copies


def _start_exchange(mine, copies):
    mine.start()
    for cp in copies:
        cp.start()


def _wait_exchange(mine, copies):
    for cp in copies:
        cp.wait_recv()
    for cp in copies:
        cp.wait_send()
    mine.wait()


_EXCHANGE_SEMS = [pltpu.SemaphoreType.DMA((N_DEV - 1,)), pltpu.SemaphoreType.DMA((N_DEV - 1,)), pltpu.SemaphoreType.DMA]


def _landing_shape(src, scatter):
    return jax.ShapeDtypeStruct((N_DEV,) + tuple(src.shape[1:] if scatter else src.shape), src.dtype)


def _call(body, operands, *, carry=None, name, grid, in_specs, out_specs, out_shape, scratch_shapes=(),
          compiler_params=None):
    if carry is None:
        return _pcall(body, name=name, grid=grid, in_specs=in_specs, out_specs=out_specs, out_shape=out_shape,
                      scratch_shapes=list(scratch_shapes), compiler_params=compiler_params)(*operands)
    single = not isinstance(out_specs, (list, tuple))
    out_specs_l = [out_specs] if single else list(out_specs)
    out_shape_l = [out_shape] if single else list(out_shape)
    n_in, n_out, n_scr, n_x = len(in_specs), len(out_specs_l), len(scratch_shapes), len(carry.items)
    flags = [sc for (_, sc) in carry.items]

    def wrapped(*refs):
        ins, refs = refs[:n_in], refs[n_in:]
        xsrc, refs = refs[:n_x], refs[n_x:]
        outs, refs = refs[:n_out], refs[n_out:]
        xdst, refs = refs[:n_x], refs[n_x:]
        scr, sems = refs[:n_scr], refs[n_scr:]
        first = functools.reduce(jnp.logical_and, [pl.program_id(i) == 0 for i in range(len(grid))])
        last = functools.reduce(jnp.logical_and, [pl.program_id(i) == grid[i] - 1 for i in range(len(grid))])

        def exchanges():
            return [_exchange_copies(xsrc[i], xdst[i], *sems[3 * i:3 * i + 3], flags[i]) for i in range(n_x)]

        @pl.when(first)
        def _():
            for mine, copies in exchanges():
                _start_exchange(mine, copies)

        body(*ins, *outs, *scr)

        @pl.when(last)
        def _():
            for mine, copies in exchanges():
                _wait_exchange(mine, copies)

    hbm = pl.BlockSpec(memory_space=pl.ANY)
    res = _pcall(
        wrapped, name=name, grid=grid, in_specs=list(in_specs) + [hbm] * n_x,
        out_specs=out_specs_l + [hbm] * n_x,
        out_shape=out_shape_l + [_landing_shape(a, sc) for (a, sc) in carry.items],
        scratch_shapes=list(scratch_shapes) + _EXCHANGE_SEMS * n_x, compiler_params=compiler_params,
    )(*operands, *[a for (a, _) in carry.items])
    carry.landed = list(res[n_out:])
    return res[0] if single else list(res[:n_out])


def _params(n_grid):
    return pltpu.CompilerParams(dimension_semantics=("arbitrary",) * n_grid, vmem_limit_bytes=VMEM_LIMIT)


def _shift_rows(x, d, up):
    n = x.shape[0]
    idx = lax.broadcasted_iota(jnp.int32, x.shape, 0)
    if up:
        return jnp.where(idx < n - d, pltpu.roll(x, n - d, 0), 0.0)
    return jnp.where(idx >= d, pltpu.roll(x, d, 0), 0.0)


@functools.partial(jax.custom_vjp, nondiff_argnums=(1,))
def _shift_down(x, d):
    return _shift_rows(x, d, False)


def _shift_down_fwd(x, d):
    return _shift_rows(x, d, False), None


def _shift_down_bwd(d, _, g):
    return (_shift_rows(g, d, True),)


_shift_down.defvjp(_shift_down_fwd, _shift_down_bwd)


def _scan_doubling(a, b, up):
    n = a.shape[0]
    d = 1
    while d < n:
        b = b + a * _shift_rows(b, d, up)
        a = a * _shift_rows(a, d, up)
        d *= 2
    return b


@jax.custom_vjp
def _linear_scan(a, b):
    return _scan_doubling(a, b, False)


def _linear_scan_fwd(a, b):
    h = _scan_doubling(a, b, False)
    return h, (a, h)


def _linear_scan_bwd(res, dh):
    a, h = res
    g = _scan_doubling(_shift_rows(a, 1, True), dh, True)
    return g * _shift_rows(h, 1, False), g


_linear_scan.defvjp(_linear_scan_fwd, _linear_scan_bwd)


def _causal_conv(x, w, b):
    k = w.shape[0]
    out = b + x * w[k - 1:k]
    for j in range(k - 1):
        out = out + _shift_down(x, k - 1 - j) * w[j:j + 1]
    return out


_GELU_C1 = math.sqrt(2.0 / math.pi)
_GELU_C2 = 0.044715 * _GELU_C1


@jax.custom_vjp
def _gelu(x):
    return 0.5 * x * (1.0 + jnp.tanh(x * (_GELU_C1 + _GELU_C2 * (x * x))))


def _gelu_fwd(x):
    x2 = x * x
    t = jnp.tanh(x * (_GELU_C1 + _GELU_C2 * x2))
    return 0.5 * x * (1.0 + t), (x, x2, t)


def _gelu_bwd(res, g):
    x, x2, t = res
    return (g * (0.5 * (1.0 + t) + (0.5 * x) * (1.0 - t * t) * (_GELU_C1 + (3.0 * _GELU_C2) * x2)),)


_gelu.defvjp(_gelu_fwd, _gelu_bwd)


def _rmsnorm(x, g):
    return x * lax.rsqrt(jnp.mean(x * x, axis=-1, keepdims=True) + RMS_EPS) * g


def _neg_expm1(x):
    series = x * (1.0 + x * 0.5 * (1.0 + x * (1.0 / 3.0) * (1.0 + x * 0.25 * (1.0 + x * 0.2))))
    return -jnp.where(x > -0.1, series, jnp.exp(x) - 1.0)


def _dot(a, b, ca=1, cb=0, precision=None):
    return lax.dot_general(a, b, (((ca,), (cb,)), ((), ())), precision=precision, preferred_element_type=F32)


def _bdot(a, b):
    return _dot(a.astype(BF16), b.astype(BF16))


def _split_bf16(x):
    hi = x.astype(BF16)
    return hi, (x - hi.astype(F32)).astype(BF16)


def _head_sum_impl(x):
    row = lax.broadcasted_iota(jnp.int32, (D, LANE), 0)
    col = lax.broadcasted_iota(jnp.int32, (D, LANE), 1)
    e = (lax.shift_right_logical(row, 6) == col).astype(BF16)
    hi, lo = _split_bf16(x)
    s_hi, s_lo = _split_bf16(_dot(hi, e) + _dot(lo, e))
    return _dot(s_hi, e, 1, 1) + _dot(s_lo, e, 1, 1)


@jax.custom_vjp
def _head_sum(x):
    return _head_sum_impl(x)


def _head_sum_fwd(x):
    return _head_sum_impl(x), None


def _head_sum_bwd(_, g):
    return (_head_sum(g),)


_head_sum.defvjp(_head_sum_fwd, _head_sum_bwd)


def _specs(items):
    return [pl.BlockSpec(it[1], it[2]) for it in items]


def _stage_fwd(name, f, grid, ins, params, outs, carry=None):
    n_in = len(ins) + len(params)

    def body(*refs):
        res = f(*[r[...] for r in refs[:n_in]])
        for o, v in zip(refs[n_in:], res):
            o[...] = v.astype(o.dtype)

    return _call(
        body, [it[0] for it in ins + params], carry=carry, name=name, grid=grid, in_specs=_specs(ins + params),
        out_specs=[pl.BlockSpec(bs, im) for (_, _, bs, im) in outs],
        out_shape=[jax.ShapeDtypeStruct(s, dt) for (s, dt, _, _) in outs],
        compiler_params=_params(len(grid)),
    )


def _stage_bwd(name, f, grid, ins, params, douts, din_dtypes, adds=None, carry=None, bf16_copy=False):
    adds = adds or {}
    n_in, n_par = len(ins), len(params)
    dout_items = [(a, bs, im) for (arrs, bs, im) in douts for a in arrs]
    add_items = [(adds[i], ins[i][1], ins[i][2]) for i in sorted(adds)]
    n_do, n_add = len(dout_items), len(add_items)

    def body(*refs):
        vals = [r[...] for r in refs[:n_in + n_par]]
        do_refs = list(refs[n_in + n_par:n_in + n_par + n_do])
        add_refs = dict(zip(sorted(adds), refs[n_in + n_par + n_do:n_in + n_par + n_do + n_add]))
        din_refs = refs[n_in + n_par + n_do + n_add:n_in + n_par + n_do + n_add + n_in]
        dpar_refs = refs[n_in + n_par + n_do + n_add + n_in:n_in + n_par + n_do + n_add + n_in + n_par]
        cts = []
        for (arrs, _, _) in douts:
            ct = do_refs.pop(0)[...].astype(F32)
            for _ in arrs[1:]:
                ct = ct + do_refs.pop(0)[...].astype(F32)
            cts.append(ct)
        _, vjp = jax.vjp(f, *vals)
        grads = vjp(tuple(cts))
        for i, r in enumerate(din_refs):
            g = grads[i]
            if i in add_refs:
                g = g + add_refs[i][...]
            r[...] = g.astype(r.dtype)
            if i == 0 and bf16_copy:
                refs[-1][...] = g.astype(BF16)

        @pl.when(pl.program_id(len(grid) - 1) == 0)
        def _():
            for r in dpar_refs:
                r[...] = jnp.zeros(r.shape, r.dtype)

        for j, r in enumerate(dpar_refs):
            r[...] += grads[n_in + j]

    din_shapes = [it[3][0] if len(it) > 3 else it[0].shape for it in ins]
    din_specs = [pl.BlockSpec(it[1], it[3][1] if len(it) > 3 else it[2]) for it in ins]
    copy_specs = [din_specs[0]] if bf16_copy else []
    copy_shapes = [jax.ShapeDtypeStruct(din_shapes[0], BF16)] if bf16_copy else []
    res = _call(
        body, [it[0] for it in ins + params + dout_items + add_items], carry=carry, name=name, grid=grid,
        in_specs=_specs(ins + params + dout_items + add_items), out_specs=din_specs + _specs(params) + copy_specs,
        out_shape=[jax.ShapeDtypeStruct(s, dt) for s, dt in zip(din_shapes, din_dtypes)]
        + [jax.ShapeDtypeStruct(it[0].shape, F32) for it in params] + copy_shapes,
        compiler_params=_params(len(grid)),
    )
    return list(res[:n_in]) + list(res[n_in + n_par:]), list(res[n_in:n_in + n_par])


def _tile(n, want):
    t = min(n, want)
    while n % t:
        t //= 2
    return t


def _matmul(name, a, b, *, mode="nn", resid=None, out_dtype=F32, tm=512, tn=512, tk=1024, b_koff=0, carry=None,
            a_plus=None, b_plus=None):
    if mode == "tn":
        (kdim, m), n = a.shape, b.shape[1]
    else:
        (m, kdim), n = a.shape, (b.shape[1] if mode == "nn" else b.shape[0])
    tm, tn, tk = _tile(m, tm), _tile(n, tn), _tile(kdim, tk)
    nk = kdim // tk
    ko = b_koff // tk
    assert ko * tk == b_koff
    a_spec = pl.BlockSpec((tk, tm), lambda i, j, k: (k, i)) if mode == "tn" else pl.BlockSpec((tm, tk), lambda i, j, k: (i, k))
    b_spec = pl.BlockSpec((tn, tk), lambda i, j, k: (j, k + ko)) if mode == "nt" else pl.BlockSpec((tk, tn), lambda i, j, k: (k + ko, j))
    ca = 0 if mode == "tn" else 1
    cb = 1 if mode == "nt" else 0
    operands = [a, b]
    in_specs = [a_spec, b_spec]
    extra = {}
    for key, arr, spec in (("a", a_plus, a_spec), ("b", b_plus, b_spec), ("resid", resid, pl.BlockSpec((tm, tn), lambda i, j, k: (i, j)))):
        if arr is not None:
            extra[key] = len(operands)
            operands.append(arr)
            in_specs.append(spec)

    def product(refs):
        av, bv = refs[0][...], refs[1][...]
        if "a" in extra:
            av = av + refs[extra["a"]][...]
        if "b" in extra:
            bv = bv + refs[extra["b"]][...]
        return _dot(av.astype(BF16), bv.astype(BF16), ca, cb)

    def finish(r, refs, o_ref):
        if resid is not None:
            r = r + refs[extra["resid"]][...]
        o_ref[...] = r.astype(o_ref.dtype)

    def body_one_step(*refs):
        finish(product(refs), refs, refs[-1])

    def body(*refs):
        o_ref, acc_ref = refs[-2], refs[-1]
        k = pl.program_id(2)

        @pl.when(k == 0)
        def _():
            acc_ref[...] = jnp.zeros(acc_ref.shape, F32)

        acc_ref[...] += product(refs)

        @pl.when(k == nk - 1)
        def _():
            finish(acc_ref[...], refs, o_ref)

    return _call(
        body_one_step if nk == 1 else body, operands, carry=carry, name=name, grid=(m // tm, n // tn, nk),
        in_specs=in_specs, out_specs=pl.BlockSpec((tm, tn), lambda i, j, k: (i, j)),
        out_shape=jax.ShapeDtypeStruct((m, n), out_dtype),
        scratch_shapes=[] if nk == 1 else [pltpu.VMEM((tm, tn), F32)],
        compiler_params=_params(3),
    )


def _f_lru_pre(x, norm, b_out):
    return _rmsnorm(x, norm), x + b_out


def _f_lru_core(uy, ux, b_y, b_x, cw, cb, gw, gb, lam):
    yb = _gelu(uy + b_y)
    xr = _causal_conv(ux + b_x, cw, cb)
    gr = jax.nn.sigmoid(_bdot(xr, gw[0, 0]) + gb[0:1])
    gi = jax.nn.sigmoid(_bdot(xr, gw[1, 0]) + gb[1:2])
    log_a = -LRU_C * gr * jax.nn.softplus(-lam)
    a = jnp.exp(log_a)
    bterm = jnp.sqrt(_neg_expm1(2.0 * log_a)) * (gi * xr)
    return (_linear_scan(a, bterm) * yb,)


def _f_norm(x, norm):
    return (_rmsnorm(x, norm),)


def _f_ffn_core(ug, uv, cw, cb):
    return (_gelu(_causal_conv(ug, cw, cb)) * uv,)


def _f_rwkv_mix(h, mix):
    xx = _shift_down(h, 1) - h
    return tuple(h + xx * mix[i:i + 1] for i in range(6))


def _f_rwkv_pre(k, lw1, la1, lg1, w0, a0, k_k, k_a, w2, a2, g2):
    wpre = w0 + _bdot(jnp.tanh(lw1), w2)
    apre = a0 + _bdot(la1, a2)
    g = _bdot(jax.nn.sigmoid(lg1), g2)
    log_decay = -jnp.exp(-jax.nn.softplus(-wpre) - 0.5)
    a = jax.nn.sigmoid(apre)
    kk = k * k_k
    kk = kk / jnp.maximum(jnp.sqrt(_head_sum(kk * kk)), 1e-12)
    kf = k * (1.0 + (a - 1.0) * k_a)
    return log_decay, kf, -kk, kk * a, g


def _f_rwkv_post(y, r, kf, v, g, ln_w, ln_b, r_k):
    inv = 1.0 / HEAD
    yc = y - _head_sum(y) * inv
    var = _head_sum(yc * yc) * inv
    yn = yc * lax.rsqrt(var + GN_EPS) * ln_w + ln_b
    bonus = _head_sum(r * kf * r_k) * v
    return ((yn + bonus) * g,)


def _hdot(a, b, ca, cb):
    return lax.dot_general(a, b, (((ca,), (cb,)), ((0,), (0,))), preferred_element_type=F32)


def _hmm_impl(a, b, ca, cb):
    return _hdot(a.astype(BF16), b.astype(BF16), ca, cb)


@functools.partial(jax.custom_vjp, nondiff_argnums=(2, 3))
def _hmm(a, b, ca, cb):
    return _hmm_impl(a, b, ca, cb)


def _hmm_fwd(a, b, ca, cb):
    return _hmm_impl(a, b, ca, cb), (a, b)


def _hmm_bwd(ca, cb, res, dc):
    a, b = res
    fa, fb = 3 - ca, 3 - cb
    da = _hmm(dc, b, 2, fb) if ca == 2 else _hmm(b, dc, fb, 2)
    db = _hmm(a, dc, fa, 1) if cb == 1 else _hmm(dc, a, 1, fa)
    return da, db


_hmm.defvjp(_hmm_fwd, _hmm_bwd)


def _tri_dot_impl(x, upper):
    g, n, _ = x.shape
    row = lax.broadcasted_iota(jnp.int32, (g, n, 3 * n), 1)
    col = lax.broadcasted_iota(jnp.int32, (g, n, 3 * n), 2)
    col = jnp.where(col >= 2 * n, col - 2 * n, jnp.where(col >= n, col - n, col))
    tri = (row <= col if upper else row >= col).astype(BF16)
    hi = x.astype(BF16)
    rem = x - hi.astype(F32)
    mid = rem.astype(BF16)
    lo = (rem - mid.astype(F32)).astype(BF16)
    return _hdot(tri, jnp.concatenate([hi, mid, lo], axis=1), 2, 1)


@functools.partial(jax.custom_vjp, nondiff_argnums=(1,))
def _tri_dot(x, upper):
    return _tri_dot_impl(x, upper)


def _tri_dot_fwd(x, upper):
    return _tri_dot_impl(x, upper), None


def _tri_dot_bwd(upper, _, g):
    return (_tri_dot(g, not upper),)


_tri_dot.defvjp(_tri_dot_fwd, _tri_dot_bwd)


def _unit_lower_inverse_impl(a):
    g, n, _ = a.shape
    row = lax.broadcasted_iota(jnp.int32, (g, n, n), 1)
    col = lax.broadcasted_iota(jnp.int32, (g, n, n), 2)
    inv = (row == col).astype(F32) + a
    p = a
    for _ in range(int(math.log2(n)) - 1):
        p = _hmm(p, p, 2, 1)
        inv = inv + _hmm(inv, p, 2, 1)
    return inv


@jax.custom_vjp
def _unit_lower_inverse(a):
    return _unit_lower_inverse_impl(a)


def _unit_lower_inverse_fwd(a):
    inv = _unit_lower_inverse_impl(a)
    return inv, inv


def _unit_lower_inverse_bwd(inv, d_inv):
    return (_hmm(_hmm(inv, d_inv, 1, 1), inv, 2, 2),)


_unit_lower_inverse.defvjp(_unit_lower_inverse_fwd, _unit_lower_inverse_bwd)


def _rwkv_chunk(z0, r, lw, k, v, a, b):
    g, n, _ = r.shape
    row = lax.broadcasted_iota(jnp.int32, (g, n, n), 1)
    col = lax.broadcasted_iota(jnp.int32, (g, n, n), 2)
    incl, strict = row >= col, row > col
    cs = _tri_dot(lw, False)
    c_last = cs[:, n - 1:n]
    inv = jnp.exp(-cs)
    ar = jnp.concatenate([a * jnp.exp(cs - lw), r * jnp.exp(cs)], axis=1)
    bk = jnp.concatenate([b * inv, k * inv], axis=1)
    pair = _hmm(ar, bk, 2, 2)
    a_ab = jnp.where(strict, pair[:, :n, :n], 0.0)
    a_ak = jnp.where(strict, pair[:, :n, n:], 0.0)
    a_rbk = jnp.concatenate([jnp.where(incl, pair[:, n:, :n], 0.0), jnp.where(incl, pair[:, n:, n:], 0.0)], axis=2)
    arz = _hmm(ar, z0, 2, 1)
    u = _hmm(_unit_lower_inverse(a_ab), arz[:, :n] + _hmm(a_ak, v, 2, 1), 2, 1)
    uv = jnp.concatenate([u, v], axis=1)
    y = arz[:, n:] + _hmm(a_rbk, uv, 2, 1)
    tail = jnp.exp(c_last - cs)
    er = lax.broadcasted_iota(jnp.int32, (g, HEAD, HEAD), 1)
    ec = lax.broadcasted_iota(jnp.int32, (g, HEAD, HEAD), 2)
    decay_all = jnp.where(er == ec, jnp.exp(c_last), 0.0)
    z_l = _hmm(jnp.concatenate([b * tail, k * tail, decay_all], axis=1), jnp.concatenate([uv, z0], axis=1), 1, 1)
    return y, z_l


def _heads_in(ref, g):
    x = ref[...]
    return jnp.stack([x[:, h * HEAD:(h + 1) * HEAD] for h in range(g)], axis=0)


def _heads_out(ref, x):
    ref[...] = jnp.concatenate([x[h] for h in range(x.shape[0])], axis=1)


def _scan_geometry(n_rows, t):
    g, n = _tile(N_HEAD, SCAN_HEADS), _tile(t, CHUNK)
    return g, n, t // n, N_HEAD // g, n_rows // t


def _rwkv_scan_fwd(seqs, t, carry=None):
    rows = seqs[0].shape[0]
    g, n, nc, hpg, bsz = _scan_geometry(rows, t)

    def body(r_ref, lw_ref, k_ref, v_ref, a_ref, b_ref, y_ref, zs_ref, z_ref):
        @pl.when(pl.program_id(1) == 0)
        def _():
            z_ref[...] = jnp.zeros(z_ref.shape, F32)

        z0 = z_ref[...]
        zs_ref[:, 0] = z0
        y, z_l = _rwkv_chunk(z0, *[_heads_in(ref, g) for ref in (r_ref, lw_ref, k_ref, v_ref, a_ref, b_ref)])
        _heads_out(y_ref, y)
        z_ref[...] = z_l

    seq_spec = pl.BlockSpec((n, g * HEAD), lambda i, c: ((i // hpg) * nc + c, i % hpg))
    return _call(
        body, list(seqs), carry=carry, name="rwkv_scan_fwd", grid=(bsz * hpg, nc), in_specs=[seq_spec] * 6,
        out_specs=[seq_spec, pl.BlockSpec((g, 1, HEAD, HEAD), lambda i, c: (i, c, 0, 0))],
        out_shape=[jax.ShapeDtypeStruct((rows, D), F32), jax.ShapeDtypeStruct((bsz * N_HEAD, nc, HEAD, HEAD), F32)],
        scratch_shapes=[pltpu.VMEM((g, HEAD, HEAD), F32)],
        compiler_params=_params(2),
    )


def _rwkv_scan_bwd(seqs, zs, dy, t, carry=None):
    rows = seqs[0].shape[0]
    g, n, nc, hpg, bsz = _scan_geometry(rows, t)

    def body(r_ref, lw_ref, k_ref, v_ref, a_ref, b_ref, zs_ref, dy_ref, dr, dlw, dk, dv, da, db, dz_ref):
        @pl.when(pl.program_id(1) == 0)
        def _():
            dz_ref[...] = jnp.zeros(dz_ref.shape, F32)

        _, vjp = jax.vjp(_rwkv_chunk, zs_ref[:, 0],
                         *[_heads_in(ref, g) for ref in (r_ref, lw_ref, k_ref, v_ref, a_ref, b_ref)])
        grads = vjp((_heads_in(dy_ref, g), dz_ref[...]))
        dz_ref[...] = grads[0]
        for o, gr in zip((dr, dlw, dk, dv, da, db), grads[1:]):
            _heads_out(o, gr)

    seq_spec = pl.BlockSpec((n, g * HEAD), lambda i, c: ((i // hpg) * nc + nc - 1 - c, i % hpg))
    return _call(
        body, [*seqs, zs, dy], carry=carry, name="rwkv_scan_bwd", grid=(bsz * hpg, nc),
        in_specs=[seq_spec] * 6 + [pl.BlockSpec((g, 1, HEAD, HEAD), lambda i, c: (i, nc - 1 - c, 0, 0)), seq_spec],
        out_specs=[seq_spec] * 6,
        out_shape=[jax.ShapeDtypeStruct((rows, D), F32)] * 6,
        scratch_shapes=[pltpu.VMEM((g, HEAD, HEAD), F32)],
        compiler_params=_params(2),
    )


def _loss_head(x, target, norm):
    n = x.shape[0]
    tr = _tile(n, 256)

    def f(xv, gv, tv):
        err = _rmsnorm(xv, gv) - tv
        return 0.5 * jnp.sum(jnp.mean(err * err, axis=-1, keepdims=True), axis=0, keepdims=True)

    def body(x_ref, t_ref, g_ref, dx_ref, dg_ref, loss_ref, dx16_ref):
        val, vjp = jax.vjp(lambda xv, gv: f(xv, gv, t_ref[...]), x_ref[...], g_ref[...])
        dx, dg = vjp(jnp.ones((1, 1), F32))
        dx_ref[...] = dx
        dx16_ref[...] = dx.astype(BF16)

        @pl.when(pl.program_id(0) == 0)
        def _():
            dg_ref[...] = jnp.zeros(dg_ref.shape, F32)
            loss_ref[...] = jnp.zeros(loss_ref.shape, F32)

        dg_ref[...] += dg
        loss_ref[...] += jnp.broadcast_to(val, loss_ref.shape)

    row = pl.BlockSpec((tr, D), lambda i: (i, 0))
    vec = pl.BlockSpec((1, D), lambda i: (0, 0))
    dx, dg, loss, dx16 = _pcall(
        body, name="loss_head", grid=(n // tr,), in_specs=[row, row, vec],
        out_specs=[row, vec, pl.BlockSpec((8, LANE), lambda i: (0, 0)), row],
        out_shape=[jax.ShapeDtypeStruct((n, D), F32), jax.ShapeDtypeStruct((1, D), F32),
                   jax.ShapeDtypeStruct((8, LANE), F32), jax.ShapeDtypeStruct((n, D), BF16)],
        compiler_params=_params(1),
    )(x, target, norm)
    return loss[0, 0], (dx, dx16), dg


def _exchange(name, items):
    n_x = len(items)

    def body(*refs):
        srcs, dsts, sems = refs[:n_x], refs[n_x:2 * n_x], refs[2 * n_x:]
        started = [_exchange_copies(srcs[i], dsts[i], *sems[3 * i:3 * i + 3], items[i][1]) for i in range(n_x)]
        for mine, copies in started:
            _start_exchange(mine, copies)
        for mine, copies in started:
            _wait_exchange(mine, copies)

    hbm = pl.BlockSpec(memory_space=pl.ANY)
    return _pcall(
        body, name=name, in_specs=[hbm] * n_x, out_specs=[hbm] * n_x,
        out_shape=[_landing_shape(a, sc) for (a, sc) in items], scratch_shapes=_EXCHANGE_SEMS * n_x,
    )(*[a for (a, _) in items])


def _adamw(name, parts, w, m, v):
    rows, cols = w.shape
    tr = _tile(rows, PACK_ROWS)

    def body(p_ref, w_ref, m_ref, v_ref, g_ref, d_ref, nm_ref, nv_ref):
        g = p_ref[0]
        for k in range(1, N_DEV):
            g = g + p_ref[k]
        nm = ADAM_B1 * m_ref[...] + (1.0 - ADAM_B1) * g
        nv = ADAM_B2 * v_ref[...] + (1.0 - ADAM_B2) * jnp.square(g)
        m_hat = nm / (1.0 - ADAM_B1 ** ADAM_STEP)
        v_hat = nv / (1.0 - ADAM_B2 ** ADAM_STEP)
        g_ref[...] = g
        d_ref[...] = -ADAM_LR * (m_hat / (jnp.sqrt(v_hat) + ADAM_EPS) + ADAM_WD * w_ref[...])
        nm_ref[...] = nm
        nv_ref[...] = nv

    row = pl.BlockSpec((tr, cols), lambda i: (i, 0))
    return _pcall(
        body, name=name, grid=(rows // tr,),
        in_specs=[pl.BlockSpec((N_DEV, tr, cols), lambda i: (0, i, 0)), row, row, row],
        out_specs=[row] * 4, out_shape=[jax.ShapeDtypeStruct((rows, cols), F32)] * 4,
        compiler_params=_params(1),
    )(parts, w, m, v)


def _pack(arrs, dtype, lead=()):
    flat = jnp.concatenate([a.astype(dtype).reshape(lead + (-1,)) for a in arrs], axis=-1)
    n = flat.shape[-1]
    quantum = PACK_W * PACK_ROWS
    total = -(-n // quantum) * quantum
    flat = jnp.pad(flat, [(0, 0)] * len(lead) + [(0, total - n)])
    return flat.reshape(lead + (total // PACK_W, PACK_W))


def _unpack(buf, shapes, lead=()):
    flat = buf.reshape(lead + (-1,))
    out, off = [], 0
    for s in shapes:
        n = math.prod(s)
        out.append(flat[..., off:off + n].reshape(lead + tuple(s)))
        off += n
    return out


def _unshard(g, axis):
    local = g.shape[1:]
    full = jnp.moveaxis(g, 0, axis)
    return full.reshape(local[:axis] + (N_DEV * local[axis],) + local[axis + 1:])


def _reshard(full, axis):
    s = full.shape
    blocked = full.reshape(s[:axis] + (N_DEV, s[axis] // N_DEV) + s[axis + 1:])
    return jnp.moveaxis(blocked, axis, 0)


def _row_item(a, tr):
    return (a, (tr, a.shape[1]), lambda i: (i, 0))


def _vec_item(a):
    return (a, a.shape, lambda i: (0,) * a.ndim)


def _seq_item(a, t, width):
    return (a, (t, width), lambda j, b: (b, j))


def _seq_halves(a, t, width):
    half = a.shape[1] // 2
    off = half // width
    grad = ((a.shape[0], half), lambda j, b: (b, j))
    return [(a, (t, width), lambda j, b: (b, j), grad), (a, (t, width), lambda j, b: (b, j + off), grad)]


def _col_item(a, width):
    return (a, (a.shape[0], width), lambda j, b: (0, j))


def _both(g):
    return g if isinstance(g, tuple) else (g, g)


def _ffn_fwd(tag, x, p, t, carries=None, late=None):
    carries = carries or {}
    n = x.shape[0]
    tr = _tile(n, 512)
    norm_ins, norm_par = [_row_item(x, tr)], [_vec_item(p["norm"])]
    (h,) = _stage_fwd(f"{tag}_norm", _f_norm, (n // tr,), norm_ins, norm_par, [((n, D), BF16, (tr, D), lambda i: (i, 0))])
    u = _matmul(f"{tag}_up", h, p["w_up"], tm=2048, tn=768, carry=carries.get("up"))
    grid = (D_FF // CT, n // t)
    core_par = [_col_item(p["conv_w"], CT), _col_item(p["conv_b"], CT)]
    (hid,) = _stage_fwd(f"{tag}_core", _f_ffn_core, grid, _seq_halves(u, t, CT), core_par,
                        [((n, D_FF), BF16, (t, CT), lambda j, b: (b, j))], carry=carries.get("core"))
    if late is not None:
        late()
    y = _matmul(f"{tag}_down", hid, p["w_down"], resid=x, tm=1024, tn=1024, tk=D_FF)
    return y, (x, h, u, hid)


def _ffn_bwd(tag, dy, saved, p, t, carries=None, down_rides=None):
    carries = carries or {}
    x, h, u, hid = saved
    n = x.shape[0]
    tr = _tile(n, 512)
    grads = {}
    dy, dy16 = _both(dy)
    dhid = _matmul(f"{tag}_down_da", dy16, p["w_down"], mode="nt", tm=1024, tn=768, carry=carries.get("down_da"))
    grads["w_down"] = _matmul(f"{tag}_down_dw", hid, dy16, mode="tn", tm=768, tn=1024, tk=2048,
                              carry=carries.get("down_dw"))
    after = down_rides(grads["w_down"]) if down_rides is not None else [None] * 3
    grid = (D_FF // CT, n // t)
    core_par = [_col_item(p["conv_w"], CT), _col_item(p["conv_b"], CT)]
    (dug, duv), (grads["conv_w"], grads["conv_b"]) = _stage_bwd(
        f"{tag}_core_bwd", _f_ffn_core, grid, _seq_halves(u, t, CT), core_par,
        [([dhid], (t, CT), lambda j, b: (b, j))], [BF16, BF16], carry=carries.get("core"))
    dh = _matmul(f"{tag}_up_da_g", dug, p["w_up"], mode="nt", tm=1024, tn=1024, tk=D_FF, carry=after[0])
    dh = _matmul(f"{tag}_up_da_v", duv, p["w_up"], mode="nt", tm=1024, tn=1024, tk=D_FF, b_koff=D_FF, resid=dh,
                 carry=after[1])
    grads["w_up"] = jnp.concatenate([
        _matmul(f"{tag}_up_dw_g", h, dug, mode="tn", tm=1024, tn=768, tk=2048, carry=after[2]),
        _matmul(f"{tag}_up_dw_v", h, duv, mode="tn", tm=1024, tn=768, tk=2048)], axis=1)
    dx, (grads["norm"],) = _stage_bwd(
        f"{tag}_norm_bwd", _f_norm, (n // tr,), [_row_item(x, tr)], [_vec_item(p["norm"])],
        [([dh], (tr, D), lambda i: (i, 0))], [F32], adds={0: dy}, bf16_copy=True)
    return tuple(dx), grads


def _lru_fwd(x, p, t, carries=None, late=None):
    carries = carries or {}
    n = x.shape[0]
    tr = _tile(n, 512)
    row = lambda dt: ((n, D), dt, (tr, D), lambda i: (i, 0))
    h, xb = _stage_fwd("lru_pre", _f_lru_pre, (n // tr,), [_row_item(x, tr)],
                       [_vec_item(p["norm"]), _vec_item(p["b_out"])], [row(BF16), row(F32)], carry=carries.get("pre"))
    if late is not None:
        late("pre")
    u = _matmul("lru_in", h, p["w_in"], tm=1024, tn=1024, carry=carries.get("in"))
    if late is not None:
        late("in")
    out, = _stage_fwd("lru_core", _f_lru_core, (D // CT, n // t), _seq_halves(u, t, CT), _lru_core_params(p),
                      [((n, D), BF16, (t, CT), lambda j, b: (b, j))], carry=carries.get("core"))
    y = _matmul("lru_out", out, p["w_out"], resid=xb, tm=1024, tn=1024)
    return y, (x, h, u, out)


def _lru_core_params(p):
    return [_col_item(p["b_y"], CT), _col_item(p["b_x"], CT), _col_item(p["conv_w"], CT), _col_item(p["conv_b"], CT),
            (p["gate_w"], (2, 1, CT, CT), lambda j, b: (0, j, 0, 0)), _col_item(p["gate_b"], CT), _col_item(p["lam"], CT)]


def _lru_bwd(dy, saved, p, t, carry=None, out_rides=None):
    x, h, u, out = saved
    n = x.shape[0]
    tr = _tile(n, 512)
    grads = {}
    dy, dy16 = _both(dy)
    dout = _matmul("lru_out_da", dy16, p["w_out"], mode="nt", tm=1024, tn=1024)
    grads["w_out"] = _matmul("lru_out_dw", out, dy16, mode="tn", tm=1024, tn=1024)
    after = out_rides(grads["w_out"]) if out_rides is not None else [None] * 2
    (duy, dux), core_grads = _stage_bwd("lru_core_bwd", _f_lru_core, (D // CT, n // t), _seq_halves(u, t, CT),
                                        _lru_core_params(p), [([dout], (t, CT), lambda j, b: (b, j))], [BF16, BF16],
                                        carry=carry)
    for k, g in zip(("b_y", "b_x", "conv_w", "conv_b", "gate_w", "gate_b", "lam"), core_grads):
        grads[k] = g
    dh = _matmul("lru_in_da_y", duy, p["w_in"], mode="nt", tm=1024, tn=1024, carry=after[0])
    dh = _matmul("lru_in_da_x", dux, p["w_in"], mode="nt", tm=1024, tn=1024, b_koff=D, resid=dh, carry=after[1])
    grads["w_in"] = jnp.concatenate([_matmul("lru_in_dw_y", h, duy, mode="tn", tm=1024, tn=1024),
                                     _matmul("lru_in_dw_x", h, dux, mode="tn", tm=1024, tn=1024)], axis=1)
    row = (tr, D), lambda i: (i, 0)
    (dx,), (grads["norm"], grads["b_out"]) = _stage_bwd(
        "lru_pre_bwd", _f_lru_pre, (n // tr,), [_row_item(x, tr)], [_vec_item(p["norm"]), _vec_item(p["b_out"])],
        [([dh], *row), ([dy], *row)], [F32])
    return dx, grads


_RWKV_PRE_VECS = ("w0", "a0", "k_k", "k_a", "w2", "a2", "g2")
_RWKV_POST_VECS = ("ln_w", "ln_b", "r_k")


def _rwkv_fwd(x, p, t, carry=None):
    n = x.shape[0]
    bsz = n // t
    tr = _tile(n, 512)
    ts = _tile(n, 256)
    (h,) = _stage_fwd("rwkv_norm", _f_norm, (n // tr,), [_row_item(x, tr)], [_vec_item(p["norm"])],
                      [((n, D), F32, (tr, D), lambda i: (i, 0))])
    mixed = _stage_fwd("rwkv_mix", _f_rwkv_mix, (D // CT, bsz), [_seq_item(h, t, CT)], [_col_item(p["mix"], CT)],
                       [((n, D), BF16, (t, CT), lambda j, b: (b, j))] * 6)
    xr, xk, xv, xw, xa, xg = mixed
    r = _matmul("rwkv_r", xr, p["w_r"], tm=1024, tn=1024)
    k = _matmul("rwkv_k", xk, p["w_k"], tm=1024, tn=1024)
    v = _matmul("rwkv_v", xv, p["w_v"], tm=1024, tn=1024)
    lw1 = _matmul("rwkv_w1", xw, p["w1"], tm=1024)
    la1 = _matmul("rwkv_a1", xa, p["a1"], tm=1024)
    lg1 = _matmul("rwkv_g1", xg, p["g1"], tm=1024)
    pre_ins = [_row_item(a, ts) for a in (k, lw1, la1, lg1)]
    pre_par = [_vec_item(p[q]) for q in _RWKV_PRE_VECS]
    row = ((n, D), F32, (ts, D), lambda i: (i, 0))
    lw, kf, aa, bb, g = _stage_fwd("rwkv_pre", _f_rwkv_pre, (n // ts,), pre_ins, pre_par, [row] * 5)
    y, zs = _rwkv_scan_fwd([r, lw, kf, v, aa, bb], t, carry=carry)
    post_ins = [_row_item(a, ts) for a in (y, r, kf, v, g)]
    post_par = [_vec_item(p[q]) for q in _RWKV_POST_VECS]
    (yg,) = _stage_fwd("rwkv_post", _f_rwkv_post, (n // ts,), post_ins, post_par,
                       [((n, D), BF16, (ts, D), lambda i: (i, 0))])
    out = _matmul("rwkv_out", yg, p["w_out"], resid=x, tm=1024, tn=1024)
    return out, (x, h, mixed, r, k, v, lw1, la1, lg1, lw, kf, aa, bb, g, zs, y, yg)


def _rwkv_bwd(dout, saved, p, t, carry=None):
    x, h, mixed, r, k, v, lw1, la1, lg1, lw, kf, aa, bb, g, zs, y, yg = saved
    xr, xk, xv, xw, xa, xg = mixed
    n = x.shape[0]
    bsz = n // t
    tr = _tile(n, 512)
    ts = _tile(n, 256)
    grads = {}
    row_s = (ts, D), lambda i: (i, 0)
    dout, dout16 = _both(dout)
    dyg = _matmul("rwkv_out_da", dout16, p["w_out"], mode="nt", tm=1024, tn=1024)
    grads["w_out"] = _matmul("rwkv_out_dw", yg, dout16, mode="tn", tm=1024, tn=1024)
    post_ins = [_row_item(a, ts) for a in (y, r, kf, v, g)]
    post_par = [_vec_item(p[q]) for q in _RWKV_POST_VECS]
    (dy, dr_a, dkf_a, dv_a, dg), post_grads = _stage_bwd(
        "rwkv_post_bwd", _f_rwkv_post, (n // ts,), post_ins, post_par, [([dyg], *row_s)], [F32] * 5)
    grads.update(zip(_RWKV_POST_VECS, post_grads))
    dr_b, dlw, dkf_b, dv_b, daa, dbb = _rwkv_scan_bwd([r, lw, kf, v, aa, bb], zs, dy, t, carry=carry)
    pre_ins = [_row_item(a, ts) for a in (k, lw1, la1, lg1)]
    pre_par = [_vec_item(p[q]) for q in _RWKV_PRE_VECS]
    (dk, dlw1, dla1, dlg1), pre_grads = _stage_bwd(
        "rwkv_pre_bwd", _f_rwkv_pre, (n // ts,), pre_ins, pre_par,
        [([dlw], *row_s), ([dkf_a, dkf_b], *row_s), ([daa], *row_s), ([dbb], *row_s), ([dg], *row_s)], [F32] * 4)
    grads.update(zip(_RWKV_PRE_VECS, pre_grads))
    dmixed = []
    for tag, xin, dz, dz2, wname in (("r", xr, dr_a, dr_b, "w_r"), ("k", xk, dk, None, "w_k"), ("v", xv, dv_a, dv_b, "w_v"),
                                     ("w1", xw, dlw1, None, "w1"), ("a1", xa, dla1, None, "a1"), ("g1", xg, dlg1, None, "g1")):
        dmixed.append(_matmul(f"rwkv_{tag}_da", dz, p[wname], mode="nt", tm=1024, tn=1024, a_plus=dz2))
        grads[wname] = _matmul(f"rwkv_{tag}_dw", xin, dz, mode="tn", tm=1024, tn=1024, b_plus=dz2)
    seq_blk = (t, CT), lambda j, b: (b, j)
    (dh,), (grads["mix"],) = _stage_bwd(
        "rwkv_mix_bwd", _f_rwkv_mix, (D // CT, bsz), [_seq_item(h, t, CT)], [_col_item(p["mix"], CT)],
        [([a], *seq_blk) for a in dmixed], [F32])
    dx, (grads["norm"],) = _stage_bwd(
        "rwkv_norm_bwd", _f_norm, (n // tr,), [_row_item(x, tr)], [_vec_item(p["norm"])],
        [([dh], (tr, D), lambda i: (i, 0))], [F32], adds={0: dout}, bf16_copy=True)
    return tuple(dx), grads


def _block_diag_gates(gate_w):
    z = jnp.zeros((2, D // CT, HEAD, HEAD), gate_w.dtype)
    even, odd = gate_w[:, 0::2], gate_w[:, 1::2]
    top = jnp.concatenate([even, z], axis=-1)
    bot = jnp.concatenate([z, odd], axis=-1)
    return jnp.concatenate([top, bot], axis=-2)


def _gate_blocks_grad(dg):
    even, odd = dg[:, :, :HEAD, :HEAD], dg[:, :, HEAD:, HEAD:]
    return jnp.stack([even, odd], axis=2).reshape(2, N_HEAD, HEAD, HEAD)


def _step(local, x, target):
    bsz, t, _ = x.shape
    n = bsz * t

    def block(name, pre=""):
        base, _, layer = name.partition("@")
        a = local[pre + base]
        return a[int(layer):int(layer) + 1] if layer else a

    def split(names):
        return [k for k in names if _base(k) in LARGE], [k for k in names if _base(k) not in LARGE]

    def gather_items(names):
        large, small = split(names)
        flat = [block(k).astype(BF16).reshape(-1) if _base(k) in GATHER_BF16
                else lax.bitcast_convert_type(block(k), BF16).reshape(-1) for k in small]
        return [(block(k).astype(BF16), False) for k in large] + ([(_pack(flat, BF16), False)] if small else [])

    def gathered(names, landed):
        large, small = split(names)
        out = {k: _unshard(g, SHARD_AXIS[_base(k)]) for k, g in zip(large, landed)}
        sizes = [(1 if _base(k) in GATHER_BF16 else 2) * math.prod(block(k).shape) for k in small]
        for k, g in zip(small, _unpack(landed[-1], [(s,) for s in sizes], (N_DEV,)) if small else []):
            if _base(k) not in GATHER_BF16:
                g = lax.bitcast_convert_type(g.reshape(N_DEV, -1, 2), F32)
            out[k] = _unshard(g.reshape((N_DEV,) + block(k).shape), SHARD_AXIS[_base(k)])
        return out

    landed_large, landed_small = {}, {}

    def scatter_items(names, grads):
        large, small = split(names)
        blocked = {k: _reshard(grads[k], SHARD_AXIS[_base(k)]) for k in names}
        return [(blocked[k], True) for k in large] + ([(_pack([blocked[k] for k in small], F32, (N_DEV,)), True)] if small else [])

    def scattered(names, landed):
        large, small = split(names)
        landed_large.update(zip(large, landed))
        if small:
            landed_small[tuple(small)] = landed[-1]

    def ffn_grads(l, g):
        return {f"ffn_w_up@{l}": g["w_up"][None], f"ffn_w_down@{l}": g["w_down"][None], f"ffn_conv_w@{l}": g["conv_w"][None]}

    lru = dict(norm=local["lru_norm"], b_out=local["lru_b_out"], b_y=local["lru_b_in"][:, :D], b_x=local["lru_b_in"][:, D:],
               conv_b=local["lru_conv_b"], gate_w=_block_diag_gates(local["lru_gate_w"][0]),
               gate_b=local["lru_gate_b"][0].reshape(2, D), lam=local["lru_lambda"])
    lru_first = ["lru_w_in", "lru_conv_w"]
    ride_pre = _Carry(gather_items(lru_first))
    ride_in = _Carry(gather_items(["lru_w_out"]))
    ffn0_first = ["ffn_w_up@0", "ffn_conv_w@0"]
    ride_core = _Carry(gather_items(ffn0_first))

    def lru_late(which):
        if which == "pre":
            full = gathered(lru_first, ride_pre.landed)
            lru.update(w_in=full["lru_w_in"][0], conv_w=full["lru_conv_w"][0])
        else:
            lru.update(w_out=gathered(["lru_w_out"], ride_in.landed)["lru_w_out"][0])

    x0 = x.reshape(n, D)
    x1, s_lru = _lru_fwd(x0, lru, t, carries={"pre": ride_pre, "in": ride_in, "core": ride_core}, late=lru_late)
    full = gathered(ffn0_first, ride_core.landed)
    ffn0 = dict(norm=local["ffn_norm"][0:1], conv_b=local["ffn_conv_b"][0:1], w_up=full["ffn_w_up@0"][0],
                conv_w=full["ffn_conv_w@0"][0])
    ride_up = _Carry(gather_items(GROUPS["rwkv"]))
    ride_down = _Carry(gather_items(["ffn_w_down@0"]))

    def ffn0_late():
        ffn0["w_down"] = gathered(["ffn_w_down@0"], ride_down.landed)["ffn_w_down@0"][0]

    x2, s_ffn0 = _ffn_fwd("ffn0", x1, ffn0, t, carries={"up": ride_up, "core": ride_down}, late=ffn0_late)
    full = gathered(GROUPS["rwkv"], ride_up.landed)
    rwkv = dict(norm=full["rwkv_norm"], mix=full["rwkv_mix"][0], w_r=full["rwkv_w_rkv"][0, 0],
                w_k=full["rwkv_w_rkv"][0, 1], w_v=full["rwkv_w_rkv"][0, 2], w0=full["rwkv_w0"], w1=full["rwkv_w1"][0],
                w2=full["rwkv_w2"][0], a0=full["rwkv_a0"], a1=full["rwkv_a1"][0], a2=full["rwkv_a2"][0],
                g1=full["rwkv_g1"][0], g2=full["rwkv_g2"][0], k_k=full["rwkv_k_k"], k_a=full["rwkv_k_a"],
                r_k=local["rwkv_r_k"].reshape(1, D), ln_w=full["rwkv_ln_w"], ln_b=full["rwkv_ln_b"],
                w_out=full["rwkv_w_out"][0])
    ride = _Carry(gather_items(GROUPS["ffn1"]))
    x3, s_rwkv = _rwkv_fwd(x2, rwkv, t, carry=ride)
    full = gathered(GROUPS["ffn1"], ride.landed)
    ffn1 = dict(norm=local["ffn_norm"][1:2], conv_b=local["ffn_conv_b"][1:2], w_up=full["ffn_w_up@1"][0],
                conv_w=full["ffn_conv_w@1"][0], w_down=full["ffn_w_down@1"][0])
    x4, s_ffn1 = _ffn_fwd("ffn1", x3, ffn1, t)
    loss, dx4, d_final = _loss_head(x4, target.reshape(n, D), local["final_norm"].reshape(1, D))

    dx3, g_ffn1 = _ffn_bwd("ffn1", dx4, s_ffn1, ffn1, t)
    ride = _Carry(scatter_items(GROUPS["ffn1"], ffn_grads(1, g_ffn1)))
    dx2, g_rwkv = _rwkv_bwd(dx3, s_rwkv, rwkv, t, carry=ride)
    scattered(GROUPS["ffn1"], ride.landed)
    rwkv_grads = {
        "rwkv_norm": g_rwkv["norm"], "rwkv_mix": g_rwkv["mix"][None],
        "rwkv_w_rkv": jnp.stack([g_rwkv["w_r"], g_rwkv["w_k"], g_rwkv["w_v"]])[None],
        "rwkv_w0": g_rwkv["w0"], "rwkv_w1": g_rwkv["w1"][None], "rwkv_w2": g_rwkv["w2"][None],
        "rwkv_a0": g_rwkv["a0"], "rwkv_a1": g_rwkv["a1"][None], "rwkv_a2": g_rwkv["a2"][None],
        "rwkv_g1": g_rwkv["g1"][None], "rwkv_g2": g_rwkv["g2"][None], "rwkv_k_k": g_rwkv["k_k"],
        "rwkv_k_a": g_rwkv["k_a"], "rwkv_ln_w": g_rwkv["ln_w"], "rwkv_ln_b": g_rwkv["ln_b"],
        "rwkv_w_out": g_rwkv["w_out"][None],
    }
    rwkv_small = split(GROUPS["rwkv"])[1]
    ride_da = _Carry(scatter_items(["rwkv_w_out"], rwkv_grads))
    ride_dw = _Carry(scatter_items(rwkv_small, rwkv_grads))
    ride_core = _Carry(scatter_items(["rwkv_w_rkv"], rwkv_grads))
    down_pieces = []

    def down_rides(g_w_down):
        blocked = _reshard(g_w_down[None], SHARD_AXIS["ffn_w_down"])
        rows = blocked.shape[2] // 3
        down_pieces.extend(_Carry([(blocked[:, :, i * rows:(i + 1) * rows], True)]) for i in range(3))
        return down_pieces

    dx1, g_ffn0 = _ffn_bwd("ffn0", dx2, s_ffn0, ffn0, t,
                           carries={"down_da": ride_da, "down_dw": ride_dw, "core": ride_core}, down_rides=down_rides)
    scattered(["rwkv_w_out"], ride_da.landed)
    scattered(rwkv_small, ride_dw.landed)
    scattered(["rwkv_w_rkv"], ride_core.landed)
    landed_large["ffn_w_down@0"] = jnp.concatenate([c.landed[0] for c in down_pieces], axis=2)
    ride = _Carry(scatter_items(ffn0_first, ffn_grads(0, g_ffn0)))
    out_pieces = []

    def out_rides(g_w_out):
        blocked = _reshard(g_w_out[None], SHARD_AXIS["lru_w_out"])
        rows = blocked.shape[2] // 2
        out_pieces.extend(_Carry([(blocked[:, :, i * rows:(i + 1) * rows], True)]) for i in range(2))
        return out_pieces

    dx0, g_lru = _lru_bwd(dx1, s_lru, lru, t, carry=ride, out_rides=out_rides)
    scattered(ffn0_first, ride.landed)
    landed_large["lru_w_out"] = jnp.concatenate([c.landed[0] for c in out_pieces], axis=2)

    g_ffn = [g_ffn0, g_ffn1]
    lru_grads = {"lru_w_in": g_lru["w_in"][None], "lru_conv_w": g_lru["conv_w"][None], "lru_w_out": g_lru["w_out"][None]}
    gfull = {
        "lru_norm": g_lru["norm"], "lru_b_in": jnp.concatenate([g_lru["b_y"], g_lru["b_x"]], axis=1),
        "lru_conv_b": g_lru["conv_b"], "lru_gate_w": _gate_blocks_grad(g_lru["gate_w"])[None],
        "lru_gate_b": g_lru["gate_b"].reshape(1, 2, N_HEAD, HEAD), "lru_lambda": g_lru["lam"],
        "lru_b_out": g_lru["b_out"], "rwkv_r_k": g_rwkv["r_k"].reshape(1, N_HEAD, HEAD),
        "ffn_norm": jnp.concatenate([g["norm"] for g in g_ffn]),
        "ffn_conv_b": jnp.concatenate([g["conv_b"] for g in g_ffn]),
        "final_norm": d_final.reshape(D),
    }
    *tail, replicated = _exchange(
        "grad_tail", scatter_items(lru_first, lru_grads) + [(_pack([gfull[k] for k in REPLICATED], F32), False)])
    scattered(lru_first, tail)

    out = {}
    kinds = ("grad", "delta", "new_m", "new_v")
    for k, landed in landed_large.items():
        flat = (math.prod(block(k).shape[:-1]), block(k).shape[-1])
        res = _adamw(f"adamw_{k.replace('@', '_')}", landed.reshape((N_DEV,) + flat),
                     *[block(k, pre).reshape(flat) for pre in ("", "m_", "v_")])
        for kind, a in zip(kinds, res):
            out[(kind, k)] = a.reshape(block(k).shape)
    small_sets = [(f"adamw_small_{i}", list(names), landed) for i, (names, landed) in enumerate(landed_small.items())]
    for tag, names, landed in small_sets + [("adamw_replicated", REPLICATED, replicated)]:
        packs = [_pack([block(k, pre) for k in names], F32) for pre in ("", "m_", "v_")]
        res = _adamw(tag, landed, *packs)
        for kind, buf in zip(kinds, res):
            for k, a in zip(names, _unpack(buf, [block(k).shape for k in names])):
                out[(kind, k)] = a
    for kind in ("grad", "delta", "new_m", "new_v"):
        for k in STACKED:
            out[(kind, k)] = jnp.concatenate([out[(kind, f"{k}@0")], out[(kind, f"{k}@1")]], axis=0)
    return loss, dx0.reshape(x.shape), out


def kernel(x, lru_norm, lru_w_in, lru_b_in, lru_conv_w, lru_conv_b, lru_gate_w, lru_gate_b, lru_lambda, lru_w_out, lru_b_out, rwkv_norm, rwkv_mix, rwkv_w_rkv, rwkv_w0, rwkv_w1, rwkv_w2, rwkv_a0, rwkv_a1, rwkv_a2, rwkv_g1, rwkv_g2, rwkv_k_k, rwkv_k_a, rwkv_r_k, rwkv_ln_w, rwkv_ln_b, rwkv_w_out, ffn_norm, ffn_w_up, ffn_conv_w, ffn_conv_b, ffn_w_down, final_norm, loss_target, m_lru_norm, m_lru_w_in, m_lru_b_in, m_lru_conv_w, m_lru_conv_b, m_lru_gate_w, m_lru_gate_b, m_lru_lambda, m_lru_w_out, m_lru_b_out, m_rwkv_norm, m_rwkv_mix, m_rwkv_w_rkv, m_rwkv_w0, m_rwkv_w1, m_rwkv_w2, m_rwkv_a0, m_rwkv_a1, m_rwkv_a2, m_rwkv_g1, m_rwkv_g2, m_rwkv_k_k, m_rwkv_k_a, m_rwkv_r_k, m_rwkv_ln_w, m_rwkv_ln_b, m_rwkv_w_out, m_ffn_norm, m_ffn_w_up, m_ffn_conv_w, m_ffn_conv_b, m_ffn_w_down, m_final_norm, v_lru_norm, v_lru_w_in, v_lru_b_in, v_lru_conv_w, v_lru_conv_b, v_lru_gate_w, v_lru_gate_b, v_lru_lambda, v_lru_w_out, v_lru_b_out, v_rwkv_norm, v_rwkv_mix, v_rwkv_w_rkv, v_rwkv_w0, v_rwkv_w1, v_rwkv_w2, v_rwkv_a0, v_rwkv_a1, v_rwkv_a2, v_rwkv_g1, v_rwkv_g2, v_rwkv_k_k, v_rwkv_k_a, v_rwkv_r_k, v_rwkv_ln_w, v_rwkv_ln_b, v_rwkv_w_out, v_ffn_norm, v_ffn_w_up, v_ffn_conv_w, v_ffn_conv_b, v_ffn_w_down, v_final_norm):
    args = locals()
    local = {(pre + k): args[pre + k] for pre in ("", "m_", "v_") for k in WEIGHTS}
    loss_local, grad_x, out = _step(local, x, loss_target)
    loss = lax.psum(loss_local, ("x", "y", "c"))
    return (loss, grad_x, *[out[(kind, k)] for kind in ("grad", "delta", "new_m", "new_v") for k in WEIGHTS])
```

```python
import functools
import math

import jax
import jax.numpy as jnp
from jax import lax
from jax.experimental import pallas as pl
from jax.experimental.pallas import tpu as pltpu

F32 = jnp.float32
BF16 = jnp.bfloat16

N_DEV = 8
D = 1024
HEAD = 64
N_HEAD = D // HEAD
D_FF = 3 * D
LANE = 128
V7X_VMEM_BYTES = 64 * 1024 * 1024
VMEM_LIMIT = V7X_VMEM_BYTES - 8 * 1024 * 1024
CT = LANE
CHUNK = 64
SCAN_HEADS = 16
PACK_W = 1024
PACK_ROWS = 256

ADAM_LR, ADAM_B1, ADAM_B2, ADAM_EPS, ADAM_WD, ADAM_STEP = 0.001, 0.9, 0.999, 1e-08, 0.01, 10
RMS_EPS = 1e-6
GN_EPS = 64e-5
LRU_C = 8.0

WEIGHTS = ['lru_norm', 'lru_w_in', 'lru_b_in', 'lru_conv_w', 'lru_conv_b', 'lru_gate_w', 'lru_gate_b',
           'lru_lambda', 'lru_w_out', 'lru_b_out', 'rwkv_norm', 'rwkv_mix', 'rwkv_w_rkv', 'rwkv_w0', 'rwkv_w1',
           'rwkv_w2', 'rwkv_a0', 'rwkv_a1', 'rwkv_a2', 'rwkv_g1', 'rwkv_g2', 'rwkv_k_k', 'rwkv_k_a', 'rwkv_r_k',
           'rwkv_ln_w', 'rwkv_ln_b', 'rwkv_w_out', 'ffn_norm', 'ffn_w_up', 'ffn_conv_w', 'ffn_conv_b',
           'ffn_w_down', 'final_norm']
SHARD_AXIS = {'lru_w_in': 2, 'lru_conv_w': 2, 'lru_w_out': 1, 'rwkv_norm': 1, 'rwkv_mix': 2, 'rwkv_w_rkv': 2,
              'rwkv_w0': 1, 'rwkv_w1': 1, 'rwkv_w2': 2, 'rwkv_a0': 1, 'rwkv_a1': 1, 'rwkv_a2': 2, 'rwkv_g1': 1,
              'rwkv_g2': 2, 'rwkv_k_k': 1, 'rwkv_k_a': 1, 'rwkv_ln_w': 1, 'rwkv_ln_b': 1, 'rwkv_w_out': 1,
              'ffn_w_up': 2, 'ffn_conv_w': 2, 'ffn_w_down': 1}
GATHER_BF16 = ['lru_w_in', 'lru_w_out', 'rwkv_w_rkv', 'rwkv_w1', 'rwkv_a1', 'rwkv_g1', 'rwkv_w_out', 'ffn_w_up',
               'ffn_w_down']
REPLICATED = [n for n in WEIGHTS if n not in SHARD_AXIS]
STACKED = ['ffn_w_up', 'ffn_w_down', 'ffn_conv_w']
LARGE = ['lru_w_in', 'lru_w_out', 'rwkv_w_rkv', 'rwkv_w_out', 'ffn_w_up', 'ffn_w_down']
GROUPS = {
    "lru": ['lru_w_in', 'lru_w_out', 'lru_conv_w'],
    "ffn0": [f"{n}@0" for n in STACKED],
    "rwkv": [n for n in WEIGHTS if n.startswith("rwkv_") and n in SHARD_AXIS],
    "ffn1": [f"{n}@1" for n in STACKED],
}


def _base(name):
    return name.partition("@")[0]


def _pcall(body, **kw):
    return pl.pallas_call(body, **kw)


class _Carry:
    def __init__(self, items):
        self.items = list(items)
        self.landed = None


def _exchange_copies(src_ref, dst_ref, send_sems, recv_sems, local_sem, scatter):
    x, y, c = lax.axis_index("x"), lax.axis_index("y"), lax.axis_index("c")
    me = 4 * x + 2 * y + c
    mine = pltpu.make_async_copy(src_ref.at[me] if scatter else src_ref, dst_ref.at[me], local_sem)
    copies = []
    for m in range(1, N_DEV):
        px = 1 - x if m & 4 else x
        py = 1 - y if m & 2 else y
        pc = 1 - c if m & 1 else c
        part = src_ref.at[4 * px + 2 * py + pc] if scatter else src_ref
        copies.append(pltpu.make_async_remote_copy(
            src_ref=part, dst_ref=dst_ref.at[me], send_sem=send_sems.at[m - 1], recv_sem=recv_sems.at[m - 1],
            device_id=(px, py, pc), device_id_type=pl.DeviceIdType.MESH))
    return mine, copies


def _start_exchange(mine, copies):
    mine.start()
    for cp in copies:
        cp.start()


def _wait_exchange(mine, copies):
    for cp in copies:
        cp.wait_recv()
    for cp in copies:
        cp.wait_send()
    mine.wait()


_EXCHANGE_SEMS = [pltpu.SemaphoreType.DMA((N_DEV - 1,)), pltpu.SemaphoreType.DMA((N_DEV - 1,)), pltpu.SemaphoreType.DMA]


def _landing_shape(src, scatter):
    return jax.ShapeDtypeStruct((N_DEV,) + tuple(src.shape[1:] if scatter else src.shape), src.dtype)


def _call(body, operands, *, carry=None, name, grid, in_specs, out_specs, out_shape, scratch_shapes=(),
          compiler_params=None):
    if carry is None:
        return _pcall(body, name=name, grid=grid, in_specs=in_specs, out_specs=out_specs, out_shape=out_shape,
                      scratch_shapes=list(scratch_shapes), compiler_params=compiler_params)(*operands)
    single = not isinstance(out_specs, (list, tuple))
    out_specs_l = [out_specs] if single else list(out_specs)
    out_shape_l = [out_shape] if single else list(out_shape)
    n_in, n_out, n_scr, n_x = len(in_specs), len(out_specs_l), len(scratch_shapes), len(carry.items)
    flags = [sc for (_, sc) in carry.items]

    def wrapped(*refs):
        ins, refs = refs[:n_in], refs[n_in:]
        xsrc, refs = refs[:n_x], refs[n_x:]
        outs, refs = refs[:n_out], refs[n_out:]
        xdst, refs = refs[:n_x], refs[n_x:]
        scr, sems = refs[:n_scr], refs[n_scr:]
        first = functools.reduce(jnp.logical_and, [pl.program_id(i) == 0 for i in range(len(grid))])
        last = functools.reduce(jnp.logical_and, [pl.program_id(i) == grid[i] - 1 for i in range(len(grid))])

        def exchanges():
            return [_exchange_copies(xsrc[i], xdst[i], *sems[3 * i:3 * i + 3], flags[i]) for i in range(n_x)]

        @pl.when(first)
        def _():
            for mine, copies in exchanges():
                _start_exchange(mine, copies)

        body(*ins, *outs, *scr)

        @pl.when(last)
        def _():
            for mine, copies in exchanges():
                _wait_exchange(mine, copies)

    hbm = pl.BlockSpec(memory_space=pl.ANY)
    res = _pcall(
        wrapped, name=name, grid=grid, in_specs=list(in_specs) + [hbm] * n_x,
        out_specs=out_specs_l + [hbm] * n_x,
        out_shape=out_shape_l + [_landing_shape(a, sc) for (a, sc) in carry.items],
        scratch_shapes=list(scratch_shapes) + _EXCHANGE_SEMS * n_x, compiler_params=compiler_params,
    )(*operands, *[a for (a, _) in carry.items])
    carry.landed = list(res[n_out:])
    return res[0] if single else list(res[:n_out])


def _params(n_grid):
    return pltpu.CompilerParams(dimension_semantics=("arbitrary",) * n_grid, vmem_limit_bytes=VMEM_LIMIT)


def _shift_rows(x, d, up):
    n = x.shape[0]
    idx = lax.broadcasted_iota(jnp.int32, x.shape, 0)
    if up:
        return jnp.where(idx < n - d, pltpu.roll(x, n - d, 0), 0.0)
    return jnp.where(idx >= d, pltpu.roll(x, d, 0), 0.0)


@functools.partial(jax.custom_vjp, nondiff_argnums=(1,))
def _shift_down(x, d):
    return _shift_rows(x, d, False)


def _shift_down_fwd(x, d):
    return _shift_rows(x, d, False), None


def _shift_down_bwd(d, _, g):
    return (_shift_rows(g, d, True),)


_shift_down.defvjp(_shift_down_fwd, _shift_down_bwd)


def _scan_doubling(a, b, up):
    n = a.shape[0]
    d = 1
    while d < n:
        b = b + a * _shift_rows(b, d, up)
        a = a * _shift_rows(a, d, up)
        d *= 2
    return b


@jax.custom_vjp
def _linear_scan(a, b):
    return _scan_doubling(a, b, False)


def _linear_scan_fwd(a, b):
    h = _scan_doubling(a, b, False)
    return h, (a, h)


def _linear_scan_bwd(res, dh):
    a, h = res
    g = _scan_doubling(_shift_rows(a, 1, True), dh, True)
    return g * _shift_rows(h, 1, False), g


_linear_scan.defvjp(_linear_scan_fwd, _linear_scan_bwd)


def _causal_conv(x, w, b):
    k = w.shape[0]
    out = b + x * w[k - 1:k]
    for j in range(k - 1):
        out = out + _shift_down(x, k - 1 - j) * w[j:j + 1]
    return out


_GELU_C1 = math.sqrt(2.0 / math.pi)
_GELU_C2 = 0.044715 * _GELU_C1


@jax.custom_vjp
def _gelu(x):
    return 0.5 * x * (1.0 + jnp.tanh(x * (_GELU_C1 + _GELU_C2 * (x * x))))


def _gelu_fwd(x):
    x2 = x * x
    t = jnp.tanh(x * (_GELU_C1 + _GELU_C2 * x2))
    return 0.5 * x * (1.0 + t), (x, x2, t)


def _gelu_bwd(res, g):
    x, x2, t = res
    return (g * (0.5 * (1.0 + t) + (0.5 * x) * (1.0 - t * t) * (_GELU_C1 + (3.0 * _GELU_C2) * x2)),)


_gelu.defvjp(_gelu_fwd, _gelu_bwd)


def _rmsnorm(x, g):
    return x * lax.rsqrt(jnp.mean(x * x, axis=-1, keepdims=True) + RMS_EPS) * g


def _neg_expm1(x):
    series = x * (1.0 + x * 0.5 * (1.0 + x * (1.0 / 3.0) * (1.0 + x * 0.25 * (1.0 + x * 0.2))))
    return -jnp.where(x > -0.1, series, jnp.exp(x) - 1.0)


def _dot(a, b, ca=1, cb=0, precision=None):
    return lax.dot_general(a, b, (((ca,), (cb,)), ((), ())), precision=precision, preferred_element_type=F32)


def _bdot(a, b):
    return _dot(a.astype(BF16), b.astype(BF16))


def _split_bf16(x):
    hi = x.astype(BF16)
    return hi, (x - hi.astype(F32)).astype(BF16)


def _head_sum_impl(x):
    row = lax.broadcasted_iota(jnp.int32, (D, LANE), 0)
    col = lax.broadcasted_iota(jnp.int32, (D, LANE), 1)
    e = (lax.shift_right_logical(row, 6) == col).astype(BF16)
    hi, lo = _split_bf16(x)
    s_hi, s_lo = _split_bf16(_dot(hi, e) + _dot(lo, e))
    return _dot(s_hi, e, 1, 1) + _dot(s_lo, e, 1, 1)


@jax.custom_vjp
def _head_sum(x):
    return _head_sum_impl(x)


def _head_sum_fwd(x):
    return _head_sum_impl(x), None


def _head_sum_bwd(_, g):
    return (_head_sum(g),)


_head_sum.defvjp(_head_sum_fwd, _head_sum_bwd)


def _specs(items):
    return [pl.BlockSpec(it[1], it[2]) for it in items]


def _stage_fwd(name, f, grid, ins, params, outs, carry=None):
    n_in = len(ins) + len(params)

    def body(*refs):
        res = f(*[r[...] for r in refs[:n_in]])
        for o, v in zip(refs[n_in:], res):
            o[...] = v.astype(o.dtype)

    return _call(
        body, [it[0] for it in ins + params], carry=carry, name=name, grid=grid, in_specs=_specs(ins + params),
        out_specs=[pl.BlockSpec(bs, im) for (_, _, bs, im) in outs],
        out_shape=[jax.ShapeDtypeStruct(s, dt) for (s, dt, _, _) in outs],
        compiler_params=_params(len(grid)),
    )


def _stage_bwd(name, f, grid, ins, params, douts, din_dtypes, adds=None, carry=None, bf16_copy=False):
    adds = adds or {}
    n_in, n_par = len(ins), len(params)
    dout_items = [(a, bs, im) for (arrs, bs, im) in douts for a in arrs]
    add_items = [(adds[i], ins[i][1], ins[i][2]) for i in sorted(adds)]
    n_do, n_add = len(dout_items), len(add_items)

    def body(*refs):
        vals = [r[...] for r in refs[:n_in + n_par]]
        do_refs = list(refs[n_in + n_par:n_in + n_par + n_do])
        add_refs = dict(zip(sorted(adds), refs[n_in + n_par + n_do:n_in + n_par + n_do + n_add]))
        din_refs = refs[n_in + n_par + n_do + n_add:n_in + n_par + n_do + n_add + n_in]
        dpar_refs = refs[n_in + n_par + n_do + n_add + n_in:n_in + n_par + n_do + n_add + n_in + n_par]
        cts = []
        for (arrs, _, _) in douts:
            ct = do_refs.pop(0)[...].astype(F32)
            for _ in arrs[1:]:
                ct = ct + do_refs.pop(0)[...].astype(F32)
            cts.append(ct)
        _, vjp = jax.vjp(f, *vals)
        grads = vjp(tuple(cts))
        for i, r in enumerate(din_refs):
            g = grads[i]
            if i in add_refs:
                g = g + add_refs[i][...]
            r[...] = g.astype(r.dtype)
            if i == 0 and bf16_copy:
                refs[-1][...] = g.astype(BF16)

        @pl.when(pl.program_id(len(grid) - 1) == 0)
        def _():
            for r in dpar_refs:
                r[...] = jnp.zeros(r.shape, r.dtype)

        for j, r in enumerate(dpar_refs):
            r[...] += grads[n_in + j]

    din_shapes = [it[3][0] if len(it) > 3 else it[0].shape for it in ins]
    din_specs = [pl.BlockSpec(it[1], it[3][1] if len(it) > 3 else it[2]) for it in ins]
    copy_specs = [din_specs[0]] if bf16_copy else []
    copy_shapes = [jax.ShapeDtypeStruct(din_shapes[0], BF16)] if bf16_copy else []
    res = _call(
        body, [it[0] for it in ins + params + dout_items + add_items], carry=carry, name=name, grid=grid,
        in_specs=_specs(ins + params + dout_items + add_items), out_specs=din_specs + _specs(params) + copy_specs,
        out_shape=[jax.ShapeDtypeStruct(s, dt) for s, dt in zip(din_shapes, din_dtypes)]
        + [jax.ShapeDtypeStruct(it[0].shape, F32) for it in params] + copy_shapes,
        compiler_params=_params(len(grid)),
    )
    return list(res[:n_in]) + list(res[n_in + n_par:]), list(res[n_in:n_in + n_par])


def _tile(n, want):
    t = min(n, want)
    while n % t:
        t //= 2
    return t


def _matmul(name, a, b, *, mode="nn", resid=None, out_dtype=F32, tm=512, tn=512, tk=1024, b_koff=0, carry=None,
            a_plus=None, b_plus=None):
    if mode == "tn":
        (kdim, m), n = a.shape, b.shape[1]
    else:
        (m, kdim), n = a.shape, (b.shape[1] if mode == "nn" else b.shape[0])
    tm, tn, tk = _tile(m, tm), _tile(n, tn), _tile(kdim, tk)
    nk = kdim // tk
    ko = b_koff // tk
    assert ko * tk == b_koff
    a_spec = pl.BlockSpec((tk, tm), lambda i, j, k: (k, i)) if mode == "tn" else pl.BlockSpec((tm, tk), lambda i, j, k: (i, k))
    b_spec = pl.BlockSpec((tn, tk), lambda i, j, k: (j, k + ko)) if mode == "nt" else pl.BlockSpec((tk, tn), lambda i, j, k: (k + ko, j))
    ca = 0 if mode == "tn" else 1
    cb = 1 if mode == "nt" else 0
    operands = [a, b]
    in_specs = [a_spec, b_spec]
    extra = {}
    for key, arr, spec in (("a", a_plus, a_spec), ("b", b_plus, b_spec), ("resid", resid, pl.BlockSpec((tm, tn), lambda i, j, k: (i, j)))):
        if arr is not None:
            extra[key] = len(operands)
            operands.append(arr)
            in_specs.append(spec)

    def product(refs):
        av, bv = refs[0][...], refs[1][...]
        if "a" in extra:
            av = av + refs[extra["a"]][...]
        if "b" in extra:
            bv = bv + refs[extra["b"]][...]
        return _dot(av.astype(BF16), bv.astype(BF16), ca, cb)

    def finish(r, refs, o_ref):
        if resid is not None:
            r = r + refs[extra["resid"]][...]
        o_ref[...] = r.astype(o_ref.dtype)

    def body_one_step(*refs):
        finish(product(refs), refs, refs[-1])

    def body(*refs):
        o_ref, acc_ref = refs[-2], refs[-1]
        k = pl.program_id(2)

        @pl.when(k == 0)
        def _():
            acc_ref[...] = jnp.zeros(acc_ref.shape, F32)

        acc_ref[...] += product(refs)

        @pl.when(k == nk - 1)
        def _():
            finish(acc_ref[...], refs, o_ref)

    return _call(
        body_one_step if nk == 1 else body, operands, carry=carry, name=name, grid=(m // tm, n // tn, nk),
        in_specs=in_specs, out_specs=pl.BlockSpec((tm, tn), lambda i, j, k: (i, j)),
        out_shape=jax.ShapeDtypeStruct((m, n), out_dtype),
        scratch_shapes=[] if nk == 1 else [pltpu.VMEM((tm, tn), F32)],
        compiler_params=_params(3),
    )


def _f_lru_pre(x, norm, b_out):
    return _rmsnorm(x, norm), x + b_out


def _f_lru_core(uy, ux, b_y, b_x, cw, cb, gw, gb, lam):
    yb = _gelu(uy + b_y)
    xr = _causal_conv(ux + b_x, cw, cb)
    gr = jax.nn.sigmoid(_bdot(xr, gw[0, 0]) + gb[0:1])
    gi = jax.nn.sigmoid(_bdot(xr, gw[1, 0]) + gb[1:2])
    log_a = -LRU_C * gr * jax.nn.softplus(-lam)
    a = jnp.exp(log_a)
    bterm = jnp.sqrt(_neg_expm1(2.0 * log_a)) * (gi * xr)
    return (_linear_scan(a, bterm) * yb,)


def _f_norm(x, norm):
    return (_rmsnorm(x, norm),)


def _f_ffn_core(ug, uv, cw, cb):
    return (_gelu(_causal_conv(ug, cw, cb)) * uv,)


def _f_rwkv_mix(h, mix):
    xx = _shift_down(h, 1) - h
    return tuple(h + xx * mix[i:i + 1] for i in range(6))


def _f_rwkv_pre(k, lw1, la1, lg1, w0, a0, k_k, k_a, w2, a2, g2):
    wpre = w0 + _bdot(jnp.tanh(lw1), w2)
    apre = a0 + _bdot(la1, a2)
    g = _bdot(jax.nn.sigmoid(lg1), g2)
    log_decay = -jnp.exp(-jax.nn.softplus(-wpre) - 0.5)
    a = jax.nn.sigmoid(apre)
    kk = k * k_k
    kk = kk / jnp.maximum(jnp.sqrt(_head_sum(kk * kk)), 1e-12)
    kf = k * (1.0 + (a - 1.0) * k_a)
    return log_decay, kf, -kk, kk * a, g


def _f_rwkv_post(y, r, kf, v, g, ln_w, ln_b, r_k):
    inv = 1.0 / HEAD
    yc = y - _head_sum(y) * inv
    var = _head_sum(yc * yc) * inv
    yn = yc * lax.rsqrt(var + GN_EPS) * ln_w + ln_b
    bonus = _head_sum(r * kf * r_k) * v
    return ((yn + bonus) * g,)


def _hdot(a, b, ca, cb):
    return lax.dot_general(a, b, (((ca,), (cb,)), ((0,), (0,))), preferred_element_type=F32)


def _hmm_impl(a, b, ca, cb):
    return _hdot(a.astype(BF16), b.astype(BF16), ca, cb)


@functools.partial(jax.custom_vjp, nondiff_argnums=(2, 3))
def _hmm(a, b, ca, cb):
    return _hmm_impl(a, b, ca, cb)


def _hmm_fwd(a, b, ca, cb):
    return _hmm_impl(a, b, ca, cb), (a, b)


def _hmm_bwd(ca, cb, res, dc):
    a, b = res
    fa, fb = 3 - ca, 3 - cb
    da = _hmm(dc, b, 2, fb) if ca == 2 else _hmm(b, dc, fb, 2)
    db = _hmm(a, dc, fa, 1) if cb == 1 else _hmm(dc, a, 1, fa)
    return da, db


_hmm.defvjp(_hmm_fwd, _hmm_bwd)


def _tri_dot_impl(x, upper):
    g, n, _ = x.shape
    row = lax.broadcasted_iota(jnp.int32, (g, n, 3 * n), 1)
    col = lax.broadcasted_iota(jnp.int32, (g, n, 3 * n), 2)
    col = jnp.where(col >= 2 * n, col - 2 * n, jnp.where(col >= n, col - n, col))
    tri = (row <= col if upper else row >= col).astype(BF16)
    hi = x.astype(BF16)
    rem = x - hi.astype(F32)
    mid = rem.astype(BF16)
    lo = (rem - mid.astype(F32)).astype(BF16)
    return _hdot(tri, jnp.concatenate([hi, mid, lo], axis=1), 2, 1)


@functools.partial(jax.custom_vjp, nondiff_argnums=(1,))
def _tri_dot(x, upper):
    return _tri_dot_impl(x, upper)


def _tri_dot_fwd(x, upper):
    return _tri_dot_impl(x, upper), None


def _tri_dot_bwd(upper, _, g):
    return (_tri_dot(g, not upper),)


_tri_dot.defvjp(_tri_dot_fwd, _tri_dot_bwd)


def _unit_lower_inverse_impl(a):
    g, n, _ = a.shape
    row = lax.broadcasted_iota(jnp.int32, (g, n, n), 1)
    col = lax.broadcasted_iota(jnp.int32, (g, n, n), 2)
    inv = (row == col).astype(F32) + a
    p = a
    for _ in range(int(math.log2(n)) - 1):
        p = _hmm(p, p, 2, 1)
        inv = inv + _hmm(inv, p, 2, 1)
    return inv


@jax.custom_vjp
def _unit_lower_inverse(a):
    return _unit_lower_inverse_impl(a)


def _unit_lower_inverse_fwd(a):
    inv = _unit_lower_inverse_impl(a)
    return inv, inv


def _unit_lower_inverse_bwd(inv, d_inv):
    return (_hmm(_hmm(inv, d_inv, 1, 1), inv, 2, 2),)


_unit_lower_inverse.defvjp(_unit_lower_inverse_fwd, _unit_lower_inverse_bwd)


def _rwkv_chunk(z0, r, lw, k, v, a, b):
    g, n, _ = r.shape
    row = lax.broadcasted_iota(jnp.int32, (g, n, n), 1)
    col = lax.broadcasted_iota(jnp.int32, (g, n, n), 2)
    incl, strict = row >= col, row > col
    cs = _tri_dot(lw, False)
    c_last = cs[:, n - 1:n]
    inv = jnp.exp(-cs)
    ar = jnp.concatenate([a * jnp.exp(cs - lw), r * jnp.exp(cs)], axis=1)
    bk = jnp.concatenate([b * inv, k * inv], axis=1)
    pair = _hmm(ar, bk, 2, 2)
    a_ab = jnp.where(strict, pair[:, :n, :n], 0.0)
    a_ak = jnp.where(strict, pair[:, :n, n:], 0.0)
    a_rbk = jnp.concatenate([jnp.where(incl, pair[:, n:, :n], 0.0), jnp.where(incl, pair[:, n:, n:], 0.0)], axis=2)
    arz = _hmm(ar, z0, 2, 1)
    u = _hmm(_unit_lower_inverse(a_ab), arz[:, :n] + _hmm(a_ak, v, 2, 1), 2, 1)
    uv = jnp.concatenate([u, v], axis=1)
    y = arz[:, n:] + _hmm(a_rbk, uv, 2, 1)
    tail = jnp.exp(c_last - cs)
    er = lax.broadcasted_iota(jnp.int32, (g, HEAD, HEAD), 1)
    ec = lax.broadcasted_iota(jnp.int32, (g, HEAD, HEAD), 2)
    decay_all = jnp.where(er == ec, jnp.exp(c_last), 0.0)
    z_l = _hmm(jnp.concatenate([b * tail, k * tail, decay_all], axis=1), jnp.concatenate([uv, z0], axis=1), 1, 1)
    return y, z_l


def _heads_in(ref, g):
    x = ref[...]
    return jnp.stack([x[:, h * HEAD:(h + 1) * HEAD] for h in range(g)], axis=0)


def _heads_out(ref, x):
    ref[...] = jnp.concatenate([x[h] for h in range(x.shape[0])], axis=1)


def _scan_geometry(n_rows, t):
    g, n = _tile(N_HEAD, SCAN_HEADS), _tile(t, CHUNK)
    return g, n, t // n, N_HEAD // g, n_rows // t


def _rwkv_scan_fwd(seqs, t, carry=None):
    rows = seqs[0].shape[0]
    g, n, nc, hpg, bsz = _scan_geometry(rows, t)

    def body(r_ref, lw_ref, k_ref, v_ref, a_ref, b_ref, y_ref, zs_ref, z_ref):
        @pl.when(pl.program_id(1) == 0)
        def _():
            z_ref[...] = jnp.zeros(z_ref.shape, F32)

        z0 = z_ref[...]
        zs_ref[:, 0] = z0
        y, z_l = _rwkv_chunk(z0, *[_heads_in(ref, g) for ref in (r_ref, lw_ref, k_ref, v_ref, a_ref, b_ref)])
        _heads_out(y_ref, y)
        z_ref[...] = z_l

    seq_spec = pl.BlockSpec((n, g * HEAD), lambda i, c: ((i // hpg) * nc + c, i % hpg))
    return _call(
        body, list(seqs), carry=carry, name="rwkv_scan_fwd", grid=(bsz * hpg, nc), in_specs=[seq_spec] * 6,
        out_specs=[seq_spec, pl.BlockSpec((g, 1, HEAD, HEAD), lambda i, c: (i, c, 0, 0))],
        out_shape=[jax.ShapeDtypeStruct((rows, D), F32), jax.ShapeDtypeStruct((bsz * N_HEAD, nc, HEAD, HEAD), F32)],
        scratch_shapes=[pltpu.VMEM((g, HEAD, HEAD), F32)],
        compiler_params=_params(2),
    )


def _rwkv_scan_bwd(seqs, zs, dy, t, carry=None):
    rows = seqs[0].shape[0]
    g, n, nc, hpg, bsz = _scan_geometry(rows, t)

    def body(r_ref, lw_ref, k_ref, v_ref, a_ref, b_ref, zs_ref, dy_ref, dr, dlw, dk, dv, da, db, dz_ref):
        @pl.when(pl.program_id(1) == 0)
        def _():
            dz_ref[...] = jnp.zeros(dz_ref.shape, F32)

        _, vjp = jax.vjp(_rwkv_chunk, zs_ref[:, 0],
                         *[_heads_in(ref, g) for ref in (r_ref, lw_ref, k_ref, v_ref, a_ref, b_ref)])
        grads = vjp((_heads_in(dy_ref, g), dz_ref[...]))
        dz_ref[...] = grads[0]
        for o, gr in zip((dr, dlw, dk, dv, da, db), grads[1:]):
            _heads_out(o, gr)

    seq_spec = pl.BlockSpec((n, g * HEAD), lambda i, c: ((i // hpg) * nc + nc - 1 - c, i % hpg))
    return _call(
        body, [*seqs, zs, dy], carry=carry, name="rwkv_scan_bwd", grid=(bsz * hpg, nc),
        in_specs=[seq_spec] * 6 + [pl.BlockSpec((g, 1, HEAD, HEAD), lambda i, c: (i, nc - 1 - c, 0, 0)), seq_spec],
        out_specs=[seq_spec] * 6,
        out_shape=[jax.ShapeDtypeStruct((rows, D), F32)] * 6,
        scratch_shapes=[pltpu.VMEM((g, HEAD, HEAD), F32)],
        compiler_params=_params(2),
    )


def _loss_head(x, target, norm):
    n = x.shape[0]
    tr = _tile(n, 256)

    def f(xv, gv, tv):
        err = _rmsnorm(xv, gv) - tv
        return 0.5 * jnp.sum(jnp.mean(err * err, axis=-1, keepdims=True), axis=0, keepdims=True)

    def body(x_ref, t_ref, g_ref, dx_ref, dg_ref, loss_ref, dx16_ref):
        val, vjp = jax.vjp(lambda xv, gv: f(xv, gv, t_ref[...]), x_ref[...], g_ref[...])
        dx, dg = vjp(jnp.ones((1, 1), F32))
        dx_ref[...] = dx
        dx16_ref[...] = dx.astype(BF16)

        @pl.when(pl.program_id(0) == 0)
        def _():
            dg_ref[...] = jnp.zeros(dg_ref.shape, F32)
            loss_ref[...] = jnp.zeros(loss_ref.shape, F32)

        dg_ref[...] += dg
        loss_ref[...] += jnp.broadcast_to(val, loss_ref.shape)

    row = pl.BlockSpec((tr, D), lambda i: (i, 0))
    vec = pl.BlockSpec((1, D), lambda i: (0, 0))
    dx, dg, loss, dx16 = _pcall(
        body, name="loss_head", grid=(n // tr,), in_specs=[row, row, vec],
        out_specs=[row, vec, pl.BlockSpec((8, LANE), lambda i: (0, 0)), row],
        out_shape=[jax.ShapeDtypeStruct((n, D), F32), jax.ShapeDtypeStruct((1, D), F32),
                   jax.ShapeDtypeStruct((8, LANE), F32), jax.ShapeDtypeStruct((n, D), BF16)],
        compiler_params=_params(1),
    )(x, target, norm)
    return loss[0, 0], (dx, dx16), dg


def _exchange(name, items):
    n_x = len(items)

    def body(*refs):
        srcs, dsts, sems = refs[:n_x], refs[n_x:2 * n_x], refs[2 * n_x:]
        started = [_exchange_copies(srcs[i], dsts[i], *sems[3 * i:3 * i + 3], items[i][1]) for i in range(n_x)]
        for mine, copies in started:
            _start_exchange(mine, copies)
        for mine, copies in started:
            _wait_exchange(mine, copies)

    hbm = pl.BlockSpec(memory_space=pl.ANY)
    return _pcall(
        body, name=name, in_specs=[hbm] * n_x, out_specs=[hbm] * n_x,
        out_shape=[_landing_shape(a, sc) for (a, sc) in items], scratch_shapes=_EXCHANGE_SEMS * n_x,
    )(*[a for (a, _) in items])


def _adamw(name, parts, w, m, v):
    rows, cols = w.shape
    tr = _tile(rows, PACK_ROWS)

    def body(p_ref, w_ref, m_ref, v_ref, g_ref, d_ref, nm_ref, nv_ref):
        g = p_ref[0]
        for k in range(1, N_DEV):
            g = g + p_ref[k]
        nm = ADAM_B1 * m_ref[...] + (1.0 - ADAM_B1) * g
        nv = ADAM_B2 * v_ref[...] + (1.0 - ADAM_B2) * jnp.square(g)
        m_hat = nm / (1.0 - ADAM_B1 ** ADAM_STEP)
        v_hat = nv / (1.0 - ADAM_B2 ** ADAM_STEP)
        g_ref[...] = g
        d_ref[...] = -ADAM_LR * (m_hat / (jnp.sqrt(v_hat) + ADAM_EPS) + ADAM_WD * w_ref[...])
        nm_ref[...] = nm
        nv_ref[...] = nv

    row = pl.BlockSpec((tr, cols), lambda i: (i, 0))
    return _pcall(
        body, name=name, grid=(rows // tr,),
        in_specs=[pl.BlockSpec((N_DEV, tr, cols), lambda i: (0, i, 0)), row, row, row],
        out_specs=[row] * 4, out_shape=[jax.ShapeDtypeStruct((rows, cols), F32)] * 4,
        compiler_params=_params(1),
    )(parts, w, m, v)


def _pack(arrs, dtype, lead=()):
    flat = jnp.concatenate([a.astype(dtype).reshape(lead + (-1,)) for a in arrs], axis=-1)
    n = flat.shape[-1]
    quantum = PACK_W * PACK_ROWS
    total = -(-n // quantum) * quantum
    flat = jnp.pad(flat, [(0, 0)] * len(lead) + [(0, total - n)])
    return flat.reshape(lead + (total // PACK_W, PACK_W))


def _unpack(buf, shapes, lead=()):
    flat = buf.reshape(lead + (-1,))
    out, off = [], 0
    for s in shapes:
        n = math.prod(s)
        out.append(flat[..., off:off + n].reshape(lead + tuple(s)))
        off += n
    return out


def _unshard(g, axis):
    local = g.shape[1:]
    full = jnp.moveaxis(g, 0, axis)
    return full.reshape(local[:axis] + (N_DEV * local[axis],) + local[axis + 1:])


def _reshard(full, axis):
    s = full.shape
    blocked = full.reshape(s[:axis] + (N_DEV, s[axis] // N_DEV) + s[axis + 1:])
    return jnp.moveaxis(blocked, axis, 0)


def _row_item(a, tr):
    return (a, (tr, a.shape[1]), lambda i: (i, 0))


def _vec_item(a):
    return (a, a.shape, lambda i: (0,) * a.ndim)


def _seq_item(a, t, width):
    return (a, (t, width), lambda j, b: (b, j))


def _seq_halves(a, t, width):
    half = a.shape[1] // 2
    off = half // width
    grad = ((a.shape[0], half), lambda j, b: (b, j))
    return [(a, (t, width), lambda j, b: (b, j), grad), (a, (t, width), lambda j, b: (b, j + off), grad)]


def _col_item(a, width):
    return (a, (a.shape[0], width), lambda j, b: (0, j))


def _both(g):
    return g if isinstance(g, tuple) else (g, g)


def _ffn_fwd(tag, x, p, t, carries=None, late=None):
    carries = carries or {}
    n = x.shape[0]
    tr = _tile(n, 512)
    norm_ins, norm_par = [_row_item(x, tr)], [_vec_item(p["norm"])]
    (h,) = _stage_fwd(f"{tag}_norm", _f_norm, (n // tr,), norm_ins, norm_par, [((n, D), BF16, (tr, D), lambda i: (i, 0))])
    u = _matmul(f"{tag}_up", h, p["w_up"], tm=2048, tn=768, carry=carries.get("up"))
    grid = (D_FF // CT, n // t)
    core_par = [_col_item(p["conv_w"], CT), _col_item(p["conv_b"], CT)]
    (hid,) = _stage_fwd(f"{tag}_core", _f_ffn_core, grid, _seq_halves(u, t, CT), core_par,
                        [((n, D_FF), BF16, (t, CT), lambda j, b: (b, j))], carry=carries.get("core"))
    if late is not None:
        late()
    y = _matmul(f"{tag}_down", hid, p["w_down"], resid=x, tm=1024, tn=1024, tk=D_FF)
    return y, (x, h, u, hid)


def _ffn_bwd(tag, dy, saved, p, t, carries=None, down_rides=None):
    carries = carries or {}
    x, h, u, hid = saved
    n = x.shape[0]
    tr = _tile(n, 512)
    grads = {}
    dy, dy16 = _both(dy)
    dhid = _matmul(f"{tag}_down_da", dy16, p["w_down"], mode="nt", tm=2048, tn=768, carry=carries.get("down_da"))
    grads["w_down"] = _matmul(f"{tag}_down_dw", hid, dy16, mode="tn", tm=768, tn=1024, tk=2048,
                              carry=carries.get("down_dw"))
    after = down_rides(grads["w_down"]) if down_rides is not None else [None] * 3
    grid = (D_FF // CT, n // t)
    core_par = [_col_item(p["conv_w"], CT), _col_item(p["conv_b"], CT)]
    (dug, duv), (grads["conv_w"], grads["conv_b"]) = _stage_bwd(
        f"{tag}_core_bwd", _f_ffn_core, grid, _seq_halves(u, t, CT), core_par,
        [([dhid], (t, CT), lambda j, b: (b, j))], [BF16, BF16], carry=carries.get("core"))
    dh = _matmul(f"{tag}_up_da_g", dug, p["w_up"], mode="nt", tm=1024, tn=1024, tk=D_FF, carry=after[0])
    dh = _matmul(f"{tag}_up_da_v", duv, p["w_up"], mode="nt", tm=1024, tn=1024, tk=D_FF, b_koff=D_FF, resid=dh,
                 carry=after[1])
    grads["w_up"] = jnp.concatenate([
        _matmul(f"{tag}_up_dw_g", h, dug, mode="tn", tm=1024, tn=768, tk=2048, carry=after[2]),
        _matmul(f"{tag}_up_dw_v", h, duv, mode="tn", tm=1024, tn=768, tk=2048)], axis=1)
    dx, (grads["norm"],) = _stage_bwd(
        f"{tag}_norm_bwd", _f_norm, (n // tr,), [_row_item(x, tr)], [_vec_item(p["norm"])],
        [([dh], (tr, D), lambda i: (i, 0))], [F32], adds={0: dy}, bf16_copy=True)
    return tuple(dx), grads


def _lru_fwd(x, p, t, carries=None, late=None):
    carries = carries or {}
    n = x.shape[0]
    tr = _tile(n, 512)
    row = lambda dt: ((n, D), dt, (tr, D), lambda i: (i, 0))
    h, xb = _stage_fwd("lru_pre", _f_lru_pre, (n // tr,), [_row_item(x, tr)],
                       [_vec_item(p["norm"]), _vec_item(p["b_out"])], [row(BF16), row(F32)], carry=carries.get("pre"))
    if late is not None:
        late("pre")
    u = _matmul("lru_in", h, p["w_in"], tm=1024, tn=1024, carry=carries.get("in"))
    if late is not None:
        late("in")
    out, = _stage_fwd("lru_core", _f_lru_core, (D // CT, n // t), _seq_halves(u, t, CT), _lru_core_params(p),
                      [((n, D), BF16, (t, CT), lambda j, b: (b, j))], carry=carries.get("core"))
    y = _matmul("lru_out", out, p["w_out"], resid=xb, tm=1024, tn=1024)
    return y, (x, h, u, out)


def _lru_core_params(p):
    return [_col_item(p["b_y"], CT), _col_item(p["b_x"], CT), _col_item(p["conv_w"], CT), _col_item(p["conv_b"], CT),
            (p["gate_w"], (2, 1, CT, CT), lambda j, b: (0, j, 0, 0)), _col_item(p["gate_b"], CT), _col_item(p["lam"], CT)]


def _lru_bwd(dy, saved, p, t, carry=None, out_rides=None):
    x, h, u, out = saved
    n = x.shape[0]
    tr = _tile(n, 512)
    grads = {}
    dy, dy16 = _both(dy)
    dout = _matmul("lru_out_da", dy16, p["w_out"], mode="nt", tm=1024, tn=1024)
    grads["w_out"] = _matmul("lru_out_dw", out, dy16, mode="tn", tm=1024, tn=1024)
    after = out_rides(grads["w_out"]) if out_rides is not None else [None] * 2
    (duy, dux), core_grads = _stage_bwd("lru_core_bwd", _f_lru_core, (D // CT, n // t), _seq_halves(u, t, CT),
                                        _lru_core_params(p), [([dout], (t, CT), lambda j, b: (b, j))], [BF16, BF16],
                                        carry=carry)
    for k, g in zip(("b_y", "b_x", "conv_w", "conv_b", "gate_w", "gate_b", "lam"), core_grads):
        grads[k] = g
    dh = _matmul("lru_in_da_y", duy, p["w_in"], mode="nt", tm=1024, tn=1024, carry=after[0])
    dh = _matmul("lru_in_da_x", dux, p["w_in"], mode="nt", tm=1024, tn=1024, b_koff=D, resid=dh, carry=after[1])
    grads["w_in"] = jnp.concatenate([_matmul("lru_in_dw_y", h, duy, mode="tn", tm=1024, tn=1024),
                                     _matmul("lru_in_dw_x", h, dux, mode="tn", tm=1024, tn=1024)], axis=1)
    row = (tr, D), lambda i: (i, 0)
    (dx,), (grads["norm"], grads["b_out"]) = _stage_bwd(
        "lru_pre_bwd", _f_lru_pre, (n // tr,), [_row_item(x, tr)], [_vec_item(p["norm"]), _vec_item(p["b_out"])],
        [([dh], *row), ([dy], *row)], [F32])
    return dx, grads


_RWKV_PRE_VECS = ("w0", "a0", "k_k", "k_a", "w2", "a2", "g2")
_RWKV_POST_VECS = ("ln_w", "ln_b", "r_k")


def _rwkv_fwd(x, p, t, carry=None):
    n = x.shape[0]
    bsz = n // t
    tr = _tile(n, 512)
    ts = _tile(n, 256)
    (h,) = _stage_fwd("rwkv_norm", _f_norm, (n // tr,), [_row_item(x, tr)], [_vec_item(p["norm"])],
                      [((n, D), F32, (tr, D), lambda i: (i, 0))])
    mixed = _stage_fwd("rwkv_mix", _f_rwkv_mix, (D // CT, bsz), [_seq_item(h, t, CT)], [_col_item(p["mix"], CT)],
                       [((n, D), BF16, (t, CT), lambda j, b: (b, j))] * 6)
    xr, xk, xv, xw, xa, xg = mixed
    r = _matmul("rwkv_r", xr, p["w_r"], tm=1024, tn=1024)
    k = _matmul("rwkv_k", xk, p["w_k"], tm=1024, tn=1024)
    v = _matmul("rwkv_v", xv, p["w_v"], tm=1024, tn=1024)
    lw1 = _matmul("rwkv_w1", xw, p["w1"], tm=1024)
    la1 = _matmul("rwkv_a1", xa, p["a1"], tm=1024)
    lg1 = _matmul("rwkv_g1", xg, p["g1"], tm=1024)
    pre_ins = [_row_item(a, ts) for a in (k, lw1, la1, lg1)]
    pre_par = [_vec_item(p[q]) for q in _RWKV_PRE_VECS]
    row = ((n, D), F32, (ts, D), lambda i: (i, 0))
    lw, kf, aa, bb, g = _stage_fwd("rwkv_pre", _f_rwkv_pre, (n // ts,), pre_ins, pre_par, [row] * 5)
    y, zs = _rwkv_scan_fwd([r, lw, kf, v, aa, bb], t, carry=carry)
    post_ins = [_row_item(a, ts) for a in (y, r, kf, v, g)]
    post_par = [_vec_item(p[q]) for q in _RWKV_POST_VECS]
    (yg,) = _stage_fwd("rwkv_post", _f_rwkv_post, (n // ts,), post_ins, post_par,
                       [((n, D), BF16, (ts, D), lambda i: (i, 0))])
    out = _matmul("rwkv_out", yg, p["w_out"], resid=x, tm=1024, tn=1024)
    return out, (x, h, mixed, r, k, v, lw1, la1, lg1, lw, kf, aa, bb, g, zs, y, yg)


def _rwkv_bwd(dout, saved, p, t, carry=None):
    x, h, mixed, r, k, v, lw1, la1, lg1, lw, kf, aa, bb, g, zs, y, yg = saved
    xr, xk, xv, xw, xa, xg = mixed
    n = x.shape[0]
    bsz = n // t
    tr = _tile(n, 512)
    ts = _tile(n, 256)
    grads = {}
    row_s = (ts, D), lambda i: (i, 0)
    dout, dout16 = _both(dout)
    dyg = _matmul("rwkv_out_da", dout16, p["w_out"], mode="nt", tm=1024, tn=1024)
    grads["w_out"] = _matmul("rwkv_out_dw", yg, dout16, mode="tn", tm=1024, tn=1024)
    post_ins = [_row_item(a, ts) for a in (y, r, kf, v, g)]
    post_par = [_vec_item(p[q]) for q in _RWKV_POST_VECS]
    (dy, dr_a, dkf_a, dv_a, dg), post_grads = _stage_bwd(
        "rwkv_post_bwd", _f_rwkv_post, (n // ts,), post_ins, post_par, [([dyg], *row_s)], [F32] * 5)
    grads.update(zip(_RWKV_POST_VECS, post_grads))
    dr_b, dlw, dkf_b, dv_b, daa, dbb = _rwkv_scan_bwd([r, lw, kf, v, aa, bb], zs, dy, t, carry=carry)
    pre_ins = [_row_item(a, ts) for a in (k, lw1, la1, lg1)]
    pre_par = [_vec_item(p[q]) for q in _RWKV_PRE_VECS]
    (dk, dlw1, dla1, dlg1), pre_grads = _stage_bwd(
        "rwkv_pre_bwd", _f_rwkv_pre, (n // ts,), pre_ins, pre_par,
        [([dlw], *row_s), ([dkf_a, dkf_b], *row_s), ([daa], *row_s), ([dbb], *row_s), ([dg], *row_s)], [F32] * 4)
    grads.update(zip(_RWKV_PRE_VECS, pre_grads))
    dmixed = []
    for tag, xin, dz, dz2, wname in (("r", xr, dr_a, dr_b, "w_r"), ("k", xk, dk, None, "w_k"), ("v", xv, dv_a, dv_b, "w_v"),
                                     ("w1", xw, dlw1, None, "w1"), ("a1", xa, dla1, None, "a1"), ("g1", xg, dlg1, None, "g1")):
        dmixed.append(_matmul(f"rwkv_{tag}_da", dz, p[wname], mode="nt", tm=1024, tn=1024, a_plus=dz2))
        grads[wname] = _matmul(f"rwkv_{tag}_dw", xin, dz, mode="tn", tm=1024, tn=1024, b_plus=dz2)
    seq_blk = (t, CT), lambda j, b: (b, j)
    (dh,), (grads["mix"],) = _stage_bwd(
        "rwkv_mix_bwd", _f_rwkv_mix, (D // CT, bsz), [_seq_item(h, t, CT)], [_col_item(p["mix"], CT)],
        [([a], *seq_blk) for a in dmixed], [F32])
    dx, (grads["norm"],) = _stage_bwd(
        "rwkv_norm_bwd", _f_norm, (n // tr,), [_row_item(x, tr)], [_vec_item(p["norm"])],
        [([dh], (tr, D), lambda i: (i, 0))], [F32], adds={0: dout}, bf16_copy=True)
    return tuple(dx), grads


def _block_diag_gates(gate_w):
    z = jnp.zeros((2, D // CT, HEAD, HEAD), gate_w.dtype)
    even, odd = gate_w[:, 0::2], gate_w[:, 1::2]
    top = jnp.concatenate([even, z], axis=-1)
    bot = jnp.concatenate([z, odd], axis=-1)
    return jnp.concatenate([top, bot], axis=-2)


def _gate_blocks_grad(dg):
    even, odd = dg[:, :, :HEAD, :HEAD], dg[:, :, HEAD:, HEAD:]
    return jnp.stack([even, odd], axis=2).reshape(2, N_HEAD, HEAD, HEAD)


def _step(local, x, target):
    bsz, t, _ = x.shape
    n = bsz * t

    def block(name, pre=""):
        base, _, layer = name.partition("@")
        a = local[pre + base]
        return a[int(layer):int(layer) + 1] if layer else a

    def split(names):
        return [k for k in names if _base(k) in LARGE], [k for k in names if _base(k) not in LARGE]

    def gather_items(names):
        large, small = split(names)
        flat = [block(k).astype(BF16).reshape(-1) if _base(k) in GATHER_BF16
                else lax.bitcast_convert_type(block(k), BF16).reshape(-1) for k in small]
        return [(block(k).astype(BF16), False) for k in large] + ([(_pack(flat, BF16), False)] if small else [])

    def gathered(names, landed):
        large, small = split(names)
        out = {k: _unshard(g, SHARD_AXIS[_base(k)]) for k, g in zip(large, landed)}
        sizes = [(1 if _base(k) in GATHER_BF16 else 2) * math.prod(block(k).shape) for k in small]
        for k, g in zip(small, _unpack(landed[-1], [(s,) for s in sizes], (N_DEV,)) if small else []):
            if _base(k) not in GATHER_BF16:
                g = lax.bitcast_convert_type(g.reshape(N_DEV, -1, 2), F32)
            out[k] = _unshard(g.reshape((N_DEV,) + block(k).shape), SHARD_AXIS[_base(k)])
        return out

    landed_large, landed_small = {}, {}

    def scatter_items(names, grads):
        large, small = split(names)
        blocked = {k: _reshard(grads[k], SHARD_AXIS[_base(k)]) for k in names}
        return [(blocked[k], True) for k in large] + ([(_pack([blocked[k] for k in small], F32, (N_DEV,)), True)] if small else [])

    def scattered(names, landed):
        large, small = split(names)
        landed_large.update(zip(large, landed))
        if small:
            landed_small[tuple(small)] = landed[-1]

    def ffn_grads(l, g):
        return {f"ffn_w_up@{l}": g["w_up"][None], f"ffn_w_down@{l}": g["w_down"][None], f"ffn_conv_w@{l}": g["conv_w"][None]}

    lru = dict(norm=local["lru_norm"], b_out=local["lru_b_out"], b_y=local["lru_b_in"][:, :D], b_x=local["lru_b_in"][:, D:],
               conv_b=local["lru_conv_b"], gate_w=_block_diag_gates(local["lru_gate_w"][0]),
               gate_b=local["lru_gate_b"][0].reshape(2, D), lam=local["lru_lambda"])
    lru_first = ["lru_w_in", "lru_conv_w"]
    ride_pre = _Carry(gather_items(lru_first))
    ride_in = _Carry(gather_items(["lru_w_out"]))
    ffn0_first = ["ffn_w_up@0", "ffn_conv_w@0"]
    ride_core = _Carry(gather_items(ffn0_first))

    def lru_late(which):
        if which == "pre":
            full = gathered(lru_first, ride_pre.landed)
            lru.update(w_in=full["lru_w_in"][0], conv_w=full["lru_conv_w"][0])
        else:
            lru.update(w_out=gathered(["lru_w_out"], ride_in.landed)["lru_w_out"][0])

    x0 = x.reshape(n, D)
    x1, s_lru = _lru_fwd(x0, lru, t, carries={"pre": ride_pre, "in": ride_in, "core": ride_core}, late=lru_late)
    full = gathered(ffn0_first, ride_core.landed)
    ffn0 = dict(norm=local["ffn_norm"][0:1], conv_b=local["ffn_conv_b"][0:1], w_up=full["ffn_w_up@0"][0],
                conv_w=full["ffn_conv_w@0"][0])
    ride_up = _Carry(gather_items(GROUPS["rwkv"]))
    ride_down = _Carry(gather_items(["ffn_w_down@0"]))

    def ffn0_late():
        ffn0["w_down"] = gathered(["ffn_w_down@0"], ride_down.landed)["ffn_w_down@0"][0]

    x2, s_ffn0 = _ffn_fwd("ffn0", x1, ffn0, t, carries={"up": ride_up, "core": ride_down}, late=ffn0_late)
    full = gathered(GROUPS["rwkv"], ride_up.landed)
    rwkv = dict(norm=full["rwkv_norm"], mix=full["rwkv_mix"][0], w_r=full["rwkv_w_rkv"][0, 0],
                w_k=full["rwkv_w_rkv"][0, 1], w_v=full["rwkv_w_rkv"][0, 2], w0=full["rwkv_w0"], w1=full["rwkv_w1"][0],
                w2=full["rwkv_w2"][0], a0=full["rwkv_a0"], a1=full["rwkv_a1"][0], a2=full["rwkv_a2"][0],
                g1=full["rwkv_g1"][0], g2=full["rwkv_g2"][0], k_k=full["rwkv_k_k"], k_a=full["rwkv_k_a"],
                r_k=local["rwkv_r_k"].reshape(1, D), ln_w=full["rwkv_ln_w"], ln_b=full["rwkv_ln_b"],
                w_out=full["rwkv_w_out"][0])
    ride = _Carry(gather_items(GROUPS["ffn1"]))
    x3, s_rwkv = _rwkv_fwd(x2, rwkv, t, carry=ride)
    full = gathered(GROUPS["ffn1"], ride.landed)
    ffn1 = dict(norm=local["ffn_norm"][1:2], conv_b=local["ffn_conv_b"][1:2], w_up=full["ffn_w_up@1"][0],
                conv_w=full["ffn_conv_w@1"][0], w_down=full["ffn_w_down@1"][0])
    x4, s_ffn1 = _ffn_fwd("ffn1", x3, ffn1, t)
    loss, dx4, d_final = _loss_head(x4, target.reshape(n, D), local["final_norm"].reshape(1, D))

    dx3, g_ffn1 = _ffn_bwd("ffn1", dx4, s_ffn1, ffn1, t)
    ride = _Carry(scatter_items(GROUPS["ffn1"], ffn_grads(1, g_ffn1)))
    dx2, g_rwkv = _rwkv_bwd(dx3, s_rwkv, rwkv, t, carry=ride)
    scattered(GROUPS["ffn1"], ride.landed)
    rwkv_grads = {
        "rwkv_norm": g_rwkv["norm"], "rwkv_mix": g_rwkv["mix"][None],
        "rwkv_w_rkv": jnp.stack([g_rwkv["w_r"], g_rwkv["w_k"], g_rwkv["w_v"]])[None],
        "rwkv_w0": g_rwkv["w0"], "rwkv_w1": g_rwkv["w1"][None], "rwkv_w2": g_rwkv["w2"][None],
        "rwkv_a0": g_rwkv["a0"], "rwkv_a1": g_rwkv["a1"][None], "rwkv_a2": g_rwkv["a2"][None],
        "rwkv_g1": g_rwkv["g1"][None], "rwkv_g2": g_rwkv["g2"][None], "rwkv_k_k": g_rwkv["k_k"],
        "rwkv_k_a": g_rwkv["k_a"], "rwkv_ln_w": g_rwkv["ln_w"], "rwkv_ln_b": g_rwkv["ln_b"],
        "rwkv_w_out": g_rwkv["w_out"][None],
    }
    rwkv_small = split(GROUPS["rwkv"])[1]
    ride_da = _Carry(scatter_items(["rwkv_w_out"], rwkv_grads))
    ride_dw = _Carry(scatter_items(rwkv_small, rwkv_grads))
    ride_core = _Carry(scatter_items(["rwkv_w_rkv"], rwkv_grads))
    down_pieces = []

    def down_rides(g_w_down):
        blocked = _reshard(g_w_down[None], SHARD_AXIS["ffn_w_down"])
        rows = blocked.shape[2] // 3
        down_pieces.extend(_Carry([(blocked[:, :, i * rows:(i + 1) * rows], True)]) for i in range(3))
        return down_pieces

    dx1, g_ffn0 = _ffn_bwd("ffn0", dx2, s_ffn0, ffn0, t,
                           carries={"down_da": ride_da, "down_dw": ride_dw, "core": ride_core}, down_rides=down_rides)
    scattered(["rwkv_w_out"], ride_da.landed)
    scattered(rwkv_small, ride_dw.landed)
    scattered(["rwkv_w_rkv"], ride_core.landed)
    landed_large["ffn_w_down@0"] = jnp.concatenate([c.landed[0] for c in down_pieces], axis=2)
    ride = _Carry(scatter_items(ffn0_first, ffn_grads(0, g_ffn0)))
    out_pieces = []

    def out_rides(g_w_out):
        blocked = _reshard(g_w_out[None], SHARD_AXIS["lru_w_out"])
        rows = blocked.shape[2] // 2
        out_pieces.extend(_Carry([(blocked[:, :, i * rows:(i + 1) * rows], True)]) for i in range(2))
        return out_pieces

    dx0, g_lru = _lru_bwd(dx1, s_lru, lru, t, carry=ride, out_rides=out_rides)
    scattered(ffn0_first, ride.landed)
    landed_large["lru_w_out"] = jnp.concatenate([c.landed[0] for c in out_pieces], axis=2)

    g_ffn = [g_ffn0, g_ffn1]
    lru_grads = {"lru_w_in": g_lru["w_in"][None], "lru_conv_w": g_lru["conv_w"][None], "lru_w_out": g_lru["w_out"][None]}
    gfull = {
        "lru_norm": g_lru["norm"], "lru_b_in": jnp.concatenate([g_lru["b_y"], g_lru["b_x"]], axis=1),
        "lru_conv_b": g_lru["conv_b"], "lru_gate_w": _gate_blocks_grad(g_lru["gate_w"])[None],
        "lru_gate_b": g_lru["gate_b"].reshape(1, 2, N_HEAD, HEAD), "lru_lambda": g_lru["lam"],
        "lru_b_out": g_lru["b_out"], "rwkv_r_k": g_rwkv["r_k"].reshape(1, N_HEAD, HEAD),
        "ffn_norm": jnp.concatenate([g["norm"] for g in g_ffn]),
        "ffn_conv_b": jnp.concatenate([g["conv_b"] for g in g_ffn]),
        "final_norm": d_final.reshape(D),
    }
    *tail, replicated = _exchange(
        "grad_tail", scatter_items(lru_first, lru_grads) + [(_pack([gfull[k] for k in REPLICATED], F32), False)])
    scattered(lru_first, tail)

    out = {}
    kinds = ("grad", "delta", "new_m", "new_v")
    for k, landed in landed_large.items():
        flat = (math.prod(block(k).shape[:-1]), block(k).shape[-1])
        res = _adamw(f"adamw_{k.replace('@', '_')}", landed.reshape((N_DEV,) + flat),
                     *[block(k, pre).reshape(flat) for pre in ("", "m_", "v_")])
        for kind, a in zip(kinds, res):
            out[(kind, k)] = a.reshape(block(k).shape)
    small_sets = [(f"adamw_small_{i}", list(names), landed) for i, (names, landed) in enumerate(landed_small.items())]
    for tag, names, landed in small_sets + [("adamw_replicated", REPLICATED, replicated)]:
        packs = [_pack([block(k, pre) for k in names], F32) for pre in ("", "m_", "v_")]
        res = _adamw(tag, landed, *packs)
        for kind, buf in zip(kinds, res):
            for k, a in zip(names, _unpack(buf, [block(k).shape for k in names])):
                out[(kind, k)] = a
    for kind in ("grad", "delta", "new_m", "new_v"):
        for k in STACKED:
            out[(kind, k)] = jnp.concatenate([out[(kind, f"{k}@0")], out[(kind, f"{k}@1")]], axis=0)
    return loss, dx0.reshape(x.shape), out


def kernel(x, lru_norm, lru_w_in, lru_b_in, lru_conv_w, lru_conv_b, lru_gate_w, lru_gate_b, lru_lambda, lru_w_out, lru_b_out, rwkv_norm, rwkv_mix, rwkv_w_rkv, rwkv_w0, rwkv_w1, rwkv_w2, rwkv_a0, rwkv_a1, rwkv_a2, rwkv_g1, rwkv_g2, rwkv_k_k, rwkv_k_a, rwkv_r_k, rwkv_ln_w, rwkv_ln_b, rwkv_w_out, ffn_norm, ffn_w_up, ffn_conv_w, ffn_conv_b, ffn_w_down, final_norm, loss_target, m_lru_norm, m_lru_w_in, m_lru_b_in, m_lru_conv_w, m_lru_conv_b, m_lru_gate_w, m_lru_gate_b, m_lru_lambda, m_lru_w_out, m_lru_b_out, m_rwkv_norm, m_rwkv_mix, m_rwkv_w_rkv, m_rwkv_w0, m_rwkv_w1, m_rwkv_w2, m_rwkv_a0, m_rwkv_a1, m_rwkv_a2, m_rwkv_g1, m_rwkv_g2, m_rwkv_k_k, m_rwkv_k_a, m_rwkv_r_k, m_rwkv_ln_w, m_rwkv_ln_b, m_rwkv_w_out, m_ffn_norm, m_ffn_w_up, m_ffn_conv_w, m_ffn_conv_b, m_ffn_w_down, m_final_norm, v_lru_norm, v_lru_w_in, v_lru_b_in, v_lru_conv_w, v_lru_conv_b, v_lru_gate_w, v_lru_gate_b, v_lru_lambda, v_lru_w_out, v_lru_b_out, v_rwkv_norm, v_rwkv_mix, v_rwkv_w_rkv, v_rwkv_w0, v_rwkv_w1, v_rwkv_w2, v_rwkv_a0, v_rwkv_a1, v_rwkv_a2, v_rwkv_g1, v_rwkv_g2, v_rwkv_k_k, v_rwkv_k_a, v_rwkv_r_k, v_rwkv_ln_w, v_rwkv_ln_b, v_rwkv_w_out, v_ffn_norm, v_ffn_w_up, v_ffn_conv_w, v_ffn_conv_b, v_ffn_w_down, v_final_norm):
    args = locals()
    local = {(pre + k): args[pre + k] for pre in ("", "m_", "v_") for k in WEIGHTS}
    loss_local, grad_x, out = _step(local, x, loss_target)
    loss = lax.psum(loss_local, ("x", "y", "c"))
    return (loss, grad_x, *[out[(kind, k)] for kind in ("grad", "delta", "new_m", "new_v") for k in WEIGHTS])
```

```python
import functools
import math

import jax
import jax.numpy as jnp
from jax import lax
from jax.experimental import pallas as pl
from jax.experimental.pallas import tpu as pltpu

F32 = jnp.float32
BF16 = jnp.bfloat16

N_DEV = 8
D = 1024
HEAD = 64
N_HEAD = D // HEAD
D_FF = 3 * D
LANE = 128
V7X_VMEM_BYTES = 64 * 1024 * 1024
VMEM_LIMIT = V7X_VMEM_BYTES - 8 * 1024 * 1024
CT = LANE
CHUNK = 64
SCAN_HEADS = 16
PACK_W = 1024
PACK_ROWS = 256

ADAM_LR, ADAM_B1, ADAM_B2, ADAM_EPS, ADAM_WD, ADAM_STEP = 0.001, 0.9, 0.999, 1e-08, 0.01, 10
RMS_EPS = 1e-6
GN_EPS = 64e-5
LRU_C = 8.0

WEIGHTS = ['lru_norm', 'lru_w_in', 'lru_b_in', 'lru_conv_w', 'lru_conv_b', 'lru_gate_w', 'lru_gate_b',
           'lru_lambda', 'lru_w_out', 'lru_b_out', 'rwkv_norm', 'rwkv_mix', 'rwkv_w_rkv', 'rwkv_w0', 'rwkv_w1',
           'rwkv_w2', 'rwkv_a0', 'rwkv_a1', 'rwkv_a2', 'rwkv_g1', 'rwkv_g2', 'rwkv_k_k', 'rwkv_k_a', 'rwkv_r_k',
           'rwkv_ln_w', 'rwkv_ln_b', 'rwkv_w_out', 'ffn_norm', 'ffn_w_up', 'ffn_conv_w', 'ffn_conv_b',
           'ffn_w_down', 'final_norm']
SHARD_AXIS = {'lru_w_in': 2, 'lru_conv_w': 2, 'lru_w_out': 1, 'rwkv_norm': 1, 'rwkv_mix': 2, 'rwkv_w_rkv': 2,
              'rwkv_w0': 1, 'rwkv_w1': 1, 'rwkv_w2': 2, 'rwkv_a0': 1, 'rwkv_a1': 1, 'rwkv_a2': 2, 'rwkv_g1': 1,
              'rwkv_g2': 2, 'rwkv_k_k': 1, 'rwkv_k_a': 1, 'rwkv_ln_w': 1, 'rwkv_ln_b': 1, 'rwkv_w_out': 1,
              'ffn_w_up': 2, 'ffn_conv_w': 2, 'ffn_w_down': 1}
GATHER_BF16 = ['lru_w_in', 'lru_w_out', 'rwkv_w_rkv', 'rwkv_w1', 'rwkv_a1', 'rwkv_g1', 'rwkv_w_out', 'ffn_w_up',
               'ffn_w_down']
REPLICATED = [n for n in WEIGHTS if n not in SHARD_AXIS]
STACKED = ['ffn_w_up', 'ffn_w_down', 'ffn_conv_w']
LARGE = ['lru_w_in', 'lru_w_out', 'rwkv_w_rkv', 'rwkv_w_out', 'ffn_w_up', 'ffn_w_down']
GROUPS = {
    "lru": ['lru_w_in', 'lru_w_out', 'lru_conv_w'],
    "ffn0": [f"{n}@0" for n in STACKED],
    "rwkv": [n for n in WEIGHTS if n.startswith("rwkv_") and n in SHARD_AXIS],
    "ffn1": [f"{n}@1" for n in STACKED],
}


def _base(name):
    return name.partition("@")[0]


def _pcall(body, **kw):
    return pl.pallas_call(body, **kw)


class _Carry:
    def __init__(self, items):
        self.items = list(items)
        self.landed = None


def _exchange_copies(src_ref, dst_ref, send_sems, recv_sems, local_sem, scatter):
    x, y, c = lax.axis_index("x"), lax.axis_index("y"), lax.axis_index("c")
    me = 4 * x + 2 * y + c
    mine = pltpu.make_async_copy(src_ref.at[me] if scatter else src_ref, dst_ref.at[me], local_sem)
    copies = []
    for m in range(1, N_DEV):
        px = 1 - x if m & 4 else x
        py = 1 - y if m & 2 else y
        pc = 1 - c if m & 1 else c
        part = src_ref.at[4 * px + 2 * py + pc] if scatter else src_ref
        copies.append(pltpu.make_async_remote_copy(
            src_ref=part, dst_ref=dst_ref.at[me], send_sem=send_sems.at[m - 1], recv_sem=recv_sems.at[m - 1],
            device_id=(px, py, pc), device_id_type=pl.DeviceIdType.MESH))
    return mine, copies


def _start_exchange(mine, copies):
    mine.start()
    for cp in copies:
        cp.start()


def _wait_exchange(mine, copies):
    for cp in copies:
        cp.wait_recv()
    for cp in copies:
        cp.wait_send()
    mine.wait()


_EXCHANGE_SEMS = [pltpu.SemaphoreType.DMA((N_DEV - 1,)), pltpu.SemaphoreType.DMA((N_DEV - 1,)), pltpu.SemaphoreType.DMA]


def _landing_shape(src, scatter):
    return jax.ShapeDtypeStruct((N_DEV,) + tuple(src.shape[1:] if scatter else src.shape), src.dtype)


def _call(body, operands, *, carry=None, name, grid, in_specs, out_specs, out_shape, scratch_shapes=(),
          compiler_params=None):
    if carry is None:
        return _pcall(body, name=name, grid=grid, in_specs=in_specs, out_specs=out_specs, out_shape=out_shape,
                      scratch_shapes=list(scratch_shapes), compiler_params=compiler_params)(*operands)
    single = not isinstance(out_specs, (list, tuple))
    out_specs_l = [out_specs] if single else list(out_specs)
    out_shape_l = [out_shape] if single else list(out_shape)
    n_in, n_out, n_scr, n_x = len(in_specs), len(out_specs_l), len(scratch_shapes), len(carry.items)
    flags = [sc for (_, sc) in carry.items]

    def wrapped(*refs):
        ins, refs = refs[:n_in], refs[n_in:]
        xsrc, refs = refs[:n_x], refs[n_x:]
        outs, refs = refs[:n_out], refs[n_out:]
        xdst, refs = refs[:n_x], refs[n_x:]
        scr, sems = refs[:n_scr], refs[n_scr:]
        first = functools.reduce(jnp.logical_and, [pl.program_id(i) == 0 for i in range(len(grid))])
        last = functools.reduce(jnp.logical_and, [pl.program_id(i) == grid[i] - 1 for i in range(len(grid))])

        def exchanges():
            return [_exchange_copies(xsrc[i], xdst[i], *sems[3 * i:3 * i + 3], flags[i]) for i in range(n_x)]

        @pl.when(first)
        def _():
            for mine, copies in exchanges():
                _start_exchange(mine, copies)

        body(*ins, *outs, *scr)

        @pl.when(last)
        def _():
            for mine, copies in exchanges():
                _wait_exchange(mine, copies)

    hbm = pl.BlockSpec(memory_space=pl.ANY)
    res = _pcall(
        wrapped, name=name, grid=grid, in_specs=list(in_specs) + [hbm] * n_x,
        out_specs=out_specs_l + [hbm] * n_x,
        out_shape=out_shape_l + [_landing_shape(a, sc) for (a, sc) in carry.items],
        scratch_shapes=list(scratch_shapes) + _EXCHANGE_SEMS * n_x, compiler_params=compiler_params,
    )(*operands, *[a for (a, _) in carry.items])
    carry.landed = list(res[n_out:])
    return res[0] if single else list(res[:n_out])


def _params(n_grid):
    return pltpu.CompilerParams(dimension_semantics=("arbitrary",) * n_grid, vmem_limit_bytes=VMEM_LIMIT)


def _shift_rows(x, d, up):
    n = x.shape[0]
    idx = lax.broadcasted_iota(jnp.int32, x.shape, 0)
    if up:
        return jnp.where(idx < n - d, pltpu.roll(x, n - d, 0), 0.0)
    return jnp.where(idx >= d, pltpu.roll(x, d, 0), 0.0)


@functools.partial(jax.custom_vjp, nondiff_argnums=(1,))
def _shift_down(x, d):
    return _shift_rows(x, d, False)


def _shift_down_fwd(x, d):
    return _shift_rows(x, d, False), None


def _shift_down_bwd(d, _, g):
    return (_shift_rows(g, d, True),)


_shift_down.defvjp(_shift_down_fwd, _shift_down_bwd)


SUBLANES = 8


def _shift_in_tile(x, d, up, fill):
    n = x.shape[0]
    pos = lax.broadcasted_iota(jnp.int32, x.shape, 0) & (SUBLANES - 1)
    if up:
        return jnp.where(pos < SUBLANES - d, pltpu.roll(x, n - d, 0), fill)
    return jnp.where(pos >= d, pltpu.roll(x, d, 0), fill)


def _scan_doubling(a, b, up):
    n, c = a.shape
    if n % (SUBLANES * SUBLANES) == 0:
        d = 1
        while d < SUBLANES:
            b = b + a * _shift_in_tile(b, d, up, 0.0)
            a = a * _shift_in_tile(a, d, up, 1.0)
            d *= 2
        tiles = n // SUBLANES
        a3, b3 = a.reshape(tiles, SUBLANES, c), b.reshape(tiles, SUBLANES, c)
        edge = 0 if up else SUBLANES - 1
        entering = _shift_rows(_scan_doubling_plain(a3[:, edge, :], b3[:, edge, :], up), 1, up)
        return (b3 + a3 * entering[:, None, :]).reshape(n, c)
    return _scan_doubling_plain(a, b, up)


def _scan_doubling_plain(a, b, up):
    n = a.shape[0]
    d = 1
    while d < n:
        b = b + a * _shift_rows(b, d, up)
        a = a * _shift_rows(a, d, up)
        d *= 2
    return b


@jax.custom_vjp
def _linear_scan(a, b):
    return _scan_doubling(a, b, False)


def _linear_scan_fwd(a, b):
    h = _scan_doubling(a, b, False)
    return h, (a, h)


def _linear_scan_bwd(res, dh):
    a, h = res
    g = _scan_doubling(_shift_rows(a, 1, True), dh, True)
    return g * _shift_rows(h, 1, False), g


_linear_scan.defvjp(_linear_scan_fwd, _linear_scan_bwd)


def _causal_conv(x, w, b):
    k = w.shape[0]
    out = b + x * w[k - 1:k]
    for j in range(k - 1):
        out = out + _shift_down(x, k - 1 - j) * w[j:j + 1]
    return out


_GELU_C1 = math.sqrt(2.0 / math.pi)
_GELU_C2 = 0.044715 * _GELU_C1


@jax.custom_vjp
def _gelu(x):
    return 0.5 * x * (1.0 + jnp.tanh(x * (_GELU_C1 + _GELU_C2 * (x * x))))


def _gelu_fwd(x):
    x2 = x * x
    t = jnp.tanh(x * (_GELU_C1 + _GELU_C2 * x2))
    return 0.5 * x * (1.0 + t), (x, x2, t)


def _gelu_bwd(res, g):
    x, x2, t = res
    return (g * (0.5 * (1.0 + t) + (0.5 * x) * (1.0 - t * t) * (_GELU_C1 + (3.0 * _GELU_C2) * x2)),)


_gelu.defvjp(_gelu_fwd, _gelu_bwd)


def _rmsnorm(x, g):
    return x * lax.rsqrt(jnp.mean(x * x, axis=-1, keepdims=True) + RMS_EPS) * g


def _neg_expm1(x):
    series = x * (1.0 + x * 0.5 * (1.0 + x * (1.0 / 3.0) * (1.0 + x * 0.25 * (1.0 + x * 0.2))))
    return -jnp.where(x > -0.1, series, jnp.exp(x) - 1.0)


def _dot(a, b, ca=1, cb=0, precision=None):
    return lax.dot_general(a, b, (((ca,), (cb,)), ((), ())), precision=precision, preferred_element_type=F32)


def _bdot(a, b):
    return _dot(a.astype(BF16), b.astype(BF16))


def _split_bf16(x):
    hi = x.astype(BF16)
    return hi, (x - hi.astype(F32)).astype(BF16)


def _head_sum_impl(x):
    row = lax.broadcasted_iota(jnp.int32, (D, LANE), 0)
    col = lax.broadcasted_iota(jnp.int32, (D, LANE), 1)
    e = (lax.shift_right_logical(row, 6) == col).astype(BF16)
    hi, lo = _split_bf16(x)
    s_hi, s_lo = _split_bf16(_dot(hi, e) + _dot(lo, e))
    return _dot(s_hi, e, 1, 1) + _dot(s_lo, e, 1, 1)


@jax.custom_vjp
def _head_sum(x):
    return _head_sum_impl(x)


def _head_sum_fwd(x):
    return _head_sum_impl(x), None


def _head_sum_bwd(_, g):
    return (_head_sum(g),)


_head_sum.defvjp(_head_sum_fwd, _head_sum_bwd)


def _specs(items):
    return [pl.BlockSpec(it[1], it[2]) for it in items]


def _stage_fwd(name, f, grid, ins, params, outs, carry=None):
    n_in = len(ins) + len(params)

    def body(*refs):
        res = f(*[r[...] for r in refs[:n_in]])
        for o, v in zip(refs[n_in:], res):
            o[...] = v.astype(o.dtype)

    return _call(
        body, [it[0] for it in ins + params], carry=carry, name=name, grid=grid, in_specs=_specs(ins + params),
        out_specs=[pl.BlockSpec(bs, im) for (_, _, bs, im) in outs],
        out_shape=[jax.ShapeDtypeStruct(s, dt) for (s, dt, _, _) in outs],
        compiler_params=_params(len(grid)),
    )


def _stage_bwd(name, f, grid, ins, params, douts, din_dtypes, adds=None, carry=None, bf16_copy=False):
    adds = adds or {}
    n_in, n_par = len(ins), len(params)
    dout_items = [(a, bs, im) for (arrs, bs, im) in douts for a in arrs]
    add_items = [(adds[i], ins[i][1], ins[i][2]) for i in sorted(adds)]
    n_do, n_add = len(dout_items), len(add_items)

    def body(*refs):
        vals = [r[...] for r in refs[:n_in + n_par]]
        do_refs = list(refs[n_in + n_par:n_in + n_par + n_do])
        add_refs = dict(zip(sorted(adds), refs[n_in + n_par + n_do:n_in + n_par + n_do + n_add]))
        din_refs = refs[n_in + n_par + n_do + n_add:n_in + n_par + n_do + n_add + n_in]
        dpar_refs = refs[n_in + n_par + n_do + n_add + n_in:n_in + n_par + n_do + n_add + n_in + n_par]
        cts = []
        for (arrs, _, _) in douts:
            ct = do_refs.pop(0)[...].astype(F32)
            for _ in arrs[1:]:
                ct = ct + do_refs.pop(0)[...].astype(F32)
            cts.append(ct)
        _, vjp = jax.vjp(f, *vals)
        grads = vjp(tuple(cts))
        for i, r in enumerate(din_refs):
            g = grads[i]
            if i in add_refs:
                g = g + add_refs[i][...]
            r[...] = g.astype(r.dtype)
            if i == 0 and bf16_copy:
                refs[-1][...] = g.astype(BF16)

        @pl.when(pl.program_id(len(grid) - 1) == 0)
        def _():
            for r in dpar_refs:
                r[...] = jnp.zeros(r.shape, r.dtype)

        for j, r in enumerate(dpar_refs):
            r[...] += grads[n_in + j]

    din_shapes = [it[3][0] if len(it) > 3 else it[0].shape for it in ins]
    din_specs = [pl.BlockSpec(it[1], it[3][1] if len(it) > 3 else it[2]) for it in ins]
    copy_specs = [din_specs[0]] if bf16_copy else []
    copy_shapes = [jax.ShapeDtypeStruct(din_shapes[0], BF16)] if bf16_copy else []
    res = _call(
        body, [it[0] for it in ins + params + dout_items + add_items], carry=carry, name=name, grid=grid,
        in_specs=_specs(ins + params + dout_items + add_items), out_specs=din_specs + _specs(params) + copy_specs,
        out_shape=[jax.ShapeDtypeStruct(s, dt) for s, dt in zip(din_shapes, din_dtypes)]
        + [jax.ShapeDtypeStruct(it[0].shape, F32) for it in params] + copy_shapes,
        compiler_params=_params(len(grid)),
    )
    return list(res[:n_in]) + list(res[n_in + n_par:]), list(res[n_in:n_in + n_par])


def _tile(n, want):
    t = min(n, want)
    while n % t:
        t //= 2
    return t


def _matmul(name, a, b, *, mode="nn", resid=None, out_dtype=F32, tm=512, tn=512, tk=1024, b_koff=0, carry=None,
            a_plus=None, b_plus=None):
    if mode == "tn":
        (kdim, m), n = a.shape, b.shape[1]
    else:
        (m, kdim), n = a.shape, (b.shape[1] if mode == "nn" else b.shape[0])
    tm, tn, tk = _tile(m, tm), _tile(n, tn), _tile(kdim, tk)
    nk = kdim // tk
    ko = b_koff // tk
    assert ko * tk == b_koff
    a_spec = pl.BlockSpec((tk, tm), lambda i, j, k: (k, i)) if mode == "tn" else pl.BlockSpec((tm, tk), lambda i, j, k: (i, k))
    b_spec = pl.BlockSpec((tn, tk), lambda i, j, k: (j, k + ko)) if mode == "nt" else pl.BlockSpec((tk, tn), lambda i, j, k: (k + ko, j))
    ca = 0 if mode == "tn" else 1
    cb = 1 if mode == "nt" else 0
    operands = [a, b]
    in_specs = [a_spec, b_spec]
    extra = {}
    for key, arr, spec in (("a", a_plus, a_spec), ("b", b_plus, b_spec), ("resid", resid, pl.BlockSpec((tm, tn), lambda i, j, k: (i, j)))):
        if arr is not None:
            extra[key] = len(operands)
            operands.append(arr)
            in_specs.append(spec)

    def product(refs):
        av, bv = refs[0][...], refs[1][...]
        if "a" in extra:
            av = av + refs[extra["a"]][...]
        if "b" in extra:
            bv = bv + refs[extra["b"]][...]
        return _dot(av.astype(BF16), bv.astype(BF16), ca, cb)

    def finish(r, refs, o_ref):
        if resid is not None:
            r = r + refs[extra["resid"]][...]
        o_ref[...] = r.astype(o_ref.dtype)

    def body_one_step(*refs):
        finish(product(refs), refs, refs[-1])

    def body(*refs):
        o_ref, acc_ref = refs[-2], refs[-1]
        k = pl.program_id(2)

        @pl.when(k == 0)
        def _():
            acc_ref[...] = jnp.zeros(acc_ref.shape, F32)

        acc_ref[...] += product(refs)

        @pl.when(k == nk - 1)
        def _():
            finish(acc_ref[...], refs, o_ref)

    return _call(
        body_one_step if nk == 1 else body, operands, carry=carry, name=name, grid=(m // tm, n // tn, nk),
        in_specs=in_specs, out_specs=pl.BlockSpec((tm, tn), lambda i, j, k: (i, j)),
        out_shape=jax.ShapeDtypeStruct((m, n), out_dtype),
        scratch_shapes=[] if nk == 1 else [pltpu.VMEM((tm, tn), F32)],
        compiler_params=_params(3),
    )


def _f_lru_pre(x, norm, b_out):
    return _rmsnorm(x, norm), x + b_out


def _f_lru_core(uy, ux, b_y, b_x, cw, cb, gw, gb, lam):
    yb = _gelu(uy + b_y)
    xr = _causal_conv(ux + b_x, cw, cb)
    gr = jax.nn.sigmoid(_bdot(xr, gw[0, 0]) + gb[0:1])
    gi = jax.nn.sigmoid(_bdot(xr, gw[1, 0]) + gb[1:2])
    log_a = -LRU_C * gr * jax.nn.softplus(-lam)
    a = jnp.exp(log_a)
    bterm = jnp.sqrt(_neg_expm1(2.0 * log_a)) * (gi * xr)
    return (_linear_scan(a, bterm) * yb,)


def _f_norm(x, norm):
    return (_rmsnorm(x, norm),)


def _f_ffn_core(ug, uv, cw, cb):
    return (_gelu(_causal_conv(ug, cw, cb)) * uv,)


def _f_rwkv_mix(h, mix):
    xx = _shift_down(h, 1) - h
    return tuple(h + xx * mix[i:i + 1] for i in range(6))


def _f_rwkv_pre(k, lw1, la1, lg1, w0, a0, k_k, k_a, w2, a2, g2):
    wpre = w0 + _bdot(jnp.tanh(lw1), w2)
    apre = a0 + _bdot(la1, a2)
    g = _bdot(jax.nn.sigmoid(lg1), g2)
    log_decay = -jnp.exp(-jax.nn.softplus(-wpre) - 0.5)
    a = jax.nn.sigmoid(apre)
    kk = k * k_k
    kk = kk / jnp.maximum(jnp.sqrt(_head_sum(kk * kk)), 1e-12)
    kf = k * (1.0 + (a - 1.0) * k_a)
    return log_decay, kf, -kk, kk * a, g


def _f_rwkv_post(y, r, kf, v, g, ln_w, ln_b, r_k):
    inv = 1.0 / HEAD
    yc = y - _head_sum(y) * inv
    var = _head_sum(yc * yc) * inv
    yn = yc * lax.rsqrt(var + GN_EPS) * ln_w + ln_b
    bonus = _head_sum(r * kf * r_k) * v
    return ((yn + bonus) * g,)


def _hdot(a, b, ca, cb):
    return lax.dot_general(a, b, (((ca,), (cb,)), ((0,), (0,))), preferred_element_type=F32)


def _hmm_impl(a, b, ca, cb):
    return _hdot(a.astype(BF16), b.astype(BF16), ca, cb)


@functools.partial(jax.custom_vjp, nondiff_argnums=(2, 3))
def _hmm(a, b, ca, cb):
    return _hmm_impl(a, b, ca, cb)


def _hmm_fwd(a, b, ca, cb):
    return _hmm_impl(a, b, ca, cb), (a, b)


def _hmm_bwd(ca, cb, res, dc):
    a, b = res
    fa, fb = 3 - ca, 3 - cb
    da = _hmm(dc, b, 2, fb) if ca == 2 else _hmm(b, dc, fb, 2)
    db = _hmm(a, dc, fa, 1) if cb == 1 else _hmm(dc, a, 1, fa)
    return da, db


_hmm.defvjp(_hmm_fwd, _hmm_bwd)


def _tri_dot_impl(x, upper):
    g, n, _ = x.shape
    row = lax.broadcasted_iota(jnp.int32, (g, n, 3 * n), 1)
    col = lax.broadcasted_iota(jnp.int32, (g, n, 3 * n), 2)
    col = jnp.where(col >= 2 * n, col - 2 * n, jnp.where(col >= n, col - n, col))
    tri = (row <= col if upper else row >= col).astype(BF16)
    hi = x.astype(BF16)
    rem = x - hi.astype(F32)
    mid = rem.astype(BF16)
    lo = (rem - mid.astype(F32)).astype(BF16)
    return _hdot(tri, jnp.concatenate([hi, mid, lo], axis=1), 2, 1)


@functools.partial(jax.custom_vjp, nondiff_argnums=(1,))
def _tri_dot(x, upper):
    return _tri_dot_impl(x, upper)


def _tri_dot_fwd(x, upper):
    return _tri_dot_impl(x, upper), None


def _tri_dot_bwd(upper, _, g):
    return (_tri_dot(g, not upper),)


_tri_dot.defvjp(_tri_dot_fwd, _tri_dot_bwd)


def _unit_lower_inverse_impl(a):
    g, n, _ = a.shape
    row = lax.broadcasted_iota(jnp.int32, (g, n, n), 1)
    col = lax.broadcasted_iota(jnp.int32, (g, n, n), 2)
    inv = (row == col).astype(F32) + a
    p = a
    for _ in range(int(math.log2(n)) - 1):
        p = _hmm(p, p, 2, 1)
        inv = inv + _hmm(inv, p, 2, 1)
    return inv


@jax.custom_vjp
def _unit_lower_inverse(a):
    return _unit_lower_inverse_impl(a)


def _unit_lower_inverse_fwd(a):
    inv = _unit_lower_inverse_impl(a)
    return inv, inv


def _unit_lower_inverse_bwd(inv, d_inv):
    return (_hmm(_hmm(inv, d_inv, 1, 1), inv, 2, 2),)


_unit_lower_inverse.defvjp(_unit_lower_inverse_fwd, _unit_lower_inverse_bwd)


def _rwkv_chunk(z0, r, lw, k, v, a, b):
    g, n, _ = r.shape
    row = lax.broadcasted_iota(jnp.int32, (g, n, n), 1)
    col = lax.broadcasted_iota(jnp.int32, (g, n, n), 2)
    incl, strict = row >= col, row > col
    cs = _tri_dot(lw, False)
    c_last = cs[:, n - 1:n]
    inv = jnp.exp(-cs)
    ar = jnp.concatenate([a * jnp.exp(cs - lw), r * jnp.exp(cs)], axis=1)
    bk = jnp.concatenate([b * inv, k * inv], axis=1)
    pair = _hmm(ar, bk, 2, 2)
    a_ab = jnp.where(strict, pair[:, :n, :n], 0.0)
    a_ak = jnp.where(strict, pair[:, :n, n:], 0.0)
    a_rbk = jnp.concatenate([jnp.where(incl, pair[:, n:, :n], 0.0), jnp.where(incl, pair[:, n:, n:], 0.0)], axis=2)
    arz = _hmm(ar, z0, 2, 1)
    u = _hmm(_unit_lower_inverse(a_ab), arz[:, :n] + _hmm(a_ak, v, 2, 1), 2, 1)
    uv = jnp.concatenate([u, v], axis=1)
    y = arz[:, n:] + _hmm(a_rbk, uv, 2, 1)
    tail = jnp.exp(c_last - cs)
    er = lax.broadcasted_iota(jnp.int32, (g, HEAD, HEAD), 1)
    ec = lax.broadcasted_iota(jnp.int32, (g, HEAD, HEAD), 2)
    decay_all = jnp.where(er == ec, jnp.exp(c_last), 0.0)
    z_l = _hmm(jnp.concatenate([b * tail, k * tail, decay_all], axis=1), jnp.concatenate([uv, z0], axis=1), 1, 1)
    return y, z_l


def _heads_in(ref, g):
    x = ref[...]
    return jnp.stack([x[:, h * HEAD:(h + 1) * HEAD] for h in range(g)], axis=0)


def _heads_out(ref, x):
    ref[...] = jnp.concatenate([x[h] for h in range(x.shape[0])], axis=1)


def _scan_geometry(n_rows, t):
    g, n = _tile(N_HEAD, SCAN_HEADS), _tile(t, CHUNK)
    return g, n, t // n, N_HEAD // g, n_rows // t


def _rwkv_scan_fwd(seqs, t, carry=None):
    rows = seqs[0].shape[0]
    g, n, nc, hpg, bsz = _scan_geometry(rows, t)

    def body(r_ref, lw_ref, k_ref, v_ref, a_ref, b_ref, y_ref, zs_ref, z_ref):
        @pl.when(pl.program_id(1) == 0)
        def _():
            z_ref[...] = jnp.zeros(z_ref.shape, F32)

        z0 = z_ref[...]
        zs_ref[:, 0] = z0
        y, z_l = _rwkv_chunk(z0, *[_heads_in(ref, g) for ref in (r_ref, lw_ref, k_ref, v_ref, a_ref, b_ref)])
        _heads_out(y_ref, y)
        z_ref[...] = z_l

    seq_spec = pl.BlockSpec((n, g * HEAD), lambda i, c: ((i // hpg) * nc + c, i % hpg))
    return _call(
        body, list(seqs), carry=carry, name="rwkv_scan_fwd", grid=(bsz * hpg, nc), in_specs=[seq_spec] * 6,
        out_specs=[seq_spec, pl.BlockSpec((g, 1, HEAD, HEAD), lambda i, c: (i, c, 0, 0))],
        out_shape=[jax.ShapeDtypeStruct((rows, D), F32), jax.ShapeDtypeStruct((bsz * N_HEAD, nc, HEAD, HEAD), F32)],
        scratch_shapes=[pltpu.VMEM((g, HEAD, HEAD), F32)],
        compiler_params=_params(2),
    )


def _rwkv_scan_bwd(seqs, zs, dy, t, carry=None):
    rows = seqs[0].shape[0]
    g, n, nc, hpg, bsz = _scan_geometry(rows, t)

    def body(r_ref, lw_ref, k_ref, v_ref, a_ref, b_ref, zs_ref, dy_ref, dr, dlw, dk, dv, da, db, dz_ref):
        @pl.when(pl.program_id(1) == 0)
        def _():
            dz_ref[...] = jnp.zeros(dz_ref.shape, F32)

        _, vjp = jax.vjp(_rwkv_chunk, zs_ref[:, 0],
                         *[_heads_in(ref, g) for ref in (r_ref, lw_ref, k_ref, v_ref, a_ref, b_ref)])
        grads = vjp((_heads_in(dy_ref, g), dz_ref[...]))
        dz_ref[...] = grads[0]
        for o, gr in zip((dr, dlw, dk, dv, da, db), grads[1:]):
            _heads_out(o, gr)

    seq_spec = pl.BlockSpec((n, g * HEAD), lambda i, c: ((i // hpg) * nc + nc - 1 - c, i % hpg))
    return _call(
        body, [*seqs, zs, dy], carry=carry, name="rwkv_scan_bwd", grid=(bsz * hpg, nc),
        in_specs=[seq_spec] * 6 + [pl.BlockSpec((g, 1, HEAD, HEAD), lambda i, c: (i, nc - 1 - c, 0, 0)), seq_spec],
        out_specs=[seq_spec] * 6,
        out_shape=[jax.ShapeDtypeStruct((rows, D), F32)] * 6,
        scratch_shapes=[pltpu.VMEM((g, HEAD, HEAD), F32)],
        compiler_params=_params(2),
    )


def _loss_head(x, target, norm):
    n = x.shape[0]
    tr = _tile(n, 256)

    def f(xv, gv, tv):
        err = _rmsnorm(xv, gv) - tv
        return 0.5 * jnp.sum(jnp.mean(err * err, axis=-1, keepdims=True), axis=0, keepdims=True)

    def body(x_ref, t_ref, g_ref, dx_ref, dg_ref, loss_ref, dx16_ref):
        val, vjp = jax.vjp(lambda xv, gv: f(xv, gv, t_ref[...]), x_ref[...], g_ref[...])
        dx, dg = vjp(jnp.ones((1, 1), F32))
        dx_ref[...] = dx
        dx16_ref[...] = dx.astype(BF16)

        @pl.when(pl.program_id(0) == 0)
        def _():
            dg_ref[...] = jnp.zeros(dg_ref.shape, F32)
            loss_ref[...] = jnp.zeros(loss_ref.shape, F32)

        dg_ref[...] += dg
        loss_ref[...] += jnp.broadcast_to(val, loss_ref.shape)

    row = pl.BlockSpec((tr, D), lambda i: (i, 0))
    vec = pl.BlockSpec((1, D), lambda i: (0, 0))
    dx, dg, loss, dx16 = _pcall(
        body, name="loss_head", grid=(n // tr,), in_specs=[row, row, vec],
        out_specs=[row, vec, pl.BlockSpec((8, LANE), lambda i: (0, 0)), row],
        out_shape=[jax.ShapeDtypeStruct((n, D), F32), jax.ShapeDtypeStruct((1, D), F32),
                   jax.ShapeDtypeStruct((8, LANE), F32), jax.ShapeDtypeStruct((n, D), BF16)],
        compiler_params=_params(1),
    )(x, target, norm)
    return loss[0, 0], (dx, dx16), dg


def _exchange(name, items):
    n_x = len(items)

    def body(*refs):
        srcs, dsts, sems = refs[:n_x], refs[n_x:2 * n_x], refs[2 * n_x:]
        started = [_exchange_copies(srcs[i], dsts[i], *sems[3 * i:3 * i + 3], items[i][1]) for i in range(n_x)]
        for mine, copies in started:
            _start_exchange(mine, copies)
        for mine, copies in started:
            _wait_exchange(mine, copies)

    hbm = pl.BlockSpec(memory_space=pl.ANY)
    return _pcall(
        body, name=name, in_specs=[hbm] * n_x, out_specs=[hbm] * n_x,
        out_shape=[_landing_shape(a, sc) for (a, sc) in items], scratch_shapes=_EXCHANGE_SEMS * n_x,
    )(*[a for (a, _) in items])


def _adamw(name, parts, w, m, v):
    rows, cols = w.shape
    tr = _tile(rows, PACK_ROWS)

    def body(p_ref, w_ref, m_ref, v_ref, g_ref, d_ref, nm_ref, nv_ref):
        g = p_ref[0]
        for k in range(1, N_DEV):
            g = g + p_ref[k]
        nm = ADAM_B1 * m_ref[...] + (1.0 - ADAM_B1) * g
        nv = ADAM_B2 * v_ref[...] + (1.0 - ADAM_B2) * jnp.square(g)
        m_hat = nm / (1.0 - ADAM_B1 ** ADAM_STEP)
        v_hat = nv / (1.0 - ADAM_B2 ** ADAM_STEP)
        g_ref[...] = g
        d_ref[...] = -ADAM_LR * (m_hat / (jnp.sqrt(v_hat) + ADAM_EPS) + ADAM_WD * w_ref[...])
        nm_ref[...] = nm
        nv_ref[...] = nv

    row = pl.BlockSpec((tr, cols), lambda i: (i, 0))
    return _pcall(
        body, name=name, grid=(rows // tr,),
        in_specs=[pl.BlockSpec((N_DEV, tr, cols), lambda i: (0, i, 0)), row, row, row],
        out_specs=[row] * 4, out_shape=[jax.ShapeDtypeStruct((rows, cols), F32)] * 4,
        compiler_params=_params(1),
    )(parts, w, m, v)


def _pack(arrs, dtype, lead=()):
    flat = jnp.concatenate([a.astype(dtype).reshape(lead + (-1,)) for a in arrs], axis=-1)
    n = flat.shape[-1]
    quantum = PACK_W * PACK_ROWS
    total = -(-n // quantum) * quantum
    flat = jnp.pad(flat, [(0, 0)] * len(lead) + [(0, total - n)])
    return flat.reshape(lead + (total // PACK_W, PACK_W))


def _unpack(buf, shapes, lead=()):
    flat = buf.reshape(lead + (-1,))
    out, off = [], 0
    for s in shapes:
        n = math.prod(s)
        out.append(flat[..., off:off + n].reshape(lead + tuple(s)))
        off += n
    return out


def _unshard(g, axis):
    local = g.shape[1:]
    full = jnp.moveaxis(g, 0, axis)
    return full.reshape(local[:axis] + (N_DEV * local[axis],) + local[axis + 1:])


def _reshard(full, axis):
    s = full.shape
    blocked = full.reshape(s[:axis] + (N_DEV, s[axis] // N_DEV) + s[axis + 1:])
    return jnp.moveaxis(blocked, axis, 0)


def _row_item(a, tr):
    return (a, (tr, a.shape[1]), lambda i: (i, 0))


def _vec_item(a):
    return (a, a.shape, lambda i: (0,) * a.ndim)


def _seq_item(a, t, width):
    return (a, (t, width), lambda j, b: (b, j))


def _seq_halves(a, t, width):
    half = a.shape[1] // 2
    off = half // width
    grad = ((a.shape[0], half), lambda j, b: (b, j))
    return [(a, (t, width), lambda j, b: (b, j), grad), (a, (t, width), lambda j, b: (b, j + off), grad)]


def _col_item(a, width):
    return (a, (a.shape[0], width), lambda j, b: (0, j))


def _both(g):
    return g if isinstance(g, tuple) else (g, g)


def _ffn_fwd(tag, x, p, t, carries=None, late=None):
    carries = carries or {}
    n = x.shape[0]
    tr = _tile(n, 512)
    norm_ins, norm_par = [_row_item(x, tr)], [_vec_item(p["norm"])]
    (h,) = _stage_fwd(f"{tag}_norm", _f_norm, (n // tr,), norm_ins, norm_par, [((n, D), BF16, (tr, D), lambda i: (i, 0))])
    u = _matmul(f"{tag}_up", h, p["w_up"], tm=2048, tn=768, carry=carries.get("up"))
    grid = (D_FF // CT, n // t)
    core_par = [_col_item(p["conv_w"], CT), _col_item(p["conv_b"], CT)]
    (hid,) = _stage_fwd(f"{tag}_core", _f_ffn_core, grid, _seq_halves(u, t, CT), core_par,
                        [((n, D_FF), BF16, (t, CT), lambda j, b: (b, j))], carry=carries.get("core"))
    if late is not None:
        late()
    y = _matmul(f"{tag}_down", hid, p["w_down"], resid=x, tm=1024, tn=1024, tk=D_FF)
    return y, (x, h, u, hid)


def _ffn_bwd(tag, dy, saved, p, t, carries=None, down_rides=None):
    carries = carries or {}
    x, h, u, hid = saved
    n = x.shape[0]
    tr = _tile(n, 512)
    grads = {}
    dy, dy16 = _both(dy)
    dhid = _matmul(f"{tag}_down_da", dy16, p["w_down"], mode="nt", tm=2048, tn=768, carry=carries.get("down_da"))
    grads["w_down"] = _matmul(f"{tag}_down_dw", hid, dy16, mode="tn", tm=768, tn=1024, tk=2048,
                              carry=carries.get("down_dw"))
    after = down_rides(grads["w_down"]) if down_rides is not None else [None] * 3
    grid = (D_FF // CT, n // t)
    core_par = [_col_item(p["conv_w"], CT), _col_item(p["conv_b"], CT)]
    (dug, duv), (grads["conv_w"], grads["conv_b"]) = _stage_bwd(
        f"{tag}_core_bwd", _f_ffn_core, grid, _seq_halves(u, t, CT), core_par,
        [([dhid], (t, CT), lambda j, b: (b, j))], [BF16, BF16], carry=carries.get("core"))
    dh = _matmul(f"{tag}_up_da_g", dug, p["w_up"], mode="nt", tm=1024, tn=1024, tk=D_FF, carry=after[0])
    dh = _matmul(f"{tag}_up_da_v", duv, p["w_up"], mode="nt", tm=1024, tn=1024, tk=D_FF, b_koff=D_FF, resid=dh,
                 carry=after[1])
    grads["w_up"] = jnp.concatenate([
        _matmul(f"{tag}_up_dw_g", h, dug, mode="tn", tm=1024, tn=768, tk=2048, carry=after[2]),
        _matmul(f"{tag}_up_dw_v", h, duv, mode="tn", tm=1024, tn=768, tk=2048)], axis=1)
    dx, (grads["norm"],) = _stage_bwd(
        f"{tag}_norm_bwd", _f_norm, (n // tr,), [_row_item(x, tr)], [_vec_item(p["norm"])],
        [([dh], (tr, D), lambda i: (i, 0))], [F32], adds={0: dy}, bf16_copy=True)
    return tuple(dx), grads


def _lru_fwd(x, p, t, carries=None, late=None):
    carries = carries or {}
    n = x.shape[0]
    tr = _tile(n, 512)
    row = lambda dt: ((n, D), dt, (tr, D), lambda i: (i, 0))
    h, xb = _stage_fwd("lru_pre", _f_lru_pre, (n // tr,), [_row_item(x, tr)],
                       [_vec_item(p["norm"]), _vec_item(p["b_out"])], [row(BF16), row(F32)], carry=carries.get("pre"))
    if late is not None:
        late("pre")
    u = _matmul("lru_in", h, p["w_in"], tm=1024, tn=1024, carry=carries.get("in"))
    if late is not None:
        late("in")
    out, = _stage_fwd("lru_core", _f_lru_core, (D // CT, n // t), _seq_halves(u, t, CT), _lru_core_params(p),
                      [((n, D), BF16, (t, CT), lambda j, b: (b, j))], carry=carries.get("core"))
    y = _matmul("lru_out", out, p["w_out"], resid=xb, tm=1024, tn=1024)
    return y, (x, h, u, out)


def _lru_core_params(p):
    return [_col_item(p["b_y"], CT), _col_item(p["b_x"], CT), _col_item(p["conv_w"], CT), _col_item(p["conv_b"], CT),
            (p["gate_w"], (2, 1, CT, CT), lambda j, b: (0, j, 0, 0)), _col_item(p["gate_b"], CT), _col_item(p["lam"], CT)]


def _lru_bwd(dy, saved, p, t, carry=None, out_rides=None):
    x, h, u, out = saved
    n = x.shape[0]
    tr = _tile(n, 512)
    grads = {}
    dy, dy16 = _both(dy)
    dout = _matmul("lru_out_da", dy16, p["w_out"], mode="nt", tm=1024, tn=1024)
    grads["w_out"] = _matmul("lru_out_dw", out, dy16, mode="tn", tm=1024, tn=1024)
    after = out_rides(grads["w_out"]) if out_rides is not None else [None] * 2
    (duy, dux), core_grads = _stage_bwd("lru_core_bwd", _f_lru_core, (D // CT, n // t), _seq_halves(u, t, CT),
                                        _lru_core_params(p), [([dout], (t, CT), lambda j, b: (b, j))], [BF16, BF16],
                                        carry=carry)
    for k, g in zip(("b_y", "b_x", "conv_w", "conv_b", "gate_w", "gate_b", "lam"), core_grads):
        grads[k] = g
    dh = _matmul("lru_in_da_y", duy, p["w_in"], mode="nt", tm=1024, tn=1024, carry=after[0])
    dh = _matmul("lru_in_da_x", dux, p["w_in"], mode="nt", tm=1024, tn=1024, b_koff=D, resid=dh, carry=after[1])
    grads["w_in"] = jnp.concatenate([_matmul("lru_in_dw_y", h, duy, mode="tn", tm=1024, tn=1024),
                                     _matmul("lru_in_dw_x", h, dux, mode="tn", tm=1024, tn=1024)], axis=1)
    row = (tr, D), lambda i: (i, 0)
    (dx,), (grads["norm"], grads["b_out"]) = _stage_bwd(
        "lru_pre_bwd", _f_lru_pre, (n // tr,), [_row_item(x, tr)], [_vec_item(p["norm"]), _vec_item(p["b_out"])],
        [([dh], *row), ([dy], *row)], [F32])
    return dx, grads


_RWKV_PRE_VECS = ("w0", "a0", "k_k", "k_a", "w2", "a2", "g2")
_RWKV_POST_VECS = ("ln_w", "ln_b", "r_k")


def _rwkv_fwd(x, p, t, carry=None):
    n = x.shape[0]
    bsz = n // t
    tr = _tile(n, 512)
    ts = _tile(n, 256)
    (h,) = _stage_fwd("rwkv_norm", _f_norm, (n // tr,), [_row_item(x, tr)], [_vec_item(p["norm"])],
                      [((n, D), F32, (tr, D), lambda i: (i, 0))])
    mixed = _stage_fwd("rwkv_mix", _f_rwkv_mix, (D // CT, bsz), [_seq_item(h, t, CT)], [_col_item(p["mix"], CT)],
                       [((n, D), BF16, (t, CT), lambda j, b: (b, j))] * 6)
    xr, xk, xv, xw, xa, xg = mixed
    r = _matmul("rwkv_r", xr, p["w_r"], tm=1024, tn=1024)
    k = _matmul("rwkv_k", xk, p["w_k"], tm=1024, tn=1024)
    v = _matmul("rwkv_v", xv, p["w_v"], tm=1024, tn=1024)
    lw1 = _matmul("rwkv_w1", xw, p["w1"], tm=1024)
    la1 = _matmul("rwkv_a1", xa, p["a1"], tm=1024)
    lg1 = _matmul("rwkv_g1", xg, p["g1"], tm=1024)
    pre_ins = [_row_item(a, ts) for a in (k, lw1, la1, lg1)]
    pre_par = [_vec_item(p[q]) for q in _RWKV_PRE_VECS]
    row = ((n, D), F32, (ts, D), lambda i: (i, 0))
    lw, kf, aa, bb, g = _stage_fwd("rwkv_pre", _f_rwkv_pre, (n // ts,), pre_ins, pre_par, [row] * 5)
    y, zs = _rwkv_scan_fwd([r, lw, kf, v, aa, bb], t, carry=carry)
    post_ins = [_row_item(a, ts) for a in (y, r, kf, v, g)]
    post_par = [_vec_item(p[q]) for q in _RWKV_POST_VECS]
    (yg,) = _stage_fwd("rwkv_post", _f_rwkv_post, (n // ts,), post_ins, post_par,
                       [((n, D), BF16, (ts, D), lambda i: (i, 0))])
    out = _matmul("rwkv_out", yg, p["w_out"], resid=x, tm=1024, tn=1024)
    return out, (x, h, mixed, r, k, v, lw1, la1, lg1, lw, kf, aa, bb, g, zs, y, yg)


def _rwkv_bwd(dout, saved, p, t, carry=None):
    x, h, mixed, r, k, v, lw1, la1, lg1, lw, kf, aa, bb, g, zs, y, yg = saved
    xr, xk, xv, xw, xa, xg = mixed
    n = x.shape[0]
    bsz = n // t
    tr = _tile(n, 512)
    ts = _tile(n, 256)
    grads = {}
    row_s = (ts, D), lambda i: (i, 0)
    dout, dout16 = _both(dout)
    dyg = _matmul("rwkv_out_da", dout16, p["w_out"], mode="nt", tm=1024, tn=1024)
    grads["w_out"] = _matmul("rwkv_out_dw", yg, dout16, mode="tn", tm=1024, tn=1024)
    post_ins = [_row_item(a, ts) for a in (y, r, kf, v, g)]
    post_par = [_vec_item(p[q]) for q in _RWKV_POST_VECS]
    (dy, dr_a, dkf_a, dv_a, dg), post_grads = _stage_bwd(
        "rwkv_post_bwd", _f_rwkv_post, (n // ts,), post_ins, post_par, [([dyg], *row_s)], [F32] * 5)
    grads.update(zip(_RWKV_POST_VECS, post_grads))
    dr_b, dlw, dkf_b, dv_b, daa, dbb = _rwkv_scan_bwd([r, lw, kf, v, aa, bb], zs, dy, t, carry=carry)
    pre_ins = [_row_item(a, ts) for a in (k, lw1, la1, lg1)]
    pre_par = [_vec_item(p[q]) for q in _RWKV_PRE_VECS]
    (dk, dlw1, dla1, dlg1), pre_grads = _stage_bwd(
        "rwkv_pre_bwd", _f_rwkv_pre, (n // ts,), pre_ins, pre_par,
        [([dlw], *row_s), ([dkf_a, dkf_b], *row_s), ([daa], *row_s), ([dbb], *row_s), ([dg], *row_s)], [F32] * 4)
    grads.update(zip(_RWKV_PRE_VECS, pre_grads))
    dmixed = []
    for tag, xin, dz, dz2, wname in (("r", xr, dr_a, dr_b, "w_r"), ("k", xk, dk, None, "w_k"), ("v", xv, dv_a, dv_b, "w_v"),
                                     ("w1", xw, dlw1, None, "w1"), ("a1", xa, dla1, None, "a1"), ("g1", xg, dlg1, None, "g1")):
        dmixed.append(_matmul(f"rwkv_{tag}_da", dz, p[wname], mode="nt", tm=1024, tn=1024, a_plus=dz2))
        grads[wname] = _matmul(f"rwkv_{tag}_dw", xin, dz, mode="tn", tm=1024, tn=1024, b_plus=dz2)
    seq_blk = (t, CT), lambda j, b: (b, j)
    (dh,), (grads["mix"],) = _stage_bwd(
        "rwkv_mix_bwd", _f_rwkv_mix, (D // CT, bsz), [_seq_item(h, t, CT)], [_col_item(p["mix"], CT)],
        [([a], *seq_blk) for a in dmixed], [F32])
    dx, (grads["norm"],) = _stage_bwd(
        "rwkv_norm_bwd", _f_norm, (n // tr,), [_row_item(x, tr)], [_vec_item(p["norm"])],
        [([dh], (tr, D), lambda i: (i, 0))], [F32], adds={0: dout}, bf16_copy=True)
    return tuple(dx), grads


def _block_diag_gates(gate_w):
    z = jnp.zeros((2, D // CT, HEAD, HEAD), gate_w.dtype)
    even, odd = gate_w[:, 0::2], gate_w[:, 1::2]
    top = jnp.concatenate([even, z], axis=-1)
    bot = jnp.concatenate([z, odd], axis=-1)
    return jnp.concatenate([top, bot], axis=-2)


def _gate_blocks_grad(dg):
    even, odd = dg[:, :, :HEAD, :HEAD], dg[:, :, HEAD:, HEAD:]
    return jnp.stack([even, odd], axis=2).reshape(2, N_HEAD, HEAD, HEAD)


def _step(local, x, target):
    bsz, t, _ = x.shape
    n = bsz * t

    def block(name, pre=""):
        base, _, layer = name.partition("@")
        a = local[pre + base]
        return a[int(layer):int(layer) + 1] if layer else a

    def split(names):
        return [k for k in names if _base(k) in LARGE], [k for k in names if _base(k) not in LARGE]

    def gather_items(names):
        large, small = split(names)
        flat = [block(k).astype(BF16).reshape(-1) if _base(k) in GATHER_BF16
                else lax.bitcast_convert_type(block(k), BF16).reshape(-1) for k in small]
        return [(block(k).astype(BF16), False) for k in large] + ([(_pack(flat, BF16), False)] if small else [])

    def gathered(names, landed):
        large, small = split(names)
        out = {k: _unshard(g, SHARD_AXIS[_base(k)]) for k, g in zip(large, landed)}
        sizes = [(1 if _base(k) in GATHER_BF16 else 2) * math.prod(block(k).shape) for k in small]
        for k, g in zip(small, _unpack(landed[-1], [(s,) for s in sizes], (N_DEV,)) if small else []):
            if _base(k) not in GATHER_BF16:
                g = lax.bitcast_convert_type(g.reshape(N_DEV, -1, 2), F32)
            out[k] = _unshard(g.reshape((N_DEV,) + block(k).shape), SHARD_AXIS[_base(k)])
        return out

    landed_large, landed_small = {}, {}

    def scatter_items(names, grads):
        large, small = split(names)
        blocked = {k: _reshard(grads[k], SHARD_AXIS[_base(k)]) for k in names}
        return [(blocked[k], True) for k in large] + ([(_pack([blocked[k] for k in small], F32, (N_DEV,)), True)] if small else [])

    def scattered(names, landed):
        large, small = split(names)
        landed_large.update(zip(large, landed))
        if small:
            landed_small[tuple(small)] = landed[-1]

    def ffn_grads(l, g):
        return {f"ffn_w_up@{l}": g["w_up"][None], f"ffn_w_down@{l}": g["w_down"][None], f"ffn_conv_w@{l}": g["conv_w"][None]}

    lru = dict(norm=local["lru_norm"], b_out=local["lru_b_out"], b_y=local["lru_b_in"][:, :D], b_x=local["lru_b_in"][:, D:],
               conv_b=local["lru_conv_b"], gate_w=_block_diag_gates(local["lru_gate_w"][0]),
               gate_b=local["lru_gate_b"][0].reshape(2, D), lam=local["lru_lambda"])
    lru_first = ["lru_w_in", "lru_conv_w"]
    ride_pre = _Carry(gather_items(lru_first))
    ride_in = _Carry(gather_items(["lru_w_out"]))
    ffn0_first = ["ffn_w_up@0", "ffn_conv_w@0"]
    ride_core = _Carry(gather_items(ffn0_first))

    def lru_late(which):
        if which == "pre":
            full = gathered(lru_first, ride_pre.landed)
            lru.update(w_in=full["lru_w_in"][0], conv_w=full["lru_conv_w"][0])
        else:
            lru.update(w_out=gathered(["lru_w_out"], ride_in.landed)["lru_w_out"][0])

    x0 = x.reshape(n, D)
    x1, s_lru = _lru_fwd(x0, lru, t, carries={"pre": ride_pre, "in": ride_in, "core": ride_core}, late=lru_late)
    full = gathered(ffn0_first, ride_core.landed)
    ffn0 = dict(norm=local["ffn_norm"][0:1], conv_b=local["ffn_conv_b"][0:1], w_up=full["ffn_w_up@0"][0],
                conv_w=full["ffn_conv_w@0"][0])
    ride_up = _Carry(gather_items(GROUPS["rwkv"]))
    ride_down = _Carry(gather_items(["ffn_w_down@0"]))

    def ffn0_late():
        ffn0["w_down"] = gathered(["ffn_w_down@0"], ride_down.landed)["ffn_w_down@0"][0]

    x2, s_ffn0 = _ffn_fwd("ffn0", x1, ffn0, t, carries={"up": ride_up, "core": ride_down}, late=ffn0_late)
    full = gathered(GROUPS["rwkv"], ride_up.landed)
    rwkv = dict(norm=full["rwkv_norm"], mix=full["rwkv_mix"][0], w_r=full["rwkv_w_rkv"][0, 0],
                w_k=full["rwkv_w_rkv"][0, 1], w_v=full["rwkv_w_rkv"][0, 2], w0=full["rwkv_w0"], w1=full["rwkv_w1"][0],
                w2=full["rwkv_w2"][0], a0=full["rwkv_a0"], a1=full["rwkv_a1"][0], a2=full["rwkv_a2"][0],
                g1=full["rwkv_g1"][0], g2=full["rwkv_g2"][0], k_k=full["rwkv_k_k"], k_a=full["rwkv_k_a"],
                r_k=local["rwkv_r_k"].reshape(1, D), ln_w=full["rwkv_ln_w"], ln_b=full["rwkv_ln_b"],
                w_out=full["rwkv_w_out"][0])
    ride = _Carry(gather_items(GROUPS["ffn1"]))
    x3, s_rwkv = _rwkv_fwd(x2, rwkv, t, carry=ride)
    full = gathered(GROUPS["ffn1"], ride.landed)
    ffn1 = dict(norm=local["ffn_norm"][1:2], conv_b=local["ffn_conv_b"][1:2], w_up=full["ffn_w_up@1"][0],
                conv_w=full["ffn_conv_w@1"][0], w_down=full["ffn_w_down@1"][0])
    x4, s_ffn1 = _ffn_fwd("ffn1", x3, ffn1, t)
    loss, dx4, d_final = _loss_head(x4, target.reshape(n, D), local["final_norm"].reshape(1, D))

    dx3, g_ffn1 = _ffn_bwd("ffn1", dx4, s_ffn1, ffn1, t)
    ride = _Carry(scatter_items(GROUPS["ffn1"], ffn_grads(1, g_ffn1)))
    dx2, g_rwkv = _rwkv_bwd(dx3, s_rwkv, rwkv, t, carry=ride)
    scattered(GROUPS["ffn1"], ride.landed)
    rwkv_grads = {
        "rwkv_norm": g_rwkv["norm"], "rwkv_mix": g_rwkv["mix"][None],
        "rwkv_w_rkv": jnp.stack([g_rwkv["w_r"], g_rwkv["w_k"], g_rwkv["w_v"]])[None],
        "rwkv_w0": g_rwkv["w0"], "rwkv_w1": g_rwkv["w1"][None], "rwkv_w2": g_rwkv["w2"][None],
        "rwkv_a0": g_rwkv["a0"], "rwkv_a1": g_rwkv["a1"][None], "rwkv_a2": g_rwkv["a2"][None],
        "rwkv_g1": g_rwkv["g1"][None], "rwkv_g2": g_rwkv["g2"][None], "rwkv_k_k": g_rwkv["k_k"],
        "rwkv_k_a": g_rwkv["k_a"], "rwkv_ln_w": g_rwkv["ln_w"], "rwkv_ln_b": g_rwkv["ln_b"],
        "rwkv_w_out": g_rwkv["w_out"][None],
    }
    rwkv_small = split(GROUPS["rwkv"])[1]
    ride_da = _Carry(scatter_items(["rwkv_w_out"], rwkv_grads))
    ride_dw = _Carry(scatter_items(rwkv_small, rwkv_grads))
    ride_core = _Carry(scatter_items(["rwkv_w_rkv"], rwkv_grads))
    down_pieces = []

    def down_rides(g_w_down):
        blocked = _reshard(g_w_down[None], SHARD_AXIS["ffn_w_down"])
        rows = blocked.shape[2] // 3
        down_pieces.extend(_Carry([(blocked[:, :, i * rows:(i + 1) * rows], True)]) for i in range(3))
        return down_pieces

    dx1, g_ffn0 = _ffn_bwd("ffn0", dx2, s_ffn0, ffn0, t,
                           carries={"down_da": ride_da, "down_dw": ride_dw, "core": ride_core}, down_rides=down_rides)
    scattered(["rwkv_w_out"], ride_da.landed)
    scattered(rwkv_small, ride_dw.landed)
    scattered(["rwkv_w_rkv"], ride_core.landed)
    landed_large["ffn_w_down@0"] = jnp.concatenate([c.landed[0] for c in down_pieces], axis=2)
    ride = _Carry(scatter_items(ffn0_first, ffn_grads(0, g_ffn0)))
    out_pieces = []

    def out_rides(g_w_out):
        blocked = _reshard(g_w_out[None], SHARD_AXIS["lru_w_out"])
        rows = blocked.shape[2] // 2
        out_pieces.extend(_Carry([(blocked[:, :, i * rows:(i + 1) * rows], True)]) for i in range(2))
        return out_pieces

    dx0, g_lru = _lru_bwd(dx1, s_lru, lru, t, carry=ride, out_rides=out_rides)
    scattered(ffn0_first, ride.landed)
    landed_large["lru_w_out"] = jnp.concatenate([c.landed[0] for c in out_pieces], axis=2)

    g_ffn = [g_ffn0, g_ffn1]
    lru_grads = {"lru_w_in": g_lru["w_in"][None], "lru_conv_w": g_lru["conv_w"][None], "lru_w_out": g_lru["w_out"][None]}
    gfull = {
        "lru_norm": g_lru["norm"], "lru_b_in": jnp.concatenate([g_lru["b_y"], g_lru["b_x"]], axis=1),
        "lru_conv_b": g_lru["conv_b"], "lru_gate_w": _gate_blocks_grad(g_lru["gate_w"])[None],
        "lru_gate_b": g_lru["gate_b"].reshape(1, 2, N_HEAD, HEAD), "lru_lambda": g_lru["lam"],
        "lru_b_out": g_lru["b_out"], "rwkv_r_k": g_rwkv["r_k"].reshape(1, N_HEAD, HEAD),
        "ffn_norm": jnp.concatenate([g["norm"] for g in g_ffn]),
        "ffn_conv_b": jnp.concatenate([g["conv_b"] for g in g_ffn]),
        "final_norm": d_final.reshape(D),
    }
    *tail, replicated = _exchange(
        "grad_tail", scatter_items(lru_first, lru_grads) + [(_pack([gfull[k] for k in REPLICATED], F32), False)])
    scattered(lru_first, tail)

    out = {}
    kinds = ("grad", "delta", "new_m", "new_v")
    for k, landed in landed_large.items():
        flat = (math.prod(block(k).shape[:-1]), block(k).shape[-1])
        res = _adamw(f"adamw_{k.replace('@', '_')}", landed.reshape((N_DEV,) + flat),
                     *[block(k, pre).reshape(flat) for pre in ("", "m_", "v_")])
        for kind, a in zip(kinds, res):
            out[(kind, k)] = a.reshape(block(k).shape)
    small_sets = [(f"adamw_small_{i}", list(names), landed) for i, (names, landed) in enumerate(landed_small.items())]
    for tag, names, landed in small_sets + [("adamw_replicated", REPLICATED, replicated)]:
        packs = [_pack([block(k, pre) for k in names], F32) for pre in ("", "m_", "v_")]
        res = _adamw(tag, landed, *packs)
        for kind, buf in zip(kinds, res):
            for k, a in zip(names, _unpack(buf, [block(k).shape for k in names])):
                out[(kind, k)] = a
    for kind in ("grad", "delta", "new_m", "new_v"):
        for k in STACKED:
            out[(kind, k)] = jnp.concatenate([out[(kind, f"{k}@0")], out[(kind, f"{k}@1")]], axis=0)
    return loss, dx0.reshape(x.shape), out


def kernel(x, lru_norm, lru_w_in, lru_b_in, lru_conv_w, lru_conv_b, lru_gate_w, lru_gate_b, lru_lambda, lru_w_out, lru_b_out, rwkv_norm, rwkv_mix, rwkv_w_rkv, rwkv_w0, rwkv_w1, rwkv_w2, rwkv_a0, rwkv_a1, rwkv_a2, rwkv_g1, rwkv_g2, rwkv_k_k, rwkv_k_a, rwkv_r_k, rwkv_ln_w, rwkv_ln_b, rwkv_w_out, ffn_norm, ffn_w_up, ffn_conv_w, ffn_conv_b, ffn_w_down, final_norm, loss_target, m_lru_norm, m_lru_w_in, m_lru_b_in, m_lru_conv_w, m_lru_conv_b, m_lru_gate_w, m_lru_gate_b, m_lru_lambda, m_lru_w_out, m_lru_b_out, m_rwkv_norm, m_rwkv_mix, m_rwkv_w_rkv, m_rwkv_w0, m_rwkv_w1, m_rwkv_w2, m_rwkv_a0, m_rwkv_a1, m_rwkv_a2, m_rwkv_g1, m_rwkv_g2, m_rwkv_k_k, m_rwkv_k_a, m_rwkv_r_k, m_rwkv_ln_w, m_rwkv_ln_b, m_rwkv_w_out, m_ffn_norm, m_ffn_w_up, m_ffn_conv_w, m_ffn_conv_b, m_ffn_w_down, m_final_norm, v_lru_norm, v_lru_w_in, v_lru_b_in, v_lru_conv_w, v_lru_conv_b, v_lru_gate_w, v_lru_gate_b, v_lru_lambda, v_lru_w_out, v_lru_b_out, v_rwkv_norm, v_rwkv_mix, v_rwkv_w_rkv, v_rwkv_w0, v_rwkv_w1, v_rwkv_w2, v_rwkv_a0, v_rwkv_a1, v_rwkv_a2, v_rwkv_g1, v_rwkv_g2, v_rwkv_k_k, v_rwkv_k_a, v_rwkv_r_k, v_rwkv_ln_w, v_rwkv_ln_b, v_rwkv_w_out, v_ffn_norm, v_ffn_w_up, v_ffn_conv_w, v_ffn_conv_b, v_ffn_w_down, v_final_norm):
    args = locals()
    local = {(pre + k): args[pre + k] for pre in ("", "m_", "v_") for k in WEIGHTS}
    loss_local, grad_x, out = _step(local, x, loss_target)
    loss = lax.psum(loss_local, ("x", "y", "c"))
    return (loss, grad_x, *[out[(kind, k)] for kind in ("grad", "delta", "new_m", "new_v") for k in WEIGHTS])
```
